```python
import jax
import jax.numpy as jnp
from jax import lax
import numpy as np

D_MODEL = 1024
BATCH = 8
SEQ = 4096
DEPTH = 4

N_MIXERS = 3
HEAD_DIM = 64
BLOCK_Q = 128
GRID_W = 64
RMS_EPS = 1e-6
D_FF = 4 * D_MODEL
A_HEADS = D_MODEL // HEAD_DIM
A_KV_HEADS = A_HEADS // 4
ROPE_THETA = 10000.0
B_GROUPS = ((128, 1), (512, 4), (2048, 16))
B_HEADS_PER_GROUP = 6
B_KV_PER_GROUP = 2
C_HEADS = D_MODEL // HEAD_DIM
C_KV_HEADS = C_HEADS // 4
C_WINDOW = 128

kernel_name = "hybrid_interleaved_bidir_encoder"


def rmsnorm(x, gain):
    xf = x.astype(jnp.float32)
    y = xf * lax.rsqrt(jnp.mean(xf * xf, axis=-1, keepdims=True) + RMS_EPS)
    return (y * gain.astype(jnp.float32)).astype(x.dtype)


def alibi_slopes(n_heads):
    return 2.0 ** (-8.0 * jnp.arange(1, n_heads + 1, dtype=jnp.float32) / n_heads)


def stack_blocks(y, batch, seq):
    y = jnp.moveaxis(y, 0, 1)
    return y.reshape((batch, seq) + y.shape[3:])


def axial_rope_angles(seq):
    rows = seq // GRID_W
    row = jnp.repeat(jnp.arange(rows, dtype=jnp.float32), GRID_W)
    col = jnp.tile(jnp.arange(GRID_W, dtype=jnp.float32), rows)
    axis_dim = HEAD_DIM // 2
    inv_freq = ROPE_THETA ** (-jnp.arange(0, axis_dim, 2, dtype=jnp.float32) / axis_dim)
    return row[:, None] * inv_freq, col[:, None] * inv_freq


def rotate(x, ang):
    shape = (ang.shape[0],) + (1,) * (x.ndim - 3) + (ang.shape[1],)
    cos = jnp.cos(ang).reshape(shape).astype(x.dtype)
    sin = jnp.sin(ang).reshape(shape).astype(x.dtype)
    x1, x2 = jnp.split(x, 2, axis=-1)
    return jnp.concatenate([x1 * cos - x2 * sin, x2 * cos + x1 * sin], axis=-1)


def axial_rope(x, ang_row, ang_col):
    half = HEAD_DIM // 2
    return jnp.concatenate([rotate(x[..., :half], ang_row), rotate(x[..., half:], ang_col)], axis=-1)


def mixer_a(h, w_qkv, q_gain, k_gain, w_o):
    b, s, _ = h.shape
    rep = A_HEADS // A_KV_HEADS
    qkv = h @ w_qkv
    q, k, v = jnp.split(qkv, [A_HEADS * HEAD_DIM, (A_HEADS + A_KV_HEADS) * HEAD_DIM], axis=-1)
    q = rmsnorm(q.reshape(b, s, A_KV_HEADS, rep, HEAD_DIM), q_gain)
    k = rmsnorm(k.reshape(b, s, A_KV_HEADS, HEAD_DIM), k_gain)
    v = v.reshape(b, s, A_KV_HEADS, HEAD_DIM)
    ang_row, ang_col = axial_rope_angles(s)
    q = axial_rope(q, ang_row, ang_col) * (HEAD_DIM ** -0.5)
    k = axial_rope(k, ang_row, ang_col)

    def block(i):
        qb = lax.dynamic_slice_in_dim(q, i * BLOCK_Q, BLOCK_Q, axis=1)
        sc = jnp.einsum('bqhgd,bkhd->bhgqk', qb, k).astype(jnp.float32)
        p = jax.nn.softmax(sc, axis=-1).astype(v.dtype)
        o = jnp.einsum('bhgqk,bkhd->bqhgd', p, v)
        return o.reshape(b, BLOCK_Q, A_HEADS * HEAD_DIM)

    o = stack_blocks(lax.map(block, jnp.arange(s // BLOCK_Q)), b, s)
    return o @ w_o


def mixer_b(h, w_qkv, w_o):
    b, s, _ = h.shape
    n_g = len(B_GROUPS)
    rep = B_HEADS_PER_GROUP // B_KV_PER_GROUP
    nq = n_g * B_HEADS_PER_GROUP * HEAD_DIM
    nk = n_g * B_KV_PER_GROUP * HEAD_DIM
    qkv = h @ w_qkv
    q, k, v = jnp.split(qkv, [nq, nq + nk], axis=-1)
    q = q.reshape(b, s, n_g, B_KV_PER_GROUP, rep, HEAD_DIM) * (HEAD_DIM ** -0.5)
    k = k.reshape(b, s, n_g, B_KV_PER_GROUP, HEAD_DIM)
    v = v.reshape(b, s, n_g, B_KV_PER_GROUP, HEAD_DIM)
    slopes = alibi_slopes(n_g * B_HEADS_PER_GROUP).reshape(n_g, B_KV_PER_GROUP, rep)
    outs, lses = [], []
    for g, (window, dil) in enumerate(B_GROUPS):
        n_side = (window // 2) // dil
        offs = jnp.arange(-n_side, n_side + 1) * dil
        bias = -slopes[g][:, :, None, None] * jnp.abs(offs).astype(jnp.float32)
        qg, kg, vg = q[:, :, g], k[:, :, g], v[:, :, g]

        def block(i, qg=qg, kg=kg, vg=vg, offs=offs, bias=bias):
            start = i * BLOCK_Q
            idx = start + jnp.arange(BLOCK_Q)[:, None] + offs[None, :]
            valid = (idx >= 0) & (idx < s)
            idx = jnp.clip(idx, 0, s - 1)
            kb = jnp.take(kg, idx, axis=1)
            vb = jnp.take(vg, idx, axis=1)
            qb = lax.dynamic_slice_in_dim(qg, start, BLOCK_Q, axis=1)
            sc = jnp.einsum('bqhgd,bqnhd->bhgqn', qb, kb).astype(jnp.float32) + bias
            sc = jnp.where(valid, sc, -jnp.inf)
            m = jnp.max(sc, axis=-1, keepdims=True)
            p = jnp.exp(sc - m)
            den = jnp.sum(p, axis=-1, keepdims=True)
            o = jnp.einsum('bhgqn,bqnhd->bqhgd', (p / den).astype(vb.dtype), vb)
            lse = (m + jnp.log(den))[..., 0].transpose(0, 3, 1, 2)
            return o, lse

        o_g, lse_g = lax.map(block, jnp.arange(s // BLOCK_Q))
        outs.append(stack_blocks(o_g, b, s))
        lses.append(stack_blocks(lse_g, b, s))
    alpha = jax.nn.softmax(jnp.stack(lses, axis=2), axis=2)
    o = jnp.stack(outs, axis=2) * alpha[..., None].astype(h.dtype)
    return o.reshape(b, s, nq) @ w_o


def mixer_c(h, w_qkv, sinks, w_o):
    b, s, _ = h.shape
    rep = C_HEADS // C_KV_HEADS
    span = BLOCK_Q + 2 * C_WINDOW
    qkv = h @ w_qkv
    q, k, v = jnp.split(qkv, [C_HEADS * HEAD_DIM, (C_HEADS + C_KV_HEADS) * HEAD_DIM], axis=-1)
    q = q.reshape(b, s, C_KV_HEADS, rep, HEAD_DIM) * (HEAD_DIM ** -0.5)
    pad = ((0, 0), (C_WINDOW, C_WINDOW), (0, 0), (0, 0))
    kp = jnp.pad(k.reshape(b, s, C_KV_HEADS, HEAD_DIM), pad)
    vp = jnp.pad(v.reshape(b, s, C_KV_HEADS, HEAD_DIM), pad)
    slopes = alibi_slopes(C_HEADS).reshape(C_KV_HEADS, rep)[:, :, None, None]
    sink = sinks.astype(jnp.float32).reshape(1, C_KV_HEADS, rep, 1, 1)

    def block(i):
        start = i * BLOCK_Q
        qb = lax.dynamic_slice_in_dim(q, start, BLOCK_Q, axis=1)
        kb = lax.dynamic_slice_in_dim(kp, start, span, axis=1)
        vb = lax.dynamic_slice_in_dim(vp, start, span, axis=1)
        tq = start + jnp.arange(BLOCK_Q)
        tk = start - C_WINDOW + jnp.arange(span)
        dist = jnp.abs(tk[None, :] - tq[:, None])
        valid = (dist <= C_WINDOW) & (tk[None, :] >= 0) & (tk[None, :] < s)
        sc = jnp.einsum('bqhgd,bkhd->bhgqk', qb, kb).astype(jnp.float32)
        sc = jnp.where(valid, sc - slopes * dist.astype(jnp.float32), -jnp.inf)
        logits = jnp.concatenate([sc, jnp.broadcast_to(sink, sc.shape[:-1] + (1,))], axis=-1)
        p = jax.nn.softmax(logits, axis=-1)[..., :-1].astype(vb.dtype)
        o = jnp.einsum('bhgqk,bkhd->bqhgd', p, vb)
        return o.reshape(b, BLOCK_Q, C_HEADS * HEAD_DIM)

    o = stack_blocks(lax.map(block, jnp.arange(s // BLOCK_Q)), b, s)
    return o @ w_o


def squared_relu_mlp(h, w1, w2):
    u = jax.nn.relu(h @ w1)
    return (u * u) @ w2


def setup_inputs(seed: int = 0) -> dict:
    key = jax.random.key(seed)
    ks = iter(jax.random.split(key, 32))
    kinds = [i % N_MIXERS for i in range(DEPTH)]
    n_a, n_b, n_c = kinds.count(0), kinds.count(1), kinds.count(2)

    def dense(k, shape):
        return jax.random.normal(k, shape, jnp.float32) * (shape[-2] ** -0.5)

    def gain(k, shape):
        return 1.0 + 0.05 * jax.random.normal(k, shape, jnp.float32)

    a_cols = (A_HEADS + 2 * A_KV_HEADS) * HEAD_DIM
    b_q = len(B_GROUPS) * B_HEADS_PER_GROUP * HEAD_DIM
    b_cols = b_q + 2 * len(B_GROUPS) * B_KV_PER_GROUP * HEAD_DIM
    c_cols = (C_HEADS + 2 * C_KV_HEADS) * HEAD_DIM
    return {
        "x": jax.random.normal(next(ks), (BATCH, SEQ, D_MODEL), jnp.float32),
        "attn_norm": gain(next(ks), (DEPTH, D_MODEL)),
        "mlp_norm": gain(next(ks), (DEPTH, D_MODEL)),
        "a_w_qkv": dense(next(ks), (n_a, D_MODEL, a_cols)),
        "a_q_gain": gain(next(ks), (n_a, HEAD_DIM)),
        "a_k_gain": gain(next(ks), (n_a, HEAD_DIM)),
        "a_w_o": dense(next(ks), (n_a, A_HEADS * HEAD_DIM, D_MODEL)),
        "b_w_qkv": dense(next(ks), (n_b, D_MODEL, b_cols)),
        "b_w_o": dense(next(ks), (n_b, b_q, D_MODEL)),
        "c_w_qkv": dense(next(ks), (n_c, D_MODEL, c_cols)),
        "c_sinks": 0.5 * jax.random.normal(next(ks), (n_c, C_HEADS), jnp.float32),
        "c_w_o": dense(next(ks), (n_c, C_HEADS * HEAD_DIM, D_MODEL)),
        "mlp_w1": dense(next(ks), (DEPTH, D_MODEL, D_FF)),
        "mlp_w2": dense(next(ks), (DEPTH, D_FF, D_MODEL)),
        "final_norm": gain(next(ks), (D_MODEL,)),
    }


def reference(x, attn_norm, mlp_norm, a_w_qkv, a_q_gain, a_k_gain, a_w_o, b_w_qkv, b_w_o,
              c_w_qkv, c_sinks, c_w_o, mlp_w1, mlp_w2, final_norm):
    h = x
    used = [0, 0, 0]
    for layer in range(DEPTH):
        kind = layer % N_MIXERS
        j = used[kind]
        used[kind] += 1
        hn = rmsnorm(h, attn_norm[layer])
        if kind == 0:
            mix = mixer_a(hn, a_w_qkv[j], a_q_gain[j], a_k_gain[j], a_w_o[j])
        elif kind == 1:
            mix = mixer_b(hn, b_w_qkv[j], b_w_o[j])
        else:
            mix = mixer_c(hn, c_w_qkv[j], c_sinks[j], c_w_o[j])
        h = h + mix
        h = h + squared_relu_mlp(rmsnorm(h, mlp_norm[layer]), mlp_w1[layer], mlp_w2[layer])
    return rmsnorm(h, final_norm)
```

```python
import functools
import math

import jax
import jax.numpy as jnp
from jax import lax
from jax.experimental import pallas as pl
from jax.experimental.pallas import tpu as pltpu

HEAD_DIM = 64
HALF_HEAD = HEAD_DIM // 2
RMS_EPS = 1e-6
GRID_W = 64
ROPE_THETA = 10000.0
N_MIXERS = 3
A_REP = 4
B_GROUPS = ((128, 1), (512, 4), (2048, 16))
B_HEADS_PER_GROUP = 6
B_KV_PER_GROUP = 2
B_REP = B_HEADS_PER_GROUP // B_KV_PER_GROUP
C_WINDOW = 128
LANES = 128
BLOCK_Q = 128
V7X_VMEM_LIMIT_BYTES = 48 * 1024 * 1024
NEG_BIG = -1e30

BF16 = jnp.bfloat16
F32 = jnp.float32


def _params(*sem):
    return pltpu.CompilerParams(dimension_semantics=sem, vmem_limit_bytes=V7X_VMEM_LIMIT_BYTES)


def _nt_dot(a, b):
    return lax.dot_general(a, b, (((1,), (1,)), ((), ())), preferred_element_type=F32)


def _rms_rows(x, gain):
    ms = jnp.mean(x * x, axis=-1, keepdims=True)
    return x * lax.rsqrt(ms + RMS_EPS) * gain


def _qkv_body(x_ref, g_ref, w_ref, o_ref, *, q_cols, q_scale):
    hn = _rms_rows(x_ref[...], g_ref[...]).astype(BF16)
    y = jnp.dot(hn, w_ref[...], preferred_element_type=F32)
    o_ref[:, :q_cols] = (y[:, :q_cols] * q_scale).astype(o_ref.dtype)
    o_ref[:, q_cols:] = y[:, q_cols:].astype(o_ref.dtype)


def _qkv_rope_body(x_ref, g_ref, w_ref, qg_ref, kg_ref, cos_ref, sin_ref, seg_ref, o_ref,
                   *, q_cols, k_cols, q_scale):
    hn = _rms_rows(x_ref[...], g_ref[...]).astype(BF16)
    y = jnp.dot(hn, w_ref[...], preferred_element_type=F32)
    slab = 2 * LANES
    cos = jnp.concatenate([cos_ref[...], cos_ref[...]], axis=1)
    sin = jnp.concatenate([sin_ref[...], sin_ref[...]], axis=1)
    lane = lax.broadcasted_iota(jnp.int32, (x_ref.shape[0], slab), 1)
    first_half = (lane % HALF_HEAD) < (HALF_HEAD // 2)
    seg = seg_ref[...]
    for s in range((q_cols + k_cols) // slab):
        is_q = s * slab < q_cols
        ys = y[:, s * slab:(s + 1) * slab]
        sq = ys * ys
        hi = sq.astype(BF16)
        lo = (sq - hi.astype(F32)).astype(BF16)
        ss = jnp.dot(hi, seg, preferred_element_type=F32) + jnp.dot(lo, seg, preferred_element_type=F32)
        gain = qg_ref[...] if is_q else kg_ref[...]
        yn = ys * lax.rsqrt(ss * (1.0 / HEAD_DIM) + RMS_EPS) * gain
        partner = jnp.where(first_half,
                            pltpu.roll(yn, slab - HALF_HEAD // 2, axis=1),
                            pltpu.roll(yn, HALF_HEAD // 2, axis=1))
        r = yn * cos + partner * sin
        if is_q:
            r = r * q_scale
        o_ref[:, s * slab:(s + 1) * slab] = r.astype(o_ref.dtype)
    o_ref[:, q_cols + k_cols:] = y[:, q_cols + k_cols:].astype(o_ref.dtype)


def _qkv_proj(h2, gain, w, *, q_cols, q_scale, tm=512, rope=None):
    t, d = h2.shape
    n = w.shape[1]
    common = dict(
        grid=(t // tm,),
        out_specs=pl.BlockSpec((tm, n), lambda i: (i, 0)),
        out_shape=jax.ShapeDtypeStruct((t, n), BF16),
        compiler_params=_params("parallel"),
    )
    x_spec = pl.BlockSpec((tm, d), lambda i: (i, 0))
    g_spec = pl.BlockSpec((1, d), lambda i: (0, 0))
    w_spec = pl.BlockSpec((d, n), lambda i: (0, 0))
    if rope is None:
        return pl.pallas_call(
            functools.partial(_qkv_body, q_cols=q_cols, q_scale=q_scale),
            in_specs=[x_spec, g_spec, w_spec], name="qkv_proj", **common,
        )(h2, gain, w)
    qg, kg, cos, sin, seg, k_cols = rope
    seq_tiles = cos.shape[0] // tm
    slab = 2 * LANES
    return pl.pallas_call(
        functools.partial(_qkv_rope_body, q_cols=q_cols, k_cols=k_cols, q_scale=q_scale),
        in_specs=[x_spec, g_spec, w_spec,
                  pl.BlockSpec((1, slab), lambda i: (0, 0)),
                  pl.BlockSpec((1, slab), lambda i: (0, 0)),
                  pl.BlockSpec((tm, LANES), lambda i: (i % seq_tiles, 0)),
                  pl.BlockSpec((tm, LANES), lambda i: (i % seq_tiles, 0)),
                  pl.BlockSpec((slab, slab), lambda i: (0, 0))],
        name="qkv_proj_rope", **common,
    )(h2, gain, w, qg, kg, cos, sin, seg)


def _rope_tables(seq):
    t = jnp.arange(seq)
    row = (t // GRID_W).astype(F32)
    col = (t % GRID_W).astype(F32)
    inv_freq = ROPE_THETA ** (-jnp.arange(0, HALF_HEAD, 2, dtype=F32) / HALF_HEAD)
    ang_row = row[:, None] * inv_freq
    ang_col = col[:, None] * inv_freq
    cos = jnp.concatenate([jnp.cos(ang_row)] * 2 + [jnp.cos(ang_col)] * 2, axis=1)
    sin = jnp.concatenate([-jnp.sin(ang_row), jnp.sin(ang_row), -jnp.sin(ang_col), jnp.sin(ang_col)], axis=1)
    return jnp.concatenate([cos, cos], axis=1), jnp.concatenate([sin, sin], axis=1)


def _attn_a_body(q_ref, k_ref, v_ref, o_ref, klo, khi, vlo, vhi, *, tq, tk, seq):
    def fill(head_in_high_lanes):
        low = lax.broadcasted_iota(jnp.int32, (seq, LANES), 1) < HEAD_DIM
        zero = jnp.zeros((seq, LANES), BF16)
        for src, lo_ref, hi_ref in ((k_ref, klo, khi), (v_ref, vlo, vhi)):
            x = src[0]
            swapped = jnp.concatenate([x[:, HEAD_DIM:], x[:, :HEAD_DIM]], axis=1)
            in_low, in_high = (swapped, x) if head_in_high_lanes else (x, swapped)
            lo_ref[...] = jnp.where(low, in_low, zero)
            hi_ref[...] = jnp.where(low, zero, in_high)

    first = pl.program_id(2) == 0
    odd = (pl.program_id(1) % 2) == 1
    pl.when(jnp.logical_and(first, jnp.logical_not(odd)))(lambda: fill(False))
    pl.when(jnp.logical_and(first, odd))(lambda: fill(True))

    q = q_ref[0]
    q2 = jnp.concatenate([q[:, :LANES], q[:, LANES:]], axis=0)
    rows = 2 * tq
    low = lax.broadcasted_iota(jnp.int32, (rows, LANES), 1) < HEAD_DIM

    def step(c, carry):
        m_e, l_e, m_o, l_o, acc = carry
        ks = pl.multiple_of(c * tk, tk)
        s_e = _nt_dot(q2, klo[pl.ds(ks, tk), :])
        s_o = _nt_dot(q2, khi[pl.ds(ks, tk), :])
        mn_e = jnp.maximum(m_e, jnp.max(s_e, axis=-1, keepdims=True))
        mn_o = jnp.maximum(m_o, jnp.max(s_o, axis=-1, keepdims=True))
        p_e = jnp.exp(s_e - mn_e)
        p_o = jnp.exp(s_o - mn_o)
        a_e = jnp.exp(m_e - mn_e)
        a_o = jnp.exp(m_o - mn_o)
        l_e = a_e * l_e + jnp.sum(p_e, axis=-1, keepdims=True)
        l_o = a_o * l_o + jnp.sum(p_o, axis=-1, keepdims=True)
        pv = (jnp.dot(p_e.astype(BF16), vlo[pl.ds(ks, tk), :], preferred_element_type=F32)
              + jnp.dot(p_o.astype(BF16), vhi[pl.ds(ks, tk), :], preferred_element_type=F32))
        acc = acc * jnp.where(low, a_e, a_o) + pv
        return mn_e, l_e, mn_o, l_o, acc

    neg = jnp.full((rows, 1), -jnp.inf, F32)
    zero1 = jnp.zeros((rows, 1), F32)
    _, l_e, _, l_o, acc = lax.fori_loop(
        0, seq // tk, step, (neg, zero1, neg, zero1, jnp.zeros((rows, LANES), F32)))
    out = acc * jnp.where(low, 1.0 / l_e, 1.0 / l_o)
    o_ref[0] = jnp.concatenate([out[:tq], out[tq:]], axis=1).astype(o_ref.dtype)


def _attn_a(qkv, *, n_heads, tq=256, tk=512):
    b, seq, _ = qkv.shape
    kv_heads = n_heads // A_REP
    q_cols = n_heads * HEAD_DIM
    q_w = A_REP * HEAD_DIM
    k_blk0 = q_cols // LANES
    v_blk0 = (q_cols + kv_heads * HEAD_DIM) // LANES
    return pl.pallas_call(
        functools.partial(_attn_a_body, tq=tq, tk=tk, seq=seq),
        grid=(b, kv_heads, seq // tq),
        in_specs=[pl.BlockSpec((1, tq, q_w), lambda bi, h, i: (bi, i, h)),
                  pl.BlockSpec((1, seq, LANES), lambda bi, h, i: (bi, 0, k_blk0 + h // 2)),
                  pl.BlockSpec((1, seq, LANES), lambda bi, h, i: (bi, 0, v_blk0 + h // 2))],
        out_specs=pl.BlockSpec((1, tq, q_w), lambda bi, h, i: (bi, i, h)),
        out_shape=jax.ShapeDtypeStruct((b, seq, q_cols), BF16),
        scratch_shapes=[pltpu.VMEM((seq, LANES), BF16)] * 4,
        compiler_params=_params("parallel", "arbitrary", "arbitrary"),
        name="attn_a",
    )(qkv, qkv, qkv)


def _band_geometry(i, n_tiles, half_window):
    r = lax.broadcasted_iota(jnp.int32, (BLOCK_Q, 3 * BLOCK_Q), 0)
    c = lax.broadcasted_iota(jnp.int32, (BLOCK_Q, 3 * BLOCK_Q), 1)
    dist = jnp.abs(c - BLOCK_Q - r)
    c_lo = jnp.where(i > 0, 0, BLOCK_Q)
    c_hi = jnp.where(i < n_tiles - 1, 3 * BLOCK_Q, 2 * BLOCK_Q)
    valid = (dist <= half_window) & (c >= c_lo) & (c < c_hi)
    return dist.astype(F32), valid


def _attn_c_body(sink_ref, q_ref, kp_ref, kc_ref, kn_ref, vp_ref, vc_ref, vn_ref, o_ref, *, n_heads, slopes):
    i = pl.program_id(1)
    dist, valid = _band_geometry(i, pl.num_programs(1), C_WINDOW)
    q = q_ref[0]
    k = jnp.concatenate([kp_ref[0], kc_ref[0], kn_ref[0]], axis=0)
    v = jnp.concatenate([vp_ref[0], vc_ref[0], vn_ref[0]], axis=0)
    outs = []
    for h in range(n_heads):
        kv = h // A_REP
        s = _nt_dot(q[:, h * HEAD_DIM:(h + 1) * HEAD_DIM], k[:, kv * HEAD_DIM:(kv + 1) * HEAD_DIM])
        s = jnp.where(valid, s - slopes[h] * dist, NEG_BIG)
        sink = sink_ref[h]
        m = jnp.maximum(jnp.max(s, axis=-1, keepdims=True), sink)
        p = jnp.exp(s - m)
        den = jnp.sum(p, axis=-1, keepdims=True) + jnp.exp(sink - m)
        o = jnp.dot(p.astype(BF16), v[:, kv * HEAD_DIM:(kv + 1) * HEAD_DIM], preferred_element_type=F32)
        outs.append(o / den)
    o_ref[0] = jnp.concatenate(outs, axis=1).astype(o_ref.dtype)


def _alibi_slopes(n):
    return [2.0 ** (-8.0 * (i + 1) / n) for i in range(n)]


def _attn_c(qkv, sinks, *, n_heads):
    b, seq, _ = qkv.shape
    kv_heads = n_heads // A_REP
    q_cols = n_heads * HEAD_DIM
    kv_cols = kv_heads * HEAD_DIM
    n_tiles = seq // BLOCK_Q
    k_blk = q_cols // kv_cols
    v_blk = k_blk + 1

    def kv_spec(col_blk, shift):
        return pl.BlockSpec(
            (1, BLOCK_Q, kv_cols),
            lambda bi, i: (bi, jnp.clip(i + shift, 0, n_tiles - 1), col_blk))

    return pl.pallas_call(
        functools.partial(_attn_c_body, n_heads=n_heads, slopes=_alibi_slopes(n_heads)),
        grid=(b, n_tiles),
        in_specs=[pl.BlockSpec(memory_space=pltpu.SMEM),
                  pl.BlockSpec((1, BLOCK_Q, q_cols), lambda bi, i: (bi, i, 0)),
                  kv_spec(k_blk, -1), kv_spec(k_blk, 0), kv_spec(k_blk, 1),
                  kv_spec(v_blk, -1), kv_spec(v_blk, 0), kv_spec(v_blk, 1)],
        out_specs=pl.BlockSpec((1, BLOCK_Q, q_cols), lambda bi, i: (bi, i, 0)),
        out_shape=jax.ShapeDtypeStruct((b, seq, q_cols), BF16),
        compiler_params=_params("parallel", "arbitrary"),
        name="attn_c",
    )(sinks, qkv, qkv, qkv, qkv, qkv, qkv, qkv)


def _attn_b_body(q_ref, kp_ref, kc_ref, kn_ref, vp_ref, vc_ref, vn_ref, o_ref, lse_ref, *, slopes, dil):
    i = pl.program_id(2)
    dist, valid = _band_geometry(i, pl.num_programs(2), BLOCK_Q // 2)
    dist = dist * float(dil)
    q = q_ref[0]
    k = jnp.concatenate([kp_ref[0], kc_ref[0], kn_ref[0]], axis=0)
    v = jnp.concatenate([vp_ref[0], vc_ref[0], vn_ref[0]], axis=0)
    outs, lses = [], []
    for h in range(B_HEADS_PER_GROUP):
        kv = h // B_REP
        s = _nt_dot(q[:, h * HEAD_DIM:(h + 1) * HEAD_DIM], k[:, kv * HEAD_DIM:(kv + 1) * HEAD_DIM])
        s = jnp.where(valid, s - slopes[h] * dist, NEG_BIG)
        m = jnp.max(s, axis=-1, keepdims=True)
        p = jnp.exp(s - m)
        den = jnp.sum(p, axis=-1, keepdims=True)
        o = jnp.dot(p.astype(BF16), v[:, kv * HEAD_DIM:(kv + 1) * HEAD_DIM], preferred_element_type=F32)
        outs.append(o / den)
        lses.append(jnp.broadcast_to(m + jnp.log(den), (BLOCK_Q, HEAD_DIM)))
    o_ref[0] = jnp.concatenate(outs, axis=1).astype(o_ref.dtype)
    lse_ref[0] = jnp.concatenate(lses, axis=1)


def _attn_b(qkv):
    b, seq, n_cols = qkv.shape
    n_g = len(B_GROUPS)
    gq = B_HEADS_PER_GROUP * HEAD_DIM
    gk = B_KV_PER_GROUP * HEAD_DIM
    q_cols = n_g * gq
    all_slopes = _alibi_slopes(n_g * B_HEADS_PER_GROUP)
    outs, lses = [], []
    for g, (window, dil) in enumerate(B_GROUPS):
        assert (window // 2) // dil == BLOCK_Q // 2
        sub = seq // dil
        n_tiles = sub // BLOCK_Q
        view = qkv.reshape(b, sub, dil * n_cols)

        def q_map(bi, r, i, g=g):
            return (bi, i, r * (n_cols // gq) + g)

        def kv_spec(base, shift, g=g, n_tiles=n_tiles):
            return pl.BlockSpec(
                (1, BLOCK_Q, gk),
                lambda bi, r, i: (bi, jnp.clip(i + shift, 0, n_tiles - 1), r * (n_cols // gk) + base + g))

        k0 = q_cols // gk
        v0 = k0 + n_g
        out_map = lambda bi, r, i: (bi, i, r)
        o, lse = pl.pallas_call(
            functools.partial(_attn_b_body, dil=dil,
                              slopes=all_slopes[g * B_HEADS_PER_GROUP:(g + 1) * B_HEADS_PER_GROUP]),
            grid=(b, dil, n_tiles),
            in_specs=[pl.BlockSpec((1, BLOCK_Q, gq), q_map),
                      kv_spec(k0, -1), kv_spec(k0, 0), kv_spec(k0, 1),
                      kv_spec(v0, -1), kv_spec(v0, 0), kv_spec(v0, 1)],
            out_specs=[pl.BlockSpec((1, BLOCK_Q, gq), out_map), pl.BlockSpec((1, BLOCK_Q, gq), out_map)],
            out_shape=[jax.ShapeDtypeStruct((b, sub, dil * gq), BF16),
                       jax.ShapeDtypeStruct((b, sub, dil * gq), F32)],
            compiler_params=_params("parallel", "arbitrary", "arbitrary"),
            name=f"attn_b_g{g}",
        )(view, view, view, view, view, view, view)
        outs.append(o.reshape(b * seq, gq))
        lses.append(lse.reshape(b * seq, gq))
    return outs, lses


def _oproj_body(h_ref, o_ref, w_ref, out_ref):
    out_ref[...] = h_ref[...] + jnp.dot(o_ref[...], w_ref[...], preferred_element_type=F32)


def _oproj_b_body(h_ref, o0, o1, o2, l0, l1, l2, w_ref, out_ref):
    ls = [l0[...], l1[...], l2[...]]
    mx = jnp.maximum(jnp.maximum(ls[0], ls[1]), ls[2])
    es = [jnp.exp(l - mx) for l in ls]
    inv = 1.0 / (es[0] + es[1] + es[2])
    acc = h_ref[...]
    gq = o0.shape[1]
    for g, o in enumerate((o0, o1, o2)):
        og = (o[...].astype(F32) * (es[g] * inv)).astype(BF16)
        acc = acc + jnp.dot(og, w_ref[g * gq:(g + 1) * gq, :], preferred_element_type=F32)
    out_ref[...] = acc


def _oproj(h2, o, w, *, tm=512):
    t, d = h2.shape
    kdim = w.shape[0]
    return pl.pallas_call(
        _oproj_body,
        grid=(t // tm,),
        in_specs=[pl.BlockSpec((tm, d), lambda i: (i, 0)),
                  pl.BlockSpec((tm, kdim), lambda i: (i, 0)),
                  pl.BlockSpec((kdim, d), lambda i: (0, 0))],
        out_specs=pl.BlockSpec((tm, d), lambda i: (i, 0)),
        out_shape=jax.ShapeDtypeStruct((t, d), F32),
        compiler_params=_params("parallel"),
        name="oproj",
    )(h2, o, w)


def _oproj_b(h2, outs, lses, w, *, tm=512):
    t, d = h2.shape
    kdim = w.shape[0]
    gq = outs[0].shape[1]
    tile = lambda cols: pl.BlockSpec((tm, cols), lambda i: (i, 0))
    return pl.pallas_call(
        _oproj_b_body,
        grid=(t // tm,),
        in_specs=[tile(d)] + [tile(gq)] * 6 + [pl.BlockSpec((kdim, d), lambda i: (0, 0))],
        out_specs=tile(d),
        out_shape=jax.ShapeDtypeStruct((t, d), F32),
        compiler_params=_params("parallel"),
        name="oproj_b",
    )(h2, *outs, *lses, w)


def _mlp_body(h_ref, g_ref, w1_ref, w2_ref, fg_ref, out_ref, hn_ref, acc_ref, *, final_norm):
    j = pl.program_id(1)

    @pl.when(j == 0)
    def _():
        hn_ref[...] = _rms_rows(h_ref[...], g_ref[...]).astype(BF16)
        acc_ref[...] = jnp.zeros_like(acc_ref)

    u = jnp.maximum(jnp.dot(hn_ref[...], w1_ref[...], preferred_element_type=F32), 0.0)
    acc_ref[...] += jnp.dot((u * u).astype(BF16), w2_ref[...], preferred_element_type=F32)

    @pl.when(j == pl.num_programs(1) - 1)
    def _():
        r = h_ref[...] + acc_ref[...]
        out_ref[...] = _rms_rows(r, fg_ref[...]) if final_norm else r


def _mlp(h2, gain, w1, w2, final_gain, *, final_norm, tm=1024, tf=512):
    t, d = h2.shape
    ff = w1.shape[1]
    return pl.pallas_call(
        functools.partial(_mlp_body, final_norm=final_norm),
        grid=(t // tm, ff // tf),
        in_specs=[pl.BlockSpec((tm, d), lambda i, j: (i, 0)),
                  pl.BlockSpec((1, d), lambda i, j: (0, 0)),
                  pl.BlockSpec((d, tf), lambda i, j: (0, j)),
                  pl.BlockSpec((tf, d), lambda i, j: (j, 0)),
                  pl.BlockSpec((1, d), lambda i, j: (0, 0))],
        out_specs=pl.BlockSpec((tm, d), lambda i, j: (i, 0)),
        out_shape=jax.ShapeDtypeStruct((t, d), F32),
        scratch_shapes=[pltpu.VMEM((tm, d), BF16), pltpu.VMEM((tm, d), F32)],
        compiler_params=_params("parallel", "arbitrary"),
        name="mlp",
    )(h2, gain, w1, w2, final_gain)


def kernel(x, attn_norm, mlp_norm, a_w_qkv, a_q_gain, a_k_gain, a_w_o, b_w_qkv, b_w_o,
           c_w_qkv, c_sinks, c_w_o, mlp_w1, mlp_w2, final_norm):
    b, seq, d = x.shape
    depth = attn_norm.shape[0]
    h = x.reshape(b * seq, d)
    scale = HEAD_DIM ** -0.5
    a_heads = a_w_o.shape[1] // HEAD_DIM
    c_heads = c_w_o.shape[1] // HEAD_DIM
    cos, sin = _rope_tables(seq)
    head = jnp.arange(2 * LANES) // HEAD_DIM
    seg = (head[:, None] == head[None, :]).astype(BF16)
    tile4 = lambda g: jnp.tile(g, 2 * LANES // HEAD_DIM)[None, :]
    used = [0, 0, 0]
    for layer in range(depth):
        kind = layer % N_MIXERS
        j = used[kind]
        used[kind] += 1
        gain = attn_norm[layer][None, :]
        if kind == 0:
            q_cols = a_heads * HEAD_DIM
            k_cols = (a_w_qkv.shape[2] - q_cols) // 2
            qkv = _qkv_proj(h, gain, a_w_qkv[j].astype(BF16), q_cols=q_cols, q_scale=scale,
                            rope=(tile4(a_q_gain[j]), tile4(a_k_gain[j]), cos, sin, seg, k_cols))
            o = _attn_a(qkv.reshape(b, seq, -1), n_heads=a_heads).reshape(b * seq, q_cols)
            h = _oproj(h, o, a_w_o[j].astype(BF16))
        elif kind == 1:
            q_cols = len(B_GROUPS) * B_HEADS_PER_GROUP * HEAD_DIM
            qkv = _qkv_proj(h, gain, b_w_qkv[j].astype(BF16), q_cols=q_cols, q_scale=scale)
            outs, lses = _attn_b(qkv.reshape(b, seq, -1))
            h = _oproj_b(h, outs, lses, b_w_o[j].astype(BF16))
        else:
            q_cols = c_heads * HEAD_DIM
            qkv = _qkv_proj(h, gain, c_w_qkv[j].astype(BF16), q_cols=q_cols, q_scale=scale)
            o = _attn_c(qkv.reshape(b, seq, -1), c_sinks[j], n_heads=c_heads).reshape(b * seq, q_cols)
            h = _oproj(h, o, c_w_o[j].astype(BF16))
        h = _mlp(h, mlp_norm[layer][None, :], mlp_w1[layer].astype(BF16), mlp_w2[layer].astype(BF16),
                 final_norm[None, :], final_norm=(layer == depth - 1))
    return h.reshape(b, seq, d)
```

```python
import functools
import math

import jax
import jax.numpy as jnp
from jax import lax
from jax.experimental import pallas as pl
from jax.experimental.pallas import tpu as pltpu

HEAD_DIM = 64
HALF_HEAD = HEAD_DIM // 2
RMS_EPS = 1e-6
GRID_W = 64
ROPE_THETA = 10000.0
N_MIXERS = 3
A_REP = 4
B_GROUPS = ((128, 1), (512, 4), (2048, 16))
B_HEADS_PER_GROUP = 6
B_KV_PER_GROUP = 2
B_REP = B_HEADS_PER_GROUP // B_KV_PER_GROUP
C_WINDOW = 128
LANES = 128
BF16_SUBLANES = 16
BLOCK_Q = 128
VT_ROWS = HEAD_DIM + BF16_SUBLANES
V7X_VMEM_LIMIT_BYTES = 48 * 1024 * 1024
NEG_BIG = -1e30
LOG2E = math.log2(math.e)

BF16 = jnp.bfloat16
F32 = jnp.float32


def _params(*sem):
    return pltpu.CompilerParams(dimension_semantics=sem, vmem_limit_bytes=V7X_VMEM_LIMIT_BYTES)


def _nt_dot(a, b):
    return lax.dot_general(a, b, (((1,), (1,)), ((), ())), preferred_element_type=F32)


def _rms_rows(x, gain):
    ms = jnp.mean(x * x, axis=-1, keepdims=True)
    return x * lax.rsqrt(ms + RMS_EPS) * gain


def _qkv_body(x_ref, g_ref, w_ref, o_ref, *, q_cols, q_scale):
    hn = _rms_rows(x_ref[...], g_ref[...]).astype(BF16)
    y = jnp.dot(hn, w_ref[...], preferred_element_type=F32)
    o_ref[:, :q_cols] = (y[:, :q_cols] * q_scale).astype(o_ref.dtype)
    o_ref[:, q_cols:] = y[:, q_cols:].astype(o_ref.dtype)


def _qkv_rope_body(x_ref, g_ref, w_ref, qg_ref, kg_ref, cos_ref, sin_ref, seg_ref, o_ref,
                   *, q_cols, k_cols, q_scale):
    hn = _rms_rows(x_ref[...], g_ref[...]).astype(BF16)
    y = jnp.dot(hn, w_ref[...], preferred_element_type=F32)
    slab = 2 * LANES
    cos = jnp.concatenate([cos_ref[...], cos_ref[...]], axis=1)
    sin = jnp.concatenate([sin_ref[...], sin_ref[...]], axis=1)
    lane = lax.broadcasted_iota(jnp.int32, (x_ref.shape[0], slab), 1)
    first_half = (lane % HALF_HEAD) < (HALF_HEAD // 2)
    seg = seg_ref[...]
    for s in range((q_cols + k_cols) // slab):
        is_q = s * slab < q_cols
        ys = y[:, s * slab:(s + 1) * slab]
        sq = ys * ys
        hi = sq.astype(BF16)
        lo = (sq - hi.astype(F32)).astype(BF16)
        ss = jnp.dot(hi, seg, preferred_element_type=F32) + jnp.dot(lo, seg, preferred_element_type=F32)
        gain = qg_ref[...] if is_q else kg_ref[...]
        yn = ys * lax.rsqrt(ss * (1.0 / HEAD_DIM) + RMS_EPS) * gain
        partner = jnp.where(first_half,
                            pltpu.roll(yn, slab - HALF_HEAD // 2, axis=1),
                            pltpu.roll(yn, HALF_HEAD // 2, axis=1))
        r = yn * cos + partner * sin
        if is_q:
            r = r * q_scale
        o_ref[:, s * slab:(s + 1) * slab] = r.astype(o_ref.dtype)
    o_ref[:, q_cols + k_cols:] = y[:, q_cols + k_cols:].astype(o_ref.dtype)


def _qkv_proj(h2, gain, w, *, q_cols, q_scale, tm=512, rope=None):
    t, d = h2.shape
    n = w.shape[1]
    common = dict(
        grid=(t // tm,),
        out_specs=pl.BlockSpec((tm, n), lambda i: (i, 0)),
        out_shape=jax.ShapeDtypeStruct((t, n), BF16),
        compiler_params=_params("parallel"),
    )
    x_spec = pl.BlockSpec((tm, d), lambda i: (i, 0))
    g_spec = pl.BlockSpec((1, d), lambda i: (0, 0))
    w_spec = pl.BlockSpec((d, n), lambda i: (0, 0))
    if rope is None:
        return pl.pallas_call(
            functools.partial(_qkv_body, q_cols=q_cols, q_scale=q_scale),
            in_specs=[x_spec, g_spec, w_spec], name="qkv_proj", **common,
        )(h2, gain, w)
    qg, kg, cos, sin, seg, k_cols = rope
    seq_tiles = cos.shape[0] // tm
    slab = 2 * LANES
    return pl.pallas_call(
        functools.partial(_qkv_rope_body, q_cols=q_cols, k_cols=k_cols, q_scale=q_scale),
        in_specs=[x_spec, g_spec, w_spec,
                  pl.BlockSpec((1, slab), lambda i: (0, 0)),
                  pl.BlockSpec((1, slab), lambda i: (0, 0)),
                  pl.BlockSpec((tm, LANES), lambda i: (i % seq_tiles, 0)),
                  pl.BlockSpec((tm, LANES), lambda i: (i % seq_tiles, 0)),
                  pl.BlockSpec((slab, slab), lambda i: (0, 0))],
        name="qkv_proj_rope", **common,
    )(h2, gain, w, qg, kg, cos, sin, seg)


def _rope_tables(seq):
    t = jnp.arange(seq)
    row = (t // GRID_W).astype(F32)
    col = (t % GRID_W).astype(F32)
    inv_freq = ROPE_THETA ** (-jnp.arange(0, HALF_HEAD, 2, dtype=F32) / HALF_HEAD)
    ang_row = row[:, None] * inv_freq
    ang_col = col[:, None] * inv_freq
    cos = jnp.concatenate([jnp.cos(ang_row)] * 2 + [jnp.cos(ang_col)] * 2, axis=1)
    sin = jnp.concatenate([-jnp.sin(ang_row), jnp.sin(ang_row), -jnp.sin(ang_col), jnp.sin(ang_col)], axis=1)
    return jnp.concatenate([cos, cos], axis=1), jnp.concatenate([sin, sin], axis=1)


def _attn_a_body(q_ref, k_ref, v_ref, o_ref, klo, khi, vt1, s_buf, *, tq, tk, seq):
    def fill(head_in_high_lanes):
        low = lax.broadcasted_iota(jnp.int32, (seq, LANES), 1) < HEAD_DIM
        zero = jnp.zeros((seq, LANES), BF16)
        x = k_ref[0]
        swapped = jnp.concatenate([x[:, HEAD_DIM:], x[:, :HEAD_DIM]], axis=1)
        in_low, in_high = (swapped, x) if head_in_high_lanes else (x, swapped)
        klo[...] = jnp.where(low, in_low, zero)
        khi[...] = jnp.where(low, zero, in_high)
        ones_row = (lax.broadcasted_iota(jnp.int32, (VT_ROWS - HEAD_DIM, tk), 0) == 0).astype(F32)
        for c in range(seq // tk):
            xt = v_ref[0, c * tk:(c + 1) * tk, :].astype(F32).T
            vt = xt[HEAD_DIM:] if head_in_high_lanes else xt[:HEAD_DIM]
            vt1[:, c * tk:(c + 1) * tk] = jnp.concatenate([vt, ones_row], axis=0).astype(BF16)

    first = pl.program_id(2) == 0
    odd = (pl.program_id(1) % 2) == 1
    pl.when(jnp.logical_and(first, jnp.logical_not(odd)))(lambda: fill(False))
    pl.when(jnp.logical_and(first, odd))(lambda: fill(True))

    q = q_ref[0]
    q2 = jnp.concatenate([q[:, :LANES], q[:, LANES:]], axis=0)
    cols = 2 * tq

    def scores(c, slot):
        ks = pl.multiple_of(c * tk, tk)
        cms = []
        for par, k_s in enumerate((klo, khi)):
            s = _nt_dot(k_s[pl.ds(ks, tk), :], q2)
            s_buf[slot, par] = s
            cms.append(jnp.max(s, axis=0, keepdims=True))
        return tuple(cms)

    def accumulate(c, slot, cms, state):
        vt = vt1[:, pl.ds(pl.multiple_of(c * tk, tk), tk)]
        new = []
        for par in range(2):
            m, acc = state[2 * par], state[2 * par + 1]
            mn = jnp.maximum(m, cms[par])
            p = jnp.exp2(s_buf[slot, par] - mn)
            new += [mn, acc * jnp.exp2(m - mn) + jnp.dot(vt, p.astype(BF16), preferred_element_type=F32)]
        return tuple(new)

    n_chunks = seq // tk
    neg = jnp.full((1, cols), -jnp.inf, F32)
    acc0 = jnp.zeros((VT_ROWS, cols), F32)
    cms0 = scores(0, 0)

    def pair(j, carry):
        cms_a, state = carry[:2], carry[2:]
        cms_b = scores(2 * j + 1, 1)
        state = accumulate(2 * j, 0, cms_a, state)
        cms_a = scores(2 * j + 2, 0)
        state = accumulate(2 * j + 1, 1, cms_b, state)
        return cms_a + state

    carry = lax.fori_loop(0, n_chunks // 2 - 1, pair, cms0 + (neg, acc0, neg, acc0))
    cms_a, state = carry[:2], carry[2:]
    cms_b = scores(n_chunks - 1, 1)
    state = accumulate(n_chunks - 2, 0, cms_a, state)
    _, acc_e, _, acc_o = accumulate(n_chunks - 1, 1, cms_b, state)
    out = jnp.concatenate([acc[:HEAD_DIM] / acc[HEAD_DIM:HEAD_DIM + 1] for acc in (acc_e, acc_o)], axis=0)
    o_ref[0] = jnp.concatenate([out[:, :tq].T, out[:, tq:].T], axis=1).astype(o_ref.dtype)


def _attn_a(qkv, *, n_heads, tq=512, tk=512):
    b, seq, _ = qkv.shape
    kv_heads = n_heads // A_REP
    q_cols = n_heads * HEAD_DIM
    q_w = A_REP * HEAD_DIM
    k_blk0 = q_cols // LANES
    v_blk0 = (q_cols + kv_heads * HEAD_DIM) // LANES
    assert (seq // tk) % 2 == 0 and seq // tk >= 4
    return pl.pallas_call(
        functools.partial(_attn_a_body, tq=tq, tk=tk, seq=seq),
        grid=(b, kv_heads, seq // tq),
        in_specs=[pl.BlockSpec((1, tq, q_w), lambda bi, h, i: (bi, i, h)),
                  pl.BlockSpec((1, seq, LANES), lambda bi, h, i: (bi, 0, k_blk0 + h // 2)),
                  pl.BlockSpec((1, seq, LANES), lambda bi, h, i: (bi, 0, v_blk0 + h // 2))],
        out_specs=pl.BlockSpec((1, tq, q_w), lambda bi, h, i: (bi, i, h)),
        out_shape=jax.ShapeDtypeStruct((b, seq, q_cols), BF16),
        scratch_shapes=[pltpu.VMEM((seq, LANES), BF16), pltpu.VMEM((seq, LANES), BF16),
                        pltpu.VMEM((VT_ROWS, seq), BF16), pltpu.VMEM((2, 2, tk, 2 * tq), F32)],
        compiler_params=_params("parallel", "arbitrary", "arbitrary"),
        name="attn_a",
    )(qkv, qkv, qkv)


def _band_geometry(i, n_tiles, half_window):
    r = lax.broadcasted_iota(jnp.int32, (BLOCK_Q, 3 * BLOCK_Q), 0)
    c = lax.broadcasted_iota(jnp.int32, (BLOCK_Q, 3 * BLOCK_Q), 1)
    dist = jnp.abs(c - BLOCK_Q - r)
    c_lo = jnp.where(i > 0, 0, BLOCK_Q)
    c_hi = jnp.where(i < n_tiles - 1, 3 * BLOCK_Q, 2 * BLOCK_Q)
    valid = (dist <= half_window) & (c >= c_lo) & (c < c_hi)
    return dist.astype(F32), valid


def _attn_c_body(sink_ref, q_ref, kp_ref, kc_ref, kn_ref, vp_ref, vc_ref, vn_ref, o_ref, *, n_heads, slopes):
    i = pl.program_id(1)
    dist, valid = _band_geometry(i, pl.num_programs(1), C_WINDOW)
    q = q_ref[0]
    k = jnp.concatenate([kp_ref[0], kc_ref[0], kn_ref[0]], axis=0)
    v = jnp.concatenate([vp_ref[0], vc_ref[0], vn_ref[0]], axis=0)
    outs = []
    for h in range(n_heads):
        kv = h // A_REP
        s = _nt_dot(q[:, h * HEAD_DIM:(h + 1) * HEAD_DIM], k[:, kv * HEAD_DIM:(kv + 1) * HEAD_DIM])
        s = jnp.where(valid, s - slopes[h] * dist, NEG_BIG)
        sink = sink_ref[h]
        m = jnp.maximum(jnp.max(s, axis=-1, keepdims=True), sink)
        p = jnp.exp(s - m)
        den = jnp.sum(p, axis=-1, keepdims=True) + jnp.exp(sink - m)
        o = jnp.dot(p.astype(BF16), v[:, kv * HEAD_DIM:(kv + 1) * HEAD_DIM], preferred_element_type=F32)
        outs.append(o / den)
    o_ref[0] = jnp.concatenate(outs, axis=1).astype(o_ref.dtype)


def _alibi_slopes(n):
    return [2.0 ** (-8.0 * (i + 1) / n) for i in range(n)]


def _attn_c(qkv, sinks, *, n_heads):
    b, seq, _ = qkv.shape
    kv_heads = n_heads // A_REP
    q_cols = n_heads * HEAD_DIM
    kv_cols = kv_heads * HEAD_DIM
    n_tiles = seq // BLOCK_Q
    k_blk = q_cols // kv_cols
    v_blk = k_blk + 1

    def kv_spec(col_blk, shift):
        return pl.BlockSpec(
            (1, BLOCK_Q, kv_cols),
            lambda bi, i: (bi, jnp.clip(i + shift, 0, n_tiles - 1), col_blk))

    return pl.pallas_call(
        functools.partial(_attn_c_body, n_heads=n_heads, slopes=_alibi_slopes(n_heads)),
        grid=(b, n_tiles),
        in_specs=[pl.BlockSpec(memory_space=pltpu.SMEM),
                  pl.BlockSpec((1, BLOCK_Q, q_cols), lambda bi, i: (bi, i, 0)),
                  kv_spec(k_blk, -1), kv_spec(k_blk, 0), kv_spec(k_blk, 1),
                  kv_spec(v_blk, -1), kv_spec(v_blk, 0), kv_spec(v_blk, 1)],
        out_specs=pl.BlockSpec((1, BLOCK_Q, q_cols), lambda bi, i: (bi, i, 0)),
        out_shape=jax.ShapeDtypeStruct((b, seq, q_cols), BF16),
        compiler_params=_params("parallel", "arbitrary"),
        name="attn_c",
    )(sinks, qkv, qkv, qkv, qkv, qkv, qkv, qkv)


def _attn_b_body(q_ref, kp_ref, kc_ref, kn_ref, vp_ref, vc_ref, vn_ref, o_ref, lse_ref, *, slopes, dil):
    i = pl.program_id(2)
    dist, valid = _band_geometry(i, pl.num_programs(2), BLOCK_Q // 2)
    dist = dist * float(dil)
    q = q_ref[0]
    k = jnp.concatenate([kp_ref[0], kc_ref[0], kn_ref[0]], axis=0)
    v = jnp.concatenate([vp_ref[0], vc_ref[0], vn_ref[0]], axis=0)
    outs, lses = [], []
    for h in range(B_HEADS_PER_GROUP):
        kv = h // B_REP
        s = _nt_dot(q[:, h * HEAD_DIM:(h + 1) * HEAD_DIM], k[:, kv * HEAD_DIM:(kv + 1) * HEAD_DIM])
        s = jnp.where(valid, s - slopes[h] * dist, NEG_BIG)
        m = jnp.max(s, axis=-1, keepdims=True)
        p = jnp.exp(s - m)
        den = jnp.sum(p, axis=-1, keepdims=True)
        o = jnp.dot(p.astype(BF16), v[:, kv * HEAD_DIM:(kv + 1) * HEAD_DIM], preferred_element_type=F32)
        outs.append(o / den)
        lses.append(jnp.broadcast_to(m + jnp.log(den), (BLOCK_Q, HEAD_DIM)))
    o_ref[0] = jnp.concatenate(outs, axis=1).astype(o_ref.dtype)
    lse_ref[0] = jnp.concatenate(lses, axis=1)


def _attn_b(qkv):
    b, seq, n_cols = qkv.shape
    n_g = len(B_GROUPS)
    gq = B_HEADS_PER_GROUP * HEAD_DIM
    gk = B_KV_PER_GROUP * HEAD_DIM
    q_cols = n_g * gq
    all_slopes = _alibi_slopes(n_g * B_HEADS_PER_GROUP)
    outs, lses = [], []
    for g, (window, dil) in enumerate(B_GROUPS):
        assert (window // 2) // dil == BLOCK_Q // 2
        sub = seq // dil
        n_tiles = sub // BLOCK_Q
        view = qkv.reshape(b, sub, dil * n_cols)

        def q_map(bi, r, i, g=g):
            return (bi, i, r * (n_cols // gq) + g)

        def kv_spec(base, shift, g=g, n_tiles=n_tiles):
            return pl.BlockSpec(
                (1, BLOCK_Q, gk),
                lambda bi, r, i: (bi, jnp.clip(i + shift, 0, n_tiles - 1), r * (n_cols // gk) + base + g))

        k0 = q_cols // gk
        v0 = k0 + n_g
        out_map = lambda bi, r, i: (bi, i, r)
        o, lse = pl.pallas_call(
            functools.partial(_attn_b_body, dil=dil,
                              slopes=all_slopes[g * B_HEADS_PER_GROUP:(g + 1) * B_HEADS_PER_GROUP]),
            grid=(b, dil, n_tiles),
            in_specs=[pl.BlockSpec((1, BLOCK_Q, gq), q_map),
                      kv_spec(k0, -1), kv_spec(k0, 0), kv_spec(k0, 1),
                      kv_spec(v0, -1), kv_spec(v0, 0), kv_spec(v0, 1)],
            out_specs=[pl.BlockSpec((1, BLOCK_Q, gq), out_map), pl.BlockSpec((1, BLOCK_Q, gq), out_map)],
            out_shape=[jax.ShapeDtypeStruct((b, sub, dil * gq), BF16),
                       jax.ShapeDtypeStruct((b, sub, dil * gq), F32)],
            compiler_params=_params("parallel", "arbitrary", "arbitrary"),
            name=f"attn_b_g{g}",
        )(view, view, view, view, view, view, view)
        outs.append(o.reshape(b * seq, gq))
        lses.append(lse.reshape(b * seq, gq))
    return outs, lses


def _oproj_body(h_ref, o_ref, w_ref, out_ref):
    out_ref[...] = h_ref[...] + jnp.dot(o_ref[...], w_ref[...], preferred_element_type=F32)


def _oproj_b_body(h_ref, o0, o1, o2, l0, l1, l2, w_ref, out_ref):
    ls = [l0[...], l1[...], l2[...]]
    mx = jnp.maximum(jnp.maximum(ls[0], ls[1]), ls[2])
    es = [jnp.exp(l - mx) for l in ls]
    inv = 1.0 / (es[0] + es[1] + es[2])
    acc = h_ref[...]
    gq = o0.shape[1]
    for g, o in enumerate((o0, o1, o2)):
        og = (o[...].astype(F32) * (es[g] * inv)).astype(BF16)
        acc = acc + jnp.dot(og, w_ref[g * gq:(g + 1) * gq, :], preferred_element_type=F32)
    out_ref[...] = acc


def _oproj(h2, o, w, *, tm=512):
    t, d = h2.shape
    kdim = w.shape[0]
    return pl.pallas_call(
        _oproj_body,
        grid=(t // tm,),
        in_specs=[pl.BlockSpec((tm, d), lambda i: (i, 0)),
                  pl.BlockSpec((tm, kdim), lambda i: (i, 0)),
                  pl.BlockSpec((kdim, d), lambda i: (0, 0))],
        out_specs=pl.BlockSpec((tm, d), lambda i: (i, 0)),
        out_shape=jax.ShapeDtypeStruct((t, d), F32),
        compiler_params=_params("parallel"),
        name="oproj",
    )(h2, o, w)


def _oproj_b(h2, outs, lses, w, *, tm=512):
    t, d = h2.shape
    kdim = w.shape[0]
    gq = outs[0].shape[1]
    tile = lambda cols: pl.BlockSpec((tm, cols), lambda i: (i, 0))
    return pl.pallas_call(
        _oproj_b_body,
        grid=(t // tm,),
        in_specs=[tile(d)] + [tile(gq)] * 6 + [pl.BlockSpec((kdim, d), lambda i: (0, 0))],
        out_specs=tile(d),
        out_shape=jax.ShapeDtypeStruct((t, d), F32),
        compiler_params=_params("parallel"),
        name="oproj_b",
    )(h2, *outs, *lses, w)


def _mlp_body(h_ref, g_ref, w1_ref, w2_ref, fg_ref, out_ref, hn_ref, acc_ref, *, final_norm):
    j = pl.program_id(1)

    @pl.when(j == 0)
    def _():
        hn_ref[...] = _rms_rows(h_ref[...], g_ref[...]).astype(BF16)
        acc_ref[...] = jnp.zeros_like(acc_ref)

    u = jnp.maximum(jnp.dot(hn_ref[...], w1_ref[...], preferred_element_type=F32), 0.0)
    acc_ref[...] += jnp.dot((u * u).astype(BF16), w2_ref[...], preferred_element_type=F32)

    @pl.when(j == pl.num_programs(1) - 1)
    def _():
        r = h_ref[...] + acc_ref[...]
        out_ref[...] = _rms_rows(r, fg_ref[...]) if final_norm else r


def _mlp(h2, gain, w1, w2, final_gain, *, final_norm, tm=1024, tf=512):
    t, d = h2.shape
    ff = w1.shape[1]
    return pl.pallas_call(
        functools.partial(_mlp_body, final_norm=final_norm),
        grid=(t // tm, ff // tf),
        in_specs=[pl.BlockSpec((tm, d), lambda i, j: (i, 0)),
                  pl.BlockSpec((1, d), lambda i, j: (0, 0)),
                  pl.BlockSpec((d, tf), lambda i, j: (0, j)),
                  pl.BlockSpec((tf, d), lambda i, j: (j, 0)),
                  pl.BlockSpec((1, d), lambda i, j: (0, 0))],
        out_specs=pl.BlockSpec((tm, d), lambda i, j: (i, 0)),
        out_shape=jax.ShapeDtypeStruct((t, d), F32),
        scratch_shapes=[pltpu.VMEM((tm, d), BF16), pltpu.VMEM((tm, d), F32)],
        compiler_params=_params("parallel", "arbitrary"),
        name="mlp",
    )(h2, gain, w1, w2, final_gain)


def kernel(x, attn_norm, mlp_norm, a_w_qkv, a_q_gain, a_k_gain, a_w_o, b_w_qkv, b_w_o,
           c_w_qkv, c_sinks, c_w_o, mlp_w1, mlp_w2, final_norm):
    b, seq, d = x.shape
    depth = attn_norm.shape[0]
    h = x.reshape(b * seq, d)
    scale = HEAD_DIM ** -0.5
    a_heads = a_w_o.shape[1] // HEAD_DIM
    c_heads = c_w_o.shape[1] // HEAD_DIM
    cos, sin = _rope_tables(seq)
    head = jnp.arange(2 * LANES) // HEAD_DIM
    seg = (head[:, None] == head[None, :]).astype(BF16)
    tile4 = lambda g: jnp.tile(g, 2 * LANES // HEAD_DIM)[None, :]
    used = [0, 0, 0]
    for layer in range(depth):
        kind = layer % N_MIXERS
        j = used[kind]
        used[kind] += 1
        gain = attn_norm[layer][None, :]
        if kind == 0:
            q_cols = a_heads * HEAD_DIM
            k_cols = (a_w_qkv.shape[2] - q_cols) // 2
            qkv = _qkv_proj(h, gain, a_w_qkv[j].astype(BF16), q_cols=q_cols, q_scale=scale * LOG2E,
                            rope=(tile4(a_q_gain[j]), tile4(a_k_gain[j]), cos, sin, seg, k_cols))
            o = _attn_a(qkv.reshape(b, seq, -1), n_heads=a_heads).reshape(b * seq, q_cols)
            h = _oproj(h, o, a_w_o[j].astype(BF16))
        elif kind == 1:
            q_cols = len(B_GROUPS) * B_HEADS_PER_GROUP * HEAD_DIM
            qkv = _qkv_proj(h, gain, b_w_qkv[j].astype(BF16), q_cols=q_cols, q_scale=scale)
            outs, lses = _attn_b(qkv.reshape(b, seq, -1))
            h = _oproj_b(h, outs, lses, b_w_o[j].astype(BF16))
        else:
            q_cols = c_heads * HEAD_DIM
            qkv = _qkv_proj(h, gain, c_w_qkv[j].astype(BF16), q_cols=q_cols, q_scale=scale)
            o = _attn_c(qkv.reshape(b, seq, -1), c_sinks[j], n_heads=c_heads).reshape(b * seq, q_cols)
            h = _oproj(h, o, c_w_o[j].astype(BF16))
        h = _mlp(h, mlp_norm[layer][None, :], mlp_w1[layer].astype(BF16), mlp_w2[layer].astype(BF16),
                 final_norm[None, :], final_norm=(layer == depth - 1))
    return h.reshape(b, seq, d)
```

```python
import functools
import math

import jax
import jax.numpy as jnp
from jax import lax
from jax.experimental import pallas as pl
from jax.experimental.pallas import tpu as pltpu

HEAD_DIM = 64
HALF_HEAD = HEAD_DIM // 2
RMS_EPS = 1e-6
GRID_W = 64
ROPE_THETA = 10000.0
N_MIXERS = 3
A_REP = 4
B_GROUPS = ((128, 1), (512, 4), (2048, 16))
B_HEADS_PER_GROUP = 6
B_KV_PER_GROUP = 2
B_REP = B_HEADS_PER_GROUP // B_KV_PER_GROUP
C_WINDOW = 128
LANES = 128
BF16_SUBLANES = 16
BLOCK_Q = 128
KEY_SPAN = 3 * BLOCK_Q
VT_ROWS = HEAD_DIM + BF16_SUBLANES
V7X_VMEM_LIMIT_BYTES = 48 * 1024 * 1024
NEG_BIG = -1e30
LOG2E = math.log2(math.e)
LN2 = math.log(2.0)

BF16 = jnp.bfloat16
F32 = jnp.float32


def _params(*sem):
    return pltpu.CompilerParams(dimension_semantics=sem, vmem_limit_bytes=V7X_VMEM_LIMIT_BYTES)


def _nt_dot(a, b):
    return lax.dot_general(a, b, (((1,), (1,)), ((), ())), preferred_element_type=F32)


def _rms_rows(x, gain):
    ms = jnp.mean(x * x, axis=-1, keepdims=True)
    return x * lax.rsqrt(ms + RMS_EPS) * gain


def _qkv_body(x_ref, g_ref, w_ref, o_ref, *, q_cols, q_scale):
    hn = _rms_rows(x_ref[...], g_ref[...]).astype(BF16)
    y = jnp.dot(hn, w_ref[...], preferred_element_type=F32)
    o_ref[:, :q_cols] = (y[:, :q_cols] * q_scale).astype(o_ref.dtype)
    o_ref[:, q_cols:] = y[:, q_cols:].astype(o_ref.dtype)


def _qkv_rope_body(x_ref, g_ref, w_ref, qg_ref, kg_ref, cos_ref, sin_ref, seg_ref, o_ref,
                   *, q_cols, k_cols, q_scale):
    hn = _rms_rows(x_ref[...], g_ref[...]).astype(BF16)
    y = jnp.dot(hn, w_ref[...], preferred_element_type=F32)
    slab = 2 * LANES
    cos = jnp.concatenate([cos_ref[...], cos_ref[...]], axis=1)
    sin = jnp.concatenate([sin_ref[...], sin_ref[...]], axis=1)
    lane = lax.broadcasted_iota(jnp.int32, (x_ref.shape[0], slab), 1)
    first_half = (lane % HALF_HEAD) < (HALF_HEAD // 2)
    seg = seg_ref[...]
    for s in range((q_cols + k_cols) // slab):
        is_q = s * slab < q_cols
        ys = y[:, s * slab:(s + 1) * slab]
        sq = ys * ys
        hi = sq.astype(BF16)
        lo = (sq - hi.astype(F32)).astype(BF16)
        ss = jnp.dot(hi, seg, preferred_element_type=F32) + jnp.dot(lo, seg, preferred_element_type=F32)
        gain = qg_ref[...] if is_q else kg_ref[...]
        yn = ys * lax.rsqrt(ss * (1.0 / HEAD_DIM) + RMS_EPS) * gain
        partner = jnp.where(first_half,
                            pltpu.roll(yn, slab - HALF_HEAD // 2, axis=1),
                            pltpu.roll(yn, HALF_HEAD // 2, axis=1))
        r = yn * cos + partner * sin
        if is_q:
            r = r * q_scale
        o_ref[:, s * slab:(s + 1) * slab] = r.astype(o_ref.dtype)
    o_ref[:, q_cols + k_cols:] = y[:, q_cols + k_cols:].astype(o_ref.dtype)


def _qkv_proj(h2, gain, w, *, q_cols, q_scale, tm=512, rope=None):
    t, d = h2.shape
    n = w.shape[1]
    common = dict(
        grid=(t // tm,),
        out_specs=pl.BlockSpec((tm, n), lambda i: (i, 0)),
        out_shape=jax.ShapeDtypeStruct((t, n), BF16),
        compiler_params=_params("parallel"),
    )
    x_spec = pl.BlockSpec((tm, d), lambda i: (i, 0))
    g_spec = pl.BlockSpec((1, d), lambda i: (0, 0))
    w_spec = pl.BlockSpec((d, n), lambda i: (0, 0))
    if rope is None:
        return pl.pallas_call(
            functools.partial(_qkv_body, q_cols=q_cols, q_scale=q_scale),
            in_specs=[x_spec, g_spec, w_spec], name="qkv_proj", **common,
        )(h2, gain, w)
    qg, kg, cos, sin, seg, k_cols = rope
    seq_tiles = cos.shape[0] // tm
    slab = 2 * LANES
    return pl.pallas_call(
        functools.partial(_qkv_rope_body, q_cols=q_cols, k_cols=k_cols, q_scale=q_scale),
        in_specs=[x_spec, g_spec, w_spec,
                  pl.BlockSpec((1, slab), lambda i: (0, 0)),
                  pl.BlockSpec((1, slab), lambda i: (0, 0)),
                  pl.BlockSpec((tm, LANES), lambda i: (i % seq_tiles, 0)),
                  pl.BlockSpec((tm, LANES), lambda i: (i % seq_tiles, 0)),
                  pl.BlockSpec((slab, slab), lambda i: (0, 0))],
        name="qkv_proj_rope", **common,
    )(h2, gain, w, qg, kg, cos, sin, seg)


def _rope_tables(seq):
    t = jnp.arange(seq)
    row = (t // GRID_W).astype(F32)
    col = (t % GRID_W).astype(F32)
    inv_freq = ROPE_THETA ** (-jnp.arange(0, HALF_HEAD, 2, dtype=F32) / HALF_HEAD)
    ang_row = row[:, None] * inv_freq
    ang_col = col[:, None] * inv_freq
    cos = jnp.concatenate([jnp.cos(ang_row)] * 2 + [jnp.cos(ang_col)] * 2, axis=1)
    sin = jnp.concatenate([-jnp.sin(ang_row), jnp.sin(ang_row), -jnp.sin(ang_col), jnp.sin(ang_col)], axis=1)
    return jnp.concatenate([cos, cos], axis=1), jnp.concatenate([sin, sin], axis=1)


def _attn_a_body(q_ref, k_ref, v_ref, o_ref, klo, khi, vt1, s_buf, *, tq, tk, seq):
    def fill(head_in_high_lanes):
        low = lax.broadcasted_iota(jnp.int32, (seq, LANES), 1) < HEAD_DIM
        zero = jnp.zeros((seq, LANES), BF16)
        x = k_ref[0]
        swapped = jnp.concatenate([x[:, HEAD_DIM:], x[:, :HEAD_DIM]], axis=1)
        in_low, in_high = (swapped, x) if head_in_high_lanes else (x, swapped)
        klo[...] = jnp.where(low, in_low, zero)
        khi[...] = jnp.where(low, zero, in_high)
        ones_row = (lax.broadcasted_iota(jnp.int32, (VT_ROWS - HEAD_DIM, tk), 0) == 0).astype(F32)
        for c in range(seq // tk):
            xt = v_ref[0, c * tk:(c + 1) * tk, :].astype(F32).T
            vt = xt[HEAD_DIM:] if head_in_high_lanes else xt[:HEAD_DIM]
            vt1[:, c * tk:(c + 1) * tk] = jnp.concatenate([vt, ones_row], axis=0).astype(BF16)

    first = pl.program_id(2) == 0
    odd = (pl.program_id(1) % 2) == 1
    pl.when(jnp.logical_and(first, jnp.logical_not(odd)))(lambda: fill(False))
    pl.when(jnp.logical_and(first, odd))(lambda: fill(True))

    q = q_ref[0]
    q2 = jnp.concatenate([q[:, :LANES], q[:, LANES:]], axis=0)
    cols = 2 * tq

    def scores(c, slot):
        ks = pl.multiple_of(c * tk, tk)
        cms = []
        for par, k_s in enumerate((klo, khi)):
            s = _nt_dot(k_s[pl.ds(ks, tk), :], q2)
            s_buf[slot, par] = s
            cms.append(jnp.max(s, axis=0, keepdims=True))
        return tuple(cms)

    def accumulate(c, slot, cms, state):
        vt = vt1[:, pl.ds(pl.multiple_of(c * tk, tk), tk)]
        new = []
        for par in range(2):
            m, acc = state[2 * par], state[2 * par + 1]
            mn = jnp.maximum(m, cms[par])
            p = jnp.exp2(s_buf[slot, par] - mn)
            new += [mn, acc * jnp.exp2(m - mn) + jnp.dot(vt, p.astype(BF16), preferred_element_type=F32)]
        return tuple(new)

    n_chunks = seq // tk
    neg = jnp.full((1, cols), -jnp.inf, F32)
    acc0 = jnp.zeros((VT_ROWS, cols), F32)
    cms0 = scores(0, 0)

    def pair(j, carry):
        cms_a, state = carry[:2], carry[2:]
        cms_b = scores(2 * j + 1, 1)
        state = accumulate(2 * j, 0, cms_a, state)
        cms_a = scores(2 * j + 2, 0)
        state = accumulate(2 * j + 1, 1, cms_b, state)
        return cms_a + state

    carry = lax.fori_loop(0, n_chunks // 2 - 1, pair, cms0 + (neg, acc0, neg, acc0))
    cms_a, state = carry[:2], carry[2:]
    cms_b = scores(n_chunks - 1, 1)
    state = accumulate(n_chunks - 2, 0, cms_a, state)
    _, acc_e, _, acc_o = accumulate(n_chunks - 1, 1, cms_b, state)
    out = jnp.concatenate([acc[:HEAD_DIM] / acc[HEAD_DIM:HEAD_DIM + 1] for acc in (acc_e, acc_o)], axis=0)
    o_ref[0] = jnp.concatenate([out[:, :tq].T, out[:, tq:].T], axis=1).astype(o_ref.dtype)


def _attn_a(qkv, *, n_heads, tq=512, tk=512):
    b, seq, _ = qkv.shape
    kv_heads = n_heads // A_REP
    q_cols = n_heads * HEAD_DIM
    q_w = A_REP * HEAD_DIM
    k_blk0 = q_cols // LANES
    v_blk0 = (q_cols + kv_heads * HEAD_DIM) // LANES
    assert (seq // tk) % 2 == 0 and seq // tk >= 4
    return pl.pallas_call(
        functools.partial(_attn_a_body, tq=tq, tk=tk, seq=seq),
        grid=(b, kv_heads, seq // tq),
        in_specs=[pl.BlockSpec((1, tq, q_w), lambda bi, h, i: (bi, i, h)),
                  pl.BlockSpec((1, seq, LANES), lambda bi, h, i: (bi, 0, k_blk0 + h // 2)),
                  pl.BlockSpec((1, seq, LANES), lambda bi, h, i: (bi, 0, v_blk0 + h // 2))],
        out_specs=pl.BlockSpec((1, tq, q_w), lambda bi, h, i: (bi, i, h)),
        out_shape=jax.ShapeDtypeStruct((b, seq, q_cols), BF16),
        scratch_shapes=[pltpu.VMEM((seq, LANES), BF16), pltpu.VMEM((seq, LANES), BF16),
                        pltpu.VMEM((VT_ROWS, seq), BF16), pltpu.VMEM((2, 2, tk, 2 * tq), F32)],
        compiler_params=_params("parallel", "arbitrary", "arbitrary"),
        name="attn_a",
    )(qkv, qkv, qkv)


def _fill_band_bias(bias_ref, slopes, half_window, dist_scale):
    c = lax.broadcasted_iota(jnp.int32, (KEY_SPAN, BLOCK_Q), 0)
    r = lax.broadcasted_iota(jnp.int32, (KEY_SPAN, BLOCK_Q), 1)
    dist = jnp.abs(c - BLOCK_Q - r)
    inside = dist <= half_window
    penalty = dist.astype(F32) * (dist_scale * LOG2E)
    for variant, ok in enumerate((inside & (c >= BLOCK_Q), inside, inside & (c < 2 * BLOCK_Q))):
        for h, slope in enumerate(slopes):
            bias_ref[variant, h] = jnp.where(ok, -slope * penalty, NEG_BIG)


def _tile_variant(i, n_tiles):
    return jnp.where(i == 0, 0, jnp.where(i == n_tiles - 1, 2, 1))


def _band_pipeline(jobs, vt1s, bias_ref, variant, s_buf):
    def scores(job, slot):
        kmat, q_rows, head_ids, _, sink_row = job
        width = len(head_ids) * BLOCK_Q
        s = _nt_dot(kmat, q_rows) + jnp.concatenate([bias_ref[variant, h] for h in head_ids], axis=1)
        s_buf[slot, :, :width] = s
        m = jnp.max(s, axis=0, keepdims=True)
        return m if sink_row is None else jnp.maximum(m, sink_row)

    def values(job, slot, m):
        _, _, head_ids, kv_ids, sink_row = job
        n = len(head_ids)
        p = jnp.exp2(s_buf[slot, :, :n * BLOCK_Q] - m).astype(BF16)
        accs, start = [], 0
        while start < n:
            stop = start
            while stop < n and kv_ids[stop] == kv_ids[start]:
                stop += 1
            accs.append(jnp.dot(vt1s[kv_ids[start]], p[:, start * BLOCK_Q:stop * BLOCK_Q],
                                preferred_element_type=F32))
            start = stop
        acc = accs[0] if len(accs) == 1 else jnp.concatenate(accs, axis=1)
        den = acc[HEAD_DIM:HEAD_DIM + 1]
        if sink_row is not None:
            den = den + jnp.exp2(sink_row - m)
        return acc[:HEAD_DIM] / den, m, den

    results = []
    m_next = scores(jobs[0], 0)
    for c, job in enumerate(jobs):
        m_cur = m_next
        if c + 1 < len(jobs):
            m_next = scores(jobs[c + 1], (c + 1) % 2)
        results.append(values(job, c % 2, m_cur))
    return results


def _band_operands(kp_ref, kc_ref, kn_ref, vp_ref, vc_ref, vn_ref):
    k = jnp.concatenate([kp_ref[0], kc_ref[0], kn_ref[0]], axis=0)
    vt = jnp.concatenate([vp_ref[0], vc_ref[0], vn_ref[0]], axis=0).astype(F32).T
    ones_row = (lax.broadcasted_iota(jnp.int32, (VT_ROWS - HEAD_DIM, KEY_SPAN), 0) == 0).astype(F32)
    vt1s = [jnp.concatenate([vt[kv * HEAD_DIM:(kv + 1) * HEAD_DIM], ones_row], axis=0).astype(BF16)
            for kv in range(vt.shape[0] // HEAD_DIM)]
    return k, vt1s


def _swap_halves(x):
    return jnp.concatenate([x[:, HEAD_DIM:], x[:, :HEAD_DIM]], axis=1)


def _attn_c_body(sink_ref, q_ref, kp_ref, kc_ref, kn_ref, vp_ref, vc_ref, vn_ref, o_ref, bias_ref, s_buf,
                 *, n_heads, slopes):
    i = pl.program_id(1)
    pl.when(i == 0)(lambda: _fill_band_bias(bias_ref, slopes, C_WINDOW, 1.0))
    variant = _tile_variant(i, pl.num_programs(1))
    q = q_ref[0]
    k, vt1s = _band_operands(kp_ref, kc_ref, kn_ref, vp_ref, vc_ref, vn_ref)
    low = lax.broadcasted_iota(jnp.int32, (BLOCK_Q, LANES), 1) < HEAD_DIM
    zero = jnp.zeros((BLOCK_Q, LANES), BF16)
    first_block = lax.broadcasted_iota(jnp.int32, (1, 2 * BLOCK_Q), 1) < BLOCK_Q
    jobs = []
    for pair in range(n_heads // A_REP // 2):
        ks = k[:, pair * LANES:(pair + 1) * LANES]
        ks_swapped = _swap_halves(ks)
        for which in range(2):
            kv = 2 * pair + which
            slabs = [q[:, (2 * kv + j) * LANES:(2 * kv + j + 1) * LANES] for j in range(2)]
            for parity in range(2):
                q_rows = jnp.concatenate(
                    [jnp.where(low, s_, zero) if parity == 0 else jnp.where(low, zero, s_) for s_ in slabs],
                    axis=0)
                heads = (A_REP * kv + parity, A_REP * kv + 2 + parity)
                sink_row = jnp.where(first_block, sink_ref[heads[0]], sink_ref[heads[1]]) * LOG2E
                jobs.append((ks if parity == which else ks_swapped, q_rows, heads, (kv, kv), sink_row))
    o_t = [None] * n_heads
    for job, (ot, _, _) in zip(jobs, _band_pipeline(jobs, vt1s, bias_ref, variant, s_buf)):
        o_t[job[2][0]], o_t[job[2][1]] = ot[:, :BLOCK_Q], ot[:, BLOCK_Q:]
    for j in range(n_heads // 2):
        pair_t = jnp.concatenate([o_t[2 * j], o_t[2 * j + 1]], axis=0)
        o_ref[0, :, j * LANES:(j + 1) * LANES] = pair_t.T.astype(o_ref.dtype)


def _alibi_slopes(n):
    return [2.0 ** (-8.0 * (i + 1) / n) for i in range(n)]


def _attn_c(qkv, sinks, *, n_heads):
    b, seq, _ = qkv.shape
    kv_heads = n_heads // A_REP
    q_cols = n_heads * HEAD_DIM
    kv_cols = kv_heads * HEAD_DIM
    n_tiles = seq // BLOCK_Q
    k_blk = q_cols // kv_cols
    v_blk = k_blk + 1

    def kv_spec(col_blk, shift):
        return pl.BlockSpec(
            (1, BLOCK_Q, kv_cols),
            lambda bi, i: (bi, jnp.clip(i + shift, 0, n_tiles - 1), col_blk))

    return pl.pallas_call(
        functools.partial(_attn_c_body, n_heads=n_heads, slopes=_alibi_slopes(n_heads)),
        grid=(b, n_tiles),
        in_specs=[pl.BlockSpec(memory_space=pltpu.SMEM),
                  pl.BlockSpec((1, BLOCK_Q, q_cols), lambda bi, i: (bi, i, 0)),
                  kv_spec(k_blk, -1), kv_spec(k_blk, 0), kv_spec(k_blk, 1),
                  kv_spec(v_blk, -1), kv_spec(v_blk, 0), kv_spec(v_blk, 1)],
        out_specs=pl.BlockSpec((1, BLOCK_Q, q_cols), lambda bi, i: (bi, i, 0)),
        out_shape=jax.ShapeDtypeStruct((b, seq, q_cols), BF16),
        scratch_shapes=[pltpu.VMEM((3, n_heads, KEY_SPAN, BLOCK_Q), F32),
                        pltpu.VMEM((2, KEY_SPAN, 2 * BLOCK_Q), F32)],
        compiler_params=_params("parallel", "arbitrary"),
        name="attn_c",
    )(sinks, qkv, qkv, qkv, qkv, qkv, qkv, qkv)


def _attn_b_body(q_ref, kp_ref, kc_ref, kn_ref, vp_ref, vc_ref, vn_ref, o_ref, lse_ref, bias_ref, s_buf,
                 *, slopes, dil):
    i = pl.program_id(2)
    pl.when(jnp.logical_and(pl.program_id(1) == 0, i == 0))(
        lambda: _fill_band_bias(bias_ref, slopes, BLOCK_Q // 2, float(dil)))
    variant = _tile_variant(i, pl.num_programs(2))
    q = q_ref[0]
    ks, vt1s = _band_operands(kp_ref, kc_ref, kn_ref, vp_ref, vc_ref, vn_ref)
    low = lax.broadcasted_iota(jnp.int32, (BLOCK_Q, LANES), 1) < HEAD_DIM
    zero = jnp.zeros((BLOCK_Q, LANES), BF16)
    slabs = [q[:, j * LANES:(j + 1) * LANES] for j in range(B_HEADS_PER_GROUP // 2)]
    heads_all = range(B_HEADS_PER_GROUP)
    plain = [h for h in heads_all if h % 2 == h // B_REP]
    crossed = [h for h in heads_all if h % 2 != h // B_REP]
    ks_swapped = _swap_halves(ks)
    jobs = []
    for kmat, heads in ((ks, plain[:2]), (ks_swapped, crossed), (ks, plain[2:])):
        q_rows = jnp.concatenate(
            [jnp.where(low, slabs[h // 2], zero) if h % 2 == 0 else jnp.where(low, zero, slabs[h // 2])
             for h in heads], axis=0)
        jobs.append((kmat, q_rows, heads, [h // B_REP for h in heads], None))
    o_t, lse_t = {}, {}
    for job, (ot, m, den) in zip(jobs, _band_pipeline(jobs, vt1s, bias_ref, variant, s_buf)):
        heads = job[2]
        lse = (m + jnp.log2(den)) * LN2
        for n, h in enumerate(heads):
            o_t[h] = ot[:, n * BLOCK_Q:(n + 1) * BLOCK_Q]
            lse_t[h] = jnp.broadcast_to(lse[:, n * BLOCK_Q:(n + 1) * BLOCK_Q], (HEAD_DIM, BLOCK_Q))
    for j in range(B_HEADS_PER_GROUP // 2):
        cols = slice(j * LANES, (j + 1) * LANES)
        o_ref[0, :, cols] = jnp.concatenate([o_t[2 * j], o_t[2 * j + 1]], axis=0).T.astype(o_ref.dtype)
        lse_ref[0, :, cols] = jnp.concatenate([lse_t[2 * j], lse_t[2 * j + 1]], axis=0).T


def _attn_b(qkv):
    b, seq, n_cols = qkv.shape
    n_g = len(B_GROUPS)
    gq = B_HEADS_PER_GROUP * HEAD_DIM
    gk = B_KV_PER_GROUP * HEAD_DIM
    q_cols = n_g * gq
    all_slopes = _alibi_slopes(n_g * B_HEADS_PER_GROUP)
    outs, lses = [], []
    for g, (window, dil) in enumerate(B_GROUPS):
        assert (window // 2) // dil == BLOCK_Q // 2
        sub = seq // dil
        n_tiles = sub // BLOCK_Q
        view = qkv.reshape(b, sub, dil * n_cols)

        def q_map(bi, r, i, g=g):
            return (bi, i, r * (n_cols // gq) + g)

        def kv_spec(base, shift, g=g, n_tiles=n_tiles):
            return pl.BlockSpec(
                (1, BLOCK_Q, gk),
                lambda bi, r, i: (bi, jnp.clip(i + shift, 0, n_tiles - 1), r * (n_cols // gk) + base + g))

        k0 = q_cols // gk
        v0 = k0 + n_g
        out_map = lambda bi, r, i: (bi, i, r)
        o, lse = pl.pallas_call(
            functools.partial(_attn_b_body, dil=dil,
                              slopes=all_slopes[g * B_HEADS_PER_GROUP:(g + 1) * B_HEADS_PER_GROUP]),
            grid=(b, dil, n_tiles),
            in_specs=[pl.BlockSpec((1, BLOCK_Q, gq), q_map),
                      kv_spec(k0, -1), kv_spec(k0, 0), kv_spec(k0, 1),
                      kv_spec(v0, -1), kv_spec(v0, 0), kv_spec(v0, 1)],
            out_specs=[pl.BlockSpec((1, BLOCK_Q, gq), out_map), pl.BlockSpec((1, BLOCK_Q, gq), out_map)],
            out_shape=[jax.ShapeDtypeStruct((b, sub, dil * gq), BF16),
                       jax.ShapeDtypeStruct((b, sub, dil * gq), F32)],
            scratch_shapes=[pltpu.VMEM((3, B_HEADS_PER_GROUP, KEY_SPAN, BLOCK_Q), F32),
                            pltpu.VMEM((2, KEY_SPAN, 2 * BLOCK_Q), F32)],
            compiler_params=_params("parallel", "arbitrary", "arbitrary"),
            name=f"attn_b_g{g}",
        )(view, view, view, view, view, view, view)
        outs.append(o.reshape(b * seq, gq))
        lses.append(lse.reshape(b * seq, gq))
    return outs, lses


def _oproj_body(h_ref, o_ref, w_ref, out_ref):
    out_ref[...] = h_ref[...] + jnp.dot(o_ref[...], w_ref[...], preferred_element_type=F32)


def _oproj_b_body(h_ref, o0, o1, o2, l0, l1, l2, w_ref, out_ref):
    ls = [l0[...], l1[...], l2[...]]
    mx = jnp.maximum(jnp.maximum(ls[0], ls[1]), ls[2])
    es = [jnp.exp(l - mx) for l in ls]
    inv = 1.0 / (es[0] + es[1] + es[2])
    acc = h_ref[...]
    gq = o0.shape[1]
    for g, o in enumerate((o0, o1, o2)):
        og = (o[...].astype(F32) * (es[g] * inv)).astype(BF16)
        acc = acc + jnp.dot(og, w_ref[g * gq:(g + 1) * gq, :], preferred_element_type=F32)
    out_ref[...] = acc


def _oproj(h2, o, w, *, tm=512):
    t, d = h2.shape
    kdim = w.shape[0]
    return pl.pallas_call(
        _oproj_body,
        grid=(t // tm,),
        in_specs=[pl.BlockSpec((tm, d), lambda i: (i, 0)),
                  pl.BlockSpec((tm, kdim), lambda i: (i, 0)),
                  pl.BlockSpec((kdim, d), lambda i: (0, 0))],
        out_specs=pl.BlockSpec((tm, d), lambda i: (i, 0)),
        out_shape=jax.ShapeDtypeStruct((t, d), F32),
        compiler_params=_params("parallel"),
        name="oproj",
    )(h2, o, w)


def _oproj_b(h2, outs, lses, w, *, tm=512):
    t, d = h2.shape
    kdim = w.shape[0]
    gq = outs[0].shape[1]
    tile = lambda cols: pl.BlockSpec((tm, cols), lambda i: (i, 0))
    return pl.pallas_call(
        _oproj_b_body,
        grid=(t // tm,),
        in_specs=[tile(d)] + [tile(gq)] * 6 + [pl.BlockSpec((kdim, d), lambda i: (0, 0))],
        out_specs=tile(d),
        out_shape=jax.ShapeDtypeStruct((t, d), F32),
        compiler_params=_params("parallel"),
        name="oproj_b",
    )(h2, *outs, *lses, w)


def _mlp_body(h_ref, g_ref, w1_ref, w2_ref, fg_ref, out_ref, hn_ref, acc_ref, *, final_norm):
    j = pl.program_id(1)

    @pl.when(j == 0)
    def _():
        hn_ref[...] = _rms_rows(h_ref[...], g_ref[...]).astype(BF16)
        acc_ref[...] = jnp.zeros_like(acc_ref)

    u = jnp.maximum(jnp.dot(hn_ref[...], w1_ref[...], preferred_element_type=F32), 0.0)
    acc_ref[...] += jnp.dot((u * u).astype(BF16), w2_ref[...], preferred_element_type=F32)

    @pl.when(j == pl.num_programs(1) - 1)
    def _():
        r = h_ref[...] + acc_ref[...]
        out_ref[...] = _rms_rows(r, fg_ref[...]) if final_norm else r


def _mlp(h2, gain, w1, w2, final_gain, *, final_norm, tm=1024, tf=512):
    t, d = h2.shape
    ff = w1.shape[1]
    return pl.pallas_call(
        functools.partial(_mlp_body, final_norm=final_norm),
        grid=(t // tm, ff // tf),
        in_specs=[pl.BlockSpec((tm, d), lambda i, j: (i, 0)),
                  pl.BlockSpec((1, d), lambda i, j: (0, 0)),
                  pl.BlockSpec((d, tf), lambda i, j: (0, j)),
                  pl.BlockSpec((tf, d), lambda i, j: (j, 0)),
                  pl.BlockSpec((1, d), lambda i, j: (0, 0))],
        out_specs=pl.BlockSpec((tm, d), lambda i, j: (i, 0)),
        out_shape=jax.ShapeDtypeStruct((t, d), F32),
        scratch_shapes=[pltpu.VMEM((tm, d), BF16), pltpu.VMEM((tm, d), F32)],
        compiler_params=_params("parallel", "arbitrary"),
        name="mlp",
    )(h2, gain, w1, w2, final_gain)


def kernel(x, attn_norm, mlp_norm, a_w_qkv, a_q_gain, a_k_gain, a_w_o, b_w_qkv, b_w_o,
           c_w_qkv, c_sinks, c_w_o, mlp_w1, mlp_w2, final_norm):
    b, seq, d = x.shape
    depth = attn_norm.shape[0]
    h = x.reshape(b * seq, d)
    scale = HEAD_DIM ** -0.5 * LOG2E
    a_heads = a_w_o.shape[1] // HEAD_DIM
    c_heads = c_w_o.shape[1] // HEAD_DIM
    cos, sin = _rope_tables(seq)
    head = jnp.arange(2 * LANES) // HEAD_DIM
    seg = (head[:, None] == head[None, :]).astype(BF16)
    tile4 = lambda g: jnp.tile(g, 2 * LANES // HEAD_DIM)[None, :]
    used = [0, 0, 0]
    for layer in range(depth):
        kind = layer % N_MIXERS
        j = used[kind]
        used[kind] += 1
        gain = attn_norm[layer][None, :]
        if kind == 0:
            q_cols = a_heads * HEAD_DIM
            k_cols = (a_w_qkv.shape[2] - q_cols) // 2
            qkv = _qkv_proj(h, gain, a_w_qkv[j].astype(BF16), q_cols=q_cols, q_scale=scale,
                            rope=(tile4(a_q_gain[j]), tile4(a_k_gain[j]), cos, sin, seg, k_cols))
            o = _attn_a(qkv.reshape(b, seq, -1), n_heads=a_heads).reshape(b * seq, q_cols)
            h = _oproj(h, o, a_w_o[j].astype(BF16))
        elif kind == 1:
            q_cols = len(B_GROUPS) * B_HEADS_PER_GROUP * HEAD_DIM
            qkv = _qkv_proj(h, gain, b_w_qkv[j].astype(BF16), q_cols=q_cols, q_scale=scale)
            outs, lses = _attn_b(qkv.reshape(b, seq, -1))
            h = _oproj_b(h, outs, lses, b_w_o[j].astype(BF16))
        else:
            q_cols = c_heads * HEAD_DIM
            qkv = _qkv_proj(h, gain, c_w_qkv[j].astype(BF16), q_cols=q_cols, q_scale=scale)
            o = _attn_c(qkv.reshape(b, seq, -1), c_sinks[j], n_heads=c_heads).reshape(b * seq, q_cols)
            h = _oproj(h, o, c_w_o[j].astype(BF16))
        h = _mlp(h, mlp_norm[layer][None, :], mlp_w1[layer].astype(BF16), mlp_w2[layer].astype(BF16),
                 final_norm[None, :], final_norm=(layer == depth - 1))
    return h.reshape(b, seq, d)
```

```python
import functools
import math

import jax
import jax.numpy as jnp
from jax import lax
from jax.experimental import pallas as pl
from jax.experimental.pallas import tpu as pltpu

HEAD_DIM = 64
HALF_HEAD = HEAD_DIM // 2
RMS_EPS = 1e-6
GRID_W = 64
ROPE_THETA = 10000.0
N_MIXERS = 3
A_REP = 4
B_GROUPS = ((128, 1), (512, 4), (2048, 16))
B_HEADS_PER_GROUP = 6
B_KV_PER_GROUP = 2
B_REP = B_HEADS_PER_GROUP // B_KV_PER_GROUP
C_WINDOW = 128
LANES = 128
BF16_SUBLANES = 16
BLOCK_Q = 128
KEY_SPAN = 3 * BLOCK_Q
B_GROUP_COLS = (B_HEADS_PER_GROUP + 2 * B_KV_PER_GROUP) * HEAD_DIM
VT_ROWS = HEAD_DIM + BF16_SUBLANES
V7X_VMEM_LIMIT_BYTES = 48 * 1024 * 1024
NEG_BIG = -1e30
LOG2E = math.log2(math.e)
LN2 = math.log(2.0)

BF16 = jnp.bfloat16
F32 = jnp.float32


def _params(*sem):
    return pltpu.CompilerParams(dimension_semantics=sem, vmem_limit_bytes=V7X_VMEM_LIMIT_BYTES)


def _nt_dot(a, b):
    return lax.dot_general(a, b, (((1,), (1,)), ((), ())), preferred_element_type=F32)


def _rms_rows(x, gain):
    ms = jnp.mean(x * x, axis=-1, keepdims=True)
    return x * lax.rsqrt(ms + RMS_EPS) * gain


def _qkv_body(x_ref, g_ref, w_ref, o_ref, *, q_cols, q_scale):
    hn = _rms_rows(x_ref[...], g_ref[...]).astype(BF16)
    y = jnp.dot(hn, w_ref[...], preferred_element_type=F32)
    o_ref[:, :q_cols] = (y[:, :q_cols] * q_scale).astype(o_ref.dtype)
    o_ref[:, q_cols:] = y[:, q_cols:].astype(o_ref.dtype)


def _qkv_rope_body(x_ref, g_ref, w_ref, qg_ref, kg_ref, cos_ref, sin_ref, seg_ref, o_ref,
                   *, q_cols, k_cols, q_scale):
    hn = _rms_rows(x_ref[...], g_ref[...]).astype(BF16)
    y = jnp.dot(hn, w_ref[...], preferred_element_type=F32)
    slab = 2 * LANES
    cos = jnp.concatenate([cos_ref[...], cos_ref[...]], axis=1)
    sin = jnp.concatenate([sin_ref[...], sin_ref[...]], axis=1)
    lane = lax.broadcasted_iota(jnp.int32, (x_ref.shape[0], slab), 1)
    first_half = (lane % HALF_HEAD) < (HALF_HEAD // 2)
    seg = seg_ref[...]
    for s in range((q_cols + k_cols) // slab):
        is_q = s * slab < q_cols
        ys = y[:, s * slab:(s + 1) * slab]
        sq = ys * ys
        hi = sq.astype(BF16)
        lo = (sq - hi.astype(F32)).astype(BF16)
        ss = jnp.dot(hi, seg, preferred_element_type=F32) + jnp.dot(lo, seg, preferred_element_type=F32)
        gain = qg_ref[...] if is_q else kg_ref[...]
        yn = ys * lax.rsqrt(ss * (1.0 / HEAD_DIM) + RMS_EPS) * gain
        partner = jnp.where(first_half,
                            pltpu.roll(yn, slab - HALF_HEAD // 2, axis=1),
                            pltpu.roll(yn, HALF_HEAD // 2, axis=1))
        r = yn * cos + partner * sin
        if is_q:
            r = r * q_scale
        o_ref[:, s * slab:(s + 1) * slab] = r.astype(o_ref.dtype)
    o_ref[:, q_cols + k_cols:] = y[:, q_cols + k_cols:].astype(o_ref.dtype)


def _qkv_proj(h2, gain, w, *, q_cols, q_scale, tm=512, rope=None):
    t, d = h2.shape
    n = w.shape[1]
    common = dict(
        grid=(t // tm,),
        out_specs=pl.BlockSpec((tm, n), lambda i: (i, 0)),
        out_shape=jax.ShapeDtypeStruct((t, n), BF16),
        compiler_params=_params("parallel"),
    )
    x_spec = pl.BlockSpec((tm, d), lambda i: (i, 0))
    g_spec = pl.BlockSpec((1, d), lambda i: (0, 0))
    w_spec = pl.BlockSpec((d, n), lambda i: (0, 0))
    if rope is None:
        return pl.pallas_call(
            functools.partial(_qkv_body, q_cols=q_cols, q_scale=q_scale),
            in_specs=[x_spec, g_spec, w_spec], name="qkv_proj", **common,
        )(h2, gain, w)
    qg, kg, cos, sin, seg, k_cols = rope
    seq_tiles = cos.shape[0] // tm
    slab = 2 * LANES
    return pl.pallas_call(
        functools.partial(_qkv_rope_body, q_cols=q_cols, k_cols=k_cols, q_scale=q_scale),
        in_specs=[x_spec, g_spec, w_spec,
                  pl.BlockSpec((1, slab), lambda i: (0, 0)),
                  pl.BlockSpec((1, slab), lambda i: (0, 0)),
                  pl.BlockSpec((tm, LANES), lambda i: (i % seq_tiles, 0)),
                  pl.BlockSpec((tm, LANES), lambda i: (i % seq_tiles, 0)),
                  pl.BlockSpec((slab, slab), lambda i: (0, 0))],
        name="qkv_proj_rope", **common,
    )(h2, gain, w, qg, kg, cos, sin, seg)


def _qkv_b_body(x_ref, g_ref, w_ref, *rest, q_cols, q_scale):
    out_refs, y_scr = rest[:-1], rest[-1]
    tm = x_ref.shape[0]
    hn = _rms_rows(x_ref[...], g_ref[...]).astype(BF16)
    y = jnp.dot(hn, w_ref[...], preferred_element_type=F32)
    for s in range(y_scr.shape[0]):
        ys = y[:, s * LANES:(s + 1) * LANES]
        y_scr[s] = ys * q_scale if s * LANES < q_cols else ys
    n_g = len(B_GROUPS)
    q_slabs = B_HEADS_PER_GROUP * HEAD_DIM // LANES
    k0 = q_cols // LANES
    for g, (_, dil) in enumerate(B_GROUPS):
        slabs = list(range(g * q_slabs, (g + 1) * q_slabs)) + [k0 + g, k0 + n_g + g]
        for r in range(dil):
            rows = pl.ds(r, tm // dil, stride=dil)
            piece = jnp.concatenate([y_scr[s, rows, :] for s in slabs], axis=1)
            out_refs[g][:, r * B_GROUP_COLS:(r + 1) * B_GROUP_COLS] = piece.astype(BF16)


def _qkv_proj_b(h2, gain, w, *, q_cols, q_scale, tm=512):
    t, d = h2.shape
    n = w.shape[1]
    return pl.pallas_call(
        functools.partial(_qkv_b_body, q_cols=q_cols, q_scale=q_scale),
        grid=(t // tm,),
        in_specs=[pl.BlockSpec((tm, d), lambda i: (i, 0)),
                  pl.BlockSpec((1, d), lambda i: (0, 0)),
                  pl.BlockSpec((d, n), lambda i: (0, 0))],
        out_specs=[pl.BlockSpec((tm // dil, dil * B_GROUP_COLS), lambda i: (i, 0)) for _, dil in B_GROUPS],
        out_shape=[jax.ShapeDtypeStruct((t // dil, dil * B_GROUP_COLS), BF16) for _, dil in B_GROUPS],
        scratch_shapes=[pltpu.VMEM((n // LANES, tm, LANES), F32)],
        compiler_params=_params("parallel"),
        name="qkv_proj_b",
    )(h2, gain, w)


def _rope_tables(seq):
    t = jnp.arange(seq)
    row = (t // GRID_W).astype(F32)
    col = (t % GRID_W).astype(F32)
    inv_freq = ROPE_THETA ** (-jnp.arange(0, HALF_HEAD, 2, dtype=F32) / HALF_HEAD)
    ang_row = row[:, None] * inv_freq
    ang_col = col[:, None] * inv_freq
    cos = jnp.concatenate([jnp.cos(ang_row)] * 2 + [jnp.cos(ang_col)] * 2, axis=1)
    sin = jnp.concatenate([-jnp.sin(ang_row), jnp.sin(ang_row), -jnp.sin(ang_col), jnp.sin(ang_col)], axis=1)
    return jnp.concatenate([cos, cos], axis=1), jnp.concatenate([sin, sin], axis=1)


def _attn_a_body(q_ref, k_ref, v_ref, o_ref, klo, khi, vt1, s_buf, *, tq, tk, seq):
    def fill(head_in_high_lanes):
        low = lax.broadcasted_iota(jnp.int32, (seq, LANES), 1) < HEAD_DIM
        zero = jnp.zeros((seq, LANES), BF16)
        x = k_ref[0]
        swapped = jnp.concatenate([x[:, HEAD_DIM:], x[:, :HEAD_DIM]], axis=1)
        in_low, in_high = (swapped, x) if head_in_high_lanes else (x, swapped)
        klo[...] = jnp.where(low, in_low, zero)
        khi[...] = jnp.where(low, zero, in_high)
        ones_row = (lax.broadcasted_iota(jnp.int32, (VT_ROWS - HEAD_DIM, tk), 0) == 0).astype(F32)
        for c in range(seq // tk):
            xt = v_ref[0, c * tk:(c + 1) * tk, :].astype(F32).T
            vt = xt[HEAD_DIM:] if head_in_high_lanes else xt[:HEAD_DIM]
            vt1[:, c * tk:(c + 1) * tk] = jnp.concatenate([vt, ones_row], axis=0).astype(BF16)

    odd = (pl.program_id(1) % 2) == 1
    pl.when(jnp.logical_not(odd))(lambda: fill(False))
    pl.when(odd)(lambda: fill(True))

    cols = 2 * tq
    n_chunks = seq // tk
    n_tiles = seq // tq
    pairs_per_tile = n_chunks // 2

    def scores(tile, c, slot):
        q = q_ref[0, pl.ds(pl.multiple_of(tile * tq, tq), tq), :]
        q2 = jnp.concatenate([q[:, :LANES], q[:, LANES:]], axis=0)
        ks = pl.multiple_of(c * tk, tk)
        cms = []
        for par, k_s in enumerate((klo, khi)):
            s = _nt_dot(k_s[pl.ds(ks, tk), :], q2)
            s_buf[slot, par] = s
            cms.append(jnp.max(s, axis=0, keepdims=True))
        return tuple(cms)

    def accumulate(c, slot, cms, state):
        vt = vt1[:, pl.ds(pl.multiple_of(c * tk, tk), tk)]
        new = []
        for par in range(2):
            m, acc = state[2 * par], state[2 * par + 1]
            mn = jnp.maximum(m, cms[par])
            p = jnp.exp2(s_buf[slot, par] - mn)
            new += [mn, acc * jnp.exp2(m - mn) + jnp.dot(vt, p.astype(BF16), preferred_element_type=F32)]
        return tuple(new)

    neg = jnp.full((1, cols), -jnp.inf, F32)
    acc0 = jnp.zeros((VT_ROWS, cols), F32)

    def pair(j, carry):
        tile, cp = j // pairs_per_tile, j % pairs_per_tile
        cms_a, (m_e, acc_e, m_o, acc_o) = carry[:2], carry[2:]
        fresh = cp == 0
        state = (jnp.where(fresh, neg, m_e), acc_e, jnp.where(fresh, neg, m_o), acc_o)
        cms_b = scores(tile, 2 * cp + 1, 1)
        state = accumulate(2 * cp, 0, cms_a, state)
        nxt = 2 * j + 2
        cms_a = scores(jnp.minimum(nxt // n_chunks, n_tiles - 1), nxt % n_chunks, 0)
        state = accumulate(2 * cp + 1, 1, cms_b, state)

        @pl.when(cp == pairs_per_tile - 1)
        def _():
            out = jnp.concatenate(
                [acc[:HEAD_DIM] / acc[HEAD_DIM:HEAD_DIM + 1] for acc in (state[1], state[3])], axis=0)
            o_ref[0, pl.ds(pl.multiple_of(tile * tq, tq), tq), :] = jnp.concatenate(
                [out[:, :tq].T, out[:, tq:].T], axis=1).astype(o_ref.dtype)

        return cms_a + state

    lax.fori_loop(0, n_tiles * pairs_per_tile, pair, scores(0, 0, 0) + (neg, acc0, neg, acc0))


def _attn_a(qkv, *, n_heads, tq=512, tk=512):
    b, seq, _ = qkv.shape
    kv_heads = n_heads // A_REP
    q_cols = n_heads * HEAD_DIM
    q_w = A_REP * HEAD_DIM
    k_blk0 = q_cols // LANES
    v_blk0 = (q_cols + kv_heads * HEAD_DIM) // LANES
    assert (seq // tk) % 2 == 0 and seq % tq == 0
    return pl.pallas_call(
        functools.partial(_attn_a_body, tq=tq, tk=tk, seq=seq),
        grid=(b, kv_heads),
        in_specs=[pl.BlockSpec((1, seq, q_w), lambda bi, h: (bi, 0, h)),
                  pl.BlockSpec((1, seq, LANES), lambda bi, h: (bi, 0, k_blk0 + h // 2)),
                  pl.BlockSpec((1, seq, LANES), lambda bi, h: (bi, 0, v_blk0 + h // 2))],
        out_specs=pl.BlockSpec((1, seq, q_w), lambda bi, h: (bi, 0, h)),
        out_shape=jax.ShapeDtypeStruct((b, seq, q_cols), BF16),
        scratch_shapes=[pltpu.VMEM((seq, LANES), BF16), pltpu.VMEM((seq, LANES), BF16),
                        pltpu.VMEM((VT_ROWS, seq), BF16), pltpu.VMEM((2, 2, tk, 2 * tq), F32)],
        compiler_params=_params("parallel", "arbitrary"),
        name="attn_a",
    )(qkv, qkv, qkv)


def _fill_band_bias(bias_ref, slopes, half_window, dist_scale):
    c = lax.broadcasted_iota(jnp.int32, (KEY_SPAN, BLOCK_Q), 0)
    r = lax.broadcasted_iota(jnp.int32, (KEY_SPAN, BLOCK_Q), 1)
    dist = jnp.abs(c - BLOCK_Q - r)
    inside = dist <= half_window
    penalty = dist.astype(F32) * (dist_scale * LOG2E)
    for variant, ok in enumerate((inside & (c >= BLOCK_Q), inside, inside & (c < 2 * BLOCK_Q))):
        for h, slope in enumerate(slopes):
            bias_ref[variant, h] = jnp.where(ok, -slope * penalty, NEG_BIG)


def _tile_variant(i, n_tiles):
    return jnp.where(i == 0, 0, jnp.where(i == n_tiles - 1, 2, 1))


def _band_pipeline(jobs, vt1s, bias_ref, variant, s_buf):
    def scores(job, slot):
        kmat, q_rows, head_ids, _, sink_row = job
        width = len(head_ids) * BLOCK_Q
        s = _nt_dot(kmat, q_rows) + jnp.concatenate([bias_ref[variant, h] for h in head_ids], axis=1)
        s_buf[slot, :, :width] = s
        m = jnp.max(s, axis=0, keepdims=True)
        return m if sink_row is None else jnp.maximum(m, sink_row)

    def values(job, slot, m):
        _, _, head_ids, kv_ids, sink_row = job
        n = len(head_ids)
        p = jnp.exp2(s_buf[slot, :, :n * BLOCK_Q] - m).astype(BF16)
        accs, start = [], 0
        while start < n:
            stop = start
            while stop < n and kv_ids[stop] == kv_ids[start]:
                stop += 1
            accs.append(jnp.dot(vt1s[kv_ids[start]], p[:, start * BLOCK_Q:stop * BLOCK_Q],
                                preferred_element_type=F32))
            start = stop
        acc = accs[0] if len(accs) == 1 else jnp.concatenate(accs, axis=1)
        den = acc[HEAD_DIM:HEAD_DIM + 1]
        if sink_row is not None:
            den = den + jnp.exp2(sink_row - m)
        return acc[:HEAD_DIM] / den, m, den

    results = []
    m_next = scores(jobs[0], 0)
    for c, job in enumerate(jobs):
        m_cur = m_next
        if c + 1 < len(jobs):
            m_next = scores(jobs[c + 1], (c + 1) % 2)
        results.append(values(job, c % 2, m_cur))
    return results


def _band_operands(kp_ref, kc_ref, kn_ref, vp_ref, vc_ref, vn_ref):
    k = jnp.concatenate([kp_ref[0], kc_ref[0], kn_ref[0]], axis=0)
    vt = jnp.concatenate([vp_ref[0], vc_ref[0], vn_ref[0]], axis=0).astype(F32).T
    ones_row = (lax.broadcasted_iota(jnp.int32, (VT_ROWS - HEAD_DIM, KEY_SPAN), 0) == 0).astype(F32)
    vt1s = [jnp.concatenate([vt[kv * HEAD_DIM:(kv + 1) * HEAD_DIM], ones_row], axis=0).astype(BF16)
            for kv in range(vt.shape[0] // HEAD_DIM)]
    return k, vt1s


def _swap_halves(x):
    return jnp.concatenate([x[:, HEAD_DIM:], x[:, :HEAD_DIM]], axis=1)


def _attn_c_body(sink_ref, q_ref, kp_ref, kc_ref, kn_ref, vp_ref, vc_ref, vn_ref, o_ref, bias_ref, s_buf,
                 *, n_heads, slopes):
    i = pl.program_id(1)
    pl.when(i == 0)(lambda: _fill_band_bias(bias_ref, slopes, C_WINDOW, 1.0))
    variant = _tile_variant(i, pl.num_programs(1))
    q = q_ref[0]
    k, vt1s = _band_operands(kp_ref, kc_ref, kn_ref, vp_ref, vc_ref, vn_ref)
    low = lax.broadcasted_iota(jnp.int32, (BLOCK_Q, LANES), 1) < HEAD_DIM
    zero = jnp.zeros((BLOCK_Q, LANES), BF16)
    first_block = lax.broadcasted_iota(jnp.int32, (1, 2 * BLOCK_Q), 1) < BLOCK_Q
    jobs = []
    for pair in range(n_heads // A_REP // 2):
        ks = k[:, pair * LANES:(pair + 1) * LANES]
        ks_swapped = _swap_halves(ks)
        for which in range(2):
            kv = 2 * pair + which
            slabs = [q[:, (2 * kv + j) * LANES:(2 * kv + j + 1) * LANES] for j in range(2)]
            for parity in range(2):
                q_rows = jnp.concatenate(
                    [jnp.where(low, s_, zero) if parity == 0 else jnp.where(low, zero, s_) for s_ in slabs],
                    axis=0)
                heads = (A_REP * kv + parity, A_REP * kv + 2 + parity)
                sink_row = jnp.where(first_block, sink_ref[heads[0]], sink_ref[heads[1]]) * LOG2E
                jobs.append((ks if parity == which else ks_swapped, q_rows, heads, (kv, kv), sink_row))
    o_t = [None] * n_heads
    for job, (ot, _, _) in zip(jobs, _band_pipeline(jobs, vt1s, bias_ref, variant, s_buf)):
        o_t[job[2][0]], o_t[job[2][1]] = ot[:, :BLOCK_Q], ot[:, BLOCK_Q:]
    for j in range(n_heads // 2):
        pair_t = jnp.concatenate([o_t[2 * j], o_t[2 * j + 1]], axis=0)
        o_ref[0, :, j * LANES:(j + 1) * LANES] = pair_t.T.astype(o_ref.dtype)


def _alibi_slopes(n):
    return [2.0 ** (-8.0 * (i + 1) / n) for i in range(n)]


def _attn_c(qkv, sinks, *, n_heads):
    b, seq, _ = qkv.shape
    kv_heads = n_heads // A_REP
    q_cols = n_heads * HEAD_DIM
    kv_cols = kv_heads * HEAD_DIM
    n_tiles = seq // BLOCK_Q
    k_blk = q_cols // kv_cols
    v_blk = k_blk + 1

    def kv_spec(col_blk, shift):
        return pl.BlockSpec(
            (1, BLOCK_Q, kv_cols),
            lambda bi, i: (bi, jnp.clip(i + shift, 0, n_tiles - 1), col_blk))

    return pl.pallas_call(
        functools.partial(_attn_c_body, n_heads=n_heads, slopes=_alibi_slopes(n_heads)),
        grid=(b, n_tiles),
        in_specs=[pl.BlockSpec(memory_space=pltpu.SMEM),
                  pl.BlockSpec((1, BLOCK_Q, q_cols), lambda bi, i: (bi, i, 0)),
                  kv_spec(k_blk, -1), kv_spec(k_blk, 0), kv_spec(k_blk, 1),
                  kv_spec(v_blk, -1), kv_spec(v_blk, 0), kv_spec(v_blk, 1)],
        out_specs=pl.BlockSpec((1, BLOCK_Q, q_cols), lambda bi, i: (bi, i, 0)),
        out_shape=jax.ShapeDtypeStruct((b, seq, q_cols), BF16),
        scratch_shapes=[pltpu.VMEM((3, n_heads, KEY_SPAN, BLOCK_Q), F32),
                        pltpu.VMEM((2, KEY_SPAN, 2 * BLOCK_Q), F32)],
        compiler_params=_params("parallel", "arbitrary"),
        name="attn_c",
    )(sinks, qkv, qkv, qkv, qkv, qkv, qkv, qkv)


def _attn_b_body(q0_ref, q1_ref, q2_ref, kp_ref, kc_ref, kn_ref, vp_ref, vc_ref, vn_ref, o_ref, lse_ref,
                 bias_ref, s_buf, *, slopes, dil):
    i = pl.program_id(2)
    pl.when(jnp.logical_and(pl.program_id(1) == 0, i == 0))(
        lambda: _fill_band_bias(bias_ref, slopes, BLOCK_Q // 2, float(dil)))
    variant = _tile_variant(i, pl.num_programs(2))
    ks, vt1s = _band_operands(kp_ref, kc_ref, kn_ref, vp_ref, vc_ref, vn_ref)
    low = lax.broadcasted_iota(jnp.int32, (BLOCK_Q, LANES), 1) < HEAD_DIM
    zero = jnp.zeros((BLOCK_Q, LANES), BF16)
    slabs = [q0_ref[0], q1_ref[0], q2_ref[0]]
    heads_all = range(B_HEADS_PER_GROUP)
    plain = [h for h in heads_all if h % 2 == h // B_REP]
    crossed = [h for h in heads_all if h % 2 != h // B_REP]
    ks_swapped = _swap_halves(ks)
    jobs = []
    for kmat, heads in ((ks, plain[:2]), (ks_swapped, crossed), (ks, plain[2:])):
        q_rows = jnp.concatenate(
            [jnp.where(low, slabs[h // 2], zero) if h % 2 == 0 else jnp.where(low, zero, slabs[h // 2])
             for h in heads], axis=0)
        jobs.append((kmat, q_rows, heads, [h // B_REP for h in heads], None))
    o_t, lse_t = {}, {}
    for job, (ot, m, den) in zip(jobs, _band_pipeline(jobs, vt1s, bias_ref, variant, s_buf)):
        heads = job[2]
        lse = (m + jnp.log2(den)) * LN2
        for n, h in enumerate(heads):
            o_t[h] = ot[:, n * BLOCK_Q:(n + 1) * BLOCK_Q]
            lse_t[h] = jnp.broadcast_to(lse[:, n * BLOCK_Q:(n + 1) * BLOCK_Q], (HEAD_DIM, BLOCK_Q))
    for j in range(B_HEADS_PER_GROUP // 2):
        cols = slice(j * LANES, (j + 1) * LANES)
        o_ref[0, :, cols] = jnp.concatenate([o_t[2 * j], o_t[2 * j + 1]], axis=0).T.astype(o_ref.dtype)
        lse_ref[0, :, cols] = jnp.concatenate([lse_t[2 * j], lse_t[2 * j + 1]], axis=0).T


def _attn_b(group_qkv, b):
    n_g = len(B_GROUPS)
    gq = B_HEADS_PER_GROUP * HEAD_DIM
    slabs_per_token = B_GROUP_COLS // LANES
    q_slabs = gq // LANES
    all_slopes = _alibi_slopes(n_g * B_HEADS_PER_GROUP)
    outs, lses = [], []
    for g, (window, dil) in enumerate(B_GROUPS):
        assert (window // 2) // dil == BLOCK_Q // 2
        sub = group_qkv[g].shape[0] // b
        n_tiles = sub // BLOCK_Q
        view = group_qkv[g].reshape(b, sub, dil * B_GROUP_COLS)

        def slab_spec(slab, shift, n_tiles=n_tiles):
            return pl.BlockSpec(
                (1, BLOCK_Q, LANES),
                lambda bi, r, i: (bi, jnp.clip(i + shift, 0, n_tiles - 1), r * slabs_per_token + slab))

        out_map = lambda bi, r, i: (bi, i, r)
        o, lse = pl.pallas_call(
            functools.partial(_attn_b_body, dil=dil,
                              slopes=all_slopes[g * B_HEADS_PER_GROUP:(g + 1) * B_HEADS_PER_GROUP]),
            grid=(b, dil, n_tiles),
            in_specs=[slab_spec(j, 0) for j in range(q_slabs)]
                     + [slab_spec(q_slabs, shift) for shift in (-1, 0, 1)]
                     + [slab_spec(q_slabs + 1, shift) for shift in (-1, 0, 1)],
            out_specs=[pl.BlockSpec((1, BLOCK_Q, gq), out_map), pl.BlockSpec((1, BLOCK_Q, gq), out_map)],
            out_shape=[jax.ShapeDtypeStruct((b, sub, dil * gq), BF16),
                       jax.ShapeDtypeStruct((b, sub, dil * gq), F32)],
            scratch_shapes=[pltpu.VMEM((3, B_HEADS_PER_GROUP, KEY_SPAN, BLOCK_Q), F32),
                            pltpu.VMEM((2, KEY_SPAN, 2 * BLOCK_Q), F32)],
            compiler_params=_params("parallel", "arbitrary", "arbitrary"),
            name=f"attn_b_g{g}",
        )(*([view] * (q_slabs + 6)))
        outs.append(o.reshape(b * sub, dil * gq))
        lses.append(lse.reshape(b * sub, dil * gq))
    return outs, lses


def _mix_plain(refs, w_ref, scratch):
    return jnp.dot(refs[0][...], w_ref[...], preferred_element_type=F32)


def _mix_groups(refs, w_ref, scratch):
    n_g = len(B_GROUPS)
    o_scr, l_scr = scratch
    slabs = o_scr.shape[0] // n_g
    tm = o_scr.shape[1]
    gq = slabs * LANES
    for g, (_, dil) in enumerate(B_GROUPS):
        for r in range(dil):
            rows = pl.ds(r, tm // dil, stride=dil)
            for s in range(slabs):
                cols = slice(r * gq + s * LANES, r * gq + (s + 1) * LANES)
                o_scr[g * slabs + s, rows, :] = refs[g][:, cols].astype(F32)
                l_scr[g * slabs + s, rows, :] = refs[n_g + g][:, cols]
    group = lambda scr, g: jnp.concatenate([scr[g * slabs + s] for s in range(slabs)], axis=1)
    ls = [group(l_scr, g) for g in range(n_g)]
    mx = functools.reduce(jnp.maximum, ls)
    es = [jnp.exp(l - mx) for l in ls]
    inv = 1.0 / functools.reduce(lambda a, b: a + b, es)
    weighted = jnp.concatenate([(group(o_scr, g) * (es[g] * inv)).astype(BF16) for g in range(n_g)], axis=1)
    return jnp.dot(weighted, w_ref[...], preferred_element_type=F32)


def _block_body(*refs, n_mix, mix_fn, final_norm):
    h_ref, mix_refs = refs[0], refs[1:1 + n_mix]
    wo_ref, g_ref, w1_ref, w2_ref, fg_ref, out_ref, hn_ref = refs[1 + n_mix:8 + n_mix]
    mix_scratch = refs[8 + n_mix:]
    j = pl.program_id(1)

    @pl.when(j == 0)
    def _():
        h1 = h_ref[...] + mix_fn(mix_refs, wo_ref, mix_scratch)
        out_ref[...] = h1
        hn_ref[...] = _rms_rows(h1, g_ref[...]).astype(BF16)

    u = jnp.maximum(jnp.dot(hn_ref[...], w1_ref[...], preferred_element_type=F32), 0.0)
    out_ref[...] += jnp.dot((u * u).astype(BF16), w2_ref[...], preferred_element_type=F32)

    if final_norm:
        @pl.when(j == pl.num_programs(1) - 1)
        def _():
            out_ref[...] = _rms_rows(out_ref[...], fg_ref[...])


def _block(h2, mix_inputs, w_o, *, gain, w1, w2, final_gain, final_norm, tm=1024, tf=512):
    t, d = h2.shape
    ff = w1.shape[1]
    row_tile = lambda cols: pl.BlockSpec((tm, cols), lambda i, j: (i, 0))
    whole = lambda a: pl.BlockSpec(a.shape, lambda i, j: (0, 0))
    scratch = [pltpu.VMEM((tm, d), BF16)]
    if len(mix_inputs) == 1:
        mix_fn, mix_specs = _mix_plain, [row_tile(mix_inputs[0].shape[1])]
    else:
        n_g = len(B_GROUPS)
        gq = B_HEADS_PER_GROUP * HEAD_DIM
        mix_fn = _mix_groups
        mix_specs = [pl.BlockSpec((tm // dil, dil * gq), lambda i, j: (i, 0)) for _, dil in B_GROUPS] * 2
        scratch += [pltpu.VMEM((n_g * gq // LANES, tm, LANES), F32)] * 2
    return pl.pallas_call(
        functools.partial(_block_body, n_mix=len(mix_inputs), mix_fn=mix_fn, final_norm=final_norm),
        grid=(t // tm, ff // tf),
        in_specs=[row_tile(d)] + mix_specs
                 + [whole(w_o), whole(gain),
                    pl.BlockSpec((d, tf), lambda i, j: (0, j)),
                    pl.BlockSpec((tf, d), lambda i, j: (j, 0)),
                    whole(final_gain)],
        out_specs=row_tile(d),
        out_shape=jax.ShapeDtypeStruct((t, d), F32),
        scratch_shapes=scratch,
        compiler_params=_params("parallel", "arbitrary"),
        name="block_mlp",
    )(h2, *mix_inputs, w_o, gain, w1, w2, final_gain)


def kernel(x, attn_norm, mlp_norm, a_w_qkv, a_q_gain, a_k_gain, a_w_o, b_w_qkv, b_w_o,
           c_w_qkv, c_sinks, c_w_o, mlp_w1, mlp_w2, final_norm):
    b, seq, d = x.shape
    depth = attn_norm.shape[0]
    h = x.reshape(b * seq, d)
    scale = HEAD_DIM ** -0.5 * LOG2E
    a_heads = a_w_o.shape[1] // HEAD_DIM
    c_heads = c_w_o.shape[1] // HEAD_DIM
    cos, sin = _rope_tables(seq)
    head = jnp.arange(2 * LANES) // HEAD_DIM
    seg = (head[:, None] == head[None, :]).astype(BF16)
    tile4 = lambda g: jnp.tile(g, 2 * LANES // HEAD_DIM)[None, :]
    used = [0, 0, 0]
    for layer in range(depth):
        kind = layer % N_MIXERS
        j = used[kind]
        used[kind] += 1
        gain = attn_norm[layer][None, :]
        block = functools.partial(_block, gain=mlp_norm[layer][None, :], w1=mlp_w1[layer].astype(BF16),
                                  w2=mlp_w2[layer].astype(BF16), final_gain=final_norm[None, :],
                                  final_norm=(layer == depth - 1))
        if kind == 0:
            q_cols = a_heads * HEAD_DIM
            k_cols = (a_w_qkv.shape[2] - q_cols) // 2
            qkv = _qkv_proj(h, gain, a_w_qkv[j].astype(BF16), q_cols=q_cols, q_scale=scale,
                            rope=(tile4(a_q_gain[j]), tile4(a_k_gain[j]), cos, sin, seg, k_cols))
            o = _attn_a(qkv.reshape(b, seq, -1), n_heads=a_heads).reshape(b * seq, q_cols)
            h = block(h, [o], a_w_o[j].astype(BF16))
        elif kind == 1:
            q_cols = len(B_GROUPS) * B_HEADS_PER_GROUP * HEAD_DIM
            groups = _qkv_proj_b(h, gain, b_w_qkv[j].astype(BF16), q_cols=q_cols, q_scale=scale)
            outs, lses = _attn_b(groups, b)
            h = block(h, outs + lses, b_w_o[j].astype(BF16), tm=512)
        else:
            q_cols = c_heads * HEAD_DIM
            qkv = _qkv_proj(h, gain, c_w_qkv[j].astype(BF16), q_cols=q_cols, q_scale=scale)
            o = _attn_c(qkv.reshape(b, seq, -1), c_sinks[j], n_heads=c_heads).reshape(b * seq, q_cols)
            h = block(h, [o], c_w_o[j].astype(BF16))
    return h.reshape(b, seq, d)
```

```python
import functools
import math

import jax
import jax.numpy as jnp
from jax import lax
from jax.experimental import pallas as pl
from jax.experimental.pallas import tpu as pltpu

HEAD_DIM = 64
HALF_HEAD = HEAD_DIM // 2
RMS_EPS = 1e-6
GRID_W = 64
ROPE_THETA = 10000.0
N_MIXERS = 3
A_REP = 4
B_GROUPS = ((128, 1), (512, 4), (2048, 16))
B_HEADS_PER_GROUP = 6
B_KV_PER_GROUP = 2
B_REP = B_HEADS_PER_GROUP // B_KV_PER_GROUP
C_WINDOW = 128
LANES = 128
BF16_SUBLANES = 16
BLOCK_Q = 128
KEY_SPAN = 3 * BLOCK_Q
B_GROUP_COLS = (B_HEADS_PER_GROUP + 2 * B_KV_PER_GROUP) * HEAD_DIM
VT_ROWS = HEAD_DIM + BF16_SUBLANES
V7X_VMEM_LIMIT_BYTES = 48 * 1024 * 1024
NEG_BIG = -1e30
LOG2E = math.log2(math.e)
LN2 = math.log(2.0)

BF16 = jnp.bfloat16
F32 = jnp.float32


def _params(*sem):
    return pltpu.CompilerParams(dimension_semantics=sem, vmem_limit_bytes=V7X_VMEM_LIMIT_BYTES)


def _nt_dot(a, b):
    return lax.dot_general(a, b, (((1,), (1,)), ((), ())), preferred_element_type=F32)


def _rms_rows(x, gain):
    ms = jnp.mean(x * x, axis=-1, keepdims=True)
    return x * lax.rsqrt(ms + RMS_EPS) * gain


def _qkv_body(x_ref, g_ref, w_ref, o_ref, *, q_cols, q_scale):
    hn = _rms_rows(x_ref[...], g_ref[...]).astype(BF16)
    y = jnp.dot(hn, w_ref[...], preferred_element_type=F32)
    o_ref[:, :q_cols] = (y[:, :q_cols] * q_scale).astype(o_ref.dtype)
    o_ref[:, q_cols:] = y[:, q_cols:].astype(o_ref.dtype)


def _qkv_rope_body(x_ref, g_ref, w_ref, qg_ref, kg_ref, cos_ref, sin_ref, seg_ref, o_ref,
                   *, q_cols, k_cols, q_scale):
    hn = _rms_rows(x_ref[...], g_ref[...]).astype(BF16)
    slab = 2 * LANES
    cos = jnp.concatenate([cos_ref[...], cos_ref[...]], axis=1)
    sin = jnp.concatenate([sin_ref[...], sin_ref[...]], axis=1)
    lane = lax.broadcasted_iota(jnp.int32, (x_ref.shape[0], slab), 1)
    first_half = (lane % HALF_HEAD) < (HALF_HEAD // 2)
    seg = seg_ref[...]
    n_qk = (q_cols + k_cols) // slab
    project = lambda s: jnp.dot(hn, w_ref[:, s * slab:(s + 1) * slab], preferred_element_type=F32)
    y_next = project(0)
    for s in range(n_qk):
        is_q = s * slab < q_cols
        ys = y_next
        y_next = project(s + 1) if s + 1 < n_qk else jnp.dot(
            hn, w_ref[:, q_cols + k_cols:], preferred_element_type=F32)
        sq = ys * ys
        hi = sq.astype(BF16)
        lo = (sq - hi.astype(F32)).astype(BF16)
        ss = jnp.dot(hi, seg, preferred_element_type=F32) + jnp.dot(lo, seg, preferred_element_type=F32)
        gain = qg_ref[...] if is_q else kg_ref[...]
        yn = ys * lax.rsqrt(ss * (1.0 / HEAD_DIM) + RMS_EPS) * gain
        partner = jnp.where(first_half,
                            pltpu.roll(yn, slab - HALF_HEAD // 2, axis=1),
                            pltpu.roll(yn, HALF_HEAD // 2, axis=1))
        r = yn * cos + partner * sin
        if is_q:
            r = r * q_scale
        o_ref[:, s * slab:(s + 1) * slab] = r.astype(o_ref.dtype)
    o_ref[:, q_cols + k_cols:] = y_next.astype(o_ref.dtype)


def _qkv_proj(h2, gain, w, *, q_cols, q_scale, tm=512, rope=None):
    t, d = h2.shape
    n = w.shape[1]
    common = dict(
        grid=(t // tm,),
        out_specs=pl.BlockSpec((tm, n), lambda i: (i, 0)),
        out_shape=jax.ShapeDtypeStruct((t, n), BF16),
        compiler_params=_params("parallel"),
    )
    x_spec = pl.BlockSpec((tm, d), lambda i: (i, 0))
    g_spec = pl.BlockSpec((1, d), lambda i: (0, 0))
    w_spec = pl.BlockSpec((d, n), lambda i: (0, 0))
    if rope is None:
        return pl.pallas_call(
            functools.partial(_qkv_body, q_cols=q_cols, q_scale=q_scale),
            in_specs=[x_spec, g_spec, w_spec], name="qkv_proj", **common,
        )(h2, gain, w)
    qg, kg, cos, sin, seg, k_cols = rope
    seq_tiles = cos.shape[0] // tm
    slab = 2 * LANES
    return pl.pallas_call(
        functools.partial(_qkv_rope_body, q_cols=q_cols, k_cols=k_cols, q_scale=q_scale),
        in_specs=[x_spec, g_spec, w_spec,
                  pl.BlockSpec((1, slab), lambda i: (0, 0)),
                  pl.BlockSpec((1, slab), lambda i: (0, 0)),
                  pl.BlockSpec((tm, LANES), lambda i: (i % seq_tiles, 0)),
                  pl.BlockSpec((tm, LANES), lambda i: (i % seq_tiles, 0)),
                  pl.BlockSpec((slab, slab), lambda i: (0, 0))],
        name="qkv_proj_rope", **common,
    )(h2, gain, w, qg, kg, cos, sin, seg)


def _qkv_b_body(x_ref, g_ref, w_ref, *rest, q_cols, q_scale):
    out_refs, y_scr = rest[:-1], rest[-1]
    tm = x_ref.shape[0]
    hn = _rms_rows(x_ref[...], g_ref[...]).astype(BF16)
    y = jnp.dot(hn, w_ref[...], preferred_element_type=F32)
    for s in range(y_scr.shape[0]):
        ys = y[:, s * LANES:(s + 1) * LANES]
        y_scr[s] = ys * q_scale if s * LANES < q_cols else ys
    n_g = len(B_GROUPS)
    q_slabs = B_HEADS_PER_GROUP * HEAD_DIM // LANES
    k0 = q_cols // LANES
    for g, (_, dil) in enumerate(B_GROUPS):
        slabs = list(range(g * q_slabs, (g + 1) * q_slabs)) + [k0 + g, k0 + n_g + g]
        for r in range(dil):
            rows = pl.ds(r, tm // dil, stride=dil)
            piece = jnp.concatenate([y_scr[s, rows, :] for s in slabs], axis=1)
            out_refs[g][:, r * B_GROUP_COLS:(r + 1) * B_GROUP_COLS] = piece.astype(BF16)


def _qkv_proj_b(h2, gain, w, *, q_cols, q_scale, tm=512):
    t, d = h2.shape
    n = w.shape[1]
    return pl.pallas_call(
        functools.partial(_qkv_b_body, q_cols=q_cols, q_scale=q_scale),
        grid=(t // tm,),
        in_specs=[pl.BlockSpec((tm, d), lambda i: (i, 0)),
                  pl.BlockSpec((1, d), lambda i: (0, 0)),
                  pl.BlockSpec((d, n), lambda i: (0, 0))],
        out_specs=[pl.BlockSpec((tm // dil, dil * B_GROUP_COLS), lambda i: (i, 0)) for _, dil in B_GROUPS],
        out_shape=[jax.ShapeDtypeStruct((t // dil, dil * B_GROUP_COLS), BF16) for _, dil in B_GROUPS],
        scratch_shapes=[pltpu.VMEM((n // LANES, tm, LANES), F32)],
        compiler_params=_params("parallel"),
        name="qkv_proj_b",
    )(h2, gain, w)


def _rope_tables(seq):
    t = jnp.arange(seq)
    row = (t // GRID_W).astype(F32)
    col = (t % GRID_W).astype(F32)
    inv_freq = ROPE_THETA ** (-jnp.arange(0, HALF_HEAD, 2, dtype=F32) / HALF_HEAD)
    ang_row = row[:, None] * inv_freq
    ang_col = col[:, None] * inv_freq
    cos = jnp.concatenate([jnp.cos(ang_row)] * 2 + [jnp.cos(ang_col)] * 2, axis=1)
    sin = jnp.concatenate([-jnp.sin(ang_row), jnp.sin(ang_row), -jnp.sin(ang_col), jnp.sin(ang_col)], axis=1)
    return jnp.concatenate([cos, cos], axis=1), jnp.concatenate([sin, sin], axis=1)


def _attn_a_body(q_ref, k_ref, v_ref, o_ref, klo, khi, vt1, s_buf, *, tq, tk, seq, unroll):
    def fill(head_in_high_lanes):
        low = lax.broadcasted_iota(jnp.int32, (seq, LANES), 1) < HEAD_DIM
        zero = jnp.zeros((seq, LANES), BF16)
        x = k_ref[0]
        swapped = jnp.concatenate([x[:, HEAD_DIM:], x[:, :HEAD_DIM]], axis=1)
        in_low, in_high = (swapped, x) if head_in_high_lanes else (x, swapped)
        klo[...] = jnp.where(low, in_low, zero)
        khi[...] = jnp.where(low, zero, in_high)
        ones_row = (lax.broadcasted_iota(jnp.int32, (VT_ROWS - HEAD_DIM, tk), 0) == 0).astype(F32)
        for c in range(seq // tk):
            xt = v_ref[0, c * tk:(c + 1) * tk, :].astype(F32).T
            vt = xt[HEAD_DIM:] if head_in_high_lanes else xt[:HEAD_DIM]
            vt1[:, c * tk:(c + 1) * tk] = jnp.concatenate([vt, ones_row], axis=0).astype(BF16)

    odd = (pl.program_id(1) % 2) == 1
    pl.when(jnp.logical_not(odd))(lambda: fill(False))
    pl.when(odd)(lambda: fill(True))

    cols = 2 * tq
    n_chunks = seq // tk
    n_tiles = seq // tq

    def scores(tile, c, slot):
        q = q_ref[0, pl.ds(pl.multiple_of(tile * tq, tq), tq), :]
        q2 = jnp.concatenate([q[:, :LANES], q[:, LANES:]], axis=0)
        ks = pl.multiple_of(c * tk, tk)
        cms = []
        for par, k_s in enumerate((klo, khi)):
            s = _nt_dot(k_s[pl.ds(ks, tk), :], q2)
            s_buf[slot, par] = s
            cms.append(jnp.max(s, axis=0, keepdims=True))
        return tuple(cms)

    def accumulate(c, slot, cms, state):
        vt = vt1[:, pl.ds(pl.multiple_of(c * tk, tk), tk)]
        new = []
        for par in range(2):
            m, acc = state[2 * par], state[2 * par + 1]
            mn = jnp.maximum(m, cms[par])
            p = jnp.exp2(s_buf[slot, par] - mn)
            new += [mn, acc * jnp.exp2(m - mn) + jnp.dot(vt, p.astype(BF16), preferred_element_type=F32)]
        return tuple(new)

    neg = jnp.full((1, cols), -jnp.inf, F32)
    acc0 = jnp.zeros((VT_ROWS, cols), F32)

    steps_per_tile = n_chunks // unroll

    def step(j, carry):
        tile, c0 = j // steps_per_tile, (j % steps_per_tile) * unroll
        cms, (m_e, acc_e, m_o, acc_o) = carry[:2], carry[2:]
        fresh = c0 == 0
        state = (jnp.where(fresh, neg, m_e), acc_e, jnp.where(fresh, neg, m_o), acc_o)
        for u in range(unroll):
            nxt = j * unroll + u + 1
            cms_next = scores(jnp.minimum(nxt // n_chunks, n_tiles - 1), nxt % n_chunks, (u + 1) % 2)
            state = accumulate(c0 + u, u % 2, cms, state)
            cms = cms_next

        @pl.when(c0 + unroll == n_chunks)
        def _():
            out = jnp.concatenate(
                [acc[:HEAD_DIM] / acc[HEAD_DIM:HEAD_DIM + 1] for acc in (state[1], state[3])], axis=0)
            o_ref[0, pl.ds(pl.multiple_of(tile * tq, tq), tq), :] = jnp.concatenate(
                [out[:, :tq].T, out[:, tq:].T], axis=1).astype(o_ref.dtype)

        return cms + state

    lax.fori_loop(0, n_tiles * steps_per_tile, step, scores(0, 0, 0) + (neg, acc0, neg, acc0))


def _attn_a(qkv, *, n_heads, tq=256, tk=512, unroll=8):
    b, seq, _ = qkv.shape
    kv_heads = n_heads // A_REP
    q_cols = n_heads * HEAD_DIM
    q_w = A_REP * HEAD_DIM
    k_blk0 = q_cols // LANES
    v_blk0 = (q_cols + kv_heads * HEAD_DIM) // LANES
    assert unroll % 2 == 0 and (seq // tk) % unroll == 0 and seq % tq == 0
    return pl.pallas_call(
        functools.partial(_attn_a_body, tq=tq, tk=tk, seq=seq, unroll=unroll),
        grid=(b, kv_heads),
        in_specs=[pl.BlockSpec((1, seq, q_w), lambda bi, h: (bi, 0, h)),
                  pl.BlockSpec((1, seq, LANES), lambda bi, h: (bi, 0, k_blk0 + h // 2)),
                  pl.BlockSpec((1, seq, LANES), lambda bi, h: (bi, 0, v_blk0 + h // 2))],
        out_specs=pl.BlockSpec((1, seq, q_w), lambda bi, h: (bi, 0, h)),
        out_shape=jax.ShapeDtypeStruct((b, seq, q_cols), BF16),
        scratch_shapes=[pltpu.VMEM((seq, LANES), BF16), pltpu.VMEM((seq, LANES), BF16),
                        pltpu.VMEM((VT_ROWS, seq), BF16), pltpu.VMEM((2, 2, tk, 2 * tq), F32)],
        compiler_params=_params("parallel", "arbitrary"),
        name="attn_a",
    )(qkv, qkv, qkv)


def _fill_band_bias(bias_ref, slopes, half_window, dist_scale):
    c = lax.broadcasted_iota(jnp.int32, (KEY_SPAN, BLOCK_Q), 0)
    r = lax.broadcasted_iota(jnp.int32, (KEY_SPAN, BLOCK_Q), 1)
    dist = jnp.abs(c - BLOCK_Q - r)
    inside = dist <= half_window
    penalty = dist.astype(F32) * (dist_scale * LOG2E)
    for variant, ok in enumerate((inside & (c >= BLOCK_Q), inside, inside & (c < 2 * BLOCK_Q))):
        for h, slope in enumerate(slopes):
            bias_ref[variant, h] = jnp.where(ok, -slope * penalty, NEG_BIG)


def _tile_variant(i, n_tiles):
    return jnp.where(i == 0, 0, jnp.where(i == n_tiles - 1, 2, 1))


def _band_pipeline(jobs, vt1s, bias_ref, variant, s_buf):
    def scores(job, slot):
        kmat, q_rows, head_ids, _, sink_row = job
        width = len(head_ids) * BLOCK_Q
        s = _nt_dot(kmat, q_rows) + jnp.concatenate([bias_ref[variant, h] for h in head_ids], axis=1)
        s_buf[slot, :, :width] = s
        m = jnp.max(s, axis=0, keepdims=True)
        return m if sink_row is None else jnp.maximum(m, sink_row)

    def values(job, slot, m):
        _, _, head_ids, kv_ids, sink_row = job
        n = len(head_ids)
        p = jnp.exp2(s_buf[slot, :, :n * BLOCK_Q] - m).astype(BF16)
        accs, start = [], 0
        while start < n:
            stop = start
            while stop < n and kv_ids[stop] == kv_ids[start]:
                stop += 1
            accs.append(jnp.dot(vt1s[kv_ids[start]], p[:, start * BLOCK_Q:stop * BLOCK_Q],
                                preferred_element_type=F32))
            start = stop
        acc = accs[0] if len(accs) == 1 else jnp.concatenate(accs, axis=1)
        den = acc[HEAD_DIM:HEAD_DIM + 1]
        if sink_row is not None:
            den = den + jnp.exp2(sink_row - m)
        return acc[:HEAD_DIM] / den, m, den

    results = []
    m_next = scores(jobs[0], 0)
    for c, job in enumerate(jobs):
        m_cur = m_next
        if c + 1 < len(jobs):
            m_next = scores(jobs[c + 1], (c + 1) % 2)
        results.append(values(job, c % 2, m_cur))
    return results


def _band_operands(kp_ref, kc_ref, kn_ref, vp_ref, vc_ref, vn_ref):
    k = jnp.concatenate([kp_ref[0], kc_ref[0], kn_ref[0]], axis=0)
    vt = jnp.concatenate([vp_ref[0], vc_ref[0], vn_ref[0]], axis=0).astype(F32).T
    ones_row = (lax.broadcasted_iota(jnp.int32, (VT_ROWS - HEAD_DIM, KEY_SPAN), 0) == 0).astype(F32)
    vt1s = [jnp.concatenate([vt[kv * HEAD_DIM:(kv + 1) * HEAD_DIM], ones_row], axis=0).astype(BF16)
            for kv in range(vt.shape[0] // HEAD_DIM)]
    return k, vt1s


def _swap_halves(x):
    return jnp.concatenate([x[:, HEAD_DIM:], x[:, :HEAD_DIM]], axis=1)


def _attn_c_body(sink_ref, q_ref, kp_ref, kc_ref, kn_ref, vp_ref, vc_ref, vn_ref, o_ref, bias_ref, s_buf,
                 *, n_heads, slopes):
    i = pl.program_id(1)
    pl.when(i == 0)(lambda: _fill_band_bias(bias_ref, slopes, C_WINDOW, 1.0))
    variant = _tile_variant(i, pl.num_programs(1))
    q = q_ref[0]
    k, vt1s = _band_operands(kp_ref, kc_ref, kn_ref, vp_ref, vc_ref, vn_ref)
    low = lax.broadcasted_iota(jnp.int32, (BLOCK_Q, LANES), 1) < HEAD_DIM
    zero = jnp.zeros((BLOCK_Q, LANES), BF16)
    first_block = lax.broadcasted_iota(jnp.int32, (1, 2 * BLOCK_Q), 1) < BLOCK_Q
    jobs = []
    for pair in range(n_heads // A_REP // 2):
        ks = k[:, pair * LANES:(pair + 1) * LANES]
        ks_swapped = _swap_halves(ks)
        for which in range(2):
            kv = 2 * pair + which
            slabs = [q[:, (2 * kv + j) * LANES:(2 * kv + j + 1) * LANES] for j in range(2)]
            for parity in range(2):
                q_rows = jnp.concatenate(
                    [jnp.where(low, s_, zero) if parity == 0 else jnp.where(low, zero, s_) for s_ in slabs],
                    axis=0)
                heads = (A_REP * kv + parity, A_REP * kv + 2 + parity)
                sink_row = jnp.where(first_block, sink_ref[heads[0]], sink_ref[heads[1]]) * LOG2E
                jobs.append((ks if parity == which else ks_swapped, q_rows, heads, (kv, kv), sink_row))
    o_t = [None] * n_heads
    for job, (ot, _, _) in zip(jobs, _band_pipeline(jobs, vt1s, bias_ref, variant, s_buf)):
        o_t[job[2][0]], o_t[job[2][1]] = ot[:, :BLOCK_Q], ot[:, BLOCK_Q:]
    for j in range(n_heads // 2):
        pair_t = jnp.concatenate([o_t[2 * j], o_t[2 * j + 1]], axis=0)
        o_ref[0, :, j * LANES:(j + 1) * LANES] = pair_t.T.astype(o_ref.dtype)


def _alibi_slopes(n):
    return [2.0 ** (-8.0 * (i + 1) / n) for i in range(n)]


def _attn_c(qkv, sinks, *, n_heads):
    b, seq, _ = qkv.shape
    kv_heads = n_heads // A_REP
    q_cols = n_heads * HEAD_DIM
    kv_cols = kv_heads * HEAD_DIM
    n_tiles = seq // BLOCK_Q
    k_blk = q_cols // kv_cols
    v_blk = k_blk + 1

    def kv_spec(col_blk, shift):
        return pl.BlockSpec(
            (1, BLOCK_Q, kv_cols),
            lambda bi, i: (bi, jnp.clip(i + shift, 0, n_tiles - 1), col_blk))

    return pl.pallas_call(
        functools.partial(_attn_c_body, n_heads=n_heads, slopes=_alibi_slopes(n_heads)),
        grid=(b, n_tiles),
        in_specs=[pl.BlockSpec(memory_space=pltpu.SMEM),
                  pl.BlockSpec((1, BLOCK_Q, q_cols), lambda bi, i: (bi, i, 0)),
                  kv_spec(k_blk, -1), kv_spec(k_blk, 0), kv_spec(k_blk, 1),
                  kv_spec(v_blk, -1), kv_spec(v_blk, 0), kv_spec(v_blk, 1)],
        out_specs=pl.BlockSpec((1, BLOCK_Q, q_cols), lambda bi, i: (bi, i, 0)),
        out_shape=jax.ShapeDtypeStruct((b, seq, q_cols), BF16),
        scratch_shapes=[pltpu.VMEM((3, n_heads, KEY_SPAN, BLOCK_Q), F32),
                        pltpu.VMEM((2, KEY_SPAN, 2 * BLOCK_Q), F32)],
        compiler_params=_params("parallel", "arbitrary"),
        name="attn_c",
    )(sinks, qkv, qkv, qkv, qkv, qkv, qkv, qkv)


def _attn_b_body(q0_ref, q1_ref, q2_ref, kp_ref, kc_ref, kn_ref, vp_ref, vc_ref, vn_ref, o_ref, lse_ref,
                 bias_ref, s_buf, *, slopes, dil):
    i = pl.program_id(2)
    pl.when(jnp.logical_and(pl.program_id(1) == 0, i == 0))(
        lambda: _fill_band_bias(bias_ref, slopes, BLOCK_Q // 2, float(dil)))
    variant = _tile_variant(i, pl.num_programs(2))
    ks, vt1s = _band_operands(kp_ref, kc_ref, kn_ref, vp_ref, vc_ref, vn_ref)
    low = lax.broadcasted_iota(jnp.int32, (BLOCK_Q, LANES), 1) < HEAD_DIM
    zero = jnp.zeros((BLOCK_Q, LANES), BF16)
    slabs = [q0_ref[0], q1_ref[0], q2_ref[0]]
    heads_all = range(B_HEADS_PER_GROUP)
    plain = [h for h in heads_all if h % 2 == h // B_REP]
    crossed = [h for h in heads_all if h % 2 != h // B_REP]
    ks_swapped = _swap_halves(ks)
    jobs = []
    for kmat, heads in ((ks, plain[:2]), (ks_swapped, crossed), (ks, plain[2:])):
        q_rows = jnp.concatenate(
            [jnp.where(low, slabs[h // 2], zero) if h % 2 == 0 else jnp.where(low, zero, slabs[h // 2])
             for h in heads], axis=0)
        jobs.append((kmat, q_rows, heads, [h // B_REP for h in heads], None))
    o_t, lse_t = {}, {}
    for job, (ot, m, den) in zip(jobs, _band_pipeline(jobs, vt1s, bias_ref, variant, s_buf)):
        heads = job[2]
        lse = (m + jnp.log2(den)) * LN2
        for n, h in enumerate(heads):
            o_t[h] = ot[:, n * BLOCK_Q:(n + 1) * BLOCK_Q]
            lse_t[h] = jnp.broadcast_to(lse[:, n * BLOCK_Q:(n + 1) * BLOCK_Q], (HEAD_DIM, BLOCK_Q))
    for j in range(B_HEADS_PER_GROUP // 2):
        cols = slice(j * LANES, (j + 1) * LANES)
        o_ref[0, :, cols] = jnp.concatenate([o_t[2 * j], o_t[2 * j + 1]], axis=0).T.astype(o_ref.dtype)
        lse_ref[0, :, cols] = jnp.concatenate([lse_t[2 * j], lse_t[2 * j + 1]], axis=0).T


def _attn_b(group_qkv, b):
    n_g = len(B_GROUPS)
    gq = B_HEADS_PER_GROUP * HEAD_DIM
    slabs_per_token = B_GROUP_COLS // LANES
    q_slabs = gq // LANES
    all_slopes = _alibi_slopes(n_g * B_HEADS_PER_GROUP)
    outs, lses = [], []
    for g, (window, dil) in enumerate(B_GROUPS):
        assert (window // 2) // dil == BLOCK_Q // 2
        sub = group_qkv[g].shape[0] // b
        n_tiles = sub // BLOCK_Q
        view = group_qkv[g].reshape(b, sub, dil * B_GROUP_COLS)

        def slab_spec(slab, shift, n_tiles=n_tiles):
            return pl.BlockSpec(
                (1, BLOCK_Q, LANES),
                lambda bi, r, i: (bi, jnp.clip(i + shift, 0, n_tiles - 1), r * slabs_per_token + slab))

        out_map = lambda bi, r, i: (bi, i, r)
        o, lse = pl.pallas_call(
            functools.partial(_attn_b_body, dil=dil,
                              slopes=all_slopes[g * B_HEADS_PER_GROUP:(g + 1) * B_HEADS_PER_GROUP]),
            grid=(b, dil, n_tiles),
            in_specs=[slab_spec(j, 0) for j in range(q_slabs)]
                     + [slab_spec(q_slabs, shift) for shift in (-1, 0, 1)]
                     + [slab_spec(q_slabs + 1, shift) for shift in (-1, 0, 1)],
            out_specs=[pl.BlockSpec((1, BLOCK_Q, gq), out_map), pl.BlockSpec((1, BLOCK_Q, gq), out_map)],
            out_shape=[jax.ShapeDtypeStruct((b, sub, dil * gq), BF16),
                       jax.ShapeDtypeStruct((b, sub, dil * gq), F32)],
            scratch_shapes=[pltpu.VMEM((3, B_HEADS_PER_GROUP, KEY_SPAN, BLOCK_Q), F32),
                            pltpu.VMEM((2, KEY_SPAN, 2 * BLOCK_Q), F32)],
            compiler_params=_params("parallel", "arbitrary", "arbitrary"),
            name=f"attn_b_g{g}",
        )(*([view] * (q_slabs + 6)))
        outs.append(o.reshape(b * sub, dil * gq))
        lses.append(lse.reshape(b * sub, dil * gq))
    return outs, lses


PROLOGUE_ROW_CHUNKS = 4


def _mix_plain(refs, scratch):
    return lambda rows: refs[0][rows, :]


def _mix_groups(refs, scratch):
    n_g = len(B_GROUPS)
    o_scr, l_scr = scratch
    slabs = o_scr.shape[0] // n_g
    tm = o_scr.shape[1]
    gq = slabs * LANES
    for g, (_, dil) in enumerate(B_GROUPS):
        for r in range(dil):
            rows = pl.ds(r, tm // dil, stride=dil)
            for s in range(slabs):
                cols = slice(r * gq + s * LANES, r * gq + (s + 1) * LANES)
                o_scr[g * slabs + s, rows, :] = refs[g][:, cols].astype(F32)
                l_scr[g * slabs + s, rows, :] = refs[n_g + g][:, cols]
    def lhs(rows):
        group = lambda scr, g: jnp.concatenate([scr[g * slabs + s, rows, :] for s in range(slabs)], axis=1)
        ls = [group(l_scr, g) for g in range(n_g)]
        mx = functools.reduce(jnp.maximum, ls)
        es = [jnp.exp(l - mx) for l in ls]
        inv = 1.0 / functools.reduce(lambda a, b: a + b, es)
        return jnp.concatenate([(group(o_scr, g) * (es[g] * inv)).astype(BF16) for g in range(n_g)], axis=1)

    return lhs


def _block_body(*refs, n_mix, mix_fn, final_norm):
    h_ref, mix_refs = refs[0], refs[1:1 + n_mix]
    wo_ref, g_ref, w1_ref, w2_ref, fg_ref, out_ref, hn_ref = refs[1 + n_mix:8 + n_mix]
    mix_scratch = refs[8 + n_mix:]
    j = pl.program_id(1)

    @pl.when(j == 0)
    def _():
        lhs = mix_fn(mix_refs, mix_scratch)
        chunk = h_ref.shape[0] // PROLOGUE_ROW_CHUNKS
        rows = [slice(c * chunk, (c + 1) * chunk) for c in range(PROLOGUE_ROW_CHUNKS)]
        project = lambda c: jnp.dot(lhs(rows[c]), wo_ref[...], preferred_element_type=F32)
        nxt = project(0)
        for c in range(PROLOGUE_ROW_CHUNKS):
            cur = nxt
            if c + 1 < PROLOGUE_ROW_CHUNKS:
                nxt = project(c + 1)
            h1 = h_ref[rows[c], :] + cur
            out_ref[rows[c], :] = h1
            hn_ref[rows[c], :] = _rms_rows(h1, g_ref[...]).astype(BF16)

    u = jnp.maximum(jnp.dot(hn_ref[...], w1_ref[...], preferred_element_type=F32), 0.0)
    out_ref[...] += jnp.dot((u * u).astype(BF16), w2_ref[...], preferred_element_type=F32)

    if final_norm:
        @pl.when(j == pl.num_programs(1) - 1)
        def _():
            out_ref[...] = _rms_rows(out_ref[...], fg_ref[...])


def _block(h2, mix_inputs, w_o, *, gain, w1, w2, final_gain, final_norm, tm=1024, tf=512):
    t, d = h2.shape
    ff = w1.shape[1]
    once = pl.Buffered(1)
    row_tile = lambda cols: pl.BlockSpec((tm, cols), lambda i, j: (i, 0))
    whole = lambda a: pl.BlockSpec(a.shape, lambda i, j: (0, 0), pipeline_mode=once)
    scratch = [pltpu.VMEM((tm, d), BF16)]
    if len(mix_inputs) == 1:
        mix_fn, mix_specs = _mix_plain, [row_tile(mix_inputs[0].shape[1])]
    else:
        n_g = len(B_GROUPS)
        gq = B_HEADS_PER_GROUP * HEAD_DIM
        mix_fn = _mix_groups
        mix_specs = [pl.BlockSpec((tm // dil, dil * gq), lambda i, j: (i, 0), pipeline_mode=once)
                     for _, dil in B_GROUPS] * 2
        scratch += [pltpu.VMEM((n_g * gq // LANES, tm, LANES), F32)] * 2
    return pl.pallas_call(
        functools.partial(_block_body, n_mix=len(mix_inputs), mix_fn=mix_fn, final_norm=final_norm),
        grid=(t // tm, ff // tf),
        in_specs=[row_tile(d)] + mix_specs
                 + [whole(w_o), whole(gain),
                    pl.BlockSpec((d, tf), lambda i, j: (0, j)),
                    pl.BlockSpec((tf, d), lambda i, j: (j, 0)),
                    whole(final_gain)],
        out_specs=row_tile(d),
        out_shape=jax.ShapeDtypeStruct((t, d), F32),
        scratch_shapes=scratch,
        compiler_params=_params("parallel", "arbitrary"),
        name="block_mlp",
    )(h2, *mix_inputs, w_o, gain, w1, w2, final_gain)


def kernel(x, attn_norm, mlp_norm, a_w_qkv, a_q_gain, a_k_gain, a_w_o, b_w_qkv, b_w_o,
           c_w_qkv, c_sinks, c_w_o, mlp_w1, mlp_w2, final_norm):
    b, seq, d = x.shape
    depth = attn_norm.shape[0]
    h = x.reshape(b * seq, d)
    scale = HEAD_DIM ** -0.5 * LOG2E
    a_heads = a_w_o.shape[1] // HEAD_DIM
    c_heads = c_w_o.shape[1] // HEAD_DIM
    cos, sin = _rope_tables(seq)
    head = jnp.arange(2 * LANES) // HEAD_DIM
    seg = (head[:, None] == head[None, :]).astype(BF16)
    tile4 = lambda g: jnp.tile(g, 2 * LANES // HEAD_DIM)[None, :]
    used = [0, 0, 0]
    for layer in range(depth):
        kind = layer % N_MIXERS
        j = used[kind]
        used[kind] += 1
        gain = attn_norm[layer][None, :]
        block = functools.partial(_block, gain=mlp_norm[layer][None, :], w1=mlp_w1[layer].astype(BF16),
                                  w2=mlp_w2[layer].astype(BF16), final_gain=final_norm[None, :],
                                  final_norm=(layer == depth - 1))
        if kind == 0:
            q_cols = a_heads * HEAD_DIM
            k_cols = (a_w_qkv.shape[2] - q_cols) // 2
            qkv = _qkv_proj(h, gain, a_w_qkv[j].astype(BF16), q_cols=q_cols, q_scale=scale,
                            rope=(tile4(a_q_gain[j]), tile4(a_k_gain[j]), cos, sin, seg, k_cols))
            o = _attn_a(qkv.reshape(b, seq, -1), n_heads=a_heads).reshape(b * seq, q_cols)
            h = block(h, [o], a_w_o[j].astype(BF16))
        elif kind == 1:
            q_cols = len(B_GROUPS) * B_HEADS_PER_GROUP * HEAD_DIM
            groups = _qkv_proj_b(h, gain, b_w_qkv[j].astype(BF16), q_cols=q_cols, q_scale=scale)
            outs, lses = _attn_b(groups, b)
            h = block(h, outs + lses, b_w_o[j].astype(BF16))
        else:
            q_cols = c_heads * HEAD_DIM
            qkv = _qkv_proj(h, gain, c_w_qkv[j].astype(BF16), q_cols=q_cols, q_scale=scale)
            o = _attn_c(qkv.reshape(b, seq, -1), c_sinks[j], n_heads=c_heads).reshape(b * seq, q_cols)
            h = block(h, [o], c_w_o[j].astype(BF16))
    return h.reshape(b, seq, d)
```

```python
import functools
import math

import jax
import jax.numpy as jnp
from jax import lax
from jax.experimental import pallas as pl
from jax.experimental.pallas import tpu as pltpu

HEAD_DIM = 64
HALF_HEAD = HEAD_DIM // 2
RMS_EPS = 1e-6
GRID_W = 64
ROPE_THETA = 10000.0
N_MIXERS = 3
A_REP = 4
B_GROUPS = ((128, 1), (512, 4), (2048, 16))
B_HEADS_PER_GROUP = 6
B_KV_PER_GROUP = 2
B_REP = B_HEADS_PER_GROUP // B_KV_PER_GROUP
C_WINDOW = 128
LANES = 128
BF16_SUBLANES = 16
BLOCK_Q = 128
B_HALF = BLOCK_Q // 2
B_TILE = 2 * BLOCK_Q
B_SPAN = BLOCK_Q + 2 * B_HALF
B_GROUP_COLS = (B_HEADS_PER_GROUP + 2 * B_KV_PER_GROUP) * HEAD_DIM
VT_ROWS = HEAD_DIM + BF16_SUBLANES
V7X_VMEM_LIMIT_BYTES = 48 * 1024 * 1024
NEG_BIG = -1e30
LOG2E = math.log2(math.e)
LN2 = math.log(2.0)

BF16 = jnp.bfloat16
F32 = jnp.float32


def _params(*sem):
    return pltpu.CompilerParams(dimension_semantics=sem, vmem_limit_bytes=V7X_VMEM_LIMIT_BYTES)


def _nt_dot(a, b):
    return lax.dot_general(a, b, (((1,), (1,)), ((), ())), preferred_element_type=F32)


def _rms_rows(x, gain):
    ms = jnp.mean(x * x, axis=-1, keepdims=True)
    return x * lax.rsqrt(ms + RMS_EPS) * gain


def _qkv_body(x_ref, g_ref, w_ref, o_ref, *, q_cols, q_scale):
    hn = _rms_rows(x_ref[...], g_ref[...]).astype(BF16)
    y = jnp.dot(hn, w_ref[...], preferred_element_type=F32)
    o_ref[:, :q_cols] = (y[:, :q_cols] * q_scale).astype(o_ref.dtype)
    o_ref[:, q_cols:] = y[:, q_cols:].astype(o_ref.dtype)


def _qkv_rope_body(x_ref, g_ref, w_ref, qg_ref, kg_ref, cos_ref, sin_ref, seg_ref, o_ref,
                   *, q_cols, k_cols, q_scale):
    hn = _rms_rows(x_ref[...], g_ref[...]).astype(BF16)
    slab = 2 * LANES
    cos = jnp.concatenate([cos_ref[...], cos_ref[...]], axis=1)
    sin = jnp.concatenate([sin_ref[...], sin_ref[...]], axis=1)
    lane = lax.broadcasted_iota(jnp.int32, (x_ref.shape[0], slab), 1)
    first_half = (lane % HALF_HEAD) < (HALF_HEAD // 2)
    seg = seg_ref[...]
    n_qk = (q_cols + k_cols) // slab
    project = lambda s: jnp.dot(hn, w_ref[:, s * slab:(s + 1) * slab], preferred_element_type=F32)
    y_next = project(0)
    for s in range(n_qk):
        is_q = s * slab < q_cols
        ys = y_next
        y_next = project(s + 1) if s + 1 < n_qk else jnp.dot(
            hn, w_ref[:, q_cols + k_cols:], preferred_element_type=F32)
        sq = ys * ys
        hi = sq.astype(BF16)
        lo = (sq - hi.astype(F32)).astype(BF16)
        ss = jnp.dot(hi, seg, preferred_element_type=F32) + jnp.dot(lo, seg, preferred_element_type=F32)
        gain = qg_ref[...] if is_q else kg_ref[...]
        yn = ys * lax.rsqrt(ss * (1.0 / HEAD_DIM) + RMS_EPS) * gain
        partner = jnp.where(first_half,
                            pltpu.roll(yn, slab - HALF_HEAD // 2, axis=1),
                            pltpu.roll(yn, HALF_HEAD // 2, axis=1))
        r = yn * cos + partner * sin
        if is_q:
            r = r * q_scale
        o_ref[:, s * slab:(s + 1) * slab] = r.astype(o_ref.dtype)
    o_ref[:, q_cols + k_cols:] = y_next.astype(o_ref.dtype)


def _qkv_proj(h2, gain, w, *, q_cols, q_scale, tm=512, rope=None):
    t, d = h2.shape
    n = w.shape[1]
    common = dict(
        grid=(t // tm,),
        out_specs=pl.BlockSpec((tm, n), lambda i: (i, 0)),
        out_shape=jax.ShapeDtypeStruct((t, n), BF16),
        compiler_params=_params("parallel"),
    )
    x_spec = pl.BlockSpec((tm, d), lambda i: (i, 0))
    g_spec = pl.BlockSpec((1, d), lambda i: (0, 0))
    w_spec = pl.BlockSpec((d, n), lambda i: (0, 0))
    if rope is None:
        return pl.pallas_call(
            functools.partial(_qkv_body, q_cols=q_cols, q_scale=q_scale),
            in_specs=[x_spec, g_spec, w_spec], name="qkv_proj", **common,
        )(h2, gain, w)
    qg, kg, cos, sin, seg, k_cols = rope
    seq_tiles = cos.shape[0] // tm
    slab = 2 * LANES
    return pl.pallas_call(
        functools.partial(_qkv_rope_body, q_cols=q_cols, k_cols=k_cols, q_scale=q_scale),
        in_specs=[x_spec, g_spec, w_spec,
                  pl.BlockSpec((1, slab), lambda i: (0, 0)),
                  pl.BlockSpec((1, slab), lambda i: (0, 0)),
                  pl.BlockSpec((tm, LANES), lambda i: (i % seq_tiles, 0)),
                  pl.BlockSpec((tm, LANES), lambda i: (i % seq_tiles, 0)),
                  pl.BlockSpec((slab, slab), lambda i: (0, 0))],
        name="qkv_proj_rope", **common,
    )(h2, gain, w, qg, kg, cos, sin, seg)


def _qkv_b_body(x_ref, g_ref, w_ref, *rest, q_cols, q_scale):
    out_refs, y_scr = rest[:-1], rest[-1]
    tm = x_ref.shape[0]
    hn = _rms_rows(x_ref[...], g_ref[...]).astype(BF16)
    y = jnp.dot(hn, w_ref[...], preferred_element_type=F32)
    for s in range(y_scr.shape[0]):
        ys = y[:, s * LANES:(s + 1) * LANES]
        y_scr[s] = ys * q_scale if s * LANES < q_cols else ys
    n_g = len(B_GROUPS)
    q_slabs = B_HEADS_PER_GROUP * HEAD_DIM // LANES
    k0 = q_cols // LANES
    for g, (_, dil) in enumerate(B_GROUPS):
        slabs = list(range(g * q_slabs, (g + 1) * q_slabs)) + [k0 + g, k0 + n_g + g]
        for r in range(dil):
            rows = pl.ds(r, tm // dil, stride=dil)
            piece = jnp.concatenate([y_scr[s, rows, :] for s in slabs], axis=1)
            out_refs[g][:, r * B_GROUP_COLS:(r + 1) * B_GROUP_COLS] = piece.astype(BF16)


def _qkv_proj_b(h2, gain, w, *, q_cols, q_scale, tm=512):
    t, d = h2.shape
    n = w.shape[1]
    return pl.pallas_call(
        functools.partial(_qkv_b_body, q_cols=q_cols, q_scale=q_scale),
        grid=(t // tm,),
        in_specs=[pl.BlockSpec((tm, d), lambda i: (i, 0)),
                  pl.BlockSpec((1, d), lambda i: (0, 0)),
                  pl.BlockSpec((d, n), lambda i: (0, 0))],
        out_specs=[pl.BlockSpec((tm // dil, dil * B_GROUP_COLS), lambda i: (i, 0)) for _, dil in B_GROUPS],
        out_shape=[jax.ShapeDtypeStruct((t // dil, dil * B_GROUP_COLS), BF16) for _, dil in B_GROUPS],
        scratch_shapes=[pltpu.VMEM((n // LANES, tm, LANES), F32)],
        compiler_params=_params("parallel"),
        name="qkv_proj_b",
    )(h2, gain, w)


def _rope_tables(seq):
    t = jnp.arange(seq)
    row = (t // GRID_W).astype(F32)
    col = (t % GRID_W).astype(F32)
    inv_freq = ROPE_THETA ** (-jnp.arange(0, HALF_HEAD, 2, dtype=F32) / HALF_HEAD)
    ang_row = row[:, None] * inv_freq
    ang_col = col[:, None] * inv_freq
    cos = jnp.concatenate([jnp.cos(ang_row)] * 2 + [jnp.cos(ang_col)] * 2, axis=1)
    sin = jnp.concatenate([-jnp.sin(ang_row), jnp.sin(ang_row), -jnp.sin(ang_col), jnp.sin(ang_col)], axis=1)
    return jnp.concatenate([cos, cos], axis=1), jnp.concatenate([sin, sin], axis=1)


def _attn_a_body(q_ref, k_ref, v_ref, o_ref, klo, khi, vt1, s_buf, *, tq, tk, seq, unroll):
    def fill(head_in_high_lanes):
        low = lax.broadcasted_iota(jnp.int32, (seq, LANES), 1) < HEAD_DIM
        zero = jnp.zeros((seq, LANES), BF16)
        x = k_ref[0]
        swapped = jnp.concatenate([x[:, HEAD_DIM:], x[:, :HEAD_DIM]], axis=1)
        in_low, in_high = (swapped, x) if head_in_high_lanes else (x, swapped)
        klo[...] = jnp.where(low, in_low, zero)
        khi[...] = jnp.where(low, zero, in_high)
        ones_row = (lax.broadcasted_iota(jnp.int32, (VT_ROWS - HEAD_DIM, tk), 0) == 0).astype(F32)
        for c in range(seq // tk):
            xt = v_ref[0, c * tk:(c + 1) * tk, :].astype(F32).T
            vt = xt[HEAD_DIM:] if head_in_high_lanes else xt[:HEAD_DIM]
            vt1[:, c * tk:(c + 1) * tk] = jnp.concatenate([vt, ones_row], axis=0).astype(BF16)

    odd = (pl.program_id(1) % 2) == 1
    pl.when(jnp.logical_not(odd))(lambda: fill(False))
    pl.when(odd)(lambda: fill(True))

    cols = 2 * tq
    n_chunks = seq // tk
    n_tiles = seq // tq

    def scores(tile, c, slot):
        q = q_ref[0, pl.ds(pl.multiple_of(tile * tq, tq), tq), :]
        q2 = jnp.concatenate([q[:, :LANES], q[:, LANES:]], axis=0)
        ks = pl.multiple_of(c * tk, tk)
        cms = []
        for par, k_s in enumerate((klo, khi)):
            s = _nt_dot(k_s[pl.ds(ks, tk), :], q2)
            s_buf[slot, par] = s
            cms.append(jnp.max(s, axis=0, keepdims=True))
        return tuple(cms)

    def accumulate(c, slot, cms, state):
        vt = vt1[:, pl.ds(pl.multiple_of(c * tk, tk), tk)]
        new = []
        for par in range(2):
            m, acc = state[2 * par], state[2 * par + 1]
            mn = jnp.maximum(m, cms[par])
            p = jnp.exp2(s_buf[slot, par] - mn)
            new += [mn, acc * jnp.exp2(m - mn) + jnp.dot(vt, p.astype(BF16), preferred_element_type=F32)]
        return tuple(new)

    neg = jnp.full((1, cols), -jnp.inf, F32)
    acc0 = jnp.zeros((VT_ROWS, cols), F32)

    steps_per_tile = n_chunks // unroll

    def step(j, carry):
        tile, c0 = j // steps_per_tile, (j % steps_per_tile) * unroll
        cms, (m_e, acc_e, m_o, acc_o) = carry[:2], carry[2:]
        fresh = c0 == 0
        state = (jnp.where(fresh, neg, m_e), acc_e, jnp.where(fresh, neg, m_o), acc_o)
        for u in range(unroll):
            nxt = j * unroll + u + 1
            cms_next = scores(jnp.minimum(nxt // n_chunks, n_tiles - 1), nxt % n_chunks, (u + 1) % 2)
            state = accumulate(c0 + u, u % 2, cms, state)
            cms = cms_next

        @pl.when(c0 + unroll == n_chunks)
        def _():
            out = jnp.concatenate(
                [acc[:HEAD_DIM] / acc[HEAD_DIM:HEAD_DIM + 1] for acc in (state[1], state[3])], axis=0)
            o_ref[0, pl.ds(pl.multiple_of(tile * tq, tq), tq), :] = jnp.concatenate(
                [out[:, :tq].T, out[:, tq:].T], axis=1).astype(o_ref.dtype)

        return cms + state

    lax.fori_loop(0, n_tiles * steps_per_tile, step, scores(0, 0, 0) + (neg, acc0, neg, acc0))


def _attn_a(qkv, *, n_heads, tq=256, tk=512, unroll=8):
    b, seq, _ = qkv.shape
    kv_heads = n_heads // A_REP
    q_cols = n_heads * HEAD_DIM
    q_w = A_REP * HEAD_DIM
    k_blk0 = q_cols // LANES
    v_blk0 = (q_cols + kv_heads * HEAD_DIM) // LANES
    assert unroll % 2 == 0 and (seq // tk) % unroll == 0 and seq % tq == 0
    return pl.pallas_call(
        functools.partial(_attn_a_body, tq=tq, tk=tk, seq=seq, unroll=unroll),
        grid=(b, kv_heads),
        in_specs=[pl.BlockSpec((1, seq, q_w), lambda bi, h: (bi, 0, h)),
                  pl.BlockSpec((1, seq, LANES), lambda bi, h: (bi, 0, k_blk0 + h // 2)),
                  pl.BlockSpec((1, seq, LANES), lambda bi, h: (bi, 0, v_blk0 + h // 2))],
        out_specs=pl.BlockSpec((1, seq, q_w), lambda bi, h: (bi, 0, h)),
        out_shape=jax.ShapeDtypeStruct((b, seq, q_cols), BF16),
        scratch_shapes=[pltpu.VMEM((seq, LANES), BF16), pltpu.VMEM((seq, LANES), BF16),
                        pltpu.VMEM((VT_ROWS, seq), BF16), pltpu.VMEM((2, 2, tk, 2 * tq), F32)],
        compiler_params=_params("parallel", "arbitrary"),
        name="attn_a",
    )(qkv, qkv, qkv)


def _fill_band_bias(bias_ref, slopes, half_window, dist_scale):
    span = bias_ref.shape[2]
    c = lax.broadcasted_iota(jnp.int32, (span, BLOCK_Q), 0)
    r = lax.broadcasted_iota(jnp.int32, (span, BLOCK_Q), 1)
    dist = jnp.abs(c - half_window - r)
    inside = dist <= half_window
    penalty = dist.astype(F32) * (dist_scale * LOG2E)
    for variant, ok in enumerate((inside & (c >= half_window), inside, inside & (c < half_window + BLOCK_Q))):
        for h, slope in enumerate(slopes):
            bias_ref[variant, h] = jnp.where(ok, -slope * penalty, NEG_BIG)


def _band_pipeline(jobs, bias_ref, s_buf):
    def scores(job, slot):
        kmat, q_rows, head_ids, _, _, variant, sink_row = job
        width = len(head_ids) * BLOCK_Q
        s = _nt_dot(kmat, q_rows) + jnp.concatenate([bias_ref[variant, h] for h in head_ids], axis=1)
        s_buf[slot, :, :width] = s
        m = jnp.max(s, axis=0, keepdims=True)
        return m if sink_row is None else jnp.maximum(m, sink_row)

    def values(job, slot, m):
        _, _, head_ids, kv_ids, vt1s, _, sink_row = job
        n = len(head_ids)
        p = jnp.exp2(s_buf[slot, :, :n * BLOCK_Q] - m).astype(BF16)
        accs, start = [], 0
        while start < n:
            stop = start
            while stop < n and kv_ids[stop] == kv_ids[start]:
                stop += 1
            accs.append(jnp.dot(vt1s[kv_ids[start]], p[:, start * BLOCK_Q:stop * BLOCK_Q],
                                preferred_element_type=F32))
            start = stop
        acc = accs[0] if len(accs) == 1 else jnp.concatenate(accs, axis=1)
        den = acc[HEAD_DIM:HEAD_DIM + 1]
        if sink_row is not None:
            den = den + jnp.exp2(sink_row - m)
        return acc[:HEAD_DIM] / den, m, den

    results = []
    m_next = scores(jobs[0], 0)
    for c, job in enumerate(jobs):
        m_cur = m_next
        if c + 1 < len(jobs):
            m_next = scores(jobs[c + 1], (c + 1) % 2)
        results.append(values(job, c % 2, m_cur))
    return results


def _band_operands(k_refs, v_refs):
    k = jnp.concatenate([r[0] for r in k_refs], axis=0)
    vt = jnp.concatenate([r[0] for r in v_refs], axis=0).astype(F32).T
    return k, vt


def _vt_with_ones(vt, kv, start, span):
    ones_row = (lax.broadcasted_iota(jnp.int32, (VT_ROWS - HEAD_DIM, span), 0) == 0).astype(F32)
    rows = vt[kv * HEAD_DIM:(kv + 1) * HEAD_DIM, start:start + span]
    return jnp.concatenate([rows, ones_row], axis=0).astype(BF16)


def _swap_halves(x):
    return jnp.concatenate([x[:, HEAD_DIM:], x[:, :HEAD_DIM]], axis=1)


def _attn_c_body(sink_ref, q_ref, kp_ref, kc_ref, kn_ref, vp_ref, vc_ref, vn_ref, o_ref, bias_ref, s_buf,
                 *, n_heads, slopes):
    i = pl.program_id(1)
    pl.when(i == 0)(lambda: _fill_band_bias(bias_ref, slopes, C_WINDOW, 1.0))
    variant = jnp.where(i == 0, 0, jnp.where(i == pl.num_programs(1) - 1, 2, 1))
    q = q_ref[0]
    k, vt = _band_operands((kp_ref, kc_ref, kn_ref), (vp_ref, vc_ref, vn_ref))
    span = k.shape[0]
    low = lax.broadcasted_iota(jnp.int32, (BLOCK_Q, LANES), 1) < HEAD_DIM
    zero = jnp.zeros((BLOCK_Q, LANES), BF16)
    first_block = lax.broadcasted_iota(jnp.int32, (1, 2 * BLOCK_Q), 1) < BLOCK_Q
    jobs = []
    for pair in range(n_heads // A_REP // 2):
        ks = k[:, pair * LANES:(pair + 1) * LANES]
        ks_swapped = _swap_halves(ks)
        for which in range(2):
            kv = 2 * pair + which
            vt1s = {kv: _vt_with_ones(vt, kv, 0, span)}
            slabs = [q[:, (2 * kv + j) * LANES:(2 * kv + j + 1) * LANES] for j in range(2)]
            for parity in range(2):
                q_rows = jnp.concatenate(
                    [jnp.where(low, s_, zero) if parity == 0 else jnp.where(low, zero, s_) for s_ in slabs],
                    axis=0)
                heads = (A_REP * kv + parity, A_REP * kv + 2 + parity)
                sink_row = jnp.where(first_block, sink_ref[heads[0]], sink_ref[heads[1]]) * LOG2E
                jobs.append((ks if parity == which else ks_swapped, q_rows, heads, (kv, kv), vt1s, variant,
                             sink_row))
    o_t = [None] * n_heads
    for job, (ot, _, _) in zip(jobs, _band_pipeline(jobs, bias_ref, s_buf)):
        o_t[job[2][0]], o_t[job[2][1]] = ot[:, :BLOCK_Q], ot[:, BLOCK_Q:]
    for j in range(n_heads // 2):
        pair_t = jnp.concatenate([o_t[2 * j], o_t[2 * j + 1]], axis=0)
        o_ref[0, :, j * LANES:(j + 1) * LANES] = pair_t.T.astype(o_ref.dtype)


def _alibi_slopes(n):
    return [2.0 ** (-8.0 * (i + 1) / n) for i in range(n)]


def _attn_c(qkv, sinks, *, n_heads):
    b, seq, _ = qkv.shape
    kv_heads = n_heads // A_REP
    q_cols = n_heads * HEAD_DIM
    kv_cols = kv_heads * HEAD_DIM
    n_tiles = seq // BLOCK_Q
    k_blk = q_cols // kv_cols
    v_blk = k_blk + 1
    span = BLOCK_Q + 2 * C_WINDOW

    def kv_spec(col_blk, shift):
        return pl.BlockSpec(
            (1, BLOCK_Q, kv_cols),
            lambda bi, i: (bi, jnp.clip(i + shift, 0, n_tiles - 1), col_blk))

    return pl.pallas_call(
        functools.partial(_attn_c_body, n_heads=n_heads, slopes=_alibi_slopes(n_heads)),
        grid=(b, n_tiles),
        in_specs=[pl.BlockSpec(memory_space=pltpu.SMEM),
                  pl.BlockSpec((1, BLOCK_Q, q_cols), lambda bi, i: (bi, i, 0)),
                  kv_spec(k_blk, -1), kv_spec(k_blk, 0), kv_spec(k_blk, 1),
                  kv_spec(v_blk, -1), kv_spec(v_blk, 0), kv_spec(v_blk, 1)],
        out_specs=pl.BlockSpec((1, BLOCK_Q, q_cols), lambda bi, i: (bi, i, 0)),
        out_shape=jax.ShapeDtypeStruct((b, seq, q_cols), BF16),
        scratch_shapes=[pltpu.VMEM((3, n_heads, span, BLOCK_Q), F32),
                        pltpu.VMEM((2, span, 2 * BLOCK_Q), F32)],
        compiler_params=_params("parallel", "arbitrary"),
        name="attn_c",
    )(sinks, qkv, qkv, qkv, qkv, qkv, qkv, qkv)


def _attn_b_body(q0_ref, q1_ref, q2_ref, kp_ref, kc_ref, kn_ref, vp_ref, vc_ref, vn_ref, o_ref, lse_ref,
                 bias_ref, s_buf, *, slopes, dil):
    i = pl.program_id(2)
    pl.when(jnp.logical_and(pl.program_id(1) == 0, i == 0))(
        lambda: _fill_band_bias(bias_ref, slopes, B_HALF, float(dil)))
    block_variants = (jnp.where(i == 0, 0, 1), jnp.where(i == pl.num_programs(2) - 1, 2, 1))
    k, vt = _band_operands((kp_ref, kc_ref, kn_ref), (vp_ref, vc_ref, vn_ref))
    low = lax.broadcasted_iota(jnp.int32, (BLOCK_Q, LANES), 1) < HEAD_DIM
    zero = jnp.zeros((BLOCK_Q, LANES), BF16)
    slabs = [q0_ref[0], q1_ref[0], q2_ref[0]]
    heads_all = range(B_HEADS_PER_GROUP)
    plain = [h for h in heads_all if h % 2 == h // B_REP]
    crossed = [h for h in heads_all if h % 2 != h // B_REP]
    jobs, job_block = [], []
    for blk in range(B_TILE // BLOCK_Q):
        q_lo = blk * BLOCK_Q
        ks = k[q_lo:q_lo + B_SPAN]
        ks_swapped = _swap_halves(ks)
        vt1s = {kv: _vt_with_ones(vt, kv, q_lo, B_SPAN) for kv in range(B_KV_PER_GROUP)}
        for kmat, heads in ((ks, plain[:2]), (ks_swapped, crossed), (ks, plain[2:])):
            q_rows = jnp.concatenate(
                [jnp.where(low, slabs[h // 2][q_lo:q_lo + BLOCK_Q], zero) if h % 2 == 0
                 else jnp.where(low, zero, slabs[h // 2][q_lo:q_lo + BLOCK_Q]) for h in heads], axis=0)
            jobs.append((kmat, q_rows, heads, [h // B_REP for h in heads], vt1s, block_variants[blk], None))
            job_block.append(blk)
    o_t, lse_t = {}, {}
    for job, blk, (ot, m, den) in zip(jobs, job_block, _band_pipeline(jobs, bias_ref, s_buf)):
        lse = (m + jnp.log2(den)) * LN2
        for n, h in enumerate(job[2]):
            o_t[blk, h] = ot[:, n * BLOCK_Q:(n + 1) * BLOCK_Q]
            lse_t[blk, h] = jnp.broadcast_to(lse[:, n * BLOCK_Q:(n + 1) * BLOCK_Q], (HEAD_DIM, BLOCK_Q))
    for blk in range(B_TILE // BLOCK_Q):
        rows = slice(blk * BLOCK_Q, (blk + 1) * BLOCK_Q)
        for j in range(B_HEADS_PER_GROUP // 2):
            cols = slice(j * LANES, (j + 1) * LANES)
            pair = lambda t: jnp.concatenate([t[blk, 2 * j], t[blk, 2 * j + 1]], axis=0).T
            o_ref[0, rows, cols] = pair(o_t).astype(o_ref.dtype)
            lse_ref[0, rows, cols] = pair(lse_t)


def _attn_b(group_qkv, b):
    n_g = len(B_GROUPS)
    gq = B_HEADS_PER_GROUP * HEAD_DIM
    slabs_per_token = B_GROUP_COLS // LANES
    q_slabs = gq // LANES
    halves_per_tile = B_TILE // B_HALF
    all_slopes = _alibi_slopes(n_g * B_HEADS_PER_GROUP)
    outs, lses = [], []
    for g, (window, dil) in enumerate(B_GROUPS):
        assert (window // 2) // dil == B_HALF
        sub = group_qkv[g].shape[0] // b
        n_tiles = sub // B_TILE
        n_halves = sub // B_HALF
        view = group_qkv[g].reshape(b, sub, dil * B_GROUP_COLS)

        def tile_spec(slab):
            return pl.BlockSpec((1, B_TILE, LANES), lambda bi, r, i: (bi, i, r * slabs_per_token + slab))

        def edge_spec(slab, offset, n_halves=n_halves):
            return pl.BlockSpec(
                (1, B_HALF, LANES),
                lambda bi, r, i: (bi, jnp.clip(i * halves_per_tile + offset, 0, n_halves - 1),
                                  r * slabs_per_token + slab))

        def kv_specs(slab):
            return [edge_spec(slab, -1), tile_spec(slab), edge_spec(slab, halves_per_tile)]

        out_map = lambda bi, r, i: (bi, i, r)
        o, lse = pl.pallas_call(
            functools.partial(_attn_b_body, dil=dil,
                              slopes=all_slopes[g * B_HEADS_PER_GROUP:(g + 1) * B_HEADS_PER_GROUP]),
            grid=(b, dil, n_tiles),
            in_specs=[tile_spec(j) for j in range(q_slabs)] + kv_specs(q_slabs) + kv_specs(q_slabs + 1),
            out_specs=[pl.BlockSpec((1, B_TILE, gq), out_map), pl.BlockSpec((1, B_TILE, gq), out_map)],
            out_shape=[jax.ShapeDtypeStruct((b, sub, dil * gq), BF16),
                       jax.ShapeDtypeStruct((b, sub, dil * gq), F32)],
            scratch_shapes=[pltpu.VMEM((3, B_HEADS_PER_GROUP, B_SPAN, BLOCK_Q), F32),
                            pltpu.VMEM((2, B_SPAN, 2 * BLOCK_Q), F32)],
            compiler_params=_params("parallel", "arbitrary", "arbitrary"),
            name=f"attn_b_g{g}",
        )(*([view] * (q_slabs + 6)))
        outs.append(o.reshape(b * sub, dil * gq))
        lses.append(lse.reshape(b * sub, dil * gq))
    return outs, lses


PROLOGUE_ROW_CHUNKS = 4


def _mix_plain(refs, scratch):
    return lambda rows: refs[0][rows, :]


def _mix_groups(refs, scratch):
    n_g = len(B_GROUPS)
    o_scr, l_scr = scratch
    slabs = o_scr.shape[0] // n_g
    tm = o_scr.shape[1]
    gq = slabs * LANES
    for g, (_, dil) in enumerate(B_GROUPS):
        for r in range(dil):
            rows = pl.ds(r, tm // dil, stride=dil)
            for s in range(slabs):
                cols = slice(r * gq + s * LANES, r * gq + (s + 1) * LANES)
                o_scr[g * slabs + s, rows, :] = refs[g][:, cols].astype(F32)
                l_scr[g * slabs + s, rows, :] = refs[n_g + g][:, cols]
    def lhs(rows):
        group = lambda scr, g: jnp.concatenate([scr[g * slabs + s, rows, :] for s in range(slabs)], axis=1)
        ls = [group(l_scr, g) for g in range(n_g)]
        mx = functools.reduce(jnp.maximum, ls)
        es = [jnp.exp(l - mx) for l in ls]
        inv = 1.0 / functools.reduce(lambda a, b: a + b, es)
        return jnp.concatenate([(group(o_scr, g) * (es[g] * inv)).astype(BF16) for g in range(n_g)], axis=1)

    return lhs


def _block_body(*refs, n_mix, mix_fn, final_norm):
    h_ref, mix_refs = refs[0], refs[1:1 + n_mix]
    wo_ref, g_ref, w1_ref, w2_ref, fg_ref, out_ref, hn_ref = refs[1 + n_mix:8 + n_mix]
    mix_scratch = refs[8 + n_mix:]
    j = pl.program_id(1)

    @pl.when(j == 0)
    def _():
        lhs = mix_fn(mix_refs, mix_scratch)
        chunk = h_ref.shape[0] // PROLOGUE_ROW_CHUNKS
        rows = [slice(c * chunk, (c + 1) * chunk) for c in range(PROLOGUE_ROW_CHUNKS)]
        project = lambda c: jnp.dot(lhs(rows[c]), wo_ref[...], preferred_element_type=F32)
        nxt = project(0)
        for c in range(PROLOGUE_ROW_CHUNKS):
            cur = nxt
            if c + 1 < PROLOGUE_ROW_CHUNKS:
                nxt = project(c + 1)
            h1 = h_ref[rows[c], :] + cur
            out_ref[rows[c], :] = h1
            hn_ref[rows[c], :] = _rms_rows(h1, g_ref[...]).astype(BF16)

    u = jnp.maximum(jnp.dot(hn_ref[...], w1_ref[...], preferred_element_type=F32), 0.0)
    out_ref[...] += jnp.dot((u * u).astype(BF16), w2_ref[...], preferred_element_type=F32)

    if final_norm:
        @pl.when(j == pl.num_programs(1) - 1)
        def _():
            out_ref[...] = _rms_rows(out_ref[...], fg_ref[...])


def _block(h2, mix_inputs, w_o, *, gain, w1, w2, final_gain, final_norm, tm=1024, tf=512):
    t, d = h2.shape
    ff = w1.shape[1]
    once = pl.Buffered(1)
    row_tile = lambda cols: pl.BlockSpec((tm, cols), lambda i, j: (i, 0))
    whole = lambda a: pl.BlockSpec(a.shape, lambda i, j: (0, 0), pipeline_mode=once)
    scratch = [pltpu.VMEM((tm, d), BF16)]
    if len(mix_inputs) == 1:
        mix_fn, mix_specs = _mix_plain, [row_tile(mix_inputs[0].shape[1])]
    else:
        n_g = len(B_GROUPS)
        gq = B_HEADS_PER_GROUP * HEAD_DIM
        mix_fn = _mix_groups
        mix_specs = [pl.BlockSpec((tm // dil, dil * gq), lambda i, j: (i, 0), pipeline_mode=once)
                     for _, dil in B_GROUPS] * 2
        scratch += [pltpu.VMEM((n_g * gq // LANES, tm, LANES), F32)] * 2
    return pl.pallas_call(
        functools.partial(_block_body, n_mix=len(mix_inputs), mix_fn=mix_fn, final_norm=final_norm),
        grid=(t // tm, ff // tf),
        in_specs=[row_tile(d)] + mix_specs
                 + [whole(w_o), whole(gain),
                    pl.BlockSpec((d, tf), lambda i, j: (0, j)),
                    pl.BlockSpec((tf, d), lambda i, j: (j, 0)),
                    whole(final_gain)],
        out_specs=row_tile(d),
        out_shape=jax.ShapeDtypeStruct((t, d), F32),
        scratch_shapes=scratch,
        compiler_params=_params("parallel", "arbitrary"),
        name="block_mlp",
    )(h2, *mix_inputs, w_o, gain, w1, w2, final_gain)


def kernel(x, attn_norm, mlp_norm, a_w_qkv, a_q_gain, a_k_gain, a_w_o, b_w_qkv, b_w_o,
           c_w_qkv, c_sinks, c_w_o, mlp_w1, mlp_w2, final_norm):
    b, seq, d = x.shape
    depth = attn_norm.shape[0]
    h = x.reshape(b * seq, d)
    scale = HEAD_DIM ** -0.5 * LOG2E
    a_heads = a_w_o.shape[1] // HEAD_DIM
    c_heads = c_w_o.shape[1] // HEAD_DIM
    cos, sin = _rope_tables(seq)
    head = jnp.arange(2 * LANES) // HEAD_DIM
    seg = (head[:, None] == head[None, :]).astype(BF16)
    tile4 = lambda g: jnp.tile(g, 2 * LANES // HEAD_DIM)[None, :]
    used = [0, 0, 0]
    for layer in range(depth):
        kind = layer % N_MIXERS
        j = used[kind]
        used[kind] += 1
        gain = attn_norm[layer][None, :]
        block = functools.partial(_block, gain=mlp_norm[layer][None, :], w1=mlp_w1[layer].astype(BF16),
                                  w2=mlp_w2[layer].astype(BF16), final_gain=final_norm[None, :],
                                  final_norm=(layer == depth - 1))
        if kind == 0:
            q_cols = a_heads * HEAD_DIM
            k_cols = (a_w_qkv.shape[2] - q_cols) // 2
            qkv = _qkv_proj(h, gain, a_w_qkv[j].astype(BF16), q_cols=q_cols, q_scale=scale,
                            rope=(tile4(a_q_gain[j]), tile4(a_k_gain[j]), cos, sin, seg, k_cols))
            o = _attn_a(qkv.reshape(b, seq, -1), n_heads=a_heads).reshape(b * seq, q_cols)
            h = block(h, [o], a_w_o[j].astype(BF16))
        elif kind == 1:
            q_cols = len(B_GROUPS) * B_HEADS_PER_GROUP * HEAD_DIM
            groups = _qkv_proj_b(h, gain, b_w_qkv[j].astype(BF16), q_cols=q_cols, q_scale=scale)
            outs, lses = _attn_b(groups, b)
            h = block(h, outs + lses, b_w_o[j].astype(BF16))
        else:
            q_cols = c_heads * HEAD_DIM
            qkv = _qkv_proj(h, gain, c_w_qkv[j].astype(BF16), q_cols=q_cols, q_scale=scale)
            o = _attn_c(qkv.reshape(b, seq, -1), c_sinks[j], n_heads=c_heads).reshape(b * seq, q_cols)
            h = block(h, [o], c_w_o[j].astype(BF16))
    return h.reshape(b, seq, d)
```

```python
import functools
import math

import jax
import jax.numpy as jnp
from jax import lax
from jax.experimental import pallas as pl
from jax.experimental.pallas import tpu as pltpu

HEAD_DIM = 64
HALF_HEAD = HEAD_DIM // 2
RMS_EPS = 1e-6
GRID_W = 64
ROPE_THETA = 10000.0
N_MIXERS = 3
A_REP = 4
B_GROUPS = ((128, 1), (512, 4), (2048, 16))
B_HEADS_PER_GROUP = 6
B_KV_PER_GROUP = 2
B_REP = B_HEADS_PER_GROUP // B_KV_PER_GROUP
C_WINDOW = 128
LANES = 128
BF16_SUBLANES = 16
BLOCK_Q = 128
B_HALF = BLOCK_Q // 2
B_TILE = 2 * BLOCK_Q
B_SPAN = BLOCK_Q + 2 * B_HALF
B_GROUP_COLS = (B_HEADS_PER_GROUP + 2 * B_KV_PER_GROUP) * HEAD_DIM
VT_ROWS = HEAD_DIM + BF16_SUBLANES
V7X_VMEM_LIMIT_BYTES = 48 * 1024 * 1024
NEG_BIG = -1e30
LOG2E = math.log2(math.e)
SAFE_SHIFT_LOG2 = 60.0
BOUND_MARGIN = 1.0 + 2.0 ** -6
LN2 = math.log(2.0)

BF16 = jnp.bfloat16
F32 = jnp.float32


def _params(*sem):
    return pltpu.CompilerParams(dimension_semantics=sem, vmem_limit_bytes=V7X_VMEM_LIMIT_BYTES)


def _nt_dot(a, b):
    return lax.dot_general(a, b, (((1,), (1,)), ((), ())), preferred_element_type=F32)


def _rms_rows(x, gain):
    ms = jnp.mean(x * x, axis=-1, keepdims=True)
    return x * lax.rsqrt(ms + RMS_EPS) * gain


def _qkv_body(x_ref, g_ref, w_ref, o_ref, *, q_cols, q_scale):
    hn = _rms_rows(x_ref[...], g_ref[...]).astype(BF16)
    y = jnp.dot(hn, w_ref[...], preferred_element_type=F32)
    o_ref[:, :q_cols] = (y[:, :q_cols] * q_scale).astype(o_ref.dtype)
    o_ref[:, q_cols:] = y[:, q_cols:].astype(o_ref.dtype)


def _qkv_rope_body(x_ref, g_ref, w_ref, qg_ref, kg_ref, cos_ref, sin_ref, seg_ref, o_ref,
                   *, q_cols, k_cols, q_scale):
    hn = _rms_rows(x_ref[...], g_ref[...]).astype(BF16)
    slab = 2 * LANES
    cos = jnp.concatenate([cos_ref[...], cos_ref[...]], axis=1)
    sin = jnp.concatenate([sin_ref[...], sin_ref[...]], axis=1)
    lane = lax.broadcasted_iota(jnp.int32, (x_ref.shape[0], slab), 1)
    first_half = (lane % HALF_HEAD) < (HALF_HEAD // 2)
    seg = seg_ref[...]
    n_qk = (q_cols + k_cols) // slab
    project = lambda s: jnp.dot(hn, w_ref[:, s * slab:(s + 1) * slab], preferred_element_type=F32)
    y_next = project(0)
    for s in range(n_qk):
        is_q = s * slab < q_cols
        ys = y_next
        y_next = project(s + 1) if s + 1 < n_qk else jnp.dot(
            hn, w_ref[:, q_cols + k_cols:], preferred_element_type=F32)
        sq = ys * ys
        hi = sq.astype(BF16)
        lo = (sq - hi.astype(F32)).astype(BF16)
        ss = jnp.dot(hi, seg, preferred_element_type=F32) + jnp.dot(lo, seg, preferred_element_type=F32)
        gain = qg_ref[...] if is_q else kg_ref[...]
        yn = ys * lax.rsqrt(ss * (1.0 / HEAD_DIM) + RMS_EPS) * gain
        partner = jnp.where(first_half,
                            pltpu.roll(yn, slab - HALF_HEAD // 2, axis=1),
                            pltpu.roll(yn, HALF_HEAD // 2, axis=1))
        r = yn * cos + partner * sin
        if is_q:
            r = r * q_scale
        o_ref[:, s * slab:(s + 1) * slab] = r.astype(o_ref.dtype)
    o_ref[:, q_cols + k_cols:] = y_next.astype(o_ref.dtype)


def _qkv_proj(h2, gain, w, *, q_cols, q_scale, tm=512, rope=None):
    t, d = h2.shape
    n = w.shape[1]
    common = dict(
        grid=(t // tm,),
        out_specs=pl.BlockSpec((tm, n), lambda i: (i, 0)),
        out_shape=jax.ShapeDtypeStruct((t, n), BF16),
        compiler_params=_params("parallel"),
    )
    x_spec = pl.BlockSpec((tm, d), lambda i: (i, 0))
    g_spec = pl.BlockSpec((1, d), lambda i: (0, 0))
    w_spec = pl.BlockSpec((d, n), lambda i: (0, 0))
    if rope is None:
        return pl.pallas_call(
            functools.partial(_qkv_body, q_cols=q_cols, q_scale=q_scale),
            in_specs=[x_spec, g_spec, w_spec], name="qkv_proj", **common,
        )(h2, gain, w)
    qg, kg, cos, sin, seg, k_cols = rope
    seq_tiles = cos.shape[0] // tm
    slab = 2 * LANES
    return pl.pallas_call(
        functools.partial(_qkv_rope_body, q_cols=q_cols, k_cols=k_cols, q_scale=q_scale),
        in_specs=[x_spec, g_spec, w_spec,
                  pl.BlockSpec((1, slab), lambda i: (0, 0)),
                  pl.BlockSpec((1, slab), lambda i: (0, 0)),
                  pl.BlockSpec((tm, LANES), lambda i: (i % seq_tiles, 0)),
                  pl.BlockSpec((tm, LANES), lambda i: (i % seq_tiles, 0)),
                  pl.BlockSpec((slab, slab), lambda i: (0, 0))],
        name="qkv_proj_rope", **common,
    )(h2, gain, w, qg, kg, cos, sin, seg)


def _qkv_b_body(x_ref, g_ref, w_ref, *rest, q_cols, q_scale):
    out_refs, y_scr = rest[:-1], rest[-1]
    tm = x_ref.shape[0]
    hn = _rms_rows(x_ref[...], g_ref[...]).astype(BF16)
    y = jnp.dot(hn, w_ref[...], preferred_element_type=F32)
    for s in range(y_scr.shape[0]):
        ys = y[:, s * LANES:(s + 1) * LANES]
        y_scr[s] = ys * q_scale if s * LANES < q_cols else ys
    n_g = len(B_GROUPS)
    q_slabs = B_HEADS_PER_GROUP * HEAD_DIM // LANES
    k0 = q_cols // LANES
    for g, (_, dil) in enumerate(B_GROUPS):
        slabs = list(range(g * q_slabs, (g + 1) * q_slabs)) + [k0 + g, k0 + n_g + g]
        for r in range(dil):
            rows = pl.ds(r, tm // dil, stride=dil)
            piece = jnp.concatenate([y_scr[s, rows, :] for s in slabs], axis=1)
            out_refs[g][:, r * B_GROUP_COLS:(r + 1) * B_GROUP_COLS] = piece.astype(BF16)


def _qkv_proj_b(h2, gain, w, *, q_cols, q_scale, tm=512):
    t, d = h2.shape
    n = w.shape[1]
    return pl.pallas_call(
        functools.partial(_qkv_b_body, q_cols=q_cols, q_scale=q_scale),
        grid=(t // tm,),
        in_specs=[pl.BlockSpec((tm, d), lambda i: (i, 0)),
                  pl.BlockSpec((1, d), lambda i: (0, 0)),
                  pl.BlockSpec((d, n), lambda i: (0, 0))],
        out_specs=[pl.BlockSpec((tm // dil, dil * B_GROUP_COLS), lambda i: (i, 0)) for _, dil in B_GROUPS],
        out_shape=[jax.ShapeDtypeStruct((t // dil, dil * B_GROUP_COLS), BF16) for _, dil in B_GROUPS],
        scratch_shapes=[pltpu.VMEM((n // LANES, tm, LANES), F32)],
        compiler_params=_params("parallel"),
        name="qkv_proj_b",
    )(h2, gain, w)


def _rope_tables(seq):
    t = jnp.arange(seq)
    row = (t // GRID_W).astype(F32)
    col = (t % GRID_W).astype(F32)
    inv_freq = ROPE_THETA ** (-jnp.arange(0, HALF_HEAD, 2, dtype=F32) / HALF_HEAD)
    ang_row = row[:, None] * inv_freq
    ang_col = col[:, None] * inv_freq
    cos = jnp.concatenate([jnp.cos(ang_row)] * 2 + [jnp.cos(ang_col)] * 2, axis=1)
    sin = jnp.concatenate([-jnp.sin(ang_row), jnp.sin(ang_row), -jnp.sin(ang_col), jnp.sin(ang_col)], axis=1)
    return jnp.concatenate([cos, cos], axis=1), jnp.concatenate([sin, sin], axis=1)


def _attn_a_body(q_ref, k_ref, v_ref, o_ref, klo, khi, vt1, kstat, s_buf, *, tq, tk, seq):
    def fill(head_in_high_lanes):
        low = lax.broadcasted_iota(jnp.int32, (seq, LANES), 1) < HEAD_DIM
        zero = jnp.zeros((seq, LANES), BF16)
        x = k_ref[0]
        swapped = jnp.concatenate([x[:, HEAD_DIM:], x[:, :HEAD_DIM]], axis=1)
        in_low, in_high = (swapped, x) if head_in_high_lanes else (x, swapped)
        klo[...] = jnp.where(low, in_low, zero)
        khi[...] = jnp.where(low, zero, in_high)
        xf = x.astype(F32)
        own = jnp.where(low != head_in_high_lanes, xf * xf, 0.0)
        k_sq = jnp.max(jnp.sum(own, axis=1, keepdims=True), axis=0, keepdims=True)
        kstat[...] = jnp.broadcast_to(k_sq, kstat.shape)
        ones_row = (lax.broadcasted_iota(jnp.int32, (VT_ROWS - HEAD_DIM, tk), 0) == 0).astype(F32)
        for c in range(seq // tk):
            xt = v_ref[0, c * tk:(c + 1) * tk, :].astype(F32).T
            vt = xt[HEAD_DIM:] if head_in_high_lanes else xt[:HEAD_DIM]
            vt1[:, c * tk:(c + 1) * tk] = jnp.concatenate([vt, ones_row], axis=0).astype(BF16)

    odd = (pl.program_id(1) % 2) == 1
    pl.when(jnp.logical_not(odd))(lambda: fill(False))
    pl.when(odd)(lambda: fill(True))

    cols = 2 * tq
    n_chunks = seq // tk
    key_sets = (klo, khi)
    half = lax.broadcasted_iota(jnp.int32, (BF16_SUBLANES, LANES), 1) // HEAD_DIM
    selector = (half == lax.broadcasted_iota(jnp.int32, (BF16_SUBLANES, LANES), 0)).astype(BF16)

    def finish(rows, acc_e, acc_o):
        out = jnp.concatenate([acc[:HEAD_DIM] / acc[HEAD_DIM:HEAD_DIM + 1] for acc in (acc_e, acc_o)], axis=0)
        o_ref[0, rows, :] = jnp.concatenate([out[:, :tq].T, out[:, tq:].T], axis=1).astype(o_ref.dtype)

    def bounded_tile(rows, q2, shift):
        accs = [jnp.zeros((VT_ROWS, cols), F32)] * 2
        units = [(c, par) for c in range(n_chunks) for par in range(2)]
        keys = lambda c: slice(c * tk, (c + 1) * tk)
        score = lambda c, par: _nt_dot(key_sets[par][keys(c), :], q2)
        s_next = score(*units[0])
        for n, (c, par) in enumerate(units):
            s = s_next
            if n + 1 < len(units):
                s_next = score(*units[n + 1])
            p = jnp.exp2(s - shift[par:par + 1]).astype(BF16)
            accs[par] = accs[par] + jnp.dot(vt1[:, keys(c)], p, preferred_element_type=F32)
        finish(rows, *accs)

    def online_tile(rows, q2):
        def scores(c, slot):
            ks = pl.multiple_of(c * tk, tk)
            cms = []
            for par in range(2):
                s = _nt_dot(key_sets[par][pl.ds(ks, tk), :], q2)
                s_buf[slot, par] = s
                cms.append(jnp.max(s, axis=0, keepdims=True))
            return tuple(cms)

        def accumulate(c, slot, cms, state):
            vt = vt1[:, pl.ds(pl.multiple_of(c * tk, tk), tk)]
            new = []
            for par in range(2):
                m, acc = state[2 * par], state[2 * par + 1]
                mn = jnp.maximum(m, cms[par])
                p = jnp.exp2(s_buf[slot, par] - mn)
                new += [mn, acc * jnp.exp2(m - mn) + jnp.dot(vt, p.astype(BF16), preferred_element_type=F32)]
            return tuple(new)

        def pair(j, carry):
            cms_a, state = carry[:2], carry[2:]
            cms_b = scores(2 * j + 1, 1)
            state = accumulate(2 * j, 0, cms_a, state)
            cms_a = scores(2 * j + 2, 0)
            state = accumulate(2 * j + 1, 1, cms_b, state)
            return cms_a + state

        neg = jnp.full((1, cols), -jnp.inf, F32)
        acc0 = jnp.zeros((VT_ROWS, cols), F32)
        carry = lax.fori_loop(0, n_chunks // 2 - 1, pair, scores(0, 0) + (neg, acc0, neg, acc0))
        cms_a, state = carry[:2], carry[2:]
        cms_b = scores(n_chunks - 1, 1)
        state = accumulate(n_chunks - 2, 0, cms_a, state)
        _, acc_e, _, acc_o = accumulate(n_chunks - 1, 1, cms_b, state)
        finish(rows, acc_e, acc_o)

    def tile(t, _):
        rows = pl.ds(pl.multiple_of(t * tq, tq), tq)
        q = q_ref[0, rows, :]
        q2 = jnp.concatenate([q[:, :LANES], q[:, LANES:]], axis=0)
        qf = q2.astype(F32)
        q_sq = _nt_dot(selector, (qf * qf).astype(BF16))
        shift = jnp.sqrt(q_sq[:2] * kstat[:1, :1]) * BOUND_MARGIN + BOUND_MARGIN
        safe = jnp.max(shift) <= SAFE_SHIFT_LOG2
        pl.when(safe)(lambda: bounded_tile(rows, q2, shift))
        pl.when(jnp.logical_not(safe))(lambda: online_tile(rows, q2))
        return 0

    lax.fori_loop(0, seq // tq, tile, 0)


def _attn_a(qkv, *, n_heads, tq=256, tk=512):
    b, seq, _ = qkv.shape
    kv_heads = n_heads // A_REP
    q_cols = n_heads * HEAD_DIM
    q_w = A_REP * HEAD_DIM
    k_blk0 = q_cols // LANES
    v_blk0 = (q_cols + kv_heads * HEAD_DIM) // LANES
    assert (seq // tk) % 2 == 0 and seq // tk >= 4 and seq % tq == 0
    return pl.pallas_call(
        functools.partial(_attn_a_body, tq=tq, tk=tk, seq=seq),
        grid=(b, kv_heads),
        in_specs=[pl.BlockSpec((1, seq, q_w), lambda bi, h: (bi, 0, h)),
                  pl.BlockSpec((1, seq, LANES), lambda bi, h: (bi, 0, k_blk0 + h // 2)),
                  pl.BlockSpec((1, seq, LANES), lambda bi, h: (bi, 0, v_blk0 + h // 2))],
        out_specs=pl.BlockSpec((1, seq, q_w), lambda bi, h: (bi, 0, h)),
        out_shape=jax.ShapeDtypeStruct((b, seq, q_cols), BF16),
        scratch_shapes=[pltpu.VMEM((seq, LANES), BF16), pltpu.VMEM((seq, LANES), BF16),
                        pltpu.VMEM((VT_ROWS, seq), BF16), pltpu.VMEM((8, LANES), F32),
                        pltpu.VMEM((2, 2, tk, 2 * tq), F32)],
        compiler_params=_params("parallel", "arbitrary"),
        name="attn_a",
    )(qkv, qkv, qkv)


def _fill_band_bias(bias_ref, slopes, half_window, dist_scale):
    span = bias_ref.shape[2]
    c = lax.broadcasted_iota(jnp.int32, (span, BLOCK_Q), 0)
    r = lax.broadcasted_iota(jnp.int32, (span, BLOCK_Q), 1)
    dist = jnp.abs(c - half_window - r)
    inside = dist <= half_window
    penalty = dist.astype(F32) * (dist_scale * LOG2E)
    for variant, ok in enumerate((inside & (c >= half_window), inside, inside & (c < half_window + BLOCK_Q))):
        for h, slope in enumerate(slopes):
            bias_ref[variant, h] = jnp.where(ok, -slope * penalty, NEG_BIG)


def _band_pipeline(jobs, bias_ref, s_buf):
    def scores(job, slot):
        kmat, q_rows, head_ids, _, _, variant, sink_row = job
        width = len(head_ids) * BLOCK_Q
        s = _nt_dot(kmat, q_rows) + jnp.concatenate([bias_ref[variant, h] for h in head_ids], axis=1)
        s_buf[slot, :, :width] = s
        m = jnp.max(s, axis=0, keepdims=True)
        return m if sink_row is None else jnp.maximum(m, sink_row)

    def values(job, slot, m):
        _, _, head_ids, kv_ids, vt1s, _, sink_row = job
        n = len(head_ids)
        p = jnp.exp2(s_buf[slot, :, :n * BLOCK_Q] - m).astype(BF16)
        accs, start = [], 0
        while start < n:
            stop = start
            while stop < n and kv_ids[stop] == kv_ids[start]:
                stop += 1
            accs.append(jnp.dot(vt1s[kv_ids[start]], p[:, start * BLOCK_Q:stop * BLOCK_Q],
                                preferred_element_type=F32))
            start = stop
        acc = accs[0] if len(accs) == 1 else jnp.concatenate(accs, axis=1)
        den = acc[HEAD_DIM:HEAD_DIM + 1]
        if sink_row is not None:
            den = den + jnp.exp2(sink_row - m)
        return acc[:HEAD_DIM] / den, m, den

    results = []
    m_next = scores(jobs[0], 0)
    for c, job in enumerate(jobs):
        m_cur = m_next
        if c + 1 < len(jobs):
            m_next = scores(jobs[c + 1], (c + 1) % 2)
        results.append(values(job, c % 2, m_cur))
    return results


def _band_operands(k_refs, v_refs):
    k = jnp.concatenate([r[0] for r in k_refs], axis=0)
    vt = jnp.concatenate([r[0] for r in v_refs], axis=0).astype(F32).T
    return k, vt


def _vt_with_ones(vt, kv, start, span):
    ones_row = (lax.broadcasted_iota(jnp.int32, (VT_ROWS - HEAD_DIM, span), 0) == 0).astype(F32)
    rows = vt[kv * HEAD_DIM:(kv + 1) * HEAD_DIM, start:start + span]
    return jnp.concatenate([rows, ones_row], axis=0).astype(BF16)


def _swap_halves(x):
    return jnp.concatenate([x[:, HEAD_DIM:], x[:, :HEAD_DIM]], axis=1)


def _attn_c_body(sink_ref, q_ref, kp_ref, kc_ref, kn_ref, vp_ref, vc_ref, vn_ref, o_ref, bias_ref, s_buf,
                 *, n_heads, slopes):
    i = pl.program_id(1)
    pl.when(i == 0)(lambda: _fill_band_bias(bias_ref, slopes, C_WINDOW, 1.0))
    variant = jnp.where(i == 0, 0, jnp.where(i == pl.num_programs(1) - 1, 2, 1))
    q = q_ref[0]
    k, vt = _band_operands((kp_ref, kc_ref, kn_ref), (vp_ref, vc_ref, vn_ref))
    span = k.shape[0]
    low = lax.broadcasted_iota(jnp.int32, (BLOCK_Q, LANES), 1) < HEAD_DIM
    zero = jnp.zeros((BLOCK_Q, LANES), BF16)
    first_block = lax.broadcasted_iota(jnp.int32, (1, 2 * BLOCK_Q), 1) < BLOCK_Q
    jobs = []
    for pair in range(n_heads // A_REP // 2):
        ks = k[:, pair * LANES:(pair + 1) * LANES]
        ks_swapped = _swap_halves(ks)
        for which in range(2):
            kv = 2 * pair + which
            vt1s = {kv: _vt_with_ones(vt, kv, 0, span)}
            slabs = [q[:, (2 * kv + j) * LANES:(2 * kv + j + 1) * LANES] for j in range(2)]
            for parity in range(2):
                q_rows = jnp.concatenate(
                    [jnp.where(low, s_, zero) if parity == 0 else jnp.where(low, zero, s_) for s_ in slabs],
                    axis=0)
                heads = (A_REP * kv + parity, A_REP * kv + 2 + parity)
                sink_row = jnp.where(first_block, sink_ref[heads[0]], sink_ref[heads[1]]) * LOG2E
                jobs.append((ks if parity == which else ks_swapped, q_rows, heads, (kv, kv), vt1s, variant,
                             sink_row))
    o_t = [None] * n_heads
    for job, (ot, _, _) in zip(jobs, _band_pipeline(jobs, bias_ref, s_buf)):
        o_t[job[2][0]], o_t[job[2][1]] = ot[:, :BLOCK_Q], ot[:, BLOCK_Q:]
    for j in range(n_heads // 2):
        pair_t = jnp.concatenate([o_t[2 * j], o_t[2 * j + 1]], axis=0)
        o_ref[0, :, j * LANES:(j + 1) * LANES] = pair_t.T.astype(o_ref.dtype)


def _alibi_slopes(n):
    return [2.0 ** (-8.0 * (i + 1) / n) for i in range(n)]


def _attn_c(qkv, sinks, *, n_heads):
    b, seq, _ = qkv.shape
    kv_heads = n_heads // A_REP
    q_cols = n_heads * HEAD_DIM
    kv_cols = kv_heads * HEAD_DIM
    n_tiles = seq // BLOCK_Q
    k_blk = q_cols // kv_cols
    v_blk = k_blk + 1
    span = BLOCK_Q + 2 * C_WINDOW

    def kv_spec(col_blk, shift):
        return pl.BlockSpec(
            (1, BLOCK_Q, kv_cols),
            lambda bi, i: (bi, jnp.clip(i + shift, 0, n_tiles - 1), col_blk))

    return pl.pallas_call(
        functools.partial(_attn_c_body, n_heads=n_heads, slopes=_alibi_slopes(n_heads)),
        grid=(b, n_tiles),
        in_specs=[pl.BlockSpec(memory_space=pltpu.SMEM),
                  pl.BlockSpec((1, BLOCK_Q, q_cols), lambda bi, i: (bi, i, 0)),
                  kv_spec(k_blk, -1), kv_spec(k_blk, 0), kv_spec(k_blk, 1),
                  kv_spec(v_blk, -1), kv_spec(v_blk, 0), kv_spec(v_blk, 1)],
        out_specs=pl.BlockSpec((1, BLOCK_Q, q_cols), lambda bi, i: (bi, i, 0)),
        out_shape=jax.ShapeDtypeStruct((b, seq, q_cols), BF16),
        scratch_shapes=[pltpu.VMEM((3, n_heads, span, BLOCK_Q), F32),
                        pltpu.VMEM((2, span, 2 * BLOCK_Q), F32)],
        compiler_params=_params("parallel", "arbitrary"),
        name="attn_c",
    )(sinks, qkv, qkv, qkv, qkv, qkv, qkv, qkv)


def _attn_b_body(q0_ref, q1_ref, q2_ref, kp_ref, kc_ref, kn_ref, vp_ref, vc_ref, vn_ref, o_ref, lse_ref,
                 bias_ref, s_buf, *, slopes, dil):
    i = pl.program_id(2)
    pl.when(jnp.logical_and(pl.program_id(1) == 0, i == 0))(
        lambda: _fill_band_bias(bias_ref, slopes, B_HALF, float(dil)))
    block_variants = (jnp.where(i == 0, 0, 1), jnp.where(i == pl.num_programs(2) - 1, 2, 1))
    k, vt = _band_operands((kp_ref, kc_ref, kn_ref), (vp_ref, vc_ref, vn_ref))
    low = lax.broadcasted_iota(jnp.int32, (BLOCK_Q, LANES), 1) < HEAD_DIM
    zero = jnp.zeros((BLOCK_Q, LANES), BF16)
    slabs = [q0_ref[0], q1_ref[0], q2_ref[0]]
    heads_all = range(B_HEADS_PER_GROUP)
    plain = [h for h in heads_all if h % 2 == h // B_REP]
    crossed = [h for h in heads_all if h % 2 != h // B_REP]
    jobs, job_block = [], []
    for blk in range(B_TILE // BLOCK_Q):
        q_lo = blk * BLOCK_Q
        ks = k[q_lo:q_lo + B_SPAN]
        ks_swapped = _swap_halves(ks)
        vt1s = {kv: _vt_with_ones(vt, kv, q_lo, B_SPAN) for kv in range(B_KV_PER_GROUP)}
        for kmat, heads in ((ks, plain[:2]), (ks_swapped, crossed), (ks, plain[2:])):
            q_rows = jnp.concatenate(
                [jnp.where(low, slabs[h // 2][q_lo:q_lo + BLOCK_Q], zero) if h % 2 == 0
                 else jnp.where(low, zero, slabs[h // 2][q_lo:q_lo + BLOCK_Q]) for h in heads], axis=0)
            jobs.append((kmat, q_rows, heads, [h // B_REP for h in heads], vt1s, block_variants[blk], None))
            job_block.append(blk)
    o_t, lse_t = {}, {}
    for job, blk, (ot, m, den) in zip(jobs, job_block, _band_pipeline(jobs, bias_ref, s_buf)):
        lse = (m + jnp.log2(den)) * LN2
        for n, h in enumerate(job[2]):
            o_t[blk, h] = ot[:, n * BLOCK_Q:(n + 1) * BLOCK_Q]
            lse_t[blk, h] = jnp.broadcast_to(lse[:, n * BLOCK_Q:(n + 1) * BLOCK_Q], (HEAD_DIM, BLOCK_Q))
    for blk in range(B_TILE // BLOCK_Q):
        rows = slice(blk * BLOCK_Q, (blk + 1) * BLOCK_Q)
        for j in range(B_HEADS_PER_GROUP // 2):
            cols = slice(j * LANES, (j + 1) * LANES)
            pair = lambda t: jnp.concatenate([t[blk, 2 * j], t[blk, 2 * j + 1]], axis=0).T
            o_ref[0, rows, cols] = pair(o_t).astype(o_ref.dtype)
            lse_ref[0, rows, cols] = pair(lse_t)


def _attn_b(group_qkv, b):
    n_g = len(B_GROUPS)
    gq = B_HEADS_PER_GROUP * HEAD_DIM
    slabs_per_token = B_GROUP_COLS // LANES
    q_slabs = gq // LANES
    halves_per_tile = B_TILE // B_HALF
    all_slopes = _alibi_slopes(n_g * B_HEADS_PER_GROUP)
    outs, lses = [], []
    for g, (window, dil) in enumerate(B_GROUPS):
        assert (window // 2) // dil == B_HALF
        sub = group_qkv[g].shape[0] // b
        n_tiles = sub // B_TILE
        n_halves = sub // B_HALF
        view = group_qkv[g].reshape(b, sub, dil * B_GROUP_COLS)

        def tile_spec(slab):
            return pl.BlockSpec((1, B_TILE, LANES), lambda bi, r, i: (bi, i, r * slabs_per_token + slab))

        def edge_spec(slab, offset, n_halves=n_halves):
            return pl.BlockSpec(
                (1, B_HALF, LANES),
                lambda bi, r, i: (bi, jnp.clip(i * halves_per_tile + offset, 0, n_halves - 1),
                                  r * slabs_per_token + slab))

        def kv_specs(slab):
            return [edge_spec(slab, -1), tile_spec(slab), edge_spec(slab, halves_per_tile)]

        out_map = lambda bi, r, i: (bi, i, r)
        o, lse = pl.pallas_call(
            functools.partial(_attn_b_body, dil=dil,
                              slopes=all_slopes[g * B_HEADS_PER_GROUP:(g + 1) * B_HEADS_PER_GROUP]),
            grid=(b, dil, n_tiles),
            in_specs=[tile_spec(j) for j in range(q_slabs)] + kv_specs(q_slabs) + kv_specs(q_slabs + 1),
            out_specs=[pl.BlockSpec((1, B_TILE, gq), out_map), pl.BlockSpec((1, B_TILE, gq), out_map)],
            out_shape=[jax.ShapeDtypeStruct((b, sub, dil * gq), BF16),
                       jax.ShapeDtypeStruct((b, sub, dil * gq), F32)],
            scratch_shapes=[pltpu.VMEM((3, B_HEADS_PER_GROUP, B_SPAN, BLOCK_Q), F32),
                            pltpu.VMEM((2, B_SPAN, 2 * BLOCK_Q), F32)],
            compiler_params=_params("parallel", "arbitrary", "arbitrary"),
            name=f"attn_b_g{g}",
        )(*([view] * (q_slabs + 6)))
        outs.append(o.reshape(b * sub, dil * gq))
        lses.append(lse.reshape(b * sub, dil * gq))
    return outs, lses


PROLOGUE_ROW_CHUNKS = 4


def _mix_plain(refs, scratch):
    return lambda rows: refs[0][rows, :]


def _mix_groups(refs, scratch):
    n_g = len(B_GROUPS)
    o_scr, l_scr = scratch
    slabs = o_scr.shape[0] // n_g
    tm = o_scr.shape[1]
    gq = slabs * LANES
    for g, (_, dil) in enumerate(B_GROUPS):
        for r in range(dil):
            rows = pl.ds(r, tm // dil, stride=dil)
            for s in range(slabs):
                cols = slice(r * gq + s * LANES, r * gq + (s + 1) * LANES)
                o_scr[g * slabs + s, rows, :] = refs[g][:, cols].astype(F32)
                l_scr[g * slabs + s, rows, :] = refs[n_g + g][:, cols]
    def lhs(rows):
        group = lambda scr, g: jnp.concatenate([scr[g * slabs + s, rows, :] for s in range(slabs)], axis=1)
        ls = [group(l_scr, g) for g in range(n_g)]
        mx = functools.reduce(jnp.maximum, ls)
        es = [jnp.exp(l - mx) for l in ls]
        inv = 1.0 / functools.reduce(lambda a, b: a + b, es)
        return jnp.concatenate([(group(o_scr, g) * (es[g] * inv)).astype(BF16) for g in range(n_g)], axis=1)

    return lhs


def _block_body(*refs, n_mix, mix_fn, final_norm):
    h_ref, mix_refs = refs[0], refs[1:1 + n_mix]
    wo_ref, g_ref, w1_ref, w2_ref, fg_ref, out_ref, hn_ref = refs[1 + n_mix:8 + n_mix]
    mix_scratch = refs[8 + n_mix:]
    j = pl.program_id(1)

    @pl.when(j == 0)
    def _():
        lhs = mix_fn(mix_refs, mix_scratch)
        chunk = h_ref.shape[0] // PROLOGUE_ROW_CHUNKS
        rows = [slice(c * chunk, (c + 1) * chunk) for c in range(PROLOGUE_ROW_CHUNKS)]
        project = lambda c: jnp.dot(lhs(rows[c]), wo_ref[...], preferred_element_type=F32)
        nxt = project(0)
        for c in range(PROLOGUE_ROW_CHUNKS):
            cur = nxt
            if c + 1 < PROLOGUE_ROW_CHUNKS:
                nxt = project(c + 1)
            h1 = h_ref[rows[c], :] + cur
            out_ref[rows[c], :] = h1
            hn_ref[rows[c], :] = _rms_rows(h1, g_ref[...]).astype(BF16)

    u = jnp.maximum(jnp.dot(hn_ref[...], w1_ref[...], preferred_element_type=F32), 0.0)
    out_ref[...] += jnp.dot((u * u).astype(BF16), w2_ref[...], preferred_element_type=F32)

    if final_norm:
        @pl.when(j == pl.num_programs(1) - 1)
        def _():
            out_ref[...] = _rms_rows(out_ref[...], fg_ref[...])


def _block(h2, mix_inputs, w_o, *, gain, w1, w2, final_gain, final_norm, tm=1024, tf=512):
    t, d = h2.shape
    ff = w1.shape[1]
    once = pl.Buffered(1)
    row_tile = lambda cols: pl.BlockSpec((tm, cols), lambda i, j: (i, 0))
    whole = lambda a: pl.BlockSpec(a.shape, lambda i, j: (0, 0), pipeline_mode=once)
    scratch = [pltpu.VMEM((tm, d), BF16)]
    if len(mix_inputs) == 1:
        mix_fn, mix_specs = _mix_plain, [row_tile(mix_inputs[0].shape[1])]
    else:
        n_g = len(B_GROUPS)
        gq = B_HEADS_PER_GROUP * HEAD_DIM
        mix_fn = _mix_groups
        mix_specs = [pl.BlockSpec((tm // dil, dil * gq), lambda i, j: (i, 0), pipeline_mode=once)
                     for _, dil in B_GROUPS] * 2
        scratch += [pltpu.VMEM((n_g * gq // LANES, tm, LANES), F32)] * 2
    return pl.pallas_call(
        functools.partial(_block_body, n_mix=len(mix_inputs), mix_fn=mix_fn, final_norm=final_norm),
        grid=(t // tm, ff // tf),
        in_specs=[row_tile(d)] + mix_specs
                 + [whole(w_o), whole(gain),
                    pl.BlockSpec((d, tf), lambda i, j: (0, j)),
                    pl.BlockSpec((tf, d), lambda i, j: (j, 0)),
                    whole(final_gain)],
        out_specs=row_tile(d),
        out_shape=jax.ShapeDtypeStruct((t, d), F32),
        scratch_shapes=scratch,
        compiler_params=_params("parallel", "arbitrary"),
        name="block_mlp",
    )(h2, *mix_inputs, w_o, gain, w1, w2, final_gain)


def kernel(x, attn_norm, mlp_norm, a_w_qkv, a_q_gain, a_k_gain, a_w_o, b_w_qkv, b_w_o,
           c_w_qkv, c_sinks, c_w_o, mlp_w1, mlp_w2, final_norm):
    b, seq, d = x.shape
    depth = attn_norm.shape[0]
    h = x.reshape(b * seq, d)
    scale = HEAD_DIM ** -0.5 * LOG2E
    a_heads = a_w_o.shape[1] // HEAD_DIM
    c_heads = c_w_o.shape[1] // HEAD_DIM
    cos, sin = _rope_tables(seq)
    head = jnp.arange(2 * LANES) // HEAD_DIM
    seg = (head[:, None] == head[None, :]).astype(BF16)
    tile4 = lambda g: jnp.tile(g, 2 * LANES // HEAD_DIM)[None, :]
    used = [0, 0, 0]
    for layer in range(depth):
        kind = layer % N_MIXERS
        j = used[kind]
        used[kind] += 1
        gain = attn_norm[layer][None, :]
        block = functools.partial(_block, gain=mlp_norm[layer][None, :], w1=mlp_w1[layer].astype(BF16),
                                  w2=mlp_w2[layer].astype(BF16), final_gain=final_norm[None, :],
                                  final_norm=(layer == depth - 1))
        if kind == 0:
            q_cols = a_heads * HEAD_DIM
            k_cols = (a_w_qkv.shape[2] - q_cols) // 2
            qkv = _qkv_proj(h, gain, a_w_qkv[j].astype(BF16), q_cols=q_cols, q_scale=scale,
                            rope=(tile4(a_q_gain[j]), tile4(a_k_gain[j]), cos, sin, seg, k_cols))
            o = _attn_a(qkv.reshape(b, seq, -1), n_heads=a_heads).reshape(b * seq, q_cols)
            h = block(h, [o], a_w_o[j].astype(BF16))
        elif kind == 1:
            q_cols = len(B_GROUPS) * B_HEADS_PER_GROUP * HEAD_DIM
            groups = _qkv_proj_b(h, gain, b_w_qkv[j].astype(BF16), q_cols=q_cols, q_scale=scale)
            outs, lses = _attn_b(groups, b)
            h = block(h, outs + lses, b_w_o[j].astype(BF16))
        else:
            q_cols = c_heads * HEAD_DIM
            qkv = _qkv_proj(h, gain, c_w_qkv[j].astype(BF16), q_cols=q_cols, q_scale=scale)
            o = _attn_c(qkv.reshape(b, seq, -1), c_sinks[j], n_heads=c_heads).reshape(b * seq, q_cols)
            h = block(h, [o], c_w_o[j].astype(BF16))
    return h.reshape(b, seq, d)
```

```python
import functools
import math

import jax
import jax.numpy as jnp
from jax import lax
from jax.experimental import pallas as pl
from jax.experimental.pallas import tpu as pltpu

HEAD_DIM = 64
HALF_HEAD = HEAD_DIM // 2
RMS_EPS = 1e-6
GRID_W = 64
ROPE_THETA = 10000.0
N_MIXERS = 3
A_REP = 4
B_GROUPS = ((128, 1), (512, 4), (2048, 16))
B_HEADS_PER_GROUP = 6
B_KV_PER_GROUP = 2
B_REP = B_HEADS_PER_GROUP // B_KV_PER_GROUP
C_WINDOW = 128
LANES = 128
BF16_SUBLANES = 16
BLOCK_Q = 128
BAND_SLOTS = 4
B_HALF = BLOCK_Q // 2
B_TILE = 2 * BLOCK_Q
B_SPAN = BLOCK_Q + 2 * B_HALF
B_GROUP_COLS = (B_HEADS_PER_GROUP + 2 * B_KV_PER_GROUP) * HEAD_DIM
VT_ROWS = HEAD_DIM + BF16_SUBLANES
V7X_VMEM_LIMIT_BYTES = 48 * 1024 * 1024
NEG_BIG = -1e30
LOG2E = math.log2(math.e)
LN2 = math.log(2.0)

BF16 = jnp.bfloat16
F32 = jnp.float32


def _params(*sem):
    return pltpu.CompilerParams(dimension_semantics=sem, vmem_limit_bytes=V7X_VMEM_LIMIT_BYTES)


def _nt_dot(a, b):
    return lax.dot_general(a, b, (((1,), (1,)), ((), ())), preferred_element_type=F32)


def _rms_rows(x, gain):
    ms = jnp.mean(x * x, axis=-1, keepdims=True)
    return x * lax.rsqrt(ms + RMS_EPS) * gain


def _qkv_body(x_ref, g_ref, w_ref, o_ref, *, q_cols, q_scale):
    hn = _rms_rows(x_ref[...], g_ref[...]).astype(BF16)
    y = jnp.dot(hn, w_ref[...], preferred_element_type=F32)
    o_ref[:, :q_cols] = (y[:, :q_cols] * q_scale).astype(o_ref.dtype)
    o_ref[:, q_cols:] = y[:, q_cols:].astype(o_ref.dtype)


def _qkv_rope_body(x_ref, g_ref, w_ref, qg_ref, kg_ref, cos_ref, sin_ref, seg_ref, o_ref,
                   *, q_cols, k_cols, q_scale):
    hn = _rms_rows(x_ref[...], g_ref[...]).astype(BF16)
    slab = 2 * LANES
    cos = jnp.concatenate([cos_ref[...], cos_ref[...]], axis=1)
    sin = jnp.concatenate([sin_ref[...], sin_ref[...]], axis=1)
    lane = lax.broadcasted_iota(jnp.int32, (x_ref.shape[0], slab), 1)
    first_half = (lane % HALF_HEAD) < (HALF_HEAD // 2)
    seg = seg_ref[...]
    n_qk = (q_cols + k_cols) // slab
    project = lambda s: jnp.dot(hn, w_ref[:, s * slab:(s + 1) * slab], preferred_element_type=F32)
    y_next = project(0)
    for s in range(n_qk):
        is_q = s * slab < q_cols
        ys = y_next
        y_next = project(s + 1) if s + 1 < n_qk else jnp.dot(
            hn, w_ref[:, q_cols + k_cols:], preferred_element_type=F32)
        sq = ys * ys
        hi = sq.astype(BF16)
        lo = (sq - hi.astype(F32)).astype(BF16)
        ss = jnp.dot(hi, seg, preferred_element_type=F32) + jnp.dot(lo, seg, preferred_element_type=F32)
        gain = qg_ref[...] if is_q else kg_ref[...]
        yn = ys * lax.rsqrt(ss * (1.0 / HEAD_DIM) + RMS_EPS) * gain
        partner = jnp.where(first_half,
                            pltpu.roll(yn, slab - HALF_HEAD // 2, axis=1),
                            pltpu.roll(yn, HALF_HEAD // 2, axis=1))
        r = yn * cos + partner * sin
        if is_q:
            r = r * q_scale
        o_ref[:, s * slab:(s + 1) * slab] = r.astype(o_ref.dtype)
    o_ref[:, q_cols + k_cols:] = y_next.astype(o_ref.dtype)


def _qkv_proj(h2, gain, w, *, q_cols, q_scale, tm=512, rope=None):
    t, d = h2.shape
    n = w.shape[1]
    common = dict(
        grid=(t // tm,),
        out_specs=pl.BlockSpec((tm, n), lambda i: (i, 0)),
        out_shape=jax.ShapeDtypeStruct((t, n), BF16),
        compiler_params=_params("parallel"),
    )
    x_spec = pl.BlockSpec((tm, d), lambda i: (i, 0))
    g_spec = pl.BlockSpec((1, d), lambda i: (0, 0))
    w_spec = pl.BlockSpec((d, n), lambda i: (0, 0))
    if rope is None:
        return pl.pallas_call(
            functools.partial(_qkv_body, q_cols=q_cols, q_scale=q_scale),
            in_specs=[x_spec, g_spec, w_spec], name="qkv_proj", **common,
        )(h2, gain, w)
    qg, kg, cos, sin, seg, k_cols = rope
    seq_tiles = cos.shape[0] // tm
    slab = 2 * LANES
    return pl.pallas_call(
        functools.partial(_qkv_rope_body, q_cols=q_cols, k_cols=k_cols, q_scale=q_scale),
        in_specs=[x_spec, g_spec, w_spec,
                  pl.BlockSpec((1, slab), lambda i: (0, 0)),
                  pl.BlockSpec((1, slab), lambda i: (0, 0)),
                  pl.BlockSpec((tm, LANES), lambda i: (i % seq_tiles, 0)),
                  pl.BlockSpec((tm, LANES), lambda i: (i % seq_tiles, 0)),
                  pl.BlockSpec((slab, slab), lambda i: (0, 0))],
        name="qkv_proj_rope", **common,
    )(h2, gain, w, qg, kg, cos, sin, seg)


def _qkv_b_body(x_ref, g_ref, w_ref, *rest, q_cols, q_scale):
    out_refs, y_scr = rest[:-1], rest[-1]
    tm = x_ref.shape[0]
    hn = _rms_rows(x_ref[...], g_ref[...]).astype(BF16)
    y = jnp.dot(hn, w_ref[...], preferred_element_type=F32)
    for s in range(y_scr.shape[0]):
        ys = y[:, s * LANES:(s + 1) * LANES]
        y_scr[s] = ys * q_scale if s * LANES < q_cols else ys
    n_g = len(B_GROUPS)
    q_slabs = B_HEADS_PER_GROUP * HEAD_DIM // LANES
    k0 = q_cols // LANES
    for g, (_, dil) in enumerate(B_GROUPS):
        slabs = list(range(g * q_slabs, (g + 1) * q_slabs)) + [k0 + g, k0 + n_g + g]
        for r in range(dil):
            rows = pl.ds(r, tm // dil, stride=dil)
            piece = jnp.concatenate([y_scr[s, rows, :] for s in slabs], axis=1)
            out_refs[g][:, r * B_GROUP_COLS:(r + 1) * B_GROUP_COLS] = piece.astype(BF16)


def _qkv_proj_b(h2, gain, w, *, q_cols, q_scale, tm=512):
    t, d = h2.shape
    n = w.shape[1]
    return pl.pallas_call(
        functools.partial(_qkv_b_body, q_cols=q_cols, q_scale=q_scale),
        grid=(t // tm,),
        in_specs=[pl.BlockSpec((tm, d), lambda i: (i, 0)),
                  pl.BlockSpec((1, d), lambda i: (0, 0)),
                  pl.BlockSpec((d, n), lambda i: (0, 0))],
        out_specs=[pl.BlockSpec((tm // dil, dil * B_GROUP_COLS), lambda i: (i, 0)) for _, dil in B_GROUPS],
        out_shape=[jax.ShapeDtypeStruct((t // dil, dil * B_GROUP_COLS), BF16) for _, dil in B_GROUPS],
        scratch_shapes=[pltpu.VMEM((n // LANES, tm, LANES), F32)],
        compiler_params=_params("parallel"),
        name="qkv_proj_b",
    )(h2, gain, w)


def _rope_tables(seq):
    t = jnp.arange(seq)
    row = (t // GRID_W).astype(F32)
    col = (t % GRID_W).astype(F32)
    inv_freq = ROPE_THETA ** (-jnp.arange(0, HALF_HEAD, 2, dtype=F32) / HALF_HEAD)
    ang_row = row[:, None] * inv_freq
    ang_col = col[:, None] * inv_freq
    cos = jnp.concatenate([jnp.cos(ang_row)] * 2 + [jnp.cos(ang_col)] * 2, axis=1)
    sin = jnp.concatenate([-jnp.sin(ang_row), jnp.sin(ang_row), -jnp.sin(ang_col), jnp.sin(ang_col)], axis=1)
    return jnp.concatenate([cos, cos], axis=1), jnp.concatenate([sin, sin], axis=1)


def _attn_a_body(q_ref, k_ref, v_ref, o_ref, klo, khi, vt1, s_buf, *, tq, tk, seq, unroll):
    def fill(head_in_high_lanes):
        low = lax.broadcasted_iota(jnp.int32, (seq, LANES), 1) < HEAD_DIM
        zero = jnp.zeros((seq, LANES), BF16)
        x = k_ref[0]
        swapped = jnp.concatenate([x[:, HEAD_DIM:], x[:, :HEAD_DIM]], axis=1)
        in_low, in_high = (swapped, x) if head_in_high_lanes else (x, swapped)
        klo[...] = jnp.where(low, in_low, zero)
        khi[...] = jnp.where(low, zero, in_high)
        ones_row = (lax.broadcasted_iota(jnp.int32, (VT_ROWS - HEAD_DIM, tk), 0) == 0).astype(F32)
        for c in range(seq // tk):
            xt = v_ref[0, c * tk:(c + 1) * tk, :].astype(F32).T
            vt = xt[HEAD_DIM:] if head_in_high_lanes else xt[:HEAD_DIM]
            vt1[:, c * tk:(c + 1) * tk] = jnp.concatenate([vt, ones_row], axis=0).astype(BF16)

    odd = (pl.program_id(1) % 2) == 1
    pl.when(jnp.logical_not(odd))(lambda: fill(False))
    pl.when(odd)(lambda: fill(True))

    cols = 2 * tq
    n_chunks = seq // tk
    n_tiles = seq // tq

    def scores(tile, c, slot):
        q = q_ref[0, pl.ds(pl.multiple_of(tile * tq, tq), tq), :]
        q2 = jnp.concatenate([q[:, :LANES], q[:, LANES:]], axis=0)
        ks = pl.multiple_of(c * tk, tk)
        cms = []
        for par, k_s in enumerate((klo, khi)):
            s = _nt_dot(k_s[pl.ds(ks, tk), :], q2)
            s_buf[slot, par] = s
            cms.append(jnp.max(s, axis=0, keepdims=True))
        return tuple(cms)

    def accumulate(c, slot, cms, state):
        vt = vt1[:, pl.ds(pl.multiple_of(c * tk, tk), tk)]
        new = []
        for par in range(2):
            m, acc = state[2 * par], state[2 * par + 1]
            mn = jnp.maximum(m, cms[par])
            p = jnp.exp2(s_buf[slot, par] - mn)
            new += [mn, acc * jnp.exp2(m - mn) + jnp.dot(vt, p.astype(BF16), preferred_element_type=F32)]
        return tuple(new)

    neg = jnp.full((1, cols), -jnp.inf, F32)
    acc0 = jnp.zeros((VT_ROWS, cols), F32)

    steps_per_tile = n_chunks // unroll

    n_slots = s_buf.shape[0]
    ahead = n_slots - 1

    def flat_scores(f, slot):
        return scores(jnp.minimum(f // n_chunks, n_tiles - 1), f % n_chunks, slot)

    def step(j, carry):
        tile, c0 = j // steps_per_tile, (j % steps_per_tile) * unroll
        cms, (m_e, acc_e, m_o, acc_o) = list(carry[:2 * ahead]), carry[2 * ahead:]
        fresh = c0 == 0
        state = (jnp.where(fresh, neg, m_e), acc_e, jnp.where(fresh, neg, m_o), acc_o)
        for u in range(unroll):
            cms += flat_scores(j * unroll + u + ahead, (u + ahead) % n_slots)
            state = accumulate(c0 + u, u % n_slots, cms[:2], state)
            cms = cms[2:]

        @pl.when(c0 + unroll == n_chunks)
        def _():
            out = jnp.concatenate(
                [acc[:HEAD_DIM] / acc[HEAD_DIM:HEAD_DIM + 1] for acc in (state[1], state[3])], axis=0)
            o_ref[0, pl.ds(pl.multiple_of(tile * tq, tq), tq), :] = jnp.concatenate(
                [out[:, :tq].T, out[:, tq:].T], axis=1).astype(o_ref.dtype)

        return tuple(cms) + state

    first = sum((flat_scores(f, f) for f in range(ahead)), ())
    lax.fori_loop(0, n_tiles * steps_per_tile, step, first + (neg, acc0, neg, acc0))


def _attn_a(qkv, *, n_heads, tq=256, tk=512, unroll=8, slots=2):
    b, seq, _ = qkv.shape
    kv_heads = n_heads // A_REP
    q_cols = n_heads * HEAD_DIM
    q_w = A_REP * HEAD_DIM
    k_blk0 = q_cols // LANES
    v_blk0 = (q_cols + kv_heads * HEAD_DIM) // LANES
    assert unroll % slots == 0 and (seq // tk) % unroll == 0 and seq % tq == 0
    return pl.pallas_call(
        functools.partial(_attn_a_body, tq=tq, tk=tk, seq=seq, unroll=unroll),
        grid=(b, kv_heads),
        in_specs=[pl.BlockSpec((1, seq, q_w), lambda bi, h: (bi, 0, h)),
                  pl.BlockSpec((1, seq, LANES), lambda bi, h: (bi, 0, k_blk0 + h // 2)),
                  pl.BlockSpec((1, seq, LANES), lambda bi, h: (bi, 0, v_blk0 + h // 2))],
        out_specs=pl.BlockSpec((1, seq, q_w), lambda bi, h: (bi, 0, h)),
        out_shape=jax.ShapeDtypeStruct((b, seq, q_cols), BF16),
        scratch_shapes=[pltpu.VMEM((seq, LANES), BF16), pltpu.VMEM((seq, LANES), BF16),
                        pltpu.VMEM((VT_ROWS, seq), BF16), pltpu.VMEM((slots, 2, tk, 2 * tq), F32)],
        compiler_params=_params("parallel", "arbitrary"),
        name="attn_a",
    )(qkv, qkv, qkv)


def _fill_band_bias(bias_ref, slopes, half_window, dist_scale):
    span = bias_ref.shape[2]
    c = lax.broadcasted_iota(jnp.int32, (span, BLOCK_Q), 0)
    r = lax.broadcasted_iota(jnp.int32, (span, BLOCK_Q), 1)
    dist = jnp.abs(c - half_window - r)
    inside = dist <= half_window
    penalty = dist.astype(F32) * (dist_scale * LOG2E)
    for variant, ok in enumerate((inside & (c >= half_window), inside, inside & (c < half_window + BLOCK_Q))):
        for h, slope in enumerate(slopes):
            bias_ref[variant, h] = jnp.where(ok, -slope * penalty, NEG_BIG)


def _band_pipeline(jobs, bias_ref, s_buf):
    def scores(job, slot):
        kmat, q_rows, head_ids, _, _, variant, sink_row = job
        width = len(head_ids) * BLOCK_Q
        s = _nt_dot(kmat, q_rows) + jnp.concatenate([bias_ref[variant, h] for h in head_ids], axis=1)
        s_buf[slot, :, :width] = s
        m = jnp.max(s, axis=0, keepdims=True)
        return m if sink_row is None else jnp.maximum(m, sink_row)

    def values(job, slot, m):
        _, _, head_ids, kv_ids, vt1s, _, sink_row = job
        n = len(head_ids)
        p = jnp.exp2(s_buf[slot, :, :n * BLOCK_Q] - m).astype(BF16)
        accs, start = [], 0
        while start < n:
            stop = start
            while stop < n and kv_ids[stop] == kv_ids[start]:
                stop += 1
            accs.append(jnp.dot(vt1s[kv_ids[start]], p[:, start * BLOCK_Q:stop * BLOCK_Q],
                                preferred_element_type=F32))
            start = stop
        acc = accs[0] if len(accs) == 1 else jnp.concatenate(accs, axis=1)
        den = acc[HEAD_DIM:HEAD_DIM + 1]
        if sink_row is not None:
            den = den + jnp.exp2(sink_row - m)
        return acc[:HEAD_DIM] / den, m, den

    results = []
    n_slots = s_buf.shape[0]
    ahead = n_slots - 1
    maxima = [scores(jobs[c], c % n_slots) for c in range(min(ahead, len(jobs)))]
    for c, job in enumerate(jobs):
        if c + ahead < len(jobs):
            maxima.append(scores(jobs[c + ahead], (c + ahead) % n_slots))
        results.append(values(job, c % n_slots, maxima[c]))
    return results


def _band_operands(k_refs, v_refs):
    k = jnp.concatenate([r[0] for r in k_refs], axis=0)
    vt = jnp.concatenate([r[0] for r in v_refs], axis=0).astype(F32).T
    return k, vt


def _vt_with_ones(vt, kv, start, span):
    ones_row = (lax.broadcasted_iota(jnp.int32, (VT_ROWS - HEAD_DIM, span), 0) == 0).astype(F32)
    rows = vt[kv * HEAD_DIM:(kv + 1) * HEAD_DIM, start:start + span]
    return jnp.concatenate([rows, ones_row], axis=0).astype(BF16)


def _swap_halves(x):
    return jnp.concatenate([x[:, HEAD_DIM:], x[:, :HEAD_DIM]], axis=1)


def _attn_c_body(sink_ref, q_ref, kp_ref, kc_ref, kn_ref, vp_ref, vc_ref, vn_ref, o_ref, bias_ref, s_buf,
                 *, n_heads, slopes):
    i = pl.program_id(1)
    pl.when(i == 0)(lambda: _fill_band_bias(bias_ref, slopes, C_WINDOW, 1.0))
    variant = jnp.where(i == 0, 0, jnp.where(i == pl.num_programs(1) - 1, 2, 1))
    q = q_ref[0]
    k, vt = _band_operands((kp_ref, kc_ref, kn_ref), (vp_ref, vc_ref, vn_ref))
    span = k.shape[0]
    low = lax.broadcasted_iota(jnp.int32, (BLOCK_Q, LANES), 1) < HEAD_DIM
    zero = jnp.zeros((BLOCK_Q, LANES), BF16)
    first_block = lax.broadcasted_iota(jnp.int32, (1, 2 * BLOCK_Q), 1) < BLOCK_Q
    jobs = []
    for pair in range(n_heads // A_REP // 2):
        ks = k[:, pair * LANES:(pair + 1) * LANES]
        ks_swapped = _swap_halves(ks)
        for which in range(2):
            kv = 2 * pair + which
            vt1s = {kv: _vt_with_ones(vt, kv, 0, span)}
            slabs = [q[:, (2 * kv + j) * LANES:(2 * kv + j + 1) * LANES] for j in range(2)]
            for parity in range(2):
                q_rows = jnp.concatenate(
                    [jnp.where(low, s_, zero) if parity == 0 else jnp.where(low, zero, s_) for s_ in slabs],
                    axis=0)
                heads = (A_REP * kv + parity, A_REP * kv + 2 + parity)
                sink_row = jnp.where(first_block, sink_ref[heads[0]], sink_ref[heads[1]]) * LOG2E
                jobs.append((ks if parity == which else ks_swapped, q_rows, heads, (kv, kv), vt1s, variant,
                             sink_row))
    o_t = [None] * n_heads
    for job, (ot, _, _) in zip(jobs, _band_pipeline(jobs, bias_ref, s_buf)):
        o_t[job[2][0]], o_t[job[2][1]] = ot[:, :BLOCK_Q], ot[:, BLOCK_Q:]
    for j in range(n_heads // 2):
        pair_t = jnp.concatenate([o_t[2 * j], o_t[2 * j + 1]], axis=0)
        o_ref[0, :, j * LANES:(j + 1) * LANES] = pair_t.T.astype(o_ref.dtype)


def _alibi_slopes(n):
    return [2.0 ** (-8.0 * (i + 1) / n) for i in range(n)]


def _attn_c(qkv, sinks, *, n_heads):
    b, seq, _ = qkv.shape
    kv_heads = n_heads // A_REP
    q_cols = n_heads * HEAD_DIM
    kv_cols = kv_heads * HEAD_DIM
    n_tiles = seq // BLOCK_Q
    k_blk = q_cols // kv_cols
    v_blk = k_blk + 1
    span = BLOCK_Q + 2 * C_WINDOW

    def kv_spec(col_blk, shift):
        return pl.BlockSpec(
            (1, BLOCK_Q, kv_cols),
            lambda bi, i: (bi, jnp.clip(i + shift, 0, n_tiles - 1), col_blk))

    return pl.pallas_call(
        functools.partial(_attn_c_body, n_heads=n_heads, slopes=_alibi_slopes(n_heads)),
        grid=(b, n_tiles),
        in_specs=[pl.BlockSpec(memory_space=pltpu.SMEM),
                  pl.BlockSpec((1, BLOCK_Q, q_cols), lambda bi, i: (bi, i, 0)),
                  kv_spec(k_blk, -1), kv_spec(k_blk, 0), kv_spec(k_blk, 1),
                  kv_spec(v_blk, -1), kv_spec(v_blk, 0), kv_spec(v_blk, 1)],
        out_specs=pl.BlockSpec((1, BLOCK_Q, q_cols), lambda bi, i: (bi, i, 0)),
        out_shape=jax.ShapeDtypeStruct((b, seq, q_cols), BF16),
        scratch_shapes=[pltpu.VMEM((3, n_heads, span, BLOCK_Q), F32),
                        pltpu.VMEM((BAND_SLOTS, span, 2 * BLOCK_Q), F32)],
        compiler_params=_params("parallel", "arbitrary"),
        name="attn_c",
    )(sinks, qkv, qkv, qkv, qkv, qkv, qkv, qkv)


def _attn_b_body(q0_ref, q1_ref, q2_ref, kp_ref, kc_ref, kn_ref, vp_ref, vc_ref, vn_ref, o_ref, lse_ref,
                 bias_ref, s_buf, *, slopes, dil):
    i = pl.program_id(2)
    pl.when(jnp.logical_and(pl.program_id(1) == 0, i == 0))(
        lambda: _fill_band_bias(bias_ref, slopes, B_HALF, float(dil)))
    block_variants = (jnp.where(i == 0, 0, 1), jnp.where(i == pl.num_programs(2) - 1, 2, 1))
    k, vt = _band_operands((kp_ref, kc_ref, kn_ref), (vp_ref, vc_ref, vn_ref))
    low = lax.broadcasted_iota(jnp.int32, (BLOCK_Q, LANES), 1) < HEAD_DIM
    zero = jnp.zeros((BLOCK_Q, LANES), BF16)
    slabs = [q0_ref[0], q1_ref[0], q2_ref[0]]
    heads_all = range(B_HEADS_PER_GROUP)
    plain = [h for h in heads_all if h % 2 == h // B_REP]
    crossed = [h for h in heads_all if h % 2 != h // B_REP]
    jobs, job_block = [], []
    for blk in range(B_TILE // BLOCK_Q):
        q_lo = blk * BLOCK_Q
        ks = k[q_lo:q_lo + B_SPAN]
        ks_swapped = _swap_halves(ks)
        vt1s = {kv: _vt_with_ones(vt, kv, q_lo, B_SPAN) for kv in range(B_KV_PER_GROUP)}
        for kmat, heads in ((ks, plain[:2]), (ks_swapped, crossed), (ks, plain[2:])):
            q_rows = jnp.concatenate(
                [jnp.where(low, slabs[h // 2][q_lo:q_lo + BLOCK_Q], zero) if h % 2 == 0
                 else jnp.where(low, zero, slabs[h // 2][q_lo:q_lo + BLOCK_Q]) for h in heads], axis=0)
            jobs.append((kmat, q_rows, heads, [h // B_REP for h in heads], vt1s, block_variants[blk], None))
            job_block.append(blk)
    o_t, lse_t = {}, {}
    for job, blk, (ot, m, den) in zip(jobs, job_block, _band_pipeline(jobs, bias_ref, s_buf)):
        lse = (m + jnp.log2(den)) * LN2
        for n, h in enumerate(job[2]):
            o_t[blk, h] = ot[:, n * BLOCK_Q:(n + 1) * BLOCK_Q]
            lse_t[blk, h] = jnp.broadcast_to(lse[:, n * BLOCK_Q:(n + 1) * BLOCK_Q], (HEAD_DIM, BLOCK_Q))
    for blk in range(B_TILE // BLOCK_Q):
        rows = slice(blk * BLOCK_Q, (blk + 1) * BLOCK_Q)
        for j in range(B_HEADS_PER_GROUP // 2):
            cols = slice(j * LANES, (j + 1) * LANES)
            pair = lambda t: jnp.concatenate([t[blk, 2 * j], t[blk, 2 * j + 1]], axis=0).T
            o_ref[0, rows, cols] = pair(o_t).astype(o_ref.dtype)
            lse_ref[0, rows, cols] = pair(lse_t)


def _attn_b(group_qkv, b):
    n_g = len(B_GROUPS)
    gq = B_HEADS_PER_GROUP * HEAD_DIM
    slabs_per_token = B_GROUP_COLS // LANES
    q_slabs = gq // LANES
    halves_per_tile = B_TILE // B_HALF
    all_slopes = _alibi_slopes(n_g * B_HEADS_PER_GROUP)
    outs, lses = [], []
    for g, (window, dil) in enumerate(B_GROUPS):
        assert (window // 2) // dil == B_HALF
        sub = group_qkv[g].shape[0] // b
        n_tiles = sub // B_TILE
        n_halves = sub // B_HALF
        view = group_qkv[g].reshape(b, sub, dil * B_GROUP_COLS)

        def tile_spec(slab):
            return pl.BlockSpec((1, B_TILE, LANES), lambda bi, r, i: (bi, i, r * slabs_per_token + slab))

        def edge_spec(slab, offset, n_halves=n_halves):
            return pl.BlockSpec(
                (1, B_HALF, LANES),
                lambda bi, r, i: (bi, jnp.clip(i * halves_per_tile + offset, 0, n_halves - 1),
                                  r * slabs_per_token + slab))

        def kv_specs(slab):
            return [edge_spec(slab, -1), tile_spec(slab), edge_spec(slab, halves_per_tile)]

        out_map = lambda bi, r, i: (bi, i, r)
        o, lse = pl.pallas_call(
            functools.partial(_attn_b_body, dil=dil,
                              slopes=all_slopes[g * B_HEADS_PER_GROUP:(g + 1) * B_HEADS_PER_GROUP]),
            grid=(b, dil, n_tiles),
            in_specs=[tile_spec(j) for j in range(q_slabs)] + kv_specs(q_slabs) + kv_specs(q_slabs + 1),
            out_specs=[pl.BlockSpec((1, B_TILE, gq), out_map), pl.BlockSpec((1, B_TILE, gq), out_map)],
            out_shape=[jax.ShapeDtypeStruct((b, sub, dil * gq), BF16),
                       jax.ShapeDtypeStruct((b, sub, dil * gq), F32)],
            scratch_shapes=[pltpu.VMEM((3, B_HEADS_PER_GROUP, B_SPAN, BLOCK_Q), F32),
                            pltpu.VMEM((BAND_SLOTS, B_SPAN, 2 * BLOCK_Q), F32)],
            compiler_params=_params("parallel", "arbitrary", "arbitrary"),
            name=f"attn_b_g{g}",
        )(*([view] * (q_slabs + 6)))
        outs.append(o.reshape(b * sub, dil * gq))
        lses.append(lse.reshape(b * sub, dil * gq))
    return outs, lses


PROLOGUE_ROW_CHUNKS = 4


def _mix_plain(refs, scratch):
    return lambda rows: refs[0][rows, :]


def _mix_groups(refs, scratch):
    n_g = len(B_GROUPS)
    o_scr, l_scr = scratch
    slabs = o_scr.shape[0] // n_g
    tm = o_scr.shape[1]
    gq = slabs * LANES
    for g, (_, dil) in enumerate(B_GROUPS):
        for r in range(dil):
            rows = pl.ds(r, tm // dil, stride=dil)
            for s in range(slabs):
                cols = slice(r * gq + s * LANES, r * gq + (s + 1) * LANES)
                o_scr[g * slabs + s, rows, :] = refs[g][:, cols].astype(F32)
                l_scr[g * slabs + s, rows, :] = refs[n_g + g][:, cols]
    def lhs(rows):
        group = lambda scr, g: jnp.concatenate([scr[g * slabs + s, rows, :] for s in range(slabs)], axis=1)
        ls = [group(l_scr, g) for g in range(n_g)]
        mx = functools.reduce(jnp.maximum, ls)
        es = [jnp.exp(l - mx) for l in ls]
        inv = 1.0 / functools.reduce(lambda a, b: a + b, es)
        return jnp.concatenate([(group(o_scr, g) * (es[g] * inv)).astype(BF16) for g in range(n_g)], axis=1)

    return lhs


def _block_body(*refs, n_mix, mix_fn, final_norm):
    h_ref, mix_refs = refs[0], refs[1:1 + n_mix]
    wo_ref, g_ref, w1_ref, w2_ref, fg_ref, out_ref, hn_ref = refs[1 + n_mix:8 + n_mix]
    mix_scratch = refs[8 + n_mix:]
    j = pl.program_id(1)

    @pl.when(j == 0)
    def _():
        lhs = mix_fn(mix_refs, mix_scratch)
        chunk = h_ref.shape[0] // PROLOGUE_ROW_CHUNKS
        rows = [slice(c * chunk, (c + 1) * chunk) for c in range(PROLOGUE_ROW_CHUNKS)]
        project = lambda c: jnp.dot(lhs(rows[c]), wo_ref[...], preferred_element_type=F32)
        nxt = project(0)
        for c in range(PROLOGUE_ROW_CHUNKS):
            cur = nxt
            if c + 1 < PROLOGUE_ROW_CHUNKS:
                nxt = project(c + 1)
            h1 = h_ref[rows[c], :] + cur
            out_ref[rows[c], :] = h1
            hn_ref[rows[c], :] = _rms_rows(h1, g_ref[...]).astype(BF16)

    u = jnp.maximum(jnp.dot(hn_ref[...], w1_ref[...], preferred_element_type=F32), 0.0)
    out_ref[...] += jnp.dot((u * u).astype(BF16), w2_ref[...], preferred_element_type=F32)

    if final_norm:
        @pl.when(j == pl.num_programs(1) - 1)
        def _():
            out_ref[...] = _rms_rows(out_ref[...], fg_ref[...])


def _block(h2, mix_inputs, w_o, *, gain, w1, w2, final_gain, final_norm, tm=1024):
    t, d = h2.shape
    ff = w1.shape[1]
    tf = 1024 if len(mix_inputs) == 1 else 512
    once = pl.Buffered(1)
    row_tile = lambda cols: pl.BlockSpec((tm, cols), lambda i, j: (i, 0))
    whole = lambda a: pl.BlockSpec(a.shape, lambda i, j: (0, 0), pipeline_mode=once)
    scratch = [pltpu.VMEM((tm, d), BF16)]
    if len(mix_inputs) == 1:
        mix_fn, mix_specs = _mix_plain, [row_tile(mix_inputs[0].shape[1])]
    else:
        n_g = len(B_GROUPS)
        gq = B_HEADS_PER_GROUP * HEAD_DIM
        mix_fn = _mix_groups
        mix_specs = [pl.BlockSpec((tm // dil, dil * gq), lambda i, j: (i, 0), pipeline_mode=once)
                     for _, dil in B_GROUPS] * 2
        scratch += [pltpu.VMEM((n_g * gq // LANES, tm, LANES), F32)] * 2
    return pl.pallas_call(
        functools.partial(_block_body, n_mix=len(mix_inputs), mix_fn=mix_fn, final_norm=final_norm),
        grid=(t // tm, ff // tf),
        in_specs=[row_tile(d)] + mix_specs
                 + [whole(w_o), whole(gain),
                    pl.BlockSpec((d, tf), lambda i, j: (0, j)),
                    pl.BlockSpec((tf, d), lambda i, j: (j, 0)),
                    whole(final_gain)],
        out_specs=row_tile(d),
        out_shape=jax.ShapeDtypeStruct((t, d), F32),
        scratch_shapes=scratch,
        compiler_params=_params("parallel", "arbitrary"),
        name="block_mlp",
    )(h2, *mix_inputs, w_o, gain, w1, w2, final_gain)


def kernel(x, attn_norm, mlp_norm, a_w_qkv, a_q_gain, a_k_gain, a_w_o, b_w_qkv, b_w_o,
           c_w_qkv, c_sinks, c_w_o, mlp_w1, mlp_w2, final_norm):
    b, seq, d = x.shape
    depth = attn_norm.shape[0]
    h = x.reshape(b * seq, d)
    scale = HEAD_DIM ** -0.5 * LOG2E
    a_heads = a_w_o.shape[1] // HEAD_DIM
    c_heads = c_w_o.shape[1] // HEAD_DIM
    cos, sin = _rope_tables(seq)
    head = jnp.arange(2 * LANES) // HEAD_DIM
    seg = (head[:, None] == head[None, :]).astype(BF16)
    tile4 = lambda g: jnp.tile(g, 2 * LANES // HEAD_DIM)[None, :]
    used = [0, 0, 0]
    for layer in range(depth):
        kind = layer % N_MIXERS
        j = used[kind]
        used[kind] += 1
        gain = attn_norm[layer][None, :]
        block = functools.partial(_block, gain=mlp_norm[layer][None, :], w1=mlp_w1[layer].astype(BF16),
                                  w2=mlp_w2[layer].astype(BF16), final_gain=final_norm[None, :],
                                  final_norm=(layer == depth - 1))
        if kind == 0:
            q_cols = a_heads * HEAD_DIM
            k_cols = (a_w_qkv.shape[2] - q_cols) // 2
            qkv = _qkv_proj(h, gain, a_w_qkv[j].astype(BF16), q_cols=q_cols, q_scale=scale,
                            rope=(tile4(a_q_gain[j]), tile4(a_k_gain[j]), cos, sin, seg, k_cols))
            o = _attn_a(qkv.reshape(b, seq, -1), n_heads=a_heads).reshape(b * seq, q_cols)
            h = block(h, [o], a_w_o[j].astype(BF16))
        elif kind == 1:
            q_cols = len(B_GROUPS) * B_HEADS_PER_GROUP * HEAD_DIM
            groups = _qkv_proj_b(h, gain, b_w_qkv[j].astype(BF16), q_cols=q_cols, q_scale=scale)
            outs, lses = _attn_b(groups, b)
            h = block(h, outs + lses, b_w_o[j].astype(BF16))
        else:
            q_cols = c_heads * HEAD_DIM
            qkv = _qkv_proj(h, gain, c_w_qkv[j].astype(BF16), q_cols=q_cols, q_scale=scale)
            o = _attn_c(qkv.reshape(b, seq, -1), c_sinks[j], n_heads=c_heads).reshape(b * seq, q_cols)
            h = block(h, [o], c_w_o[j].astype(BF16))
    return h.reshape(b, seq, d)
```

```python
import functools
import math

import jax
import jax.numpy as jnp
from jax import lax
from jax.experimental import pallas as pl
from jax.experimental.pallas import tpu as pltpu

HEAD_DIM = 64
HALF_HEAD = HEAD_DIM // 2
RMS_EPS = 1e-6
GRID_W = 64
ROPE_THETA = 10000.0
N_MIXERS = 3
A_REP = 4
B_GROUPS = ((128, 1), (512, 4), (2048, 16))
B_HEADS_PER_GROUP = 6
B_KV_PER_GROUP = 2
B_REP = B_HEADS_PER_GROUP // B_KV_PER_GROUP
C_WINDOW = 128
LANES = 128
BF16_SUBLANES = 16
BLOCK_Q = 128
BAND_SLOTS = 4
B_HALF = BLOCK_Q // 2
B_TILE = 2 * BLOCK_Q
B_SPAN = BLOCK_Q + 2 * B_HALF
B_GROUP_COLS = (B_HEADS_PER_GROUP + 2 * B_KV_PER_GROUP) * HEAD_DIM
VT_ROWS = HEAD_DIM + BF16_SUBLANES
V7X_VMEM_LIMIT_BYTES = 48 * 1024 * 1024
NEG_BIG = -1e30
LOG2E = math.log2(math.e)
LN2 = math.log(2.0)

BF16 = jnp.bfloat16
F32 = jnp.float32


def _params(*sem):
    return pltpu.CompilerParams(dimension_semantics=sem, vmem_limit_bytes=V7X_VMEM_LIMIT_BYTES)


def _nt_dot(a, b):
    return lax.dot_general(a, b, (((1,), (1,)), ((), ())), preferred_element_type=F32)


def _rms_rows(x, gain):
    ms = jnp.mean(x * x, axis=-1, keepdims=True)
    return x * lax.rsqrt(ms + RMS_EPS) * gain


def _qkv_body(x_ref, g_ref, w_ref, o_ref, *, q_cols, q_scale):
    hn = _rms_rows(x_ref[...], g_ref[...]).astype(BF16)
    y = jnp.dot(hn, w_ref[...], preferred_element_type=F32)
    o_ref[:, :q_cols] = (y[:, :q_cols] * q_scale).astype(o_ref.dtype)
    o_ref[:, q_cols:] = y[:, q_cols:].astype(o_ref.dtype)


def _qkv_rope_body(x_ref, g_ref, w_ref, qg_ref, kg_ref, cos_ref, sin_ref, seg_ref, o_ref,
                   *, q_cols, k_cols, q_scale):
    hn = _rms_rows(x_ref[...], g_ref[...]).astype(BF16)
    slab = 2 * LANES
    cos = jnp.concatenate([cos_ref[...], cos_ref[...]], axis=1)
    sin = jnp.concatenate([sin_ref[...], sin_ref[...]], axis=1)
    lane = lax.broadcasted_iota(jnp.int32, (x_ref.shape[0], slab), 1)
    first_half = (lane % HALF_HEAD) < (HALF_HEAD // 2)
    seg = seg_ref[...]
    n_qk = (q_cols + k_cols) // slab
    project = lambda s: jnp.dot(hn, w_ref[:, s * slab:(s + 1) * slab], preferred_element_type=F32)
    y_next = project(0)
    for s in range(n_qk):
        is_q = s * slab < q_cols
        ys = y_next
        y_next = project(s + 1) if s + 1 < n_qk else jnp.dot(
            hn, w_ref[:, q_cols + k_cols:], preferred_element_type=F32)
        sq = ys * ys
        hi = sq.astype(BF16)
        lo = (sq - hi.astype(F32)).astype(BF16)
        ss = jnp.dot(hi, seg, preferred_element_type=F32) + jnp.dot(lo, seg, preferred_element_type=F32)
        gain = qg_ref[...] if is_q else kg_ref[...]
        yn = ys * lax.rsqrt(ss * (1.0 / HEAD_DIM) + RMS_EPS) * gain
        partner = jnp.where(first_half,
                            pltpu.roll(yn, slab - HALF_HEAD // 2, axis=1),
                            pltpu.roll(yn, HALF_HEAD // 2, axis=1))
        r = yn * cos + partner * sin
        if is_q:
            r = r * q_scale
        o_ref[:, s * slab:(s + 1) * slab] = r.astype(o_ref.dtype)
    o_ref[:, q_cols + k_cols:] = y_next.astype(o_ref.dtype)


def _qkv_proj(h2, gain, w, *, q_cols, q_scale, tm=512, rope=None):
    t, d = h2.shape
    n = w.shape[1]
    common = dict(
        grid=(t // tm,),
        out_specs=pl.BlockSpec((tm, n), lambda i: (i, 0)),
        out_shape=jax.ShapeDtypeStruct((t, n), BF16),
        compiler_params=_params("parallel"),
    )
    x_spec = pl.BlockSpec((tm, d), lambda i: (i, 0))
    g_spec = pl.BlockSpec((1, d), lambda i: (0, 0))
    w_spec = pl.BlockSpec((d, n), lambda i: (0, 0))
    if rope is None:
        return pl.pallas_call(
            functools.partial(_qkv_body, q_cols=q_cols, q_scale=q_scale),
            in_specs=[x_spec, g_spec, w_spec], name="qkv_proj", **common,
        )(h2, gain, w)
    qg, kg, cos, sin, seg, k_cols = rope
    seq_tiles = cos.shape[0] // tm
    slab = 2 * LANES
    return pl.pallas_call(
        functools.partial(_qkv_rope_body, q_cols=q_cols, k_cols=k_cols, q_scale=q_scale),
        in_specs=[x_spec, g_spec, w_spec,
                  pl.BlockSpec((1, slab), lambda i: (0, 0)),
                  pl.BlockSpec((1, slab), lambda i: (0, 0)),
                  pl.BlockSpec((tm, LANES), lambda i: (i % seq_tiles, 0)),
                  pl.BlockSpec((tm, LANES), lambda i: (i % seq_tiles, 0)),
                  pl.BlockSpec((slab, slab), lambda i: (0, 0))],
        name="qkv_proj_rope", **common,
    )(h2, gain, w, qg, kg, cos, sin, seg)


def _qkv_b_body(x_ref, g_ref, w_ref, *rest, q_cols, q_scale):
    out_refs, y_scr = rest[:-1], rest[-1]
    tm = x_ref.shape[0]
    hn = _rms_rows(x_ref[...], g_ref[...]).astype(BF16)
    y = jnp.dot(hn, w_ref[...], preferred_element_type=F32)
    for s in range(y_scr.shape[0]):
        ys = y[:, s * LANES:(s + 1) * LANES]
        y_scr[s] = ys * q_scale if s * LANES < q_cols else ys
    n_g = len(B_GROUPS)
    q_slabs = B_HEADS_PER_GROUP * HEAD_DIM // LANES
    k0 = q_cols // LANES
    for g, (_, dil) in enumerate(B_GROUPS):
        slabs = list(range(g * q_slabs, (g + 1) * q_slabs)) + [k0 + g, k0 + n_g + g]
        for r in range(dil):
            rows = pl.ds(r, tm // dil, stride=dil)
            piece = jnp.concatenate([y_scr[s, rows, :] for s in slabs], axis=1)
            out_refs[g][:, r * B_GROUP_COLS:(r + 1) * B_GROUP_COLS] = piece.astype(BF16)


def _qkv_proj_b(h2, gain, w, *, q_cols, q_scale, tm=512):
    t, d = h2.shape
    n = w.shape[1]
    return pl.pallas_call(
        functools.partial(_qkv_b_body, q_cols=q_cols, q_scale=q_scale),
        grid=(t // tm,),
        in_specs=[pl.BlockSpec((tm, d), lambda i: (i, 0)),
                  pl.BlockSpec((1, d), lambda i: (0, 0)),
                  pl.BlockSpec((d, n), lambda i: (0, 0))],
        out_specs=[pl.BlockSpec((tm // dil, dil * B_GROUP_COLS), lambda i: (i, 0)) for _, dil in B_GROUPS],
        out_shape=[jax.ShapeDtypeStruct((t // dil, dil * B_GROUP_COLS), BF16) for _, dil in B_GROUPS],
        scratch_shapes=[pltpu.VMEM((n // LANES, tm, LANES), F32)],
        compiler_params=_params("parallel"),
        name="qkv_proj_b",
    )(h2, gain, w)


def _rope_tables(seq):
    t = jnp.arange(seq)
    row = (t // GRID_W).astype(F32)
    col = (t % GRID_W).astype(F32)
    inv_freq = ROPE_THETA ** (-jnp.arange(0, HALF_HEAD, 2, dtype=F32) / HALF_HEAD)
    ang_row = row[:, None] * inv_freq
    ang_col = col[:, None] * inv_freq
    cos = jnp.concatenate([jnp.cos(ang_row)] * 2 + [jnp.cos(ang_col)] * 2, axis=1)
    sin = jnp.concatenate([-jnp.sin(ang_row), jnp.sin(ang_row), -jnp.sin(ang_col), jnp.sin(ang_col)], axis=1)
    return jnp.concatenate([cos, cos], axis=1), jnp.concatenate([sin, sin], axis=1)


def _attn_a_body(q_ref, k_ref, v_ref, o_ref, klo, khi, vt1, s_buf, *, tq, tk, seq, unroll):
    def fill(head_in_high_lanes):
        low = lax.broadcasted_iota(jnp.int32, (seq, LANES), 1) < HEAD_DIM
        zero = jnp.zeros((seq, LANES), BF16)
        x = k_ref[0]
        swapped = jnp.concatenate([x[:, HEAD_DIM:], x[:, :HEAD_DIM]], axis=1)
        in_low, in_high = (swapped, x) if head_in_high_lanes else (x, swapped)
        klo[...] = jnp.where(low, in_low, zero)
        khi[...] = jnp.where(low, zero, in_high)
        ones_row = (lax.broadcasted_iota(jnp.int32, (VT_ROWS - HEAD_DIM, tk), 0) == 0).astype(F32)
        for c in range(seq // tk):
            xt = v_ref[0, c * tk:(c + 1) * tk, :].astype(F32).T
            vt = xt[HEAD_DIM:] if head_in_high_lanes else xt[:HEAD_DIM]
            vt1[:, c * tk:(c + 1) * tk] = jnp.concatenate([vt, ones_row], axis=0).astype(BF16)

    odd = (pl.program_id(1) % 2) == 1
    pl.when(jnp.logical_not(odd))(lambda: fill(False))
    pl.when(odd)(lambda: fill(True))

    cols = 2 * tq
    n_chunks = seq // tk
    n_tiles = seq // tq

    def scores(tile, c, slot):
        q = q_ref[0, pl.ds(pl.multiple_of(tile * tq, tq), tq), :]
        q2 = jnp.concatenate([q[:, :LANES], q[:, LANES:]], axis=0)
        ks = pl.multiple_of(c * tk, tk)
        cms = []
        for par, k_s in enumerate((klo, khi)):
            s = _nt_dot(k_s[pl.ds(ks, tk), :], q2)
            s_buf[slot, par] = s
            cms.append(jnp.max(s, axis=0, keepdims=True))
        return tuple(cms)

    def accumulate(c, slot, cms, state):
        vt = vt1[:, pl.ds(pl.multiple_of(c * tk, tk), tk)]
        new = []
        for par in range(2):
            m, acc = state[2 * par], state[2 * par + 1]
            mn = jnp.maximum(m, cms[par])
            p = jnp.exp2(s_buf[slot, par] - mn)
            new += [mn, acc * jnp.exp2(m - mn) + jnp.dot(vt, p.astype(BF16), preferred_element_type=F32)]
        return tuple(new)

    neg = jnp.full((1, cols), -jnp.inf, F32)
    acc0 = jnp.zeros((VT_ROWS, cols), F32)

    steps_per_tile = n_chunks // unroll

    n_slots = s_buf.shape[0]
    ahead = n_slots - 1

    def flat_scores(f, slot):
        return scores(jnp.minimum(f // n_chunks, n_tiles - 1), f % n_chunks, slot)

    def step(j, carry):
        tile, c0 = j // steps_per_tile, (j % steps_per_tile) * unroll
        cms, (m_e, acc_e, m_o, acc_o) = list(carry[:2 * ahead]), carry[2 * ahead:]
        fresh = c0 == 0
        state = (jnp.where(fresh, neg, m_e), acc_e, jnp.where(fresh, neg, m_o), acc_o)
        for u in range(unroll):
            cms += flat_scores(j * unroll + u + ahead, (u + ahead) % n_slots)
            state = accumulate(c0 + u, u % n_slots, cms[:2], state)
            cms = cms[2:]

        @pl.when(c0 + unroll == n_chunks)
        def _():
            out = jnp.concatenate(
                [acc[:HEAD_DIM] / acc[HEAD_DIM:HEAD_DIM + 1] for acc in (state[1], state[3])], axis=0)
            o_ref[0, pl.ds(pl.multiple_of(tile * tq, tq), tq), :] = jnp.concatenate(
                [out[:, :tq].T, out[:, tq:].T], axis=1).astype(o_ref.dtype)

        return tuple(cms) + state

    first = sum((flat_scores(f, f) for f in range(ahead)), ())
    lax.fori_loop(0, n_tiles * steps_per_tile, step, first + (neg, acc0, neg, acc0))


def _attn_a(qkv, *, n_heads, tq=256, tk=512, unroll=8, slots=2):
    b, seq, _ = qkv.shape
    kv_heads = n_heads // A_REP
    q_cols = n_heads * HEAD_DIM
    q_w = A_REP * HEAD_DIM
    k_blk0 = q_cols // LANES
    v_blk0 = (q_cols + kv_heads * HEAD_DIM) // LANES
    assert unroll % slots == 0 and (seq // tk) % unroll == 0 and seq % tq == 0
    return pl.pallas_call(
        functools.partial(_attn_a_body, tq=tq, tk=tk, seq=seq, unroll=unroll),
        grid=(b, kv_heads),
        in_specs=[pl.BlockSpec((1, seq, q_w), lambda bi, h: (bi, 0, h)),
                  pl.BlockSpec((1, seq, LANES), lambda bi, h: (bi, 0, k_blk0 + h // 2)),
                  pl.BlockSpec((1, seq, LANES), lambda bi, h: (bi, 0, v_blk0 + h // 2))],
        out_specs=pl.BlockSpec((1, seq, q_w), lambda bi, h: (bi, 0, h)),
        out_shape=jax.ShapeDtypeStruct((b, seq, q_cols), BF16),
        scratch_shapes=[pltpu.VMEM((seq, LANES), BF16), pltpu.VMEM((seq, LANES), BF16),
                        pltpu.VMEM((VT_ROWS, seq), BF16), pltpu.VMEM((slots, 2, tk, 2 * tq), F32)],
        compiler_params=_params("parallel", "arbitrary"),
        name="attn_a",
    )(qkv, qkv, qkv)


def _fill_band_bias(bias_ref, slopes, half_window, dist_scale):
    span = bias_ref.shape[2]
    c = lax.broadcasted_iota(jnp.int32, (span, BLOCK_Q), 0)
    r = lax.broadcasted_iota(jnp.int32, (span, BLOCK_Q), 1)
    dist = jnp.abs(c - half_window - r)
    inside = dist <= half_window
    penalty = dist.astype(F32) * (dist_scale * LOG2E)
    for variant, ok in enumerate((inside & (c >= half_window), inside, inside & (c < half_window + BLOCK_Q))):
        for h, slope in enumerate(slopes):
            bias_ref[variant, h] = jnp.where(ok, -slope * penalty, NEG_BIG)


def _band_pipeline(jobs, bias_ref, s_buf):
    def scores(job, slot):
        kmat, q_rows, head_ids, _, _, variant, sink_row = job
        width = len(head_ids) * BLOCK_Q
        s = _nt_dot(kmat, q_rows) + jnp.concatenate([bias_ref[variant, h] for h in head_ids], axis=1)
        s_buf[slot, :, :width] = s
        m = jnp.max(s, axis=0, keepdims=True)
        return m if sink_row is None else jnp.maximum(m, sink_row)

    def values(job, slot, m):
        _, _, head_ids, kv_ids, vt1s, _, sink_row = job
        n = len(head_ids)
        p = jnp.exp2(s_buf[slot, :, :n * BLOCK_Q] - m).astype(BF16)
        accs, start = [], 0
        while start < n:
            stop = start
            while stop < n and kv_ids[stop] == kv_ids[start]:
                stop += 1
            accs.append(jnp.dot(vt1s[kv_ids[start]], p[:, start * BLOCK_Q:stop * BLOCK_Q],
                                preferred_element_type=F32))
            start = stop
        acc = accs[0] if len(accs) == 1 else jnp.concatenate(accs, axis=1)
        den = acc[HEAD_DIM:HEAD_DIM + 1]
        if sink_row is not None:
            den = den + jnp.exp2(sink_row - m)
        return acc[:HEAD_DIM] / den, m, den

    results = []
    n_slots = s_buf.shape[0]
    ahead = n_slots - 1
    maxima = [scores(jobs[c], c % n_slots) for c in range(min(ahead, len(jobs)))]
    for c, job in enumerate(jobs):
        if c + ahead < len(jobs):
            maxima.append(scores(jobs[c + ahead], (c + ahead) % n_slots))
        results.append(values(job, c % n_slots, maxima[c]))
    return results


def _band_operands(k_refs, v_refs):
    k = jnp.concatenate([r[0] for r in k_refs], axis=0)
    vt = jnp.concatenate([r[0] for r in v_refs], axis=0).astype(F32).T
    return k, vt


def _vt_with_ones(vt, kv, start, span):
    ones_row = (lax.broadcasted_iota(jnp.int32, (VT_ROWS - HEAD_DIM, span), 0) == 0).astype(F32)
    rows = vt[kv * HEAD_DIM:(kv + 1) * HEAD_DIM, start:start + span]
    return jnp.concatenate([rows, ones_row], axis=0).astype(BF16)


def _swap_halves(x):
    return jnp.concatenate([x[:, HEAD_DIM:], x[:, :HEAD_DIM]], axis=1)


def _attn_c_body(sink_ref, q_ref, kp_ref, kc_ref, kn_ref, vp_ref, vc_ref, vn_ref, o_ref, bias_ref, s_buf,
                 *, n_heads, slopes):
    i = pl.program_id(1)
    pl.when(i == 0)(lambda: _fill_band_bias(bias_ref, slopes, C_WINDOW, 1.0))
    variant = jnp.where(i == 0, 0, jnp.where(i == pl.num_programs(1) - 1, 2, 1))
    q = q_ref[0]
    k, vt = _band_operands((kp_ref, kc_ref, kn_ref), (vp_ref, vc_ref, vn_ref))
    span = k.shape[0]
    low = lax.broadcasted_iota(jnp.int32, (BLOCK_Q, LANES), 1) < HEAD_DIM
    zero = jnp.zeros((BLOCK_Q, LANES), BF16)
    first_block = lax.broadcasted_iota(jnp.int32, (1, 2 * BLOCK_Q), 1) < BLOCK_Q
    jobs = []
    for pair in range(n_heads // A_REP // 2):
        ks = k[:, pair * LANES:(pair + 1) * LANES]
        ks_swapped = _swap_halves(ks)
        for which in range(2):
            kv = 2 * pair + which
            vt1s = {kv: _vt_with_ones(vt, kv, 0, span)}
            slabs = [q[:, (2 * kv + j) * LANES:(2 * kv + j + 1) * LANES] for j in range(2)]
            for parity in range(2):
                q_rows = jnp.concatenate(
                    [jnp.where(low, s_, zero) if parity == 0 else jnp.where(low, zero, s_) for s_ in slabs],
                    axis=0)
                heads = (A_REP * kv + parity, A_REP * kv + 2 + parity)
                sink_row = jnp.where(first_block, sink_ref[heads[0]], sink_ref[heads[1]]) * LOG2E
                jobs.append((ks if parity == which else ks_swapped, q_rows, heads, (kv, kv), vt1s, variant,
                             sink_row))
    o_t = [None] * n_heads
    for job, (ot, _, _) in zip(jobs, _band_pipeline(jobs, bias_ref, s_buf)):
        o_t[job[2][0]], o_t[job[2][1]] = ot[:, :BLOCK_Q], ot[:, BLOCK_Q:]
    for j in range(n_heads // 2):
        pair_t = jnp.concatenate([o_t[2 * j], o_t[2 * j + 1]], axis=0)
        o_ref[0, :, j * LANES:(j + 1) * LANES] = pair_t.T.astype(o_ref.dtype)


def _alibi_slopes(n):
    return [2.0 ** (-8.0 * (i + 1) / n) for i in range(n)]


def _attn_c(qkv, sinks, *, n_heads):
    b, seq, _ = qkv.shape
    kv_heads = n_heads // A_REP
    q_cols = n_heads * HEAD_DIM
    kv_cols = kv_heads * HEAD_DIM
    n_tiles = seq // BLOCK_Q
    k_blk = q_cols // kv_cols
    v_blk = k_blk + 1
    span = BLOCK_Q + 2 * C_WINDOW

    def kv_spec(col_blk, shift):
        return pl.BlockSpec(
            (1, BLOCK_Q, kv_cols),
            lambda bi, i: (bi, jnp.clip(i + shift, 0, n_tiles - 1), col_blk))

    return pl.pallas_call(
        functools.partial(_attn_c_body, n_heads=n_heads, slopes=_alibi_slopes(n_heads)),
        grid=(b, n_tiles),
        in_specs=[pl.BlockSpec(memory_space=pltpu.SMEM),
                  pl.BlockSpec((1, BLOCK_Q, q_cols), lambda bi, i: (bi, i, 0)),
                  kv_spec(k_blk, -1), kv_spec(k_blk, 0), kv_spec(k_blk, 1),
                  kv_spec(v_blk, -1), kv_spec(v_blk, 0), kv_spec(v_blk, 1)],
        out_specs=pl.BlockSpec((1, BLOCK_Q, q_cols), lambda bi, i: (bi, i, 0)),
        out_shape=jax.ShapeDtypeStruct((b, seq, q_cols), BF16),
        scratch_shapes=[pltpu.VMEM((3, n_heads, span, BLOCK_Q), F32),
                        pltpu.VMEM((BAND_SLOTS, span, 2 * BLOCK_Q), F32)],
        compiler_params=_params("parallel", "arbitrary"),
        name="attn_c",
    )(sinks, qkv, qkv, qkv, qkv, qkv, qkv, qkv)


def _attn_b_body(q0_ref, q1_ref, q2_ref, kp_ref, kc_ref, kn_ref, vp_ref, vc_ref, vn_ref, o_ref, lse_ref,
                 bias_ref, s_buf, *, slopes, dil):
    i = pl.program_id(2)
    pl.when(jnp.logical_and(pl.program_id(1) == 0, i == 0))(
        lambda: _fill_band_bias(bias_ref, slopes, B_HALF, float(dil)))
    block_variants = (jnp.where(i == 0, 0, 1), jnp.where(i == pl.num_programs(2) - 1, 2, 1))
    k, vt = _band_operands((kp_ref, kc_ref, kn_ref), (vp_ref, vc_ref, vn_ref))
    low = lax.broadcasted_iota(jnp.int32, (BLOCK_Q, LANES), 1) < HEAD_DIM
    zero = jnp.zeros((BLOCK_Q, LANES), BF16)
    slabs = [q0_ref[0], q1_ref[0], q2_ref[0]]
    heads_all = range(B_HEADS_PER_GROUP)
    plain = [h for h in heads_all if h % 2 == h // B_REP]
    crossed = [h for h in heads_all if h % 2 != h // B_REP]
    jobs, job_block = [], []
    for blk in range(B_TILE // BLOCK_Q):
        q_lo = blk * BLOCK_Q
        ks = k[q_lo:q_lo + B_SPAN]
        ks_swapped = _swap_halves(ks)
        vt1s = {kv: _vt_with_ones(vt, kv, q_lo, B_SPAN) for kv in range(B_KV_PER_GROUP)}
        for kmat, heads in ((ks, plain[:2]), (ks_swapped, crossed), (ks, plain[2:])):
            q_rows = jnp.concatenate(
                [jnp.where(low, slabs[h // 2][q_lo:q_lo + BLOCK_Q], zero) if h % 2 == 0
                 else jnp.where(low, zero, slabs[h // 2][q_lo:q_lo + BLOCK_Q]) for h in heads], axis=0)
            jobs.append((kmat, q_rows, heads, [h // B_REP for h in heads], vt1s, block_variants[blk], None))
            job_block.append(blk)
    o_t, lse_t = {}, {}
    for job, blk, (ot, m, den) in zip(jobs, job_block, _band_pipeline(jobs, bias_ref, s_buf)):
        lse = (m + jnp.log2(den)) * LN2
        for n, h in enumerate(job[2]):
            o_t[blk, h] = ot[:, n * BLOCK_Q:(n + 1) * BLOCK_Q]
            lse_t[blk, h] = jnp.broadcast_to(lse[:, n * BLOCK_Q:(n + 1) * BLOCK_Q], (HEAD_DIM, BLOCK_Q))
    for blk in range(B_TILE // BLOCK_Q):
        rows = slice(blk * BLOCK_Q, (blk + 1) * BLOCK_Q)
        for j in range(B_HEADS_PER_GROUP // 2):
            cols = slice(j * LANES, (j + 1) * LANES)
            pair = lambda t: jnp.concatenate([t[blk, 2 * j], t[blk, 2 * j + 1]], axis=0).T
            o_ref[0, rows, cols] = pair(o_t).astype(o_ref.dtype)
            lse_ref[0, rows, cols] = pair(lse_t)


def _attn_b(group_qkv, b):
    n_g = len(B_GROUPS)
    gq = B_HEADS_PER_GROUP * HEAD_DIM
    slabs_per_token = B_GROUP_COLS // LANES
    q_slabs = gq // LANES
    halves_per_tile = B_TILE // B_HALF
    all_slopes = _alibi_slopes(n_g * B_HEADS_PER_GROUP)
    outs, lses = [], []
    for g, (window, dil) in enumerate(B_GROUPS):
        assert (window // 2) // dil == B_HALF
        sub = group_qkv[g].shape[0] // b
        n_tiles = sub // B_TILE
        n_halves = sub // B_HALF
        view = group_qkv[g].reshape(b, sub, dil * B_GROUP_COLS)

        def tile_spec(slab):
            return pl.BlockSpec((1, B_TILE, LANES), lambda bi, r, i: (bi, i, r * slabs_per_token + slab))

        def edge_spec(slab, offset, n_halves=n_halves):
            return pl.BlockSpec(
                (1, B_HALF, LANES),
                lambda bi, r, i: (bi, jnp.clip(i * halves_per_tile + offset, 0, n_halves - 1),
                                  r * slabs_per_token + slab))

        def kv_specs(slab):
            return [edge_spec(slab, -1), tile_spec(slab), edge_spec(slab, halves_per_tile)]

        out_map = lambda bi, r, i: (bi, i, r)
        o, lse = pl.pallas_call(
            functools.partial(_attn_b_body, dil=dil,
                              slopes=all_slopes[g * B_HEADS_PER_GROUP:(g + 1) * B_HEADS_PER_GROUP]),
            grid=(b, dil, n_tiles),
            in_specs=[tile_spec(j) for j in range(q_slabs)] + kv_specs(q_slabs) + kv_specs(q_slabs + 1),
            out_specs=[pl.BlockSpec((1, B_TILE, gq), out_map), pl.BlockSpec((1, B_TILE, gq), out_map)],
            out_shape=[jax.ShapeDtypeStruct((b, sub, dil * gq), BF16),
                       jax.ShapeDtypeStruct((b, sub, dil * gq), F32)],
            scratch_shapes=[pltpu.VMEM((3, B_HEADS_PER_GROUP, B_SPAN, BLOCK_Q), F32),
                            pltpu.VMEM((BAND_SLOTS, B_SPAN, 2 * BLOCK_Q), F32)],
            compiler_params=_params("parallel", "arbitrary", "arbitrary"),
            name=f"attn_b_g{g}",
        )(*([view] * (q_slabs + 6)))
        outs.append(o.reshape(b * sub, dil * gq))
        lses.append(lse.reshape(b * sub, dil * gq))
    return outs, lses


PROLOGUE_ROW_CHUNKS = 4


def _mix_plain(refs, scratch):
    return lambda rows: refs[0][rows, :]


def _mix_groups(refs, scratch):
    n_g = len(B_GROUPS)
    o_scr, l_scr = scratch
    slabs = o_scr.shape[0] // n_g
    tm = o_scr.shape[1]
    gq = slabs * LANES
    for g, (_, dil) in enumerate(B_GROUPS):
        for r in range(dil):
            rows = pl.ds(r, tm // dil, stride=dil)
            for s in range(slabs):
                cols = slice(r * gq + s * LANES, r * gq + (s + 1) * LANES)
                o_scr[g * slabs + s, rows, :] = refs[g][:, cols].astype(F32)
                l_scr[g * slabs + s, rows, :] = refs[n_g + g][:, cols]
    def lhs(rows):
        group = lambda scr, g: jnp.concatenate([scr[g * slabs + s, rows, :] for s in range(slabs)], axis=1)
        ls = [group(l_scr, g) for g in range(n_g)]
        mx = functools.reduce(jnp.maximum, ls)
        es = [jnp.exp(l - mx) for l in ls]
        inv = 1.0 / functools.reduce(lambda a, b: a + b, es)
        return jnp.concatenate([(group(o_scr, g) * (es[g] * inv)).astype(BF16) for g in range(n_g)], axis=1)

    return lhs


MLP_HIDDEN_CHUNK = 1024


def _block_body(*refs, n_mix, mix_fn, final_norm):
    h_ref, mix_refs = refs[0], refs[1:1 + n_mix]
    wo_ref, g_ref, w1_ref, w2_ref, fg_ref, out_ref, hn_ref = refs[1 + n_mix:8 + n_mix]
    mix_scratch = refs[8 + n_mix:]

    lhs = mix_fn(mix_refs, mix_scratch)
    chunk = h_ref.shape[0] // PROLOGUE_ROW_CHUNKS
    rows = [slice(c * chunk, (c + 1) * chunk) for c in range(PROLOGUE_ROW_CHUNKS)]
    project = lambda c: jnp.dot(lhs(rows[c]), wo_ref[...], preferred_element_type=F32)
    nxt = project(0)
    for c in range(PROLOGUE_ROW_CHUNKS):
        cur = nxt
        if c + 1 < PROLOGUE_ROW_CHUNKS:
            nxt = project(c + 1)
        h1 = h_ref[rows[c], :] + cur
        out_ref[rows[c], :] = h1
        hn_ref[rows[c], :] = _rms_rows(h1, g_ref[...]).astype(BF16)

    hn = hn_ref[...]
    for c in range(w1_ref.shape[1] // MLP_HIDDEN_CHUNK):
        cols = slice(c * MLP_HIDDEN_CHUNK, (c + 1) * MLP_HIDDEN_CHUNK)
        u = jnp.maximum(jnp.dot(hn, w1_ref[:, cols], preferred_element_type=F32), 0.0)
        out_ref[...] += jnp.dot((u * u).astype(BF16), w2_ref[cols, :], preferred_element_type=F32)

    if final_norm:
        out_ref[...] = _rms_rows(out_ref[...], fg_ref[...])


def _block(h2, mix_inputs, w_o, *, gain, w1, w2, final_gain, final_norm, tm=512):
    t, d = h2.shape
    row_tile = lambda cols: pl.BlockSpec((tm, cols), lambda i: (i, 0))
    whole = lambda a: pl.BlockSpec(a.shape, lambda i: (0, 0), pipeline_mode=pl.Buffered(1))
    scratch = [pltpu.VMEM((tm, d), BF16)]
    if len(mix_inputs) == 1:
        mix_fn, mix_specs = _mix_plain, [row_tile(mix_inputs[0].shape[1])]
    else:
        n_g = len(B_GROUPS)
        gq = B_HEADS_PER_GROUP * HEAD_DIM
        mix_fn = _mix_groups
        mix_specs = [pl.BlockSpec((tm // dil, dil * gq), lambda i: (i, 0)) for _, dil in B_GROUPS] * 2
        scratch += [pltpu.VMEM((n_g * gq // LANES, tm, LANES), F32)] * 2
    return pl.pallas_call(
        functools.partial(_block_body, n_mix=len(mix_inputs), mix_fn=mix_fn, final_norm=final_norm),
        grid=(t // tm,),
        in_specs=[row_tile(d)] + mix_specs + [whole(w_o), whole(gain), whole(w1), whole(w2), whole(final_gain)],
        out_specs=row_tile(d),
        out_shape=jax.ShapeDtypeStruct((t, d), F32),
        scratch_shapes=scratch,
        compiler_params=_params("parallel"),
        name="block_mlp",
    )(h2, *mix_inputs, w_o, gain, w1, w2, final_gain)


def kernel(x, attn_norm, mlp_norm, a_w_qkv, a_q_gain, a_k_gain, a_w_o, b_w_qkv, b_w_o,
           c_w_qkv, c_sinks, c_w_o, mlp_w1, mlp_w2, final_norm):
    b, seq, d = x.shape
    depth = attn_norm.shape[0]
    h = x.reshape(b * seq, d)
    scale = HEAD_DIM ** -0.5 * LOG2E
    a_heads = a_w_o.shape[1] // HEAD_DIM
    c_heads = c_w_o.shape[1] // HEAD_DIM
    cos, sin = _rope_tables(seq)
    head = jnp.arange(2 * LANES) // HEAD_DIM
    seg = (head[:, None] == head[None, :]).astype(BF16)
    tile4 = lambda g: jnp.tile(g, 2 * LANES // HEAD_DIM)[None, :]
    used = [0, 0, 0]
    for layer in range(depth):
        kind = layer % N_MIXERS
        j = used[kind]
        used[kind] += 1
        gain = attn_norm[layer][None, :]
        block = functools.partial(_block, gain=mlp_norm[layer][None, :], w1=mlp_w1[layer].astype(BF16),
                                  w2=mlp_w2[layer].astype(BF16), final_gain=final_norm[None, :],
                                  final_norm=(layer == depth - 1))
        if kind == 0:
            q_cols = a_heads * HEAD_DIM
            k_cols = (a_w_qkv.shape[2] - q_cols) // 2
            qkv = _qkv_proj(h, gain, a_w_qkv[j].astype(BF16), q_cols=q_cols, q_scale=scale,
                            rope=(tile4(a_q_gain[j]), tile4(a_k_gain[j]), cos, sin, seg, k_cols))
            o = _attn_a(qkv.reshape(b, seq, -1), n_heads=a_heads).reshape(b * seq, q_cols)
            h = block(h, [o], a_w_o[j].astype(BF16))
        elif kind == 1:
            q_cols = len(B_GROUPS) * B_HEADS_PER_GROUP * HEAD_DIM
            groups = _qkv_proj_b(h, gain, b_w_qkv[j].astype(BF16), q_cols=q_cols, q_scale=scale)
            outs, lses = _attn_b(groups, b)
            h = block(h, outs + lses, b_w_o[j].astype(BF16))
        else:
            q_cols = c_heads * HEAD_DIM
            qkv = _qkv_proj(h, gain, c_w_qkv[j].astype(BF16), q_cols=q_cols, q_scale=scale)
            o = _attn_c(qkv.reshape(b, seq, -1), c_sinks[j], n_heads=c_heads).reshape(b * seq, q_cols)
            h = block(h, [o], c_w_o[j].astype(BF16))
    return h.reshape(b, seq, d)
```

```python
import functools
import math

import jax
import jax.numpy as jnp
from jax import lax
from jax.experimental import pallas as pl
from jax.experimental.pallas import tpu as pltpu

HEAD_DIM = 64
HALF_HEAD = HEAD_DIM // 2
RMS_EPS = 1e-6
GRID_W = 64
ROPE_THETA = 10000.0
N_MIXERS = 3
A_REP = 4
B_GROUPS = ((128, 1), (512, 4), (2048, 16))
B_HEADS_PER_GROUP = 6
B_KV_PER_GROUP = 2
B_REP = B_HEADS_PER_GROUP // B_KV_PER_GROUP
C_WINDOW = 128
LANES = 128
BF16_SUBLANES = 16
BLOCK_Q = 128
BAND_SLOTS = 4
B_HALF = BLOCK_Q // 2
B_TILE = 2 * BLOCK_Q
B_SPAN = BLOCK_Q + 2 * B_HALF
B_GROUP_COLS = (B_HEADS_PER_GROUP + 2 * B_KV_PER_GROUP) * HEAD_DIM
VT_ROWS = HEAD_DIM + BF16_SUBLANES
V7X_VMEM_LIMIT_BYTES = 48 * 1024 * 1024
NEG_BIG = -1e30
LOG2E = math.log2(math.e)
LN2 = math.log(2.0)

BF16 = jnp.bfloat16
F32 = jnp.float32


def _params(*sem):
    return pltpu.CompilerParams(dimension_semantics=sem, vmem_limit_bytes=V7X_VMEM_LIMIT_BYTES)


def _nt_dot(a, b):
    return lax.dot_general(a, b, (((1,), (1,)), ((), ())), preferred_element_type=F32)


def _rms_rows(x, gain):
    ms = jnp.mean(x * x, axis=-1, keepdims=True)
    return x * lax.rsqrt(ms + RMS_EPS) * gain


def _qkv_body(x_ref, g_ref, w_ref, o_ref, *, q_cols, q_scale):
    hn = _rms_rows(x_ref[...], g_ref[...]).astype(BF16)
    y = jnp.dot(hn, w_ref[...], preferred_element_type=F32)
    o_ref[:, :q_cols] = (y[:, :q_cols] * q_scale).astype(o_ref.dtype)
    o_ref[:, q_cols:] = y[:, q_cols:].astype(o_ref.dtype)


def _qkv_rope_body(x_ref, g_ref, w_ref, qg_ref, kg_ref, cos_ref, sin_ref, seg_ref, o_ref,
                   *, q_cols, k_cols, q_scale):
    hn = _rms_rows(x_ref[...], g_ref[...]).astype(BF16)
    slab = 2 * LANES
    cos = jnp.concatenate([cos_ref[...], cos_ref[...]], axis=1)
    sin = jnp.concatenate([sin_ref[...], sin_ref[...]], axis=1)
    lane = lax.broadcasted_iota(jnp.int32, (x_ref.shape[0], slab), 1)
    first_half = (lane % HALF_HEAD) < (HALF_HEAD // 2)
    seg = seg_ref[...]
    n_qk = (q_cols + k_cols) // slab
    project = lambda s: jnp.dot(hn, w_ref[:, s * slab:(s + 1) * slab], preferred_element_type=F32)
    y_next = project(0)
    for s in range(n_qk):
        is_q = s * slab < q_cols
        ys = y_next
        y_next = project(s + 1) if s + 1 < n_qk else jnp.dot(
            hn, w_ref[:, q_cols + k_cols:], preferred_element_type=F32)
        sq = ys * ys
        hi = sq.astype(BF16)
        lo = (sq - hi.astype(F32)).astype(BF16)
        ss = jnp.dot(hi, seg, preferred_element_type=F32) + jnp.dot(lo, seg, preferred_element_type=F32)
        gain = qg_ref[...] if is_q else kg_ref[...]
        yn = ys * lax.rsqrt(ss * (1.0 / HEAD_DIM) + RMS_EPS) * gain
        partner = jnp.where(first_half,
                            pltpu.roll(yn, slab - HALF_HEAD // 2, axis=1),
                            pltpu.roll(yn, HALF_HEAD // 2, axis=1))
        r = yn * cos + partner * sin
        if is_q:
            r = r * q_scale
        o_ref[:, s * slab:(s + 1) * slab] = r.astype(o_ref.dtype)
    o_ref[:, q_cols + k_cols:] = y_next.astype(o_ref.dtype)


def _qkv_proj(h2, gain, w, *, q_cols, q_scale, tm=512, rope=None):
    t, d = h2.shape
    n = w.shape[1]
    common = dict(
        grid=(t // tm,),
        out_specs=pl.BlockSpec((tm, n), lambda i: (i, 0)),
        out_shape=jax.ShapeDtypeStruct((t, n), BF16),
        compiler_params=_params("parallel"),
    )
    x_spec = pl.BlockSpec((tm, d), lambda i: (i, 0))
    g_spec = pl.BlockSpec((1, d), lambda i: (0, 0))
    w_spec = pl.BlockSpec((d, n), lambda i: (0, 0))
    if rope is None:
        return pl.pallas_call(
            functools.partial(_qkv_body, q_cols=q_cols, q_scale=q_scale),
            in_specs=[x_spec, g_spec, w_spec], name="qkv_proj", **common,
        )(h2, gain, w)
    qg, kg, cos, sin, seg, k_cols = rope
    seq_tiles = cos.shape[0] // tm
    slab = 2 * LANES
    return pl.pallas_call(
        functools.partial(_qkv_rope_body, q_cols=q_cols, k_cols=k_cols, q_scale=q_scale),
        in_specs=[x_spec, g_spec, w_spec,
                  pl.BlockSpec((1, slab), lambda i: (0, 0)),
                  pl.BlockSpec((1, slab), lambda i: (0, 0)),
                  pl.BlockSpec((tm, LANES), lambda i: (i % seq_tiles, 0)),
                  pl.BlockSpec((tm, LANES), lambda i: (i % seq_tiles, 0)),
                  pl.BlockSpec((slab, slab), lambda i: (0, 0))],
        name="qkv_proj_rope", **common,
    )(h2, gain, w, qg, kg, cos, sin, seg)


def _qkv_b_body(x_ref, g_ref, w_ref, *rest, q_cols, q_scale):
    out_refs, y_scr = rest[:-1], rest[-1]
    tm = x_ref.shape[0]
    hn = _rms_rows(x_ref[...], g_ref[...]).astype(BF16)
    y = jnp.dot(hn, w_ref[...], preferred_element_type=F32)
    for s in range(y_scr.shape[0]):
        ys = y[:, s * LANES:(s + 1) * LANES]
        y_scr[s] = ys * q_scale if s * LANES < q_cols else ys
    n_g = len(B_GROUPS)
    q_slabs = B_HEADS_PER_GROUP * HEAD_DIM // LANES
    k0 = q_cols // LANES
    for g, (_, dil) in enumerate(B_GROUPS):
        slabs = list(range(g * q_slabs, (g + 1) * q_slabs)) + [k0 + g, k0 + n_g + g]
        for r in range(dil):
            rows = pl.ds(r, tm // dil, stride=dil)
            piece = jnp.concatenate([y_scr[s, rows, :] for s in slabs], axis=1)
            out_refs[g][:, r * B_GROUP_COLS:(r + 1) * B_GROUP_COLS] = piece.astype(BF16)


def _qkv_proj_b(h2, gain, w, *, q_cols, q_scale, tm=512):
    t, d = h2.shape
    n = w.shape[1]
    return pl.pallas_call(
        functools.partial(_qkv_b_body, q_cols=q_cols, q_scale=q_scale),
        grid=(t // tm,),
        in_specs=[pl.BlockSpec((tm, d), lambda i: (i, 0)),
                  pl.BlockSpec((1, d), lambda i: (0, 0)),
                  pl.BlockSpec((d, n), lambda i: (0, 0))],
        out_specs=[pl.BlockSpec((tm // dil, dil * B_GROUP_COLS), lambda i: (i, 0)) for _, dil in B_GROUPS],
        out_shape=[jax.ShapeDtypeStruct((t // dil, dil * B_GROUP_COLS), BF16) for _, dil in B_GROUPS],
        scratch_shapes=[pltpu.VMEM((n // LANES, tm, LANES), F32)],
        compiler_params=_params("parallel"),
        name="qkv_proj_b",
    )(h2, gain, w)


def _rope_tables(seq):
    t = jnp.arange(seq)
    row = (t // GRID_W).astype(F32)
    col = (t % GRID_W).astype(F32)
    inv_freq = ROPE_THETA ** (-jnp.arange(0, HALF_HEAD, 2, dtype=F32) / HALF_HEAD)
    ang_row = row[:, None] * inv_freq
    ang_col = col[:, None] * inv_freq
    cos = jnp.concatenate([jnp.cos(ang_row)] * 2 + [jnp.cos(ang_col)] * 2, axis=1)
    sin = jnp.concatenate([-jnp.sin(ang_row), jnp.sin(ang_row), -jnp.sin(ang_col), jnp.sin(ang_col)], axis=1)
    return jnp.concatenate([cos, cos], axis=1), jnp.concatenate([sin, sin], axis=1)


def _attn_a_body(q_ref, k_ref, v_ref, o_ref, klo, khi, vt1, s_buf, *, tq, tk, seq, tiles_per_step):
    def fill(head_in_high_lanes):
        low = lax.broadcasted_iota(jnp.int32, (seq, LANES), 1) < HEAD_DIM
        zero = jnp.zeros((seq, LANES), BF16)
        x = k_ref[0]
        swapped = jnp.concatenate([x[:, HEAD_DIM:], x[:, :HEAD_DIM]], axis=1)
        in_low, in_high = (swapped, x) if head_in_high_lanes else (x, swapped)
        klo[...] = jnp.where(low, in_low, zero)
        khi[...] = jnp.where(low, zero, in_high)
        ones_row = (lax.broadcasted_iota(jnp.int32, (VT_ROWS - HEAD_DIM, tk), 0) == 0).astype(F32)
        for c in range(seq // tk):
            xt = v_ref[0, c * tk:(c + 1) * tk, :].astype(F32).T
            vt = xt[HEAD_DIM:] if head_in_high_lanes else xt[:HEAD_DIM]
            vt1[:, c * tk:(c + 1) * tk] = jnp.concatenate([vt, ones_row], axis=0).astype(BF16)

    odd = (pl.program_id(1) % 2) == 1
    pl.when(jnp.logical_not(odd))(lambda: fill(False))
    pl.when(odd)(lambda: fill(True))

    cols = 2 * tq
    n_chunks = seq // tk
    n_tiles = seq // tq

    def scores(tile, c, slot):
        q = q_ref[0, pl.ds(pl.multiple_of(tile * tq, tq), tq), :]
        q2 = jnp.concatenate([q[:, :LANES], q[:, LANES:]], axis=0)
        cms = []
        for par, k_s in enumerate((klo, khi)):
            s = _nt_dot(k_s[c * tk:(c + 1) * tk, :], q2)
            s_buf[slot, par] = s
            cms.append(jnp.max(s, axis=0, keepdims=True))
        return tuple(cms)

    def accumulate(c, slot, cms, state):
        vt = vt1[:, c * tk:(c + 1) * tk]
        new = []
        for par in range(2):
            m, acc = state[2 * par], state[2 * par + 1]
            mn = jnp.maximum(m, cms[par])
            p = jnp.exp2(s_buf[slot, par] - mn)
            new += [mn, acc * jnp.exp2(m - mn) + jnp.dot(vt, p.astype(BF16), preferred_element_type=F32)]
        return tuple(new)

    neg = jnp.full((1, cols), -jnp.inf, F32)
    acc0 = jnp.zeros((VT_ROWS, cols), F32)

    def step(j, cms):
        for t in range(tiles_per_step):
            tile = j * tiles_per_step + t
            state = (neg, acc0, neg, acc0)
            for c in range(n_chunks):
                n = t * n_chunks + c
                nxt_tile = tile if c + 1 < n_chunks else jnp.minimum(tile + 1, n_tiles - 1)
                cms_next = scores(nxt_tile, (c + 1) % n_chunks, (n + 1) % 2)
                state = accumulate(c, n % 2, cms, state)
                cms = cms_next
            out = jnp.concatenate(
                [acc[:HEAD_DIM] / acc[HEAD_DIM:HEAD_DIM + 1] for acc in (state[1], state[3])], axis=0)
            o_ref[0, pl.ds(pl.multiple_of(tile * tq, tq), tq), :] = jnp.concatenate(
                [out[:, :tq].T, out[:, tq:].T], axis=1).astype(o_ref.dtype)
        return cms

    lax.fori_loop(0, n_tiles // tiles_per_step, step, scores(0, 0, 0))


def _attn_a(qkv, *, n_heads, tq=256, tk=512, tiles_per_step=2):
    b, seq, _ = qkv.shape
    kv_heads = n_heads // A_REP
    q_cols = n_heads * HEAD_DIM
    q_w = A_REP * HEAD_DIM
    k_blk0 = q_cols // LANES
    v_blk0 = (q_cols + kv_heads * HEAD_DIM) // LANES
    assert (seq // tk) % 2 == 0 and seq % (tq * tiles_per_step) == 0
    return pl.pallas_call(
        functools.partial(_attn_a_body, tq=tq, tk=tk, seq=seq, tiles_per_step=tiles_per_step),
        grid=(b, kv_heads),
        in_specs=[pl.BlockSpec((1, seq, q_w), lambda bi, h: (bi, 0, h)),
                  pl.BlockSpec((1, seq, LANES), lambda bi, h: (bi, 0, k_blk0 + h // 2)),
                  pl.BlockSpec((1, seq, LANES), lambda bi, h: (bi, 0, v_blk0 + h // 2))],
        out_specs=pl.BlockSpec((1, seq, q_w), lambda bi, h: (bi, 0, h)),
        out_shape=jax.ShapeDtypeStruct((b, seq, q_cols), BF16),
        scratch_shapes=[pltpu.VMEM((seq, LANES), BF16), pltpu.VMEM((seq, LANES), BF16),
                        pltpu.VMEM((VT_ROWS, seq), BF16), pltpu.VMEM((2, 2, tk, 2 * tq), F32)],
        compiler_params=_params("parallel", "arbitrary"),
        name="attn_a",
    )(qkv, qkv, qkv)


def _fill_band_bias(bias_ref, slopes, half_window, dist_scale):
    span = bias_ref.shape[2]
    c = lax.broadcasted_iota(jnp.int32, (span, BLOCK_Q), 0)
    r = lax.broadcasted_iota(jnp.int32, (span, BLOCK_Q), 1)
    dist = jnp.abs(c - half_window - r)
    inside = dist <= half_window
    penalty = dist.astype(F32) * (dist_scale * LOG2E)
    for variant, ok in enumerate((inside & (c >= half_window), inside, inside & (c < half_window + BLOCK_Q))):
        for h, slope in enumerate(slopes):
            bias_ref[variant, h] = jnp.where(ok, -slope * penalty, NEG_BIG)


def _band_pipeline(jobs, bias_ref, s_buf):
    def scores(job, slot):
        kmat, q_rows, head_ids, _, _, variant, sink_row = job
        width = len(head_ids) * BLOCK_Q
        s = _nt_dot(kmat, q_rows) + jnp.concatenate([bias_ref[variant, h] for h in head_ids], axis=1)
        s_buf[slot, :, :width] = s
        m = jnp.max(s, axis=0, keepdims=True)
        return m if sink_row is None else jnp.maximum(m, sink_row)

    def values(job, slot, m):
        _, _, head_ids, kv_ids, vt1s, _, sink_row = job
        n = len(head_ids)
        p = jnp.exp2(s_buf[slot, :, :n * BLOCK_Q] - m).astype(BF16)
        accs, start = [], 0
        while start < n:
            stop = start
            while stop < n and kv_ids[stop] == kv_ids[start]:
                stop += 1
            accs.append(jnp.dot(vt1s[kv_ids[start]], p[:, start * BLOCK_Q:stop * BLOCK_Q],
                                preferred_element_type=F32))
            start = stop
        acc = accs[0] if len(accs) == 1 else jnp.concatenate(accs, axis=1)
        den = acc[HEAD_DIM:HEAD_DIM + 1]
        if sink_row is not None:
            den = den + jnp.exp2(sink_row - m)
        return acc[:HEAD_DIM] / den, m, den

    results = []
    n_slots = s_buf.shape[0]
    ahead = n_slots - 1
    maxima = [scores(jobs[c], c % n_slots) for c in range(min(ahead, len(jobs)))]
    for c, job in enumerate(jobs):
        if c + ahead < len(jobs):
            maxima.append(scores(jobs[c + ahead], (c + ahead) % n_slots))
        results.append(values(job, c % n_slots, maxima[c]))
    return results


def _band_operands(k_refs, v_refs):
    k = jnp.concatenate([r[0] for r in k_refs], axis=0)
    vt = jnp.concatenate([r[0] for r in v_refs], axis=0).astype(F32).T
    return k, vt


def _vt_with_ones(vt, kv, start, span):
    ones_row = (lax.broadcasted_iota(jnp.int32, (VT_ROWS - HEAD_DIM, span), 0) == 0).astype(F32)
    rows = vt[kv * HEAD_DIM:(kv + 1) * HEAD_DIM, start:start + span]
    return jnp.concatenate([rows, ones_row], axis=0).astype(BF16)


def _swap_halves(x):
    return jnp.concatenate([x[:, HEAD_DIM:], x[:, :HEAD_DIM]], axis=1)


def _attn_c_body(sink_ref, q_ref, kp_ref, kc_ref, kn_ref, vp_ref, vc_ref, vn_ref, o_ref, bias_ref, s_buf,
                 *, n_heads, slopes):
    i = pl.program_id(1)
    pl.when(i == 0)(lambda: _fill_band_bias(bias_ref, slopes, C_WINDOW, 1.0))
    variant = jnp.where(i == 0, 0, jnp.where(i == pl.num_programs(1) - 1, 2, 1))
    q = q_ref[0]
    k, vt = _band_operands((kp_ref, kc_ref, kn_ref), (vp_ref, vc_ref, vn_ref))
    span = k.shape[0]
    low = lax.broadcasted_iota(jnp.int32, (BLOCK_Q, LANES), 1) < HEAD_DIM
    zero = jnp.zeros((BLOCK_Q, LANES), BF16)
    first_block = lax.broadcasted_iota(jnp.int32, (1, 2 * BLOCK_Q), 1) < BLOCK_Q
    jobs = []
    for pair in range(n_heads // A_REP // 2):
        ks = k[:, pair * LANES:(pair + 1) * LANES]
        ks_swapped = _swap_halves(ks)
        for which in range(2):
            kv = 2 * pair + which
            vt1s = {kv: _vt_with_ones(vt, kv, 0, span)}
            slabs = [q[:, (2 * kv + j) * LANES:(2 * kv + j + 1) * LANES] for j in range(2)]
            for parity in range(2):
                q_rows = jnp.concatenate(
                    [jnp.where(low, s_, zero) if parity == 0 else jnp.where(low, zero, s_) for s_ in slabs],
                    axis=0)
                heads = (A_REP * kv + parity, A_REP * kv + 2 + parity)
                sink_row = jnp.where(first_block, sink_ref[heads[0]], sink_ref[heads[1]]) * LOG2E
                jobs.append((ks if parity == which else ks_swapped, q_rows, heads, (kv, kv), vt1s, variant,
                             sink_row))
    o_t = [None] * n_heads
    for job, (ot, _, _) in zip(jobs, _band_pipeline(jobs, bias_ref, s_buf)):
        o_t[job[2][0]], o_t[job[2][1]] = ot[:, :BLOCK_Q], ot[:, BLOCK_Q:]
    for j in range(n_heads // 2):
        pair_t = jnp.concatenate([o_t[2 * j], o_t[2 * j + 1]], axis=0)
        o_ref[0, :, j * LANES:(j + 1) * LANES] = pair_t.T.astype(o_ref.dtype)


def _alibi_slopes(n):
    return [2.0 ** (-8.0 * (i + 1) / n) for i in range(n)]


def _attn_c(qkv, sinks, *, n_heads):
    b, seq, _ = qkv.shape
    kv_heads = n_heads // A_REP
    q_cols = n_heads * HEAD_DIM
    kv_cols = kv_heads * HEAD_DIM
    n_tiles = seq // BLOCK_Q
    k_blk = q_cols // kv_cols
    v_blk = k_blk + 1
    span = BLOCK_Q + 2 * C_WINDOW

    def kv_spec(col_blk, shift):
        return pl.BlockSpec(
            (1, BLOCK_Q, kv_cols),
            lambda bi, i: (bi, jnp.clip(i + shift, 0, n_tiles - 1), col_blk))

    return pl.pallas_call(
        functools.partial(_attn_c_body, n_heads=n_heads, slopes=_alibi_slopes(n_heads)),
        grid=(b, n_tiles),
        in_specs=[pl.BlockSpec(memory_space=pltpu.SMEM),
                  pl.BlockSpec((1, BLOCK_Q, q_cols), lambda bi, i: (bi, i, 0)),
                  kv_spec(k_blk, -1), kv_spec(k_blk, 0), kv_spec(k_blk, 1),
                  kv_spec(v_blk, -1), kv_spec(v_blk, 0), kv_spec(v_blk, 1)],
        out_specs=pl.BlockSpec((1, BLOCK_Q, q_cols), lambda bi, i: (bi, i, 0)),
        out_shape=jax.ShapeDtypeStruct((b, seq, q_cols), BF16),
        scratch_shapes=[pltpu.VMEM((3, n_heads, span, BLOCK_Q), F32),
                        pltpu.VMEM((BAND_SLOTS, span, 2 * BLOCK_Q), F32)],
        compiler_params=_params("parallel", "arbitrary"),
        name="attn_c",
    )(sinks, qkv, qkv, qkv, qkv, qkv, qkv, qkv)


def _attn_b_body(q0_ref, q1_ref, q2_ref, kp_ref, kc_ref, kn_ref, vp_ref, vc_ref, vn_ref, o_ref, lse_ref,
                 bias_ref, s_buf, *, slopes, dil):
    i = pl.program_id(2)
    pl.when(jnp.logical_and(pl.program_id(1) == 0, i == 0))(
        lambda: _fill_band_bias(bias_ref, slopes, B_HALF, float(dil)))
    block_variants = (jnp.where(i == 0, 0, 1), jnp.where(i == pl.num_programs(2) - 1, 2, 1))
    k, vt = _band_operands((kp_ref, kc_ref, kn_ref), (vp_ref, vc_ref, vn_ref))
    low = lax.broadcasted_iota(jnp.int32, (BLOCK_Q, LANES), 1) < HEAD_DIM
    zero = jnp.zeros((BLOCK_Q, LANES), BF16)
    slabs = [q0_ref[0], q1_ref[0], q2_ref[0]]
    heads_all = range(B_HEADS_PER_GROUP)
    plain = [h for h in heads_all if h % 2 == h // B_REP]
    crossed = [h for h in heads_all if h % 2 != h // B_REP]
    jobs, job_block = [], []
    for blk in range(B_TILE // BLOCK_Q):
        q_lo = blk * BLOCK_Q
        ks = k[q_lo:q_lo + B_SPAN]
        ks_swapped = _swap_halves(ks)
        vt1s = {kv: _vt_with_ones(vt, kv, q_lo, B_SPAN) for kv in range(B_KV_PER_GROUP)}
        for kmat, heads in ((ks, plain[:2]), (ks_swapped, crossed), (ks, plain[2:])):
            q_rows = jnp.concatenate(
                [jnp.where(low, slabs[h // 2][q_lo:q_lo + BLOCK_Q], zero) if h % 2 == 0
                 else jnp.where(low, zero, slabs[h // 2][q_lo:q_lo + BLOCK_Q]) for h in heads], axis=0)
            jobs.append((kmat, q_rows, heads, [h // B_REP for h in heads], vt1s, block_variants[blk], None))
            job_block.append(blk)
    o_t, lse_t = {}, {}
    for job, blk, (ot, m, den) in zip(jobs, job_block, _band_pipeline(jobs, bias_ref, s_buf)):
        lse = (m + jnp.log2(den)) * LN2
        for n, h in enumerate(job[2]):
            o_t[blk, h] = ot[:, n * BLOCK_Q:(n + 1) * BLOCK_Q]
            lse_t[blk, h] = jnp.broadcast_to(lse[:, n * BLOCK_Q:(n + 1) * BLOCK_Q], (HEAD_DIM, BLOCK_Q))
    for blk in range(B_TILE // BLOCK_Q):
        rows = slice(blk * BLOCK_Q, (blk + 1) * BLOCK_Q)
        for j in range(B_HEADS_PER_GROUP // 2):
            cols = slice(j * LANES, (j + 1) * LANES)
            pair = lambda t: jnp.concatenate([t[blk, 2 * j], t[blk, 2 * j + 1]], axis=0).T
            o_ref[0, rows, cols] = pair(o_t).astype(o_ref.dtype)
            lse_ref[0, rows, cols] = pair(lse_t)


def _attn_b(group_qkv, b):
    n_g = len(B_GROUPS)
    gq = B_HEADS_PER_GROUP * HEAD_DIM
    slabs_per_token = B_GROUP_COLS // LANES
    q_slabs = gq // LANES
    halves_per_tile = B_TILE // B_HALF
    all_slopes = _alibi_slopes(n_g * B_HEADS_PER_GROUP)
    outs, lses = [], []
    for g, (window, dil) in enumerate(B_GROUPS):
        assert (window // 2) // dil == B_HALF
        sub = group_qkv[g].shape[0] // b
        n_tiles = sub // B_TILE
        n_halves = sub // B_HALF
        view = group_qkv[g].reshape(b, sub, dil * B_GROUP_COLS)

        def tile_spec(slab):
            return pl.BlockSpec((1, B_TILE, LANES), lambda bi, r, i: (bi, i, r * slabs_per_token + slab))

        def edge_spec(slab, offset, n_halves=n_halves):
            return pl.BlockSpec(
                (1, B_HALF, LANES),
                lambda bi, r, i: (bi, jnp.clip(i * halves_per_tile + offset, 0, n_halves - 1),
                                  r * slabs_per_token + slab))

        def kv_specs(slab):
            return [edge_spec(slab, -1), tile_spec(slab), edge_spec(slab, halves_per_tile)]

        out_map = lambda bi, r, i: (bi, i, r)
        o, lse = pl.pallas_call(
            functools.partial(_attn_b_body, dil=dil,
                              slopes=all_slopes[g * B_HEADS_PER_GROUP:(g + 1) * B_HEADS_PER_GROUP]),
            grid=(b, dil, n_tiles),
            in_specs=[tile_spec(j) for j in range(q_slabs)] + kv_specs(q_slabs) + kv_specs(q_slabs + 1),
            out_specs=[pl.BlockSpec((1, B_TILE, gq), out_map), pl.BlockSpec((1, B_TILE, gq), out_map)],
            out_shape=[jax.ShapeDtypeStruct((b, sub, dil * gq), BF16),
                       jax.ShapeDtypeStruct((b, sub, dil * gq), F32)],
            scratch_shapes=[pltpu.VMEM((3, B_HEADS_PER_GROUP, B_SPAN, BLOCK_Q), F32),
                            pltpu.VMEM((BAND_SLOTS, B_SPAN, 2 * BLOCK_Q), F32)],
            compiler_params=_params("parallel", "arbitrary", "arbitrary"),
            name=f"attn_b_g{g}",
        )(*([view] * (q_slabs + 6)))
        outs.append(o.reshape(b * sub, dil * gq))
        lses.append(lse.reshape(b * sub, dil * gq))
    return outs, lses


PROLOGUE_ROW_CHUNKS = 4


def _mix_plain(refs, scratch):
    return lambda rows: refs[0][rows, :]


def _mix_groups(refs, scratch):
    n_g = len(B_GROUPS)
    o_scr, l_scr = scratch
    slabs = o_scr.shape[0] // n_g
    tm = o_scr.shape[1]
    gq = slabs * LANES
    for g, (_, dil) in enumerate(B_GROUPS):
        for r in range(dil):
            rows = pl.ds(r, tm // dil, stride=dil)
            for s in range(slabs):
                cols = slice(r * gq + s * LANES, r * gq + (s + 1) * LANES)
                o_scr[g * slabs + s, rows, :] = refs[g][:, cols].astype(F32)
                l_scr[g * slabs + s, rows, :] = refs[n_g + g][:, cols]
    def lhs(rows):
        group = lambda scr, g: jnp.concatenate([scr[g * slabs + s, rows, :] for s in range(slabs)], axis=1)
        ls = [group(l_scr, g) for g in range(n_g)]
        mx = functools.reduce(jnp.maximum, ls)
        es = [jnp.exp(l - mx) for l in ls]
        inv = 1.0 / functools.reduce(lambda a, b: a + b, es)
        return jnp.concatenate([(group(o_scr, g) * (es[g] * inv)).astype(BF16) for g in range(n_g)], axis=1)

    return lhs


MLP_HIDDEN_CHUNK = 1024


def _block_body(*refs, n_mix, mix_fn, final_norm):
    h_ref, mix_refs = refs[0], refs[1:1 + n_mix]
    wo_ref, g_ref, w1_ref, w2_ref, fg_ref, out_ref, hn_ref = refs[1 + n_mix:8 + n_mix]
    mix_scratch = refs[8 + n_mix:]

    lhs = mix_fn(mix_refs, mix_scratch)
    chunk = h_ref.shape[0] // PROLOGUE_ROW_CHUNKS
    rows = [slice(c * chunk, (c + 1) * chunk) for c in range(PROLOGUE_ROW_CHUNKS)]
    project = lambda c: jnp.dot(lhs(rows[c]), wo_ref[...], preferred_element_type=F32)
    nxt = project(0)
    for c in range(PROLOGUE_ROW_CHUNKS):
        cur = nxt
        if c + 1 < PROLOGUE_ROW_CHUNKS:
            nxt = project(c + 1)
        h1 = h_ref[rows[c], :] + cur
        out_ref[rows[c], :] = h1
        hn_ref[rows[c], :] = _rms_rows(h1, g_ref[...]).astype(BF16)

    hn = hn_ref[...]
    for c in range(w1_ref.shape[1] // MLP_HIDDEN_CHUNK):
        cols = slice(c * MLP_HIDDEN_CHUNK, (c + 1) * MLP_HIDDEN_CHUNK)
        u = jnp.maximum(jnp.dot(hn, w1_ref[:, cols], preferred_element_type=F32), 0.0)
        out_ref[...] += jnp.dot((u * u).astype(BF16), w2_ref[cols, :], preferred_element_type=F32)

    if final_norm:
        out_ref[...] = _rms_rows(out_ref[...], fg_ref[...])


def _block(h2, mix_inputs, w_o, *, gain, w1, w2, final_gain, final_norm, tm=512):
    t, d = h2.shape
    row_tile = lambda cols: pl.BlockSpec((tm, cols), lambda i: (i, 0))
    whole = lambda a: pl.BlockSpec(a.shape, lambda i: (0, 0), pipeline_mode=pl.Buffered(1))
    scratch = [pltpu.VMEM((tm, d), BF16)]
    if len(mix_inputs) == 1:
        mix_fn, mix_specs = _mix_plain, [row_tile(mix_inputs[0].shape[1])]
    else:
        n_g = len(B_GROUPS)
        gq = B_HEADS_PER_GROUP * HEAD_DIM
        mix_fn = _mix_groups
        mix_specs = [pl.BlockSpec((tm // dil, dil * gq), lambda i: (i, 0)) for _, dil in B_GROUPS] * 2
        scratch += [pltpu.VMEM((n_g * gq // LANES, tm, LANES), F32)] * 2
    return pl.pallas_call(
        functools.partial(_block_body, n_mix=len(mix_inputs), mix_fn=mix_fn, final_norm=final_norm),
        grid=(t // tm,),
        in_specs=[row_tile(d)] + mix_specs + [whole(w_o), whole(gain), whole(w1), whole(w2), whole(final_gain)],
        out_specs=row_tile(d),
        out_shape=jax.ShapeDtypeStruct((t, d), F32),
        scratch_shapes=scratch,
        compiler_params=_params("parallel"),
        name="block_mlp",
    )(h2, *mix_inputs, w_o, gain, w1, w2, final_gain)


def kernel(x, attn_norm, mlp_norm, a_w_qkv, a_q_gain, a_k_gain, a_w_o, b_w_qkv, b_w_o,
           c_w_qkv, c_sinks, c_w_o, mlp_w1, mlp_w2, final_norm):
    b, seq, d = x.shape
    depth = attn_norm.shape[0]
    h = x.reshape(b * seq, d)
    scale = HEAD_DIM ** -0.5 * LOG2E
    a_heads = a_w_o.shape[1] // HEAD_DIM
    c_heads = c_w_o.shape[1] // HEAD_DIM
    cos, sin = _rope_tables(seq)
    head = jnp.arange(2 * LANES) // HEAD_DIM
    seg = (head[:, None] == head[None, :]).astype(BF16)
    tile4 = lambda g: jnp.tile(g, 2 * LANES // HEAD_DIM)[None, :]
    used = [0, 0, 0]
    for layer in range(depth):
        kind = layer % N_MIXERS
        j = used[kind]
        used[kind] += 1
        gain = attn_norm[layer][None, :]
        block = functools.partial(_block, gain=mlp_norm[layer][None, :], w1=mlp_w1[layer].astype(BF16),
                                  w2=mlp_w2[layer].astype(BF16), final_gain=final_norm[None, :],
                                  final_norm=(layer == depth - 1))
        if kind == 0:
            q_cols = a_heads * HEAD_DIM
            k_cols = (a_w_qkv.shape[2] - q_cols) // 2
            qkv = _qkv_proj(h, gain, a_w_qkv[j].astype(BF16), q_cols=q_cols, q_scale=scale,
                            rope=(tile4(a_q_gain[j]), tile4(a_k_gain[j]), cos, sin, seg, k_cols))
            o = _attn_a(qkv.reshape(b, seq, -1), n_heads=a_heads).reshape(b * seq, q_cols)
            h = block(h, [o], a_w_o[j].astype(BF16))
        elif kind == 1:
            q_cols = len(B_GROUPS) * B_HEADS_PER_GROUP * HEAD_DIM
            groups = _qkv_proj_b(h, gain, b_w_qkv[j].astype(BF16), q_cols=q_cols, q_scale=scale)
            outs, lses = _attn_b(groups, b)
            h = block(h, outs + lses, b_w_o[j].astype(BF16))
        else:
            q_cols = c_heads * HEAD_DIM
            qkv = _qkv_proj(h, gain, c_w_qkv[j].astype(BF16), q_cols=q_cols, q_scale=scale)
            o = _attn_c(qkv.reshape(b, seq, -1), c_sinks[j], n_heads=c_heads).reshape(b * seq, q_cols)
            h = block(h, [o], c_w_o[j].astype(BF16))
    return h.reshape(b, seq, d)
```

```python
import functools
import math

import jax
import jax.numpy as jnp
from jax import lax
from jax.experimental import pallas as pl
from jax.experimental.pallas import tpu as pltpu

HEAD_DIM = 64
HALF_HEAD = HEAD_DIM // 2
RMS_EPS = 1e-6
GRID_W = 64
ROPE_THETA = 10000.0
N_MIXERS = 3
A_REP = 4
B_GROUPS = ((128, 1), (512, 4), (2048, 16))
B_HEADS_PER_GROUP = 6
B_KV_PER_GROUP = 2
B_REP = B_HEADS_PER_GROUP // B_KV_PER_GROUP
C_WINDOW = 128
LANES = 128
BF16_SUBLANES = 16
BLOCK_Q = 128
BAND_SLOTS = 4
C_TILE = 2 * BLOCK_Q
B_HALF = BLOCK_Q // 2
B_TILE = 2 * BLOCK_Q
B_SPAN = BLOCK_Q + 2 * B_HALF
B_GROUP_COLS = (B_HEADS_PER_GROUP + 2 * B_KV_PER_GROUP) * HEAD_DIM
VT_ROWS = HEAD_DIM + BF16_SUBLANES
V7X_VMEM_LIMIT_BYTES = 48 * 1024 * 1024
NEG_BIG = -1e30
LOG2E = math.log2(math.e)
LN2 = math.log(2.0)

BF16 = jnp.bfloat16
F32 = jnp.float32


def _params(*sem):
    return pltpu.CompilerParams(dimension_semantics=sem, vmem_limit_bytes=V7X_VMEM_LIMIT_BYTES)


def _nt_dot(a, b):
    return lax.dot_general(a, b, (((1,), (1,)), ((), ())), preferred_element_type=F32)


def _rms_rows(x, gain):
    ms = jnp.mean(x * x, axis=-1, keepdims=True)
    return x * lax.rsqrt(ms + RMS_EPS) * gain


def _qkv_body(x_ref, g_ref, w_ref, o_ref, *, q_cols, q_scale):
    hn = _rms_rows(x_ref[...], g_ref[...]).astype(BF16)
    y = jnp.dot(hn, w_ref[...], preferred_element_type=F32)
    o_ref[:, :q_cols] = (y[:, :q_cols] * q_scale).astype(o_ref.dtype)
    o_ref[:, q_cols:] = y[:, q_cols:].astype(o_ref.dtype)


def _qkv_rope_body(x_ref, g_ref, w_ref, qg_ref, kg_ref, cos_ref, sin_ref, seg_ref, o_ref,
                   *, q_cols, k_cols, q_scale):
    hn = _rms_rows(x_ref[...], g_ref[...]).astype(BF16)
    slab = 2 * LANES
    cos = jnp.concatenate([cos_ref[...], cos_ref[...]], axis=1)
    sin = jnp.concatenate([sin_ref[...], sin_ref[...]], axis=1)
    lane = lax.broadcasted_iota(jnp.int32, (x_ref.shape[0], slab), 1)
    first_half = (lane % HALF_HEAD) < (HALF_HEAD // 2)
    seg = seg_ref[...]
    n_qk = (q_cols + k_cols) // slab
    project = lambda s: jnp.dot(hn, w_ref[:, s * slab:(s + 1) * slab], preferred_element_type=F32)
    y_next = project(0)
    for s in range(n_qk):
        is_q = s * slab < q_cols
        ys = y_next
        y_next = project(s + 1) if s + 1 < n_qk else jnp.dot(
            hn, w_ref[:, q_cols + k_cols:], preferred_element_type=F32)
        sq = ys * ys
        hi = sq.astype(BF16)
        lo = (sq - hi.astype(F32)).astype(BF16)
        ss = jnp.dot(hi, seg, preferred_element_type=F32) + jnp.dot(lo, seg, preferred_element_type=F32)
        gain = qg_ref[...] if is_q else kg_ref[...]
        yn = ys * lax.rsqrt(ss * (1.0 / HEAD_DIM) + RMS_EPS) * gain
        partner = jnp.where(first_half,
                            pltpu.roll(yn, slab - HALF_HEAD // 2, axis=1),
                            pltpu.roll(yn, HALF_HEAD // 2, axis=1))
        r = yn * cos + partner * sin
        if is_q:
            r = r * q_scale
        o_ref[:, s * slab:(s + 1) * slab] = r.astype(o_ref.dtype)
    o_ref[:, q_cols + k_cols:] = y_next.astype(o_ref.dtype)


def _qkv_proj(h2, gain, w, *, q_cols, q_scale, tm=1024, rope=None):
    t, d = h2.shape
    n = w.shape[1]
    common = dict(
        grid=(t // tm,),
        out_specs=pl.BlockSpec((tm, n), lambda i: (i, 0)),
        out_shape=jax.ShapeDtypeStruct((t, n), BF16),
        compiler_params=_params("parallel"),
    )
    x_spec = pl.BlockSpec((tm, d), lambda i: (i, 0))
    g_spec = pl.BlockSpec((1, d), lambda i: (0, 0))
    w_spec = pl.BlockSpec((d, n), lambda i: (0, 0))
    if rope is None:
        return pl.pallas_call(
            functools.partial(_qkv_body, q_cols=q_cols, q_scale=q_scale),
            in_specs=[x_spec, g_spec, w_spec], name="qkv_proj", **common,
        )(h2, gain, w)
    qg, kg, cos, sin, seg, k_cols = rope
    seq_tiles = cos.shape[0] // tm
    slab = 2 * LANES
    return pl.pallas_call(
        functools.partial(_qkv_rope_body, q_cols=q_cols, k_cols=k_cols, q_scale=q_scale),
        in_specs=[x_spec, g_spec, w_spec,
                  pl.BlockSpec((1, slab), lambda i: (0, 0)),
                  pl.BlockSpec((1, slab), lambda i: (0, 0)),
                  pl.BlockSpec((tm, LANES), lambda i: (i % seq_tiles, 0)),
                  pl.BlockSpec((tm, LANES), lambda i: (i % seq_tiles, 0)),
                  pl.BlockSpec((slab, slab), lambda i: (0, 0))],
        name="qkv_proj_rope", **common,
    )(h2, gain, w, qg, kg, cos, sin, seg)


def _qkv_b_body(x_ref, g_ref, w_ref, *rest, q_cols, q_scale):
    out_refs, y_scr = rest[:-1], rest[-1]
    tm = x_ref.shape[0]
    hn = _rms_rows(x_ref[...], g_ref[...]).astype(BF16)
    y = jnp.dot(hn, w_ref[...], preferred_element_type=F32)
    for s in range(y_scr.shape[0]):
        ys = y[:, s * LANES:(s + 1) * LANES]
        y_scr[s] = ys * q_scale if s * LANES < q_cols else ys
    n_g = len(B_GROUPS)
    q_slabs = B_HEADS_PER_GROUP * HEAD_DIM // LANES
    k0 = q_cols // LANES
    for g, (_, dil) in enumerate(B_GROUPS):
        slabs = list(range(g * q_slabs, (g + 1) * q_slabs)) + [k0 + g, k0 + n_g + g]
        for r in range(dil):
            rows = pl.ds(r, tm // dil, stride=dil)
            piece = jnp.concatenate([y_scr[s, rows, :] for s in slabs], axis=1)
            out_refs[g][:, r * B_GROUP_COLS:(r + 1) * B_GROUP_COLS] = piece.astype(BF16)


def _qkv_proj_b(h2, gain, w, *, q_cols, q_scale, tm=1024):
    t, d = h2.shape
    n = w.shape[1]
    return pl.pallas_call(
        functools.partial(_qkv_b_body, q_cols=q_cols, q_scale=q_scale),
        grid=(t // tm,),
        in_specs=[pl.BlockSpec((tm, d), lambda i: (i, 0)),
                  pl.BlockSpec((1, d), lambda i: (0, 0)),
                  pl.BlockSpec((d, n), lambda i: (0, 0))],
        out_specs=[pl.BlockSpec((tm // dil, dil * B_GROUP_COLS), lambda i: (i, 0)) for _, dil in B_GROUPS],
        out_shape=[jax.ShapeDtypeStruct((t // dil, dil * B_GROUP_COLS), BF16) for _, dil in B_GROUPS],
        scratch_shapes=[pltpu.VMEM((n // LANES, tm, LANES), F32)],
        compiler_params=_params("parallel"),
        name="qkv_proj_b",
    )(h2, gain, w)


def _rope_tables(seq):
    t = jnp.arange(seq)
    row = (t // GRID_W).astype(F32)
    col = (t % GRID_W).astype(F32)
    inv_freq = ROPE_THETA ** (-jnp.arange(0, HALF_HEAD, 2, dtype=F32) / HALF_HEAD)
    ang_row = row[:, None] * inv_freq
    ang_col = col[:, None] * inv_freq
    cos = jnp.concatenate([jnp.cos(ang_row)] * 2 + [jnp.cos(ang_col)] * 2, axis=1)
    sin = jnp.concatenate([-jnp.sin(ang_row), jnp.sin(ang_row), -jnp.sin(ang_col), jnp.sin(ang_col)], axis=1)
    return jnp.concatenate([cos, cos], axis=1), jnp.concatenate([sin, sin], axis=1)


def _attn_a_body(q_ref, k_ref, v_ref, o_ref, klo, khi, vt1, s_buf, *, tq, tk, seq, tiles_per_step):
    def fill(head_in_high_lanes):
        low = lax.broadcasted_iota(jnp.int32, (seq, LANES), 1) < HEAD_DIM
        zero = jnp.zeros((seq, LANES), BF16)
        x = k_ref[0]
        swapped = jnp.concatenate([x[:, HEAD_DIM:], x[:, :HEAD_DIM]], axis=1)
        in_low, in_high = (swapped, x) if head_in_high_lanes else (x, swapped)
        klo[...] = jnp.where(low, in_low, zero)
        khi[...] = jnp.where(low, zero, in_high)
        ones_row = (lax.broadcasted_iota(jnp.int32, (VT_ROWS - HEAD_DIM, tk), 0) == 0).astype(F32)
        for c in range(seq // tk):
            xt = v_ref[0, c * tk:(c + 1) * tk, :].astype(F32).T
            vt = xt[HEAD_DIM:] if head_in_high_lanes else xt[:HEAD_DIM]
            vt1[:, c * tk:(c + 1) * tk] = jnp.concatenate([vt, ones_row], axis=0).astype(BF16)

    odd = (pl.program_id(1) % 2) == 1
    pl.when(jnp.logical_not(odd))(lambda: fill(False))
    pl.when(odd)(lambda: fill(True))

    cols = 2 * tq
    n_chunks = seq // tk
    n_tiles = seq // tq

    def scores(tile, c, slot):
        q = q_ref[0, pl.ds(pl.multiple_of(tile * tq, tq), tq), :]
        q2 = jnp.concatenate([q[:, :LANES], q[:, LANES:]], axis=0)
        cms = []
        for par, k_s in enumerate((klo, khi)):
            s = _nt_dot(k_s[c * tk:(c + 1) * tk, :], q2)
            s_buf[slot, par] = s
            cms.append(jnp.max(s, axis=0, keepdims=True))
        return tuple(cms)

    def accumulate(c, slot, cms, state):
        vt = vt1[:, c * tk:(c + 1) * tk]
        new = []
        for par in range(2):
            m, acc = state[2 * par], state[2 * par + 1]
            mn = jnp.maximum(m, cms[par])
            p = jnp.exp2(s_buf[slot, par] - mn)
            new += [mn, acc * jnp.exp2(m - mn) + jnp.dot(vt, p.astype(BF16), preferred_element_type=F32)]
        return tuple(new)

    neg = jnp.full((1, cols), -jnp.inf, F32)
    acc0 = jnp.zeros((VT_ROWS, cols), F32)

    def step(j, cms):
        for t in range(tiles_per_step):
            tile = j * tiles_per_step + t
            state = (neg, acc0, neg, acc0)
            for c in range(n_chunks):
                n = t * n_chunks + c
                nxt_tile = tile if c + 1 < n_chunks else jnp.minimum(tile + 1, n_tiles - 1)
                cms_next = scores(nxt_tile, (c + 1) % n_chunks, (n + 1) % 2)
                state = accumulate(c, n % 2, cms, state)
                cms = cms_next
            out = jnp.concatenate(
                [acc[:HEAD_DIM] / acc[HEAD_DIM:HEAD_DIM + 1] for acc in (state[1], state[3])], axis=0)
            o_ref[0, pl.ds(pl.multiple_of(tile * tq, tq), tq), :] = jnp.concatenate(
                [out[:, :tq].T, out[:, tq:].T], axis=1).astype(o_ref.dtype)
        return cms

    lax.fori_loop(0, n_tiles // tiles_per_step, step, scores(0, 0, 0))


def _attn_a(qkv, *, n_heads, tq=256, tk=512, tiles_per_step=2):
    b, seq, _ = qkv.shape
    kv_heads = n_heads // A_REP
    q_cols = n_heads * HEAD_DIM
    q_w = A_REP * HEAD_DIM
    k_blk0 = q_cols // LANES
    v_blk0 = (q_cols + kv_heads * HEAD_DIM) // LANES
    assert (seq // tk) % 2 == 0 and seq % (tq * tiles_per_step) == 0
    return pl.pallas_call(
        functools.partial(_attn_a_body, tq=tq, tk=tk, seq=seq, tiles_per_step=tiles_per_step),
        grid=(b, kv_heads),
        in_specs=[pl.BlockSpec((1, seq, q_w), lambda bi, h: (bi, 0, h)),
                  pl.BlockSpec((1, seq, LANES), lambda bi, h: (bi, 0, k_blk0 + h // 2)),
                  pl.BlockSpec((1, seq, LANES), lambda bi, h: (bi, 0, v_blk0 + h // 2))],
        out_specs=pl.BlockSpec((1, seq, q_w), lambda bi, h: (bi, 0, h)),
        out_shape=jax.ShapeDtypeStruct((b, seq, q_cols), BF16),
        scratch_shapes=[pltpu.VMEM((seq, LANES), BF16), pltpu.VMEM((seq, LANES), BF16),
                        pltpu.VMEM((VT_ROWS, seq), BF16), pltpu.VMEM((2, 2, tk, 2 * tq), F32)],
        compiler_params=_params("parallel", "arbitrary"),
        name="attn_a",
    )(qkv, qkv, qkv)


def _fill_band_bias(bias_ref, slopes, half_window, dist_scale):
    span = bias_ref.shape[2]
    c = lax.broadcasted_iota(jnp.int32, (span, BLOCK_Q), 0)
    r = lax.broadcasted_iota(jnp.int32, (span, BLOCK_Q), 1)
    dist = jnp.abs(c - half_window - r)
    inside = dist <= half_window
    penalty = dist.astype(F32) * (dist_scale * LOG2E)
    for variant, ok in enumerate((inside & (c >= half_window), inside, inside & (c < half_window + BLOCK_Q))):
        for h, slope in enumerate(slopes):
            bias_ref[variant, h] = jnp.where(ok, -slope * penalty, NEG_BIG)


def _band_pipeline(jobs, bias_ref, s_buf):
    def scores(job, slot):
        kmat, q_rows, head_ids, _, _, variant, sink_row = job
        width = len(head_ids) * BLOCK_Q
        s = _nt_dot(kmat, q_rows) + jnp.concatenate([bias_ref[variant, h] for h in head_ids], axis=1)
        s_buf[slot, :, :width] = s
        m = jnp.max(s, axis=0, keepdims=True)
        return m if sink_row is None else jnp.maximum(m, sink_row)

    def values(job, slot, m):
        _, _, head_ids, kv_ids, vt1s, _, sink_row = job
        n = len(head_ids)
        p = jnp.exp2(s_buf[slot, :, :n * BLOCK_Q] - m).astype(BF16)
        accs, start = [], 0
        while start < n:
            stop = start
            while stop < n and kv_ids[stop] == kv_ids[start]:
                stop += 1
            accs.append(jnp.dot(vt1s[kv_ids[start]], p[:, start * BLOCK_Q:stop * BLOCK_Q],
                                preferred_element_type=F32))
            start = stop
        acc = accs[0] if len(accs) == 1 else jnp.concatenate(accs, axis=1)
        den = acc[HEAD_DIM:HEAD_DIM + 1]
        if sink_row is not None:
            den = den + jnp.exp2(sink_row - m)
        return acc[:HEAD_DIM] / den, m, den

    results = []
    n_slots = s_buf.shape[0]
    ahead = n_slots - 1
    maxima = [scores(jobs[c], c % n_slots) for c in range(min(ahead, len(jobs)))]
    for c, job in enumerate(jobs):
        if c + ahead < len(jobs):
            maxima.append(scores(jobs[c + ahead], (c + ahead) % n_slots))
        results.append(values(job, c % n_slots, maxima[c]))
    return results


def _band_operands(k_refs, v_refs):
    k = jnp.concatenate([r[0] for r in k_refs], axis=0)
    vt = jnp.concatenate([r[0] for r in v_refs], axis=0).astype(F32).T
    return k, vt


def _vt_with_ones(vt, kv, start, span):
    ones_row = (lax.broadcasted_iota(jnp.int32, (VT_ROWS - HEAD_DIM, span), 0) == 0).astype(F32)
    rows = vt[kv * HEAD_DIM:(kv + 1) * HEAD_DIM, start:start + span]
    return jnp.concatenate([rows, ones_row], axis=0).astype(BF16)


def _swap_halves(x):
    return jnp.concatenate([x[:, HEAD_DIM:], x[:, :HEAD_DIM]], axis=1)


def _attn_c_body(sink_ref, q_ref, kp_ref, kc_ref, kn_ref, vp_ref, vc_ref, vn_ref, o_ref, bias_ref, s_buf,
                 *, n_heads, slopes):
    i = pl.program_id(1)
    pl.when(i == 0)(lambda: _fill_band_bias(bias_ref, slopes, C_WINDOW, 1.0))
    block_variants = (jnp.where(i == 0, 0, 1), jnp.where(i == pl.num_programs(1) - 1, 2, 1))
    q = q_ref[0]
    k, vt = _band_operands((kp_ref, kc_ref, kn_ref), (vp_ref, vc_ref, vn_ref))
    span = BLOCK_Q + 2 * C_WINDOW
    low = lax.broadcasted_iota(jnp.int32, (BLOCK_Q, LANES), 1) < HEAD_DIM
    zero = jnp.zeros((BLOCK_Q, LANES), BF16)
    first_block = lax.broadcasted_iota(jnp.int32, (1, 2 * BLOCK_Q), 1) < BLOCK_Q
    jobs, job_block = [], []
    for blk in range(C_TILE // BLOCK_Q):
        q_lo = blk * BLOCK_Q
        for pair in range(n_heads // A_REP // 2):
            ks = k[q_lo:q_lo + span, pair * LANES:(pair + 1) * LANES]
            ks_swapped = _swap_halves(ks)
            for which in range(2):
                kv = 2 * pair + which
                vt1s = {kv: _vt_with_ones(vt, kv, q_lo, span)}
                slabs = [q[q_lo:q_lo + BLOCK_Q, (2 * kv + j) * LANES:(2 * kv + j + 1) * LANES] for j in range(2)]
                for parity in range(2):
                    q_rows = jnp.concatenate(
                        [jnp.where(low, s_, zero) if parity == 0 else jnp.where(low, zero, s_) for s_ in slabs],
                        axis=0)
                    heads = (A_REP * kv + parity, A_REP * kv + 2 + parity)
                    sink_row = jnp.where(first_block, sink_ref[heads[0]], sink_ref[heads[1]]) * LOG2E
                    jobs.append((ks if parity == which else ks_swapped, q_rows, heads, (kv, kv), vt1s,
                                 block_variants[blk], sink_row))
                    job_block.append(blk)
    o_t = {}
    for job, blk, (ot, _, _) in zip(jobs, job_block, _band_pipeline(jobs, bias_ref, s_buf)):
        o_t[blk, job[2][0]], o_t[blk, job[2][1]] = ot[:, :BLOCK_Q], ot[:, BLOCK_Q:]
    for blk in range(C_TILE // BLOCK_Q):
        for j in range(n_heads // 2):
            pair_t = jnp.concatenate([o_t[blk, 2 * j], o_t[blk, 2 * j + 1]], axis=0)
            o_ref[0, blk * BLOCK_Q:(blk + 1) * BLOCK_Q, j * LANES:(j + 1) * LANES] = pair_t.T.astype(o_ref.dtype)


def _alibi_slopes(n):
    return [2.0 ** (-8.0 * (i + 1) / n) for i in range(n)]


def _attn_c(qkv, sinks, *, n_heads):
    b, seq, _ = qkv.shape
    kv_heads = n_heads // A_REP
    q_cols = n_heads * HEAD_DIM
    kv_cols = kv_heads * HEAD_DIM
    n_tiles = seq // C_TILE
    n_edges = seq // C_WINDOW
    edges_per_tile = C_TILE // C_WINDOW
    k_blk = q_cols // kv_cols
    v_blk = k_blk + 1
    span = BLOCK_Q + 2 * C_WINDOW

    def kv_specs(col_blk):
        edge = lambda offset: pl.BlockSpec(
            (1, C_WINDOW, kv_cols),
            lambda bi, i: (bi, jnp.clip(i * edges_per_tile + offset, 0, n_edges - 1), col_blk))
        return [edge(-1), pl.BlockSpec((1, C_TILE, kv_cols), lambda bi, i: (bi, i, col_blk)),
                edge(edges_per_tile)]

    return pl.pallas_call(
        functools.partial(_attn_c_body, n_heads=n_heads, slopes=_alibi_slopes(n_heads)),
        grid=(b, n_tiles),
        in_specs=[pl.BlockSpec(memory_space=pltpu.SMEM),
                  pl.BlockSpec((1, C_TILE, q_cols), lambda bi, i: (bi, i, 0))] + kv_specs(k_blk) + kv_specs(v_blk),
        out_specs=pl.BlockSpec((1, C_TILE, q_cols), lambda bi, i: (bi, i, 0)),
        out_shape=jax.ShapeDtypeStruct((b, seq, q_cols), BF16),
        scratch_shapes=[pltpu.VMEM((3, n_heads, span, BLOCK_Q), F32),
                        pltpu.VMEM((BAND_SLOTS, span, 2 * BLOCK_Q), F32)],
        compiler_params=_params("parallel", "arbitrary"),
        name="attn_c",
    )(sinks, qkv, qkv, qkv, qkv, qkv, qkv, qkv)


def _attn_b_body(q0_ref, q1_ref, q2_ref, kp_ref, kc_ref, kn_ref, vp_ref, vc_ref, vn_ref, o_ref, lse_ref,
                 bias_ref, s_buf, *, slopes, dil):
    i = pl.program_id(2)
    pl.when(jnp.logical_and(pl.program_id(1) == 0, i == 0))(
        lambda: _fill_band_bias(bias_ref, slopes, B_HALF, float(dil)))
    block_variants = (jnp.where(i == 0, 0, 1), jnp.where(i == pl.num_programs(2) - 1, 2, 1))
    k, vt = _band_operands((kp_ref, kc_ref, kn_ref), (vp_ref, vc_ref, vn_ref))
    low = lax.broadcasted_iota(jnp.int32, (BLOCK_Q, LANES), 1) < HEAD_DIM
    zero = jnp.zeros((BLOCK_Q, LANES), BF16)
    slabs = [q0_ref[0], q1_ref[0], q2_ref[0]]
    heads_all = range(B_HEADS_PER_GROUP)
    plain = [h for h in heads_all if h % 2 == h // B_REP]
    crossed = [h for h in heads_all if h % 2 != h // B_REP]
    jobs, job_block = [], []
    for blk in range(B_TILE // BLOCK_Q):
        q_lo = blk * BLOCK_Q
        ks = k[q_lo:q_lo + B_SPAN]
        ks_swapped = _swap_halves(ks)
        vt1s = {kv: _vt_with_ones(vt, kv, q_lo, B_SPAN) for kv in range(B_KV_PER_GROUP)}
        for kmat, heads in ((ks, plain[:2]), (ks_swapped, crossed), (ks, plain[2:])):
            q_rows = jnp.concatenate(
                [jnp.where(low, slabs[h // 2][q_lo:q_lo + BLOCK_Q], zero) if h % 2 == 0
                 else jnp.where(low, zero, slabs[h // 2][q_lo:q_lo + BLOCK_Q]) for h in heads], axis=0)
            jobs.append((kmat, q_rows, heads, [h // B_REP for h in heads], vt1s, block_variants[blk], None))
            job_block.append(blk)
    o_t, lse_t = {}, {}
    for job, blk, (ot, m, den) in zip(jobs, job_block, _band_pipeline(jobs, bias_ref, s_buf)):
        lse = (m + jnp.log2(den)) * LN2
        for n, h in enumerate(job[2]):
            o_t[blk, h] = ot[:, n * BLOCK_Q:(n + 1) * BLOCK_Q]
            lse_t[blk, h] = jnp.broadcast_to(lse[:, n * BLOCK_Q:(n + 1) * BLOCK_Q], (HEAD_DIM, BLOCK_Q))
    for blk in range(B_TILE // BLOCK_Q):
        rows = slice(blk * BLOCK_Q, (blk + 1) * BLOCK_Q)
        for j in range(B_HEADS_PER_GROUP // 2):
            cols = slice(j * LANES, (j + 1) * LANES)
            pair = lambda t: jnp.concatenate([t[blk, 2 * j], t[blk, 2 * j + 1]], axis=0).T
            o_ref[0, rows, cols] = pair(o_t).astype(o_ref.dtype)
            lse_ref[0, rows, cols] = pair(lse_t)


def _attn_b(group_qkv, b):
    n_g = len(B_GROUPS)
    gq = B_HEADS_PER_GROUP * HEAD_DIM
    slabs_per_token = B_GROUP_COLS // LANES
    q_slabs = gq // LANES
    halves_per_tile = B_TILE // B_HALF
    all_slopes = _alibi_slopes(n_g * B_HEADS_PER_GROUP)
    outs, lses = [], []
    for g, (window, dil) in enumerate(B_GROUPS):
        assert (window // 2) // dil == B_HALF
        sub = group_qkv[g].shape[0] // b
        n_tiles = sub // B_TILE
        n_halves = sub // B_HALF
        view = group_qkv[g].reshape(b, sub, dil * B_GROUP_COLS)

        def tile_spec(slab):
            return pl.BlockSpec((1, B_TILE, LANES), lambda bi, r, i: (bi, i, r * slabs_per_token + slab))

        def edge_spec(slab, offset, n_halves=n_halves):
            return pl.BlockSpec(
                (1, B_HALF, LANES),
                lambda bi, r, i: (bi, jnp.clip(i * halves_per_tile + offset, 0, n_halves - 1),
                                  r * slabs_per_token + slab))

        def kv_specs(slab):
            return [edge_spec(slab, -1), tile_spec(slab), edge_spec(slab, halves_per_tile)]

        out_map = lambda bi, r, i: (bi, i, r)
        o, lse = pl.pallas_call(
            functools.partial(_attn_b_body, dil=dil,
                              slopes=all_slopes[g * B_HEADS_PER_GROUP:(g + 1) * B_HEADS_PER_GROUP]),
            grid=(b, dil, n_tiles),
            in_specs=[tile_spec(j) for j in range(q_slabs)] + kv_specs(q_slabs) + kv_specs(q_slabs + 1),
            out_specs=[pl.BlockSpec((1, B_TILE, gq), out_map), pl.BlockSpec((1, B_TILE, gq), out_map)],
            out_shape=[jax.ShapeDtypeStruct((b, sub, dil * gq), BF16),
                       jax.ShapeDtypeStruct((b, sub, dil * gq), F32)],
            scratch_shapes=[pltpu.VMEM((3, B_HEADS_PER_GROUP, B_SPAN, BLOCK_Q), F32),
                            pltpu.VMEM((BAND_SLOTS, B_SPAN, 2 * BLOCK_Q), F32)],
            compiler_params=_params("parallel", "arbitrary", "arbitrary"),
            name=f"attn_b_g{g}",
        )(*([view] * (q_slabs + 6)))
        outs.append(o.reshape(b * sub, dil * gq))
        lses.append(lse.reshape(b * sub, dil * gq))
    return outs, lses


PROLOGUE_ROW_CHUNKS = 4


def _mix_plain(refs, scratch):
    return lambda rows: refs[0][rows, :]


def _mix_groups(refs, scratch):
    n_g = len(B_GROUPS)
    o_scr, l_scr = scratch
    slabs = o_scr.shape[0] // n_g
    tm = o_scr.shape[1]
    gq = slabs * LANES
    for g, (_, dil) in enumerate(B_GROUPS):
        for r in range(dil):
            rows = pl.ds(r, tm // dil, stride=dil)
            for s in range(slabs):
                cols = slice(r * gq + s * LANES, r * gq + (s + 1) * LANES)
                o_scr[g * slabs + s, rows, :] = refs[g][:, cols].astype(F32)
                l_scr[g * slabs + s, rows, :] = refs[n_g + g][:, cols]
    def lhs(rows):
        group = lambda scr, g: jnp.concatenate([scr[g * slabs + s, rows, :] for s in range(slabs)], axis=1)
        ls = [group(l_scr, g) for g in range(n_g)]
        mx = functools.reduce(jnp.maximum, ls)
        es = [jnp.exp(l - mx) for l in ls]
        inv = 1.0 / functools.reduce(lambda a, b: a + b, es)
        return jnp.concatenate([(group(o_scr, g) * (es[g] * inv)).astype(BF16) for g in range(n_g)], axis=1)

    return lhs


MLP_HIDDEN_CHUNK = 1024


def _block_body(*refs, n_mix, mix_fn, final_norm):
    h_ref, mix_refs = refs[0], refs[1:1 + n_mix]
    wo_ref, g_ref, w1_ref, w2_ref, fg_ref, out_ref, hn_ref = refs[1 + n_mix:8 + n_mix]
    mix_scratch = refs[8 + n_mix:]

    lhs = mix_fn(mix_refs, mix_scratch)
    chunk = h_ref.shape[0] // PROLOGUE_ROW_CHUNKS
    rows = [slice(c * chunk, (c + 1) * chunk) for c in range(PROLOGUE_ROW_CHUNKS)]
    project = lambda c: jnp.dot(lhs(rows[c]), wo_ref[...], preferred_element_type=F32)
    nxt = project(0)
    for c in range(PROLOGUE_ROW_CHUNKS):
        cur = nxt
        if c + 1 < PROLOGUE_ROW_CHUNKS:
            nxt = project(c + 1)
        h1 = h_ref[rows[c], :] + cur
        out_ref[rows[c], :] = h1
        hn_ref[rows[c], :] = _rms_rows(h1, g_ref[...]).astype(BF16)

    hn = hn_ref[...]
    for c in range(w1_ref.shape[1] // MLP_HIDDEN_CHUNK):
        cols = slice(c * MLP_HIDDEN_CHUNK, (c + 1) * MLP_HIDDEN_CHUNK)
        u = jnp.maximum(jnp.dot(hn, w1_ref[:, cols], preferred_element_type=F32), 0.0)
        out_ref[...] += jnp.dot((u * u).astype(BF16), w2_ref[cols, :], preferred_element_type=F32)

    if final_norm:
        out_ref[...] = _rms_rows(out_ref[...], fg_ref[...])


def _block(h2, mix_inputs, w_o, *, gain, w1, w2, final_gain, final_norm, tm=512):
    t, d = h2.shape
    row_tile = lambda cols: pl.BlockSpec((tm, cols), lambda i: (i, 0))
    whole = lambda a: pl.BlockSpec(a.shape, lambda i: (0, 0), pipeline_mode=pl.Buffered(1))
    scratch = [pltpu.VMEM((tm, d), BF16)]
    if len(mix_inputs) == 1:
        mix_fn, mix_specs = _mix_plain, [row_tile(mix_inputs[0].shape[1])]
    else:
        n_g = len(B_GROUPS)
        gq = B_HEADS_PER_GROUP * HEAD_DIM
        mix_fn = _mix_groups
        mix_specs = [pl.BlockSpec((tm // dil, dil * gq), lambda i: (i, 0)) for _, dil in B_GROUPS] * 2
        scratch += [pltpu.VMEM((n_g * gq // LANES, tm, LANES), F32)] * 2
    return pl.pallas_call(
        functools.partial(_block_body, n_mix=len(mix_inputs), mix_fn=mix_fn, final_norm=final_norm),
        grid=(t // tm,),
        in_specs=[row_tile(d)] + mix_specs + [whole(w_o), whole(gain), whole(w1), whole(w2), whole(final_gain)],
        out_specs=row_tile(d),
        out_shape=jax.ShapeDtypeStruct((t, d), F32),
        scratch_shapes=scratch,
        compiler_params=_params("parallel"),
        name="block_mlp",
    )(h2, *mix_inputs, w_o, gain, w1, w2, final_gain)


def kernel(x, attn_norm, mlp_norm, a_w_qkv, a_q_gain, a_k_gain, a_w_o, b_w_qkv, b_w_o,
           c_w_qkv, c_sinks, c_w_o, mlp_w1, mlp_w2, final_norm):
    b, seq, d = x.shape
    depth = attn_norm.shape[0]
    h = x.reshape(b * seq, d)
    scale = HEAD_DIM ** -0.5 * LOG2E
    a_heads = a_w_o.shape[1] // HEAD_DIM
    c_heads = c_w_o.shape[1] // HEAD_DIM
    cos, sin = _rope_tables(seq)
    head = jnp.arange(2 * LANES) // HEAD_DIM
    seg = (head[:, None] == head[None, :]).astype(BF16)
    tile4 = lambda g: jnp.tile(g, 2 * LANES // HEAD_DIM)[None, :]
    used = [0, 0, 0]
    for layer in range(depth):
        kind = layer % N_MIXERS
        j = used[kind]
        used[kind] += 1
        gain = attn_norm[layer][None, :]
        block = functools.partial(_block, gain=mlp_norm[layer][None, :], w1=mlp_w1[layer].astype(BF16),
                                  w2=mlp_w2[layer].astype(BF16), final_gain=final_norm[None, :],
                                  final_norm=(layer == depth - 1))
        if kind == 0:
            q_cols = a_heads * HEAD_DIM
            k_cols = (a_w_qkv.shape[2] - q_cols) // 2
            qkv = _qkv_proj(h, gain, a_w_qkv[j].astype(BF16), q_cols=q_cols, q_scale=scale,
                            rope=(tile4(a_q_gain[j]), tile4(a_k_gain[j]), cos, sin, seg, k_cols))
            o = _attn_a(qkv.reshape(b, seq, -1), n_heads=a_heads).reshape(b * seq, q_cols)
            h = block(h, [o], a_w_o[j].astype(BF16))
        elif kind == 1:
            q_cols = len(B_GROUPS) * B_HEADS_PER_GROUP * HEAD_DIM
            groups = _qkv_proj_b(h, gain, b_w_qkv[j].astype(BF16), q_cols=q_cols, q_scale=scale)
            outs, lses = _attn_b(groups, b)
            h = block(h, outs + lses, b_w_o[j].astype(BF16))
        else:
            q_cols = c_heads * HEAD_DIM
            qkv = _qkv_proj(h, gain, c_w_qkv[j].astype(BF16), q_cols=q_cols, q_scale=scale)
            o = _attn_c(qkv.reshape(b, seq, -1), c_sinks[j], n_heads=c_heads).reshape(b * seq, q_cols)
            h = block(h, [o], c_w_o[j].astype(BF16))
    return h.reshape(b, seq, d)
```

```python
import functools
import math

import jax
import jax.numpy as jnp
from jax import lax
from jax.experimental import pallas as pl
from jax.experimental.pallas import tpu as pltpu

HEAD_DIM = 64
HALF_HEAD = HEAD_DIM // 2
RMS_EPS = 1e-6
GRID_W = 64
ROPE_THETA = 10000.0
N_MIXERS = 3
A_REP = 4
B_GROUPS = ((128, 1), (512, 4), (2048, 16))
B_HEADS_PER_GROUP = 6
B_KV_PER_GROUP = 2
B_REP = B_HEADS_PER_GROUP // B_KV_PER_GROUP
C_WINDOW = 128
LANES = 128
BF16_SUBLANES = 16
BLOCK_Q = 128
BAND_SLOTS = 4
C_TILE = 2 * BLOCK_Q
B_HALF = BLOCK_Q // 2
B_TILE = 2 * BLOCK_Q
B_SPAN = BLOCK_Q + 2 * B_HALF
B_GROUP_COLS = (B_HEADS_PER_GROUP + 2 * B_KV_PER_GROUP) * HEAD_DIM
VT_ROWS = HEAD_DIM + BF16_SUBLANES
V7X_VMEM_LIMIT_BYTES = 48 * 1024 * 1024
NEG_BIG = -1e30
LOG2E = math.log2(math.e)
LN2 = math.log(2.0)

BF16 = jnp.bfloat16
F32 = jnp.float32


def _params(*sem):
    return pltpu.CompilerParams(dimension_semantics=sem, vmem_limit_bytes=V7X_VMEM_LIMIT_BYTES)


def _nt_dot(a, b):
    return lax.dot_general(a, b, (((1,), (1,)), ((), ())), preferred_element_type=F32)


def _rms_rows(x, gain):
    ms = jnp.mean(x * x, axis=-1, keepdims=True)
    return x * lax.rsqrt(ms + RMS_EPS) * gain


def _qkv_body(x_ref, g_ref, w_ref, o_ref, *, q_cols, q_scale):
    hn = _rms_rows(x_ref[...], g_ref[...]).astype(BF16)
    y = jnp.dot(hn, w_ref[...], preferred_element_type=F32)
    o_ref[:, :q_cols] = (y[:, :q_cols] * q_scale).astype(o_ref.dtype)
    o_ref[:, q_cols:] = y[:, q_cols:].astype(o_ref.dtype)


def _qkv_rope_body(x_ref, g_ref, w_ref, qg_ref, kg_ref, cos_ref, sin_ref, seg_ref, o_ref,
                   *, q_cols, k_cols, q_scale):
    hn = _rms_rows(x_ref[...], g_ref[...]).astype(BF16)
    slab = 2 * LANES
    cos = jnp.concatenate([cos_ref[...], cos_ref[...]], axis=1)
    sin = jnp.concatenate([sin_ref[...], sin_ref[...]], axis=1)
    lane = lax.broadcasted_iota(jnp.int32, (x_ref.shape[0], slab), 1)
    first_half = (lane % HALF_HEAD) < (HALF_HEAD // 2)
    seg = seg_ref[...]
    n_qk = (q_cols + k_cols) // slab
    project = lambda s: jnp.dot(hn, w_ref[:, s * slab:(s + 1) * slab], preferred_element_type=F32)
    y_next = project(0)
    for s in range(n_qk):
        is_q = s * slab < q_cols
        ys = y_next
        y_next = project(s + 1) if s + 1 < n_qk else jnp.dot(
            hn, w_ref[:, q_cols + k_cols:], preferred_element_type=F32)
        sq = ys * ys
        hi = sq.astype(BF16)
        lo = (sq - hi.astype(F32)).astype(BF16)
        ss = jnp.dot(hi, seg, preferred_element_type=F32) + jnp.dot(lo, seg, preferred_element_type=F32)
        gain = qg_ref[...] if is_q else kg_ref[...]
        yn = ys * lax.rsqrt(ss * (1.0 / HEAD_DIM) + RMS_EPS) * gain
        partner = jnp.where(first_half,
                            pltpu.roll(yn, slab - HALF_HEAD // 2, axis=1),
                            pltpu.roll(yn, HALF_HEAD // 2, axis=1))
        r = yn * cos + partner * sin
        if is_q:
            r = r * q_scale
        o_ref[:, s * slab:(s + 1) * slab] = r.astype(o_ref.dtype)
    o_ref[:, q_cols + k_cols:] = y_next.astype(o_ref.dtype)


def _qkv_proj(h2, gain, w, *, q_cols, q_scale, tm=1024, rope=None):
    t, d = h2.shape
    n = w.shape[1]
    common = dict(
        grid=(t // tm,),
        out_specs=pl.BlockSpec((tm, n), lambda i: (i, 0)),
        out_shape=jax.ShapeDtypeStruct((t, n), BF16),
        compiler_params=_params("parallel"),
    )
    x_spec = pl.BlockSpec((tm, d), lambda i: (i, 0))
    g_spec = pl.BlockSpec((1, d), lambda i: (0, 0))
    w_spec = pl.BlockSpec((d, n), lambda i: (0, 0))
    if rope is None:
        return pl.pallas_call(
            functools.partial(_qkv_body, q_cols=q_cols, q_scale=q_scale),
            in_specs=[x_spec, g_spec, w_spec], name="qkv_proj", **common,
        )(h2, gain, w)
    qg, kg, cos, sin, seg, k_cols = rope
    seq_tiles = cos.shape[0] // tm
    slab = 2 * LANES
    return pl.pallas_call(
        functools.partial(_qkv_rope_body, q_cols=q_cols, k_cols=k_cols, q_scale=q_scale),
        in_specs=[x_spec, g_spec, w_spec,
                  pl.BlockSpec((1, slab), lambda i: (0, 0)),
                  pl.BlockSpec((1, slab), lambda i: (0, 0)),
                  pl.BlockSpec((tm, LANES), lambda i: (i % seq_tiles, 0)),
                  pl.BlockSpec((tm, LANES), lambda i: (i % seq_tiles, 0)),
                  pl.BlockSpec((slab, slab), lambda i: (0, 0))],
        name="qkv_proj_rope", **common,
    )(h2, gain, w, qg, kg, cos, sin, seg)


def _qkv_b_body(x_ref, g_ref, w_ref, *rest, q_cols, q_scale):
    out_refs, y_scr = rest[:-1], rest[-1]
    tm = x_ref.shape[0]
    hn = _rms_rows(x_ref[...], g_ref[...]).astype(BF16)
    y = jnp.dot(hn, w_ref[...], preferred_element_type=F32)
    for s in range(y_scr.shape[0]):
        ys = y[:, s * LANES:(s + 1) * LANES]
        y_scr[s] = ys * q_scale if s * LANES < q_cols else ys
    n_g = len(B_GROUPS)
    q_slabs = B_HEADS_PER_GROUP * HEAD_DIM // LANES
    k0 = q_cols // LANES
    for g, (_, dil) in enumerate(B_GROUPS):
        slabs = list(range(g * q_slabs, (g + 1) * q_slabs)) + [k0 + g, k0 + n_g + g]
        for r in range(dil):
            rows = pl.ds(r, tm // dil, stride=dil)
            piece = jnp.concatenate([y_scr[s, rows, :] for s in slabs], axis=1)
            out_refs[g][:, r * B_GROUP_COLS:(r + 1) * B_GROUP_COLS] = piece.astype(BF16)


def _qkv_proj_b(h2, gain, w, *, q_cols, q_scale, tm=1024):
    t, d = h2.shape
    n = w.shape[1]
    return pl.pallas_call(
        functools.partial(_qkv_b_body, q_cols=q_cols, q_scale=q_scale),
        grid=(t // tm,),
        in_specs=[pl.BlockSpec((tm, d), lambda i: (i, 0)),
                  pl.BlockSpec((1, d), lambda i: (0, 0)),
                  pl.BlockSpec((d, n), lambda i: (0, 0))],
        out_specs=[pl.BlockSpec((tm // dil, dil * B_GROUP_COLS), lambda i: (i, 0)) for _, dil in B_GROUPS],
        out_shape=[jax.ShapeDtypeStruct((t // dil, dil * B_GROUP_COLS), BF16) for _, dil in B_GROUPS],
        scratch_shapes=[pltpu.VMEM((n // LANES, tm, LANES), F32)],
        compiler_params=_params("parallel"),
        name="qkv_proj_b",
    )(h2, gain, w)


def _rope_tables(seq):
    t = jnp.arange(seq)
    row = (t // GRID_W).astype(F32)
    col = (t % GRID_W).astype(F32)
    inv_freq = ROPE_THETA ** (-jnp.arange(0, HALF_HEAD, 2, dtype=F32) / HALF_HEAD)
    ang_row = row[:, None] * inv_freq
    ang_col = col[:, None] * inv_freq
    cos = jnp.concatenate([jnp.cos(ang_row)] * 2 + [jnp.cos(ang_col)] * 2, axis=1)
    sin = jnp.concatenate([-jnp.sin(ang_row), jnp.sin(ang_row), -jnp.sin(ang_col), jnp.sin(ang_col)], axis=1)
    return jnp.concatenate([cos, cos], axis=1), jnp.concatenate([sin, sin], axis=1)


def _attn_a_body(q_ref, k_ref, v_ref, o_ref, klo, khi, vt1, s_buf, *, tq, tk, seq, tiles_per_step):
    def fill(head_in_high_lanes):
        low = lax.broadcasted_iota(jnp.int32, (seq, LANES), 1) < HEAD_DIM
        zero = jnp.zeros((seq, LANES), BF16)
        x = k_ref[0]
        swapped = jnp.concatenate([x[:, HEAD_DIM:], x[:, :HEAD_DIM]], axis=1)
        in_low, in_high = (swapped, x) if head_in_high_lanes else (x, swapped)
        klo[...] = jnp.where(low, in_low, zero)
        khi[...] = jnp.where(low, zero, in_high)
        ones_row = (lax.broadcasted_iota(jnp.int32, (VT_ROWS - HEAD_DIM, tk), 0) == 0).astype(F32)
        for c in range(seq // tk):
            xt = v_ref[0, c * tk:(c + 1) * tk, :].astype(F32).T
            vt = xt[HEAD_DIM:] if head_in_high_lanes else xt[:HEAD_DIM]
            vt1[:, c * tk:(c + 1) * tk] = jnp.concatenate([vt, ones_row], axis=0).astype(BF16)

    odd = (pl.program_id(1) % 2) == 1
    pl.when(jnp.logical_not(odd))(lambda: fill(False))
    pl.when(odd)(lambda: fill(True))

    cols = 2 * tq
    n_chunks = seq // tk
    n_tiles = seq // tq

    def scores(tile, c, slot):
        q = q_ref[0, pl.ds(pl.multiple_of(tile * tq, tq), tq), :]
        q2 = jnp.concatenate([q[:, :LANES], q[:, LANES:]], axis=0)
        cms = []
        for par, k_s in enumerate((klo, khi)):
            s = _nt_dot(k_s[c * tk:(c + 1) * tk, :], q2)
            s_buf[slot, par] = s
            cms.append(jnp.max(s, axis=0, keepdims=True))
        return tuple(cms)

    def accumulate(c, slot, cms, state):
        vt = vt1[:, c * tk:(c + 1) * tk]
        new = []
        for par in range(2):
            m, acc = state[2 * par], state[2 * par + 1]
            mn = jnp.maximum(m, cms[par])
            p = jnp.exp2(s_buf[slot, par] - mn)
            new += [mn, acc * jnp.exp2(m - mn) + jnp.dot(vt, p.astype(BF16), preferred_element_type=F32)]
        return tuple(new)

    neg = jnp.full((1, cols), -jnp.inf, F32)
    acc0 = jnp.zeros((VT_ROWS, cols), F32)

    def step(j, cms):
        for t in range(tiles_per_step):
            tile = j * tiles_per_step + t
            state = (neg, acc0, neg, acc0)
            for c in range(n_chunks):
                n = t * n_chunks + c
                nxt_tile = tile if c + 1 < n_chunks else jnp.minimum(tile + 1, n_tiles - 1)
                cms_next = scores(nxt_tile, (c + 1) % n_chunks, (n + 1) % 2)
                state = accumulate(c, n % 2, cms, state)
                cms = cms_next
            out = jnp.concatenate(
                [acc[:HEAD_DIM] / acc[HEAD_DIM:HEAD_DIM + 1] for acc in (state[1], state[3])], axis=0)
            o_ref[0, pl.ds(pl.multiple_of(tile * tq, tq), tq), :] = jnp.concatenate(
                [out[:, :tq].T, out[:, tq:].T], axis=1).astype(o_ref.dtype)
        return cms

    lax.fori_loop(0, n_tiles // tiles_per_step, step, scores(0, 0, 0))


def _attn_a(qkv, *, n_heads, tq=256, tk=512, tiles_per_step=2):
    b, seq, _ = qkv.shape
    kv_heads = n_heads // A_REP
    q_cols = n_heads * HEAD_DIM
    q_w = A_REP * HEAD_DIM
    k_blk0 = q_cols // LANES
    v_blk0 = (q_cols + kv_heads * HEAD_DIM) // LANES
    assert (seq // tk) % 2 == 0 and seq % (tq * tiles_per_step) == 0
    return pl.pallas_call(
        functools.partial(_attn_a_body, tq=tq, tk=tk, seq=seq, tiles_per_step=tiles_per_step),
        grid=(b, kv_heads),
        in_specs=[pl.BlockSpec((1, seq, q_w), lambda bi, h: (bi, 0, h)),
                  pl.BlockSpec((1, seq, LANES), lambda bi, h: (bi, 0, k_blk0 + h // 2)),
                  pl.BlockSpec((1, seq, LANES), lambda bi, h: (bi, 0, v_blk0 + h // 2))],
        out_specs=pl.BlockSpec((1, seq, q_w), lambda bi, h: (bi, 0, h)),
        out_shape=jax.ShapeDtypeStruct((b, seq, q_cols), BF16),
        scratch_shapes=[pltpu.VMEM((seq, LANES), BF16), pltpu.VMEM((seq, LANES), BF16),
                        pltpu.VMEM((VT_ROWS, seq), BF16), pltpu.VMEM((2, 2, tk, 2 * tq), F32)],
        compiler_params=_params("parallel", "arbitrary"),
        name="attn_a",
    )(qkv, qkv, qkv)


def _fill_band_bias(bias_ref, slopes, half_window, dist_scale):
    span = bias_ref.shape[2]
    c = lax.broadcasted_iota(jnp.int32, (span, BLOCK_Q), 0)
    r = lax.broadcasted_iota(jnp.int32, (span, BLOCK_Q), 1)
    dist = jnp.abs(c - half_window - r)
    inside = dist <= half_window
    penalty = dist.astype(F32) * (dist_scale * LOG2E)
    for variant, ok in enumerate((inside & (c >= half_window), inside, inside & (c < half_window + BLOCK_Q))):
        for h, slope in enumerate(slopes):
            bias_ref[variant, h] = jnp.where(ok, -slope * penalty, NEG_BIG)


def _band_pipeline(jobs, bias_ref, s_buf):
    def scores(job, slot):
        kmat, q_rows, head_ids, _, _, variant, sink_row = job
        width = len(head_ids) * BLOCK_Q
        s = _nt_dot(kmat, q_rows) + jnp.concatenate([bias_ref[variant, h] for h in head_ids], axis=1)
        s_buf[slot, :, :width] = s
        m = jnp.max(s, axis=0, keepdims=True)
        return m if sink_row is None else jnp.maximum(m, sink_row)

    def values(job, slot, m):
        _, _, head_ids, kv_ids, vt1s, _, sink_row = job
        n = len(head_ids)
        p = jnp.exp2(s_buf[slot, :, :n * BLOCK_Q] - m).astype(BF16)
        accs, start = [], 0
        while start < n:
            stop = start
            while stop < n and kv_ids[stop] == kv_ids[start]:
                stop += 1
            accs.append(jnp.dot(vt1s[kv_ids[start]], p[:, start * BLOCK_Q:stop * BLOCK_Q],
                                preferred_element_type=F32))
            start = stop
        acc = accs[0] if len(accs) == 1 else jnp.concatenate(accs, axis=1)
        den = acc[HEAD_DIM:HEAD_DIM + 1]
        if sink_row is not None:
            den = den + jnp.exp2(sink_row - m)
        return acc[:HEAD_DIM] / den, m, den

    results = []
    n_slots = s_buf.shape[0]
    ahead = n_slots - 1
    maxima = [scores(jobs[c], c % n_slots) for c in range(min(ahead, len(jobs)))]
    for c, job in enumerate(jobs):
        if c + ahead < len(jobs):
            maxima.append(scores(jobs[c + ahead], (c + ahead) % n_slots))
        results.append(values(job, c % n_slots, maxima[c]))
    return results


def _band_operands(k_refs, v_refs):
    k = jnp.concatenate([r[0] for r in k_refs], axis=0)
    vt = jnp.concatenate([r[0] for r in v_refs], axis=0).astype(F32).T
    return k, vt


def _vt_with_ones(vt, kv, start, span):
    ones_row = (lax.broadcasted_iota(jnp.int32, (VT_ROWS - HEAD_DIM, span), 0) == 0).astype(F32)
    rows = vt[kv * HEAD_DIM:(kv + 1) * HEAD_DIM, start:start + span]
    return jnp.concatenate([rows, ones_row], axis=0).astype(BF16)


def _swap_halves(x):
    return jnp.concatenate([x[:, HEAD_DIM:], x[:, :HEAD_DIM]], axis=1)


def _attn_c_body(sink_ref, q_ref, kp_ref, kc_ref, kn_ref, vp_ref, vc_ref, vn_ref, o_ref, bias_ref, s_buf,
                 *, n_heads, slopes):
    i = pl.program_id(1)
    pl.when(i == 0)(lambda: _fill_band_bias(bias_ref, slopes, C_WINDOW, 1.0))
    block_variants = (jnp.where(i == 0, 0, 1), jnp.where(i == pl.num_programs(1) - 1, 2, 1))
    q = q_ref[0]
    k, vt = _band_operands((kp_ref, kc_ref, kn_ref), (vp_ref, vc_ref, vn_ref))
    span = BLOCK_Q + 2 * C_WINDOW
    low = lax.broadcasted_iota(jnp.int32, (BLOCK_Q, LANES), 1) < HEAD_DIM
    zero = jnp.zeros((BLOCK_Q, LANES), BF16)
    first_block = lax.broadcasted_iota(jnp.int32, (1, 2 * BLOCK_Q), 1) < BLOCK_Q
    jobs, job_block = [], []
    for blk in range(C_TILE // BLOCK_Q):
        q_lo = blk * BLOCK_Q
        for pair in range(n_heads // A_REP // 2):
            ks = k[q_lo:q_lo + span, pair * LANES:(pair + 1) * LANES]
            ks_swapped = _swap_halves(ks)
            for which in range(2):
                kv = 2 * pair + which
                vt1s = {kv: _vt_with_ones(vt, kv, q_lo, span)}
                slabs = [q[q_lo:q_lo + BLOCK_Q, (2 * kv + j) * LANES:(2 * kv + j + 1) * LANES] for j in range(2)]
                for parity in range(2):
                    q_rows = jnp.concatenate(
                        [jnp.where(low, s_, zero) if parity == 0 else jnp.where(low, zero, s_) for s_ in slabs],
                        axis=0)
                    heads = (A_REP * kv + parity, A_REP * kv + 2 + parity)
                    sink_row = jnp.where(first_block, sink_ref[heads[0]], sink_ref[heads[1]]) * LOG2E
                    jobs.append((ks if parity == which else ks_swapped, q_rows, heads, (kv, kv), vt1s,
                                 block_variants[blk], sink_row))
                    job_block.append(blk)
    o_t = {}
    for job, blk, (ot, _, _) in zip(jobs, job_block, _band_pipeline(jobs, bias_ref, s_buf)):
        o_t[blk, job[2][0]], o_t[blk, job[2][1]] = ot[:, :BLOCK_Q], ot[:, BLOCK_Q:]
    for blk in range(C_TILE // BLOCK_Q):
        for j in range(n_heads // 2):
            pair_t = jnp.concatenate([o_t[blk, 2 * j], o_t[blk, 2 * j + 1]], axis=0)
            o_ref[0, blk * BLOCK_Q:(blk + 1) * BLOCK_Q, j * LANES:(j + 1) * LANES] = pair_t.T.astype(o_ref.dtype)


def _alibi_slopes(n):
    return [2.0 ** (-8.0 * (i + 1) / n) for i in range(n)]


def _attn_c(qkv, sinks, *, n_heads):
    b, seq, _ = qkv.shape
    kv_heads = n_heads // A_REP
    q_cols = n_heads * HEAD_DIM
    kv_cols = kv_heads * HEAD_DIM
    n_tiles = seq // C_TILE
    n_edges = seq // C_WINDOW
    edges_per_tile = C_TILE // C_WINDOW
    k_blk = q_cols // kv_cols
    v_blk = k_blk + 1
    span = BLOCK_Q + 2 * C_WINDOW

    def kv_specs(col_blk):
        edge = lambda offset: pl.BlockSpec(
            (1, C_WINDOW, kv_cols),
            lambda bi, i: (bi, jnp.clip(i * edges_per_tile + offset, 0, n_edges - 1), col_blk))
        return [edge(-1), pl.BlockSpec((1, C_TILE, kv_cols), lambda bi, i: (bi, i, col_blk)),
                edge(edges_per_tile)]

    return pl.pallas_call(
        functools.partial(_attn_c_body, n_heads=n_heads, slopes=_alibi_slopes(n_heads)),
        grid=(b, n_tiles),
        in_specs=[pl.BlockSpec(memory_space=pltpu.SMEM),
                  pl.BlockSpec((1, C_TILE, q_cols), lambda bi, i: (bi, i, 0))] + kv_specs(k_blk) + kv_specs(v_blk),
        out_specs=pl.BlockSpec((1, C_TILE, q_cols), lambda bi, i: (bi, i, 0)),
        out_shape=jax.ShapeDtypeStruct((b, seq, q_cols), BF16),
        scratch_shapes=[pltpu.VMEM((3, n_heads, span, BLOCK_Q), F32),
                        pltpu.VMEM((BAND_SLOTS, span, 2 * BLOCK_Q), F32)],
        compiler_params=_params("parallel", "arbitrary"),
        name="attn_c",
    )(sinks, qkv, qkv, qkv, qkv, qkv, qkv, qkv)


def _attn_b_body(q0_ref, q1_ref, q2_ref, kp_ref, kc_ref, kn_ref, vp_ref, vc_ref, vn_ref, o_ref, lse_ref,
                 bias_ref, s_buf, *, slopes, dil):
    i = pl.program_id(2)
    pl.when(jnp.logical_and(pl.program_id(1) == 0, i == 0))(
        lambda: _fill_band_bias(bias_ref, slopes, B_HALF, float(dil)))
    block_variants = (jnp.where(i == 0, 0, 1), jnp.where(i == pl.num_programs(2) - 1, 2, 1))
    k, vt = _band_operands((kp_ref, kc_ref, kn_ref), (vp_ref, vc_ref, vn_ref))
    low = lax.broadcasted_iota(jnp.int32, (BLOCK_Q, LANES), 1) < HEAD_DIM
    zero = jnp.zeros((BLOCK_Q, LANES), BF16)
    slabs = [q0_ref[0], q1_ref[0], q2_ref[0]]
    heads_all = range(B_HEADS_PER_GROUP)
    plain = [h for h in heads_all if h % 2 == h // B_REP]
    crossed = [h for h in heads_all if h % 2 != h // B_REP]
    jobs, job_block = [], []
    for blk in range(B_TILE // BLOCK_Q):
        q_lo = blk * BLOCK_Q
        ks = k[q_lo:q_lo + B_SPAN]
        ks_swapped = _swap_halves(ks)
        vt1s = {kv: _vt_with_ones(vt, kv, q_lo, B_SPAN) for kv in range(B_KV_PER_GROUP)}
        for kmat, heads in ((ks, plain[:2]), (ks_swapped, crossed), (ks, plain[2:])):
            q_rows = jnp.concatenate(
                [jnp.where(low, slabs[h // 2][q_lo:q_lo + BLOCK_Q], zero) if h % 2 == 0
                 else jnp.where(low, zero, slabs[h // 2][q_lo:q_lo + BLOCK_Q]) for h in heads], axis=0)
            jobs.append((kmat, q_rows, heads, [h // B_REP for h in heads], vt1s, block_variants[blk], None))
            job_block.append(blk)
    o_t, lse_t = {}, {}
    for job, blk, (ot, m, den) in zip(jobs, job_block, _band_pipeline(jobs, bias_ref, s_buf)):
        lse = (m + jnp.log2(den)) * LN2
        for n, h in enumerate(job[2]):
            o_t[blk, h] = ot[:, n * BLOCK_Q:(n + 1) * BLOCK_Q]
            lse_t[blk, h] = jnp.broadcast_to(lse[:, n * BLOCK_Q:(n + 1) * BLOCK_Q], (HEAD_DIM, BLOCK_Q))
    for blk in range(B_TILE // BLOCK_Q):
        rows = slice(blk * BLOCK_Q, (blk + 1) * BLOCK_Q)
        for j in range(B_HEADS_PER_GROUP // 2):
            cols = slice(j * LANES, (j + 1) * LANES)
            pair = lambda t: jnp.concatenate([t[blk, 2 * j], t[blk, 2 * j + 1]], axis=0).T
            o_ref[0, rows, cols] = pair(o_t).astype(o_ref.dtype)
            lse_ref[0, rows, cols] = pair(lse_t)


def _attn_b(group_qkv, b):
    n_g = len(B_GROUPS)
    gq = B_HEADS_PER_GROUP * HEAD_DIM
    slabs_per_token = B_GROUP_COLS // LANES
    q_slabs = gq // LANES
    halves_per_tile = B_TILE // B_HALF
    all_slopes = _alibi_slopes(n_g * B_HEADS_PER_GROUP)
    outs, lses = [], []
    for g, (window, dil) in enumerate(B_GROUPS):
        assert (window // 2) // dil == B_HALF
        sub = group_qkv[g].shape[0] // b
        n_tiles = sub // B_TILE
        n_halves = sub // B_HALF
        view = group_qkv[g].reshape(b, sub, dil * B_GROUP_COLS)

        def tile_spec(slab):
            return pl.BlockSpec((1, B_TILE, LANES), lambda bi, r, i: (bi, i, r * slabs_per_token + slab))

        def edge_spec(slab, offset, n_halves=n_halves):
            return pl.BlockSpec(
                (1, B_HALF, LANES),
                lambda bi, r, i: (bi, jnp.clip(i * halves_per_tile + offset, 0, n_halves - 1),
                                  r * slabs_per_token + slab))

        def kv_specs(slab):
            return [edge_spec(slab, -1), tile_spec(slab), edge_spec(slab, halves_per_tile)]

        out_map = lambda bi, r, i: (bi, i, r)
        o, lse = pl.pallas_call(
            functools.partial(_attn_b_body, dil=dil,
                              slopes=all_slopes[g * B_HEADS_PER_GROUP:(g + 1) * B_HEADS_PER_GROUP]),
            grid=(b, dil, n_tiles),
            in_specs=[tile_spec(j) for j in range(q_slabs)] + kv_specs(q_slabs) + kv_specs(q_slabs + 1),
            out_specs=[pl.BlockSpec((1, B_TILE, gq), out_map), pl.BlockSpec((1, B_TILE, gq), out_map)],
            out_shape=[jax.ShapeDtypeStruct((b, sub, dil * gq), BF16),
                       jax.ShapeDtypeStruct((b, sub, dil * gq), F32)],
            scratch_shapes=[pltpu.VMEM((3, B_HEADS_PER_GROUP, B_SPAN, BLOCK_Q), F32),
                            pltpu.VMEM((BAND_SLOTS, B_SPAN, 2 * BLOCK_Q), F32)],
            compiler_params=_params("parallel", "arbitrary", "arbitrary"),
            name=f"attn_b_g{g}",
        )(*([view] * (q_slabs + 6)))
        outs.append(o.reshape(b * sub, dil * gq))
        lses.append(lse.reshape(b * sub, dil * gq))
    return outs, lses


PROLOGUE_ROW_CHUNKS = 4


def _mix_plain(refs, scratch):
    return lambda rows: refs[0][rows, :]


def _mix_groups(refs, scratch):
    n_g = len(B_GROUPS)
    o_scr, l_scr = scratch
    slabs = o_scr.shape[0] // n_g
    tm = o_scr.shape[1]
    gq = slabs * LANES
    for g, (_, dil) in enumerate(B_GROUPS):
        for r in range(dil):
            rows = pl.ds(r, tm // dil, stride=dil)
            for s in range(slabs):
                cols = slice(r * gq + s * LANES, r * gq + (s + 1) * LANES)
                o_scr[g * slabs + s, rows, :] = refs[g][:, cols].astype(F32)
                l_scr[g * slabs + s, rows, :] = refs[n_g + g][:, cols]
    def lhs(rows):
        group = lambda scr, g: jnp.concatenate([scr[g * slabs + s, rows, :] for s in range(slabs)], axis=1)
        ls = [group(l_scr, g) for g in range(n_g)]
        mx = functools.reduce(jnp.maximum, ls)
        es = [jnp.exp(l - mx) for l in ls]
        inv = 1.0 / functools.reduce(lambda a, b: a + b, es)
        return jnp.concatenate([(group(o_scr, g) * (es[g] * inv)).astype(BF16) for g in range(n_g)], axis=1)

    return lhs


MLP_HIDDEN_CHUNK = 1024


def _block_body(*refs, n_mix, mix_fn, final_norm):
    h_ref, mix_refs = refs[0], refs[1:1 + n_mix]
    wo_ref, g_ref, w1_ref, w2_ref, fg_ref, out_ref, hn_ref = refs[1 + n_mix:8 + n_mix]
    mix_scratch = refs[8 + n_mix:]

    lhs = mix_fn(mix_refs, mix_scratch)
    chunk = h_ref.shape[0] // PROLOGUE_ROW_CHUNKS
    rows = [slice(c * chunk, (c + 1) * chunk) for c in range(PROLOGUE_ROW_CHUNKS)]
    project = lambda c: jnp.dot(lhs(rows[c]), wo_ref[...], preferred_element_type=F32)
    nxt = project(0)
    for c in range(PROLOGUE_ROW_CHUNKS):
        cur = nxt
        if c + 1 < PROLOGUE_ROW_CHUNKS:
            nxt = project(c + 1)
        h1 = h_ref[rows[c], :] + cur
        out_ref[rows[c], :] = h1
        hn_ref[rows[c], :] = _rms_rows(h1, g_ref[...]).astype(BF16)

    hn = hn_ref[...]
    for c in range(w1_ref.shape[1] // MLP_HIDDEN_CHUNK):
        cols = slice(c * MLP_HIDDEN_CHUNK, (c + 1) * MLP_HIDDEN_CHUNK)
        u = jnp.maximum(jnp.dot(hn, w1_ref[:, cols], preferred_element_type=F32), 0.0)
        out_ref[...] += jnp.dot((u * u).astype(BF16), w2_ref[cols, :], preferred_element_type=F32)

    if final_norm:
        out_ref[...] = _rms_rows(out_ref[...], fg_ref[...])


def _block(h2, mix_inputs, w_o, *, gain, w1, w2, layer, final_gain, final_norm, tm=512):
    t, d = h2.shape
    row_tile = lambda cols: pl.BlockSpec((tm, cols), lambda i: (i, 0))
    whole = lambda a: pl.BlockSpec(a.shape, lambda i: (0, 0), pipeline_mode=pl.Buffered(1))
    of_layer = lambda a: pl.BlockSpec((None,) + a.shape[1:], lambda i: (layer, 0, 0),
                                      pipeline_mode=pl.Buffered(1))
    scratch = [pltpu.VMEM((tm, d), BF16)]
    if len(mix_inputs) == 1:
        mix_fn, mix_specs = _mix_plain, [row_tile(mix_inputs[0].shape[1])]
    else:
        n_g = len(B_GROUPS)
        gq = B_HEADS_PER_GROUP * HEAD_DIM
        mix_fn = _mix_groups
        mix_specs = [pl.BlockSpec((tm // dil, dil * gq), lambda i: (i, 0)) for _, dil in B_GROUPS] * 2
        scratch += [pltpu.VMEM((n_g * gq // LANES, tm, LANES), F32)] * 2
    return pl.pallas_call(
        functools.partial(_block_body, n_mix=len(mix_inputs), mix_fn=mix_fn, final_norm=final_norm),
        grid=(t // tm,),
        in_specs=[row_tile(d)] + mix_specs
                 + [whole(w_o), whole(gain), of_layer(w1), of_layer(w2), whole(final_gain)],
        out_specs=row_tile(d),
        out_shape=jax.ShapeDtypeStruct((t, d), F32),
        scratch_shapes=scratch,
        compiler_params=_params("parallel"),
        name="block_mlp",
    )(h2, *mix_inputs, w_o, gain, w1, w2, final_gain)


def kernel(x, attn_norm, mlp_norm, a_w_qkv, a_q_gain, a_k_gain, a_w_o, b_w_qkv, b_w_o,
           c_w_qkv, c_sinks, c_w_o, mlp_w1, mlp_w2, final_norm):
    b, seq, d = x.shape
    depth = attn_norm.shape[0]
    h = x.reshape(b * seq, d)
    scale = HEAD_DIM ** -0.5 * LOG2E
    a_heads = a_w_o.shape[1] // HEAD_DIM
    c_heads = c_w_o.shape[1] // HEAD_DIM
    cos, sin = _rope_tables(seq)
    head = jnp.arange(2 * LANES) // HEAD_DIM
    seg = (head[:, None] == head[None, :]).astype(BF16)
    tile4 = lambda g: jnp.tile(g, 2 * LANES // HEAD_DIM)[None, :]
    w1_all, w2_all = mlp_w1.astype(BF16), mlp_w2.astype(BF16)
    used = [0, 0, 0]
    for layer in range(depth):
        kind = layer % N_MIXERS
        j = used[kind]
        used[kind] += 1
        gain = attn_norm[layer][None, :]
        block = functools.partial(_block, gain=mlp_norm[layer][None, :], w1=w1_all, w2=w2_all, layer=layer,
                                  final_gain=final_norm[None, :], final_norm=(layer == depth - 1))
        if kind == 0:
            q_cols = a_heads * HEAD_DIM
            k_cols = (a_w_qkv.shape[2] - q_cols) // 2
            qkv = _qkv_proj(h, gain, a_w_qkv[j].astype(BF16), q_cols=q_cols, q_scale=scale,
                            rope=(tile4(a_q_gain[j]), tile4(a_k_gain[j]), cos, sin, seg, k_cols))
            o = _attn_a(qkv.reshape(b, seq, -1), n_heads=a_heads).reshape(b * seq, q_cols)
            h = block(h, [o], a_w_o[j].astype(BF16))
        elif kind == 1:
            q_cols = len(B_GROUPS) * B_HEADS_PER_GROUP * HEAD_DIM
            groups = _qkv_proj_b(h, gain, b_w_qkv[j].astype(BF16), q_cols=q_cols, q_scale=scale)
            outs, lses = _attn_b(groups, b)
            h = block(h, outs + lses, b_w_o[j].astype(BF16))
        else:
            q_cols = c_heads * HEAD_DIM
            qkv = _qkv_proj(h, gain, c_w_qkv[j].astype(BF16), q_cols=q_cols, q_scale=scale)
            o = _attn_c(qkv.reshape(b, seq, -1), c_sinks[j], n_heads=c_heads).reshape(b * seq, q_cols)
            h = block(h, [o], c_w_o[j].astype(BF16))
    return h.reshape(b, seq, d)
```

```python
import functools
import math

import jax
import jax.numpy as jnp
from jax import lax
from jax.experimental import pallas as pl
from jax.experimental.pallas import tpu as pltpu

HEAD_DIM = 64
HALF_HEAD = HEAD_DIM // 2
RMS_EPS = 1e-6
GRID_W = 64
ROPE_THETA = 10000.0
N_MIXERS = 3
A_REP = 4
B_GROUPS = ((128, 1), (512, 4), (2048, 16))
B_HEADS_PER_GROUP = 6
B_KV_PER_GROUP = 2
B_REP = B_HEADS_PER_GROUP // B_KV_PER_GROUP
C_WINDOW = 128
LANES = 128
BF16_SUBLANES = 16
BLOCK_Q = 128
BAND_SLOTS = 4
C_TILE = 2 * BLOCK_Q
B_HALF = BLOCK_Q // 2
B_TILE = 4 * BLOCK_Q
B_SPAN = BLOCK_Q + 2 * B_HALF
B_GROUP_COLS = (B_HEADS_PER_GROUP + 2 * B_KV_PER_GROUP) * HEAD_DIM
VT_ROWS = HEAD_DIM + BF16_SUBLANES
V7X_VMEM_LIMIT_BYTES = 48 * 1024 * 1024
NEG_BIG = -1e30
LOG2E = math.log2(math.e)
LN2 = math.log(2.0)

BF16 = jnp.bfloat16
F32 = jnp.float32


def _params(*sem):
    return pltpu.CompilerParams(dimension_semantics=sem, vmem_limit_bytes=V7X_VMEM_LIMIT_BYTES)


def _nt_dot(a, b):
    return lax.dot_general(a, b, (((1,), (1,)), ((), ())), preferred_element_type=F32)


def _rms_rows(x, gain):
    ms = jnp.mean(x * x, axis=-1, keepdims=True)
    return x * lax.rsqrt(ms + RMS_EPS) * gain


def _qkv_body(x_ref, g_ref, w_ref, o_ref, *, q_cols, q_scale):
    hn = _rms_rows(x_ref[...], g_ref[...]).astype(BF16)
    y = jnp.dot(hn, w_ref[...], preferred_element_type=F32)
    o_ref[:, :q_cols] = (y[:, :q_cols] * q_scale).astype(o_ref.dtype)
    o_ref[:, q_cols:] = y[:, q_cols:].astype(o_ref.dtype)


def _qkv_rope_body(x_ref, g_ref, w_ref, qg_ref, kg_ref, cos_ref, sin_ref, seg_ref, o_ref,
                   *, q_cols, k_cols, q_scale):
    hn = _rms_rows(x_ref[...], g_ref[...]).astype(BF16)
    slab = 2 * LANES
    cos = jnp.concatenate([cos_ref[...], cos_ref[...]], axis=1)
    sin = jnp.concatenate([sin_ref[...], sin_ref[...]], axis=1)
    lane = lax.broadcasted_iota(jnp.int32, (x_ref.shape[0], slab), 1)
    first_half = (lane % HALF_HEAD) < (HALF_HEAD // 2)
    seg = seg_ref[...]
    n_qk = (q_cols + k_cols) // slab
    project = lambda s: jnp.dot(hn, w_ref[:, s * slab:(s + 1) * slab], preferred_element_type=F32)
    y_next = project(0)
    for s in range(n_qk):
        is_q = s * slab < q_cols
        ys = y_next
        y_next = project(s + 1) if s + 1 < n_qk else jnp.dot(
            hn, w_ref[:, q_cols + k_cols:], preferred_element_type=F32)
        sq = ys * ys
        hi = sq.astype(BF16)
        lo = (sq - hi.astype(F32)).astype(BF16)
        ss = jnp.dot(hi, seg, preferred_element_type=F32) + jnp.dot(lo, seg, preferred_element_type=F32)
        gain = qg_ref[...] if is_q else kg_ref[...]
        yn = ys * lax.rsqrt(ss * (1.0 / HEAD_DIM) + RMS_EPS) * gain
        partner = jnp.where(first_half,
                            pltpu.roll(yn, slab - HALF_HEAD // 2, axis=1),
                            pltpu.roll(yn, HALF_HEAD // 2, axis=1))
        r = yn * cos + partner * sin
        if is_q:
            r = r * q_scale
        o_ref[:, s * slab:(s + 1) * slab] = r.astype(o_ref.dtype)
    o_ref[:, q_cols + k_cols:] = y_next.astype(o_ref.dtype)


def _qkv_proj(h2, gain, w, *, q_cols, q_scale, tm=1024, rope=None):
    t, d = h2.shape
    n = w.shape[1]
    common = dict(
        grid=(t // tm,),
        out_specs=pl.BlockSpec((tm, n), lambda i: (i, 0)),
        out_shape=jax.ShapeDtypeStruct((t, n), BF16),
        compiler_params=_params("parallel"),
    )
    x_spec = pl.BlockSpec((tm, d), lambda i: (i, 0))
    g_spec = pl.BlockSpec((1, d), lambda i: (0, 0))
    w_spec = pl.BlockSpec((d, n), lambda i: (0, 0))
    if rope is None:
        return pl.pallas_call(
            functools.partial(_qkv_body, q_cols=q_cols, q_scale=q_scale),
            in_specs=[x_spec, g_spec, w_spec], name="qkv_proj", **common,
        )(h2, gain, w)
    qg, kg, cos, sin, seg, k_cols = rope
    seq_tiles = cos.shape[0] // tm
    slab = 2 * LANES
    return pl.pallas_call(
        functools.partial(_qkv_rope_body, q_cols=q_cols, k_cols=k_cols, q_scale=q_scale),
        in_specs=[x_spec, g_spec, w_spec,
                  pl.BlockSpec((1, slab), lambda i: (0, 0)),
                  pl.BlockSpec((1, slab), lambda i: (0, 0)),
                  pl.BlockSpec((tm, LANES), lambda i: (i % seq_tiles, 0)),
                  pl.BlockSpec((tm, LANES), lambda i: (i % seq_tiles, 0)),
                  pl.BlockSpec((slab, slab), lambda i: (0, 0))],
        name="qkv_proj_rope", **common,
    )(h2, gain, w, qg, kg, cos, sin, seg)


def _qkv_b_body(x_ref, g_ref, w_ref, *rest, q_cols, q_scale):
    out_refs, y_scr = rest[:-1], rest[-1]
    tm = x_ref.shape[0]
    hn = _rms_rows(x_ref[...], g_ref[...]).astype(BF16)
    y = jnp.dot(hn, w_ref[...], preferred_element_type=F32)
    for s in range(y_scr.shape[0]):
        ys = y[:, s * LANES:(s + 1) * LANES]
        y_scr[s] = ys * q_scale if s * LANES < q_cols else ys
    n_g = len(B_GROUPS)
    q_slabs = B_HEADS_PER_GROUP * HEAD_DIM // LANES
    k0 = q_cols // LANES
    for g, (_, dil) in enumerate(B_GROUPS):
        slabs = list(range(g * q_slabs, (g + 1) * q_slabs)) + [k0 + g, k0 + n_g + g]
        for r in range(dil):
            rows = pl.ds(r, tm // dil, stride=dil)
            piece = jnp.concatenate([y_scr[s, rows, :] for s in slabs], axis=1)
            out_refs[g][:, r * B_GROUP_COLS:(r + 1) * B_GROUP_COLS] = piece.astype(BF16)


def _qkv_proj_b(h2, gain, w, *, q_cols, q_scale, tm=1024):
    t, d = h2.shape
    n = w.shape[1]
    return pl.pallas_call(
        functools.partial(_qkv_b_body, q_cols=q_cols, q_scale=q_scale),
        grid=(t // tm,),
        in_specs=[pl.BlockSpec((tm, d), lambda i: (i, 0)),
                  pl.BlockSpec((1, d), lambda i: (0, 0)),
                  pl.BlockSpec((d, n), lambda i: (0, 0))],
        out_specs=[pl.BlockSpec((tm // dil, dil * B_GROUP_COLS), lambda i: (i, 0)) for _, dil in B_GROUPS],
        out_shape=[jax.ShapeDtypeStruct((t // dil, dil * B_GROUP_COLS), BF16) for _, dil in B_GROUPS],
        scratch_shapes=[pltpu.VMEM((n // LANES, tm, LANES), F32)],
        compiler_params=_params("parallel"),
        name="qkv_proj_b",
    )(h2, gain, w)


def _rope_tables(seq):
    inv_freq = ROPE_THETA ** (-jnp.arange(0, HALF_HEAD, 2, dtype=F32) / HALF_HEAD)
    rows = seq // GRID_W
    ang_row = jnp.arange(rows, dtype=F32)[:, None] * inv_freq
    ang_col = jnp.arange(GRID_W, dtype=F32)[:, None] * inv_freq
    per_row = lambda x: jnp.repeat(x, GRID_W, axis=0)
    per_col = lambda x: jnp.tile(x, (rows, 1))
    cos = jnp.concatenate([per_row(jnp.cos(ang_row))] * 2 + [per_col(jnp.cos(ang_col))] * 2, axis=1)
    sin_r, sin_c = per_row(jnp.sin(ang_row)), per_col(jnp.sin(ang_col))
    sin = jnp.concatenate([-sin_r, sin_r, -sin_c, sin_c], axis=1)
    return jnp.concatenate([cos, cos], axis=1), jnp.concatenate([sin, sin], axis=1)


def _attn_a_body(q_ref, k_ref, v_ref, o_ref, klo, khi, vt1, s_buf, *, tq, tk, seq, tiles_per_step):
    def fill(head_in_high_lanes):
        low = lax.broadcasted_iota(jnp.int32, (seq, LANES), 1) < HEAD_DIM
        zero = jnp.zeros((seq, LANES), BF16)
        x = k_ref[0]
        swapped = jnp.concatenate([x[:, HEAD_DIM:], x[:, :HEAD_DIM]], axis=1)
        in_low, in_high = (swapped, x) if head_in_high_lanes else (x, swapped)
        klo[...] = jnp.where(low, in_low, zero)
        khi[...] = jnp.where(low, zero, in_high)
        ones_row = (lax.broadcasted_iota(jnp.int32, (VT_ROWS - HEAD_DIM, tk), 0) == 0).astype(F32)
        for c in range(seq // tk):
            xt = v_ref[0, c * tk:(c + 1) * tk, :].astype(F32).T
            vt = xt[HEAD_DIM:] if head_in_high_lanes else xt[:HEAD_DIM]
            vt1[:, c * tk:(c + 1) * tk] = jnp.concatenate([vt, ones_row], axis=0).astype(BF16)

    odd = (pl.program_id(1) % 2) == 1
    pl.when(jnp.logical_not(odd))(lambda: fill(False))
    pl.when(odd)(lambda: fill(True))

    cols = 2 * tq
    n_chunks = seq // tk
    n_tiles = seq // tq

    def scores(tile, c, slot):
        q = q_ref[0, pl.ds(pl.multiple_of(tile * tq, tq), tq), :]
        q2 = jnp.concatenate([q[:, :LANES], q[:, LANES:]], axis=0)
        cms = []
        for par, k_s in enumerate((klo, khi)):
            s = _nt_dot(k_s[c * tk:(c + 1) * tk, :], q2)
            s_buf[slot, par] = s
            cms.append(jnp.max(s, axis=0, keepdims=True))
        return tuple(cms)

    def accumulate(c, slot, cms, state):
        vt = vt1[:, c * tk:(c + 1) * tk]
        new = []
        for par in range(2):
            m, acc = state[2 * par], state[2 * par + 1]
            mn = jnp.maximum(m, cms[par])
            p = jnp.exp2(s_buf[slot, par] - mn)
            new += [mn, acc * jnp.exp2(m - mn) + jnp.dot(vt, p.astype(BF16), preferred_element_type=F32)]
        return tuple(new)

    neg = jnp.full((1, cols), -jnp.inf, F32)
    acc0 = jnp.zeros((VT_ROWS, cols), F32)

    def step(j, cms):
        for t in range(tiles_per_step):
            tile = j * tiles_per_step + t
            state = (neg, acc0, neg, acc0)
            for c in range(n_chunks):
                n = t * n_chunks + c
                nxt_tile = tile if c + 1 < n_chunks else jnp.minimum(tile + 1, n_tiles - 1)
                cms_next = scores(nxt_tile, (c + 1) % n_chunks, (n + 1) % 2)
                state = accumulate(c, n % 2, cms, state)
                cms = cms_next
            out = jnp.concatenate(
                [acc[:HEAD_DIM] / acc[HEAD_DIM:HEAD_DIM + 1] for acc in (state[1], state[3])], axis=0)
            o_ref[0, pl.ds(pl.multiple_of(tile * tq, tq), tq), :] = jnp.concatenate(
                [out[:, :tq].T, out[:, tq:].T], axis=1).astype(o_ref.dtype)
        return cms

    lax.fori_loop(0, n_tiles // tiles_per_step, step, scores(0, 0, 0))


def _attn_a(qkv, *, n_heads, tq=256, tk=512, tiles_per_step=2):
    b, seq, _ = qkv.shape
    kv_heads = n_heads // A_REP
    q_cols = n_heads * HEAD_DIM
    q_w = A_REP * HEAD_DIM
    k_blk0 = q_cols // LANES
    v_blk0 = (q_cols + kv_heads * HEAD_DIM) // LANES
    assert (seq // tk) % 2 == 0 and seq % (tq * tiles_per_step) == 0
    return pl.pallas_call(
        functools.partial(_attn_a_body, tq=tq, tk=tk, seq=seq, tiles_per_step=tiles_per_step),
        grid=(b, kv_heads),
        in_specs=[pl.BlockSpec((1, seq, q_w), lambda bi, h: (bi, 0, h)),
                  pl.BlockSpec((1, seq, LANES), lambda bi, h: (bi, 0, k_blk0 + h // 2)),
                  pl.BlockSpec((1, seq, LANES), lambda bi, h: (bi, 0, v_blk0 + h // 2))],
        out_specs=pl.BlockSpec((1, seq, q_w), lambda bi, h: (bi, 0, h)),
        out_shape=jax.ShapeDtypeStruct((b, seq, q_cols), BF16),
        scratch_shapes=[pltpu.VMEM((seq, LANES), BF16), pltpu.VMEM((seq, LANES), BF16),
                        pltpu.VMEM((VT_ROWS, seq), BF16), pltpu.VMEM((2, 2, tk, 2 * tq), F32)],
        compiler_params=_params("parallel", "arbitrary"),
        name="attn_a",
    )(qkv, qkv, qkv)


def _fill_band_bias(bias_ref, slopes, half_window, dist_scale):
    span = bias_ref.shape[2]
    c = lax.broadcasted_iota(jnp.int32, (span, BLOCK_Q), 0)
    r = lax.broadcasted_iota(jnp.int32, (span, BLOCK_Q), 1)
    dist = jnp.abs(c - half_window - r)
    inside = dist <= half_window
    penalty = dist.astype(F32) * (dist_scale * LOG2E)
    for variant, ok in enumerate((inside & (c >= half_window), inside, inside & (c < half_window + BLOCK_Q))):
        for h, slope in enumerate(slopes):
            bias_ref[variant, h] = jnp.where(ok, -slope * penalty, NEG_BIG)


def _band_pipeline(jobs, bias_ref, s_buf):
    def scores(job, slot):
        kmat, q_rows, head_ids, _, _, variant, sink_row = job
        width = len(head_ids) * BLOCK_Q
        s = _nt_dot(kmat, q_rows) + jnp.concatenate([bias_ref[variant, h] for h in head_ids], axis=1)
        s_buf[slot, :, :width] = s
        m = jnp.max(s, axis=0, keepdims=True)
        return m if sink_row is None else jnp.maximum(m, sink_row)

    def values(job, slot, m):
        _, _, head_ids, kv_ids, vt1s, _, sink_row = job
        n = len(head_ids)
        p = jnp.exp2(s_buf[slot, :, :n * BLOCK_Q] - m).astype(BF16)
        accs, start = [], 0
        while start < n:
            stop = start
            while stop < n and kv_ids[stop] == kv_ids[start]:
                stop += 1
            accs.append(jnp.dot(vt1s[kv_ids[start]], p[:, start * BLOCK_Q:stop * BLOCK_Q],
                                preferred_element_type=F32))
            start = stop
        acc = accs[0] if len(accs) == 1 else jnp.concatenate(accs, axis=1)
        den = acc[HEAD_DIM:HEAD_DIM + 1]
        if sink_row is not None:
            den = den + jnp.exp2(sink_row - m)
        return acc[:HEAD_DIM] / den, m, den

    results = []
    n_slots = s_buf.shape[0]
    ahead = n_slots - 1
    maxima = [scores(jobs[c], c % n_slots) for c in range(min(ahead, len(jobs)))]
    for c, job in enumerate(jobs):
        if c + ahead < len(jobs):
            maxima.append(scores(jobs[c + ahead], (c + ahead) % n_slots))
        results.append(values(job, c % n_slots, maxima[c]))
    return results


def _band_operands(k_refs, v_refs):
    k = jnp.concatenate([r[0] for r in k_refs], axis=0)
    vt = jnp.concatenate([r[0] for r in v_refs], axis=0).astype(F32).T
    return k, vt


def _vt_with_ones(vt, kv, start, span):
    ones_row = (lax.broadcasted_iota(jnp.int32, (VT_ROWS - HEAD_DIM, span), 0) == 0).astype(F32)
    rows = vt[kv * HEAD_DIM:(kv + 1) * HEAD_DIM, start:start + span]
    return jnp.concatenate([rows, ones_row], axis=0).astype(BF16)


def _swap_halves(x):
    return jnp.concatenate([x[:, HEAD_DIM:], x[:, :HEAD_DIM]], axis=1)


def _attn_c_body(sink_ref, q_ref, kp_ref, kc_ref, kn_ref, vp_ref, vc_ref, vn_ref, o_ref, bias_ref, s_buf,
                 *, n_heads, slopes):
    i = pl.program_id(1)
    pl.when(i == 0)(lambda: _fill_band_bias(bias_ref, slopes, C_WINDOW, 1.0))
    block_variants = (jnp.where(i == 0, 0, 1), jnp.where(i == pl.num_programs(1) - 1, 2, 1))
    q = q_ref[0]
    k, vt = _band_operands((kp_ref, kc_ref, kn_ref), (vp_ref, vc_ref, vn_ref))
    span = BLOCK_Q + 2 * C_WINDOW
    low = lax.broadcasted_iota(jnp.int32, (BLOCK_Q, LANES), 1) < HEAD_DIM
    zero = jnp.zeros((BLOCK_Q, LANES), BF16)
    first_block = lax.broadcasted_iota(jnp.int32, (1, 2 * BLOCK_Q), 1) < BLOCK_Q
    jobs, job_block = [], []
    for blk in range(C_TILE // BLOCK_Q):
        q_lo = blk * BLOCK_Q
        for pair in range(n_heads // A_REP // 2):
            ks = k[q_lo:q_lo + span, pair * LANES:(pair + 1) * LANES]
            ks_swapped = _swap_halves(ks)
            for which in range(2):
                kv = 2 * pair + which
                vt1s = {kv: _vt_with_ones(vt, kv, q_lo, span)}
                slabs = [q[q_lo:q_lo + BLOCK_Q, (2 * kv + j) * LANES:(2 * kv + j + 1) * LANES] for j in range(2)]
                for parity in range(2):
                    q_rows = jnp.concatenate(
                        [jnp.where(low, s_, zero) if parity == 0 else jnp.where(low, zero, s_) for s_ in slabs],
                        axis=0)
                    heads = (A_REP * kv + parity, A_REP * kv + 2 + parity)
                    sink_row = jnp.where(first_block, sink_ref[heads[0]], sink_ref[heads[1]]) * LOG2E
                    jobs.append((ks if parity == which else ks_swapped, q_rows, heads, (kv, kv), vt1s,
                                 block_variants[blk], sink_row))
                    job_block.append(blk)
    o_t = {}
    for job, blk, (ot, _, _) in zip(jobs, job_block, _band_pipeline(jobs, bias_ref, s_buf)):
        o_t[blk, job[2][0]], o_t[blk, job[2][1]] = ot[:, :BLOCK_Q], ot[:, BLOCK_Q:]
    for blk in range(C_TILE // BLOCK_Q):
        for j in range(n_heads // 2):
            pair_t = jnp.concatenate([o_t[blk, 2 * j], o_t[blk, 2 * j + 1]], axis=0)
            o_ref[0, blk * BLOCK_Q:(blk + 1) * BLOCK_Q, j * LANES:(j + 1) * LANES] = pair_t.T.astype(o_ref.dtype)


def _alibi_slopes(n):
    return [2.0 ** (-8.0 * (i + 1) / n) for i in range(n)]


def _attn_c(qkv, sinks, *, n_heads):
    b, seq, _ = qkv.shape
    kv_heads = n_heads // A_REP
    q_cols = n_heads * HEAD_DIM
    kv_cols = kv_heads * HEAD_DIM
    n_tiles = seq // C_TILE
    n_edges = seq // C_WINDOW
    edges_per_tile = C_TILE // C_WINDOW
    k_blk = q_cols // kv_cols
    v_blk = k_blk + 1
    span = BLOCK_Q + 2 * C_WINDOW

    def kv_specs(col_blk):
        edge = lambda offset: pl.BlockSpec(
            (1, C_WINDOW, kv_cols),
            lambda bi, i: (bi, jnp.clip(i * edges_per_tile + offset, 0, n_edges - 1), col_blk))
        return [edge(-1), pl.BlockSpec((1, C_TILE, kv_cols), lambda bi, i: (bi, i, col_blk)),
                edge(edges_per_tile)]

    return pl.pallas_call(
        functools.partial(_attn_c_body, n_heads=n_heads, slopes=_alibi_slopes(n_heads)),
        grid=(b, n_tiles),
        in_specs=[pl.BlockSpec(memory_space=pltpu.SMEM),
                  pl.BlockSpec((1, C_TILE, q_cols), lambda bi, i: (bi, i, 0))] + kv_specs(k_blk) + kv_specs(v_blk),
        out_specs=pl.BlockSpec((1, C_TILE, q_cols), lambda bi, i: (bi, i, 0)),
        out_shape=jax.ShapeDtypeStruct((b, seq, q_cols), BF16),
        scratch_shapes=[pltpu.VMEM((3, n_heads, span, BLOCK_Q), F32),
                        pltpu.VMEM((BAND_SLOTS, span, 2 * BLOCK_Q), F32)],
        compiler_params=_params("parallel", "arbitrary"),
        name="attn_c",
    )(sinks, qkv, qkv, qkv, qkv, qkv, qkv, qkv)


def _attn_b_body(q0_ref, q1_ref, q2_ref, kp_ref, kc_ref, kn_ref, vp_ref, vc_ref, vn_ref, o_ref, lse_ref,
                 bias_ref, s_buf, *, slopes, dil):
    i = pl.program_id(2)
    pl.when(jnp.logical_and(pl.program_id(1) == 0, i == 0))(
        lambda: _fill_band_bias(bias_ref, slopes, B_HALF, float(dil)))
    n_blocks = q0_ref.shape[1] // BLOCK_Q
    block_variants = ([jnp.where(i == 0, 0, 1)] + [1] * (n_blocks - 2)
                      + [jnp.where(i == pl.num_programs(2) - 1, 2, 1)])
    k, vt = _band_operands((kp_ref, kc_ref, kn_ref), (vp_ref, vc_ref, vn_ref))
    low = lax.broadcasted_iota(jnp.int32, (BLOCK_Q, LANES), 1) < HEAD_DIM
    zero = jnp.zeros((BLOCK_Q, LANES), BF16)
    slabs = [q0_ref[0], q1_ref[0], q2_ref[0]]
    heads_all = range(B_HEADS_PER_GROUP)
    plain = [h for h in heads_all if h % 2 == h // B_REP]
    crossed = [h for h in heads_all if h % 2 != h // B_REP]
    jobs, job_block = [], []
    for blk in range(n_blocks):
        q_lo = blk * BLOCK_Q
        ks = k[q_lo:q_lo + B_SPAN]
        ks_swapped = _swap_halves(ks)
        vt1s = {kv: _vt_with_ones(vt, kv, q_lo, B_SPAN) for kv in range(B_KV_PER_GROUP)}
        for kmat, heads in ((ks, plain[:2]), (ks_swapped, crossed), (ks, plain[2:])):
            q_rows = jnp.concatenate(
                [jnp.where(low, slabs[h // 2][q_lo:q_lo + BLOCK_Q], zero) if h % 2 == 0
                 else jnp.where(low, zero, slabs[h // 2][q_lo:q_lo + BLOCK_Q]) for h in heads], axis=0)
            jobs.append((kmat, q_rows, heads, [h // B_REP for h in heads], vt1s, block_variants[blk], None))
            job_block.append(blk)
    o_t, lse_t = {}, {}
    for job, blk, (ot, m, den) in zip(jobs, job_block, _band_pipeline(jobs, bias_ref, s_buf)):
        lse = (m + jnp.log2(den)) * LN2
        for n, h in enumerate(job[2]):
            o_t[blk, h] = ot[:, n * BLOCK_Q:(n + 1) * BLOCK_Q]
            lse_t[blk, h] = jnp.broadcast_to(lse[:, n * BLOCK_Q:(n + 1) * BLOCK_Q], (HEAD_DIM, BLOCK_Q))
    for blk in range(n_blocks):
        rows = slice(blk * BLOCK_Q, (blk + 1) * BLOCK_Q)
        for j in range(B_HEADS_PER_GROUP // 2):
            cols = slice(j * LANES, (j + 1) * LANES)
            pair = lambda t: jnp.concatenate([t[blk, 2 * j], t[blk, 2 * j + 1]], axis=0).T
            o_ref[0, rows, cols] = pair(o_t).astype(o_ref.dtype)
            lse_ref[0, rows, cols] = pair(lse_t)


def _attn_b(group_qkv, b):
    n_g = len(B_GROUPS)
    gq = B_HEADS_PER_GROUP * HEAD_DIM
    slabs_per_token = B_GROUP_COLS // LANES
    q_slabs = gq // LANES
    all_slopes = _alibi_slopes(n_g * B_HEADS_PER_GROUP)
    outs, lses = [], []
    for g, (window, dil) in enumerate(B_GROUPS):
        assert (window // 2) // dil == B_HALF
        sub = group_qkv[g].shape[0] // b
        tile = min(B_TILE, sub)
        assert tile >= 2 * BLOCK_Q and sub % tile == 0
        n_tiles = sub // tile
        n_halves = sub // B_HALF
        halves_per_tile = tile // B_HALF
        view = group_qkv[g].reshape(b, sub, dil * B_GROUP_COLS)

        def tile_spec(slab, tile=tile):
            return pl.BlockSpec((1, tile, LANES), lambda bi, r, i: (bi, i, r * slabs_per_token + slab))

        def edge_spec(slab, offset, n_halves=n_halves, halves_per_tile=halves_per_tile):
            return pl.BlockSpec(
                (1, B_HALF, LANES),
                lambda bi, r, i: (bi, jnp.clip(i * halves_per_tile + offset, 0, n_halves - 1),
                                  r * slabs_per_token + slab))

        def kv_specs(slab):
            return [edge_spec(slab, -1), tile_spec(slab), edge_spec(slab, halves_per_tile)]

        out_map = lambda bi, r, i: (bi, i, r)
        o, lse = pl.pallas_call(
            functools.partial(_attn_b_body, dil=dil,
                              slopes=all_slopes[g * B_HEADS_PER_GROUP:(g + 1) * B_HEADS_PER_GROUP]),
            grid=(b, dil, n_tiles),
            in_specs=[tile_spec(j) for j in range(q_slabs)] + kv_specs(q_slabs) + kv_specs(q_slabs + 1),
            out_specs=[pl.BlockSpec((1, tile, gq), out_map), pl.BlockSpec((1, tile, gq), out_map)],
            out_shape=[jax.ShapeDtypeStruct((b, sub, dil * gq), BF16),
                       jax.ShapeDtypeStruct((b, sub, dil * gq), F32)],
            scratch_shapes=[pltpu.VMEM((3, B_HEADS_PER_GROUP, B_SPAN, BLOCK_Q), F32),
                            pltpu.VMEM((BAND_SLOTS, B_SPAN, 2 * BLOCK_Q), F32)],
            compiler_params=_params("parallel", "arbitrary", "arbitrary"),
            name=f"attn_b_g{g}",
        )(*([view] * (q_slabs + 6)))
        outs.append(o.reshape(b * sub, dil * gq))
        lses.append(lse.reshape(b * sub, dil * gq))
    return outs, lses


PROLOGUE_ROW_CHUNKS = 4


def _mix_plain(refs, scratch):
    return lambda rows: refs[0][rows, :]


def _mix_groups(refs, scratch):
    n_g = len(B_GROUPS)
    o_scr, l_scr = scratch
    slabs = o_scr.shape[0] // n_g
    tm = o_scr.shape[1]
    gq = slabs * LANES
    for g, (_, dil) in enumerate(B_GROUPS):
        for r in range(dil):
            rows = pl.ds(r, tm // dil, stride=dil)
            for s in range(slabs):
                cols = slice(r * gq + s * LANES, r * gq + (s + 1) * LANES)
                o_scr[g * slabs + s, rows, :] = refs[g][:, cols].astype(F32)
                l_scr[g * slabs + s, rows, :] = refs[n_g + g][:, cols]
    def lhs(rows):
        group = lambda scr, g: jnp.concatenate([scr[g * slabs + s, rows, :] for s in range(slabs)], axis=1)
        ls = [group(l_scr, g) for g in range(n_g)]
        mx = functools.reduce(jnp.maximum, ls)
        es = [jnp.exp(l - mx) for l in ls]
        inv = 1.0 / functools.reduce(lambda a, b: a + b, es)
        return jnp.concatenate([(group(o_scr, g) * (es[g] * inv)).astype(BF16) for g in range(n_g)], axis=1)

    return lhs


MLP_HIDDEN_CHUNK = 1024


def _block_body(*refs, n_mix, mix_fn, final_norm):
    h_ref, mix_refs = refs[0], refs[1:1 + n_mix]
    wo_ref, g_ref, w1_ref, w2_ref, fg_ref, out_ref, hn_ref = refs[1 + n_mix:8 + n_mix]
    mix_scratch = refs[8 + n_mix:]

    lhs = mix_fn(mix_refs, mix_scratch)
    chunk = h_ref.shape[0] // PROLOGUE_ROW_CHUNKS
    rows = [slice(c * chunk, (c + 1) * chunk) for c in range(PROLOGUE_ROW_CHUNKS)]
    project = lambda c: jnp.dot(lhs(rows[c]), wo_ref[...], preferred_element_type=F32)
    nxt = project(0)
    for c in range(PROLOGUE_ROW_CHUNKS):
        cur = nxt
        if c + 1 < PROLOGUE_ROW_CHUNKS:
            nxt = project(c + 1)
        h1 = h_ref[rows[c], :] + cur
        out_ref[rows[c], :] = h1
        hn_ref[rows[c], :] = _rms_rows(h1, g_ref[...]).astype(BF16)

    hn = hn_ref[...]
    for c in range(w1_ref.shape[1] // MLP_HIDDEN_CHUNK):
        cols = slice(c * MLP_HIDDEN_CHUNK, (c + 1) * MLP_HIDDEN_CHUNK)
        u = jnp.maximum(jnp.dot(hn, w1_ref[:, cols], preferred_element_type=F32), 0.0)
        out_ref[...] += jnp.dot((u * u).astype(BF16), w2_ref[cols, :], preferred_element_type=F32)

    if final_norm:
        out_ref[...] = _rms_rows(out_ref[...], fg_ref[...])


def _block(h2, mix_inputs, w_o, *, gain, w1, w2, layer, final_gain, final_norm, tm=512):
    t, d = h2.shape
    row_tile = lambda cols: pl.BlockSpec((tm, cols), lambda i: (i, 0))
    whole = lambda a: pl.BlockSpec(a.shape, lambda i: (0, 0), pipeline_mode=pl.Buffered(1))
    of_layer = lambda a: pl.BlockSpec((None,) + a.shape[1:], lambda i: (layer, 0, 0),
                                      pipeline_mode=pl.Buffered(1))
    scratch = [pltpu.VMEM((tm, d), BF16)]
    if len(mix_inputs) == 1:
        mix_fn, mix_specs = _mix_plain, [row_tile(mix_inputs[0].shape[1])]
    else:
        n_g = len(B_GROUPS)
        gq = B_HEADS_PER_GROUP * HEAD_DIM
        mix_fn = _mix_groups
        mix_specs = [pl.BlockSpec((tm // dil, dil * gq), lambda i: (i, 0)) for _, dil in B_GROUPS] * 2
        scratch += [pltpu.VMEM((n_g * gq // LANES, tm, LANES), F32)] * 2
    return pl.pallas_call(
        functools.partial(_block_body, n_mix=len(mix_inputs), mix_fn=mix_fn, final_norm=final_norm),
        grid=(t // tm,),
        in_specs=[row_tile(d)] + mix_specs
                 + [whole(w_o), whole(gain), of_layer(w1), of_layer(w2), whole(final_gain)],
        out_specs=row_tile(d),
        out_shape=jax.ShapeDtypeStruct((t, d), F32),
        scratch_shapes=scratch,
        compiler_params=_params("parallel"),
        name="block_mlp",
    )(h2, *mix_inputs, w_o, gain, w1, w2, final_gain)


def kernel(x, attn_norm, mlp_norm, a_w_qkv, a_q_gain, a_k_gain, a_w_o, b_w_qkv, b_w_o,
           c_w_qkv, c_sinks, c_w_o, mlp_w1, mlp_w2, final_norm):
    b, seq, d = x.shape
    depth = attn_norm.shape[0]
    h = x.reshape(b * seq, d)
    scale = HEAD_DIM ** -0.5 * LOG2E
    a_heads = a_w_o.shape[1] // HEAD_DIM
    c_heads = c_w_o.shape[1] // HEAD_DIM
    cos, sin = _rope_tables(seq)
    head = jnp.arange(2 * LANES) // HEAD_DIM
    seg = (head[:, None] == head[None, :]).astype(BF16)
    tile4 = lambda g: jnp.tile(g, 2 * LANES // HEAD_DIM)[None, :]
    w1_all, w2_all = mlp_w1.astype(BF16), mlp_w2.astype(BF16)
    used = [0, 0, 0]
    for layer in range(depth):
        kind = layer % N_MIXERS
        j = used[kind]
        used[kind] += 1
        gain = attn_norm[layer][None, :]
        block = functools.partial(_block, gain=mlp_norm[layer][None, :], w1=w1_all, w2=w2_all, layer=layer,
                                  final_gain=final_norm[None, :], final_norm=(layer == depth - 1))
        if kind == 0:
            q_cols = a_heads * HEAD_DIM
            k_cols = (a_w_qkv.shape[2] - q_cols) // 2
            qkv = _qkv_proj(h, gain, a_w_qkv[j].astype(BF16), q_cols=q_cols, q_scale=scale,
                            rope=(tile4(a_q_gain[j]), tile4(a_k_gain[j]), cos, sin, seg, k_cols))
            o = _attn_a(qkv.reshape(b, seq, -1), n_heads=a_heads).reshape(b * seq, q_cols)
            h = block(h, [o], a_w_o[j].astype(BF16))
        elif kind == 1:
            q_cols = len(B_GROUPS) * B_HEADS_PER_GROUP * HEAD_DIM
            groups = _qkv_proj_b(h, gain, b_w_qkv[j].astype(BF16), q_cols=q_cols, q_scale=scale)
            outs, lses = _attn_b(groups, b)
            h = block(h, outs + lses, b_w_o[j].astype(BF16))
        else:
            q_cols = c_heads * HEAD_DIM
            qkv = _qkv_proj(h, gain, c_w_qkv[j].astype(BF16), q_cols=q_cols, q_scale=scale)
            o = _attn_c(qkv.reshape(b, seq, -1), c_sinks[j], n_heads=c_heads).reshape(b * seq, q_cols)
            h = block(h, [o], c_w_o[j].astype(BF16))
    return h.reshape(b, seq, d)
```

```python
import functools
import math

import jax
import jax.numpy as jnp
from jax import lax
from jax.experimental import pallas as pl
from jax.experimental.pallas import tpu as pltpu

HEAD_DIM = 64
HALF_HEAD = HEAD_DIM // 2
RMS_EPS = 1e-6
GRID_W = 64
ROPE_THETA = 10000.0
N_MIXERS = 3
A_REP = 4
B_GROUPS = ((128, 1), (512, 4), (2048, 16))
B_HEADS_PER_GROUP = 6
B_KV_PER_GROUP = 2
B_REP = B_HEADS_PER_GROUP // B_KV_PER_GROUP
C_WINDOW = 128
LANES = 128
BF16_SUBLANES = 16
BLOCK_Q = 128
BAND_SLOTS = 4
C_TILE = 4 * BLOCK_Q
B_HALF = BLOCK_Q // 2
B_TILE = 4 * BLOCK_Q
B_SPAN = BLOCK_Q + 2 * B_HALF
B_GROUP_COLS = (B_HEADS_PER_GROUP + 2 * B_KV_PER_GROUP) * HEAD_DIM
VT_ROWS = HEAD_DIM + BF16_SUBLANES
V7X_VMEM_LIMIT_BYTES = 48 * 1024 * 1024
NEG_BIG = -1e30
LOG2E = math.log2(math.e)
LN2 = math.log(2.0)

BF16 = jnp.bfloat16
F32 = jnp.float32


def _params(*sem):
    return pltpu.CompilerParams(dimension_semantics=sem, vmem_limit_bytes=V7X_VMEM_LIMIT_BYTES)


def _nt_dot(a, b):
    return lax.dot_general(a, b, (((1,), (1,)), ((), ())), preferred_element_type=F32)


def _rms_rows(x, gain):
    ms = jnp.mean(x * x, axis=-1, keepdims=True)
    return x * lax.rsqrt(ms + RMS_EPS) * gain


def _qkv_body(x_ref, g_ref, w_ref, o_ref, *, q_cols, q_scale):
    hn = _rms_rows(x_ref[...], g_ref[...]).astype(BF16)
    y = jnp.dot(hn, w_ref[...], preferred_element_type=F32)
    o_ref[:, :q_cols] = (y[:, :q_cols] * q_scale).astype(o_ref.dtype)
    o_ref[:, q_cols:] = y[:, q_cols:].astype(o_ref.dtype)


def _qkv_rope_body(x_ref, g_ref, w_ref, qg_ref, kg_ref, cos_ref, sin_ref, seg_ref, o_ref,
                   *, q_cols, k_cols, q_scale):
    hn = _rms_rows(x_ref[...], g_ref[...]).astype(BF16)
    slab = 2 * LANES
    cos = jnp.concatenate([cos_ref[...], cos_ref[...]], axis=1)
    sin = jnp.concatenate([sin_ref[...], sin_ref[...]], axis=1)
    lane = lax.broadcasted_iota(jnp.int32, (x_ref.shape[0], slab), 1)
    first_half = (lane % HALF_HEAD) < (HALF_HEAD // 2)
    seg = seg_ref[...]
    n_qk = (q_cols + k_cols) // slab
    project = lambda s: jnp.dot(hn, w_ref[:, s * slab:(s + 1) * slab], preferred_element_type=F32)
    y_next = project(0)
    for s in range(n_qk):
        is_q = s * slab < q_cols
        ys = y_next
        y_next = project(s + 1) if s + 1 < n_qk else jnp.dot(
            hn, w_ref[:, q_cols + k_cols:], preferred_element_type=F32)
        sq = ys * ys
        hi = sq.astype(BF16)
        lo = (sq - hi.astype(F32)).astype(BF16)
        ss = jnp.dot(hi, seg, preferred_element_type=F32) + jnp.dot(lo, seg, preferred_element_type=F32)
        gain = qg_ref[...] if is_q else kg_ref[...]
        yn = ys * lax.rsqrt(ss * (1.0 / HEAD_DIM) + RMS_EPS) * gain
        partner = jnp.where(first_half,
                            pltpu.roll(yn, slab - HALF_HEAD // 2, axis=1),
                            pltpu.roll(yn, HALF_HEAD // 2, axis=1))
        r = yn * cos + partner * sin
        if is_q:
            r = r * q_scale
        o_ref[:, s * slab:(s + 1) * slab] = r.astype(o_ref.dtype)
    o_ref[:, q_cols + k_cols:] = y_next.astype(o_ref.dtype)


def _qkv_proj(h2, gain, w, *, q_cols, q_scale, tm=1024, rope=None):
    t, d = h2.shape
    n = w.shape[1]
    common = dict(
        grid=(t // tm,),
        out_specs=pl.BlockSpec((tm, n), lambda i: (i, 0)),
        out_shape=jax.ShapeDtypeStruct((t, n), BF16),
        compiler_params=_params("parallel"),
    )
    x_spec = pl.BlockSpec((tm, d), lambda i: (i, 0))
    g_spec = pl.BlockSpec((1, d), lambda i: (0, 0))
    w_spec = pl.BlockSpec((d, n), lambda i: (0, 0))
    if rope is None:
        return pl.pallas_call(
            functools.partial(_qkv_body, q_cols=q_cols, q_scale=q_scale),
            in_specs=[x_spec, g_spec, w_spec], name="qkv_proj", **common,
        )(h2, gain, w)
    qg, kg, cos, sin, seg, k_cols = rope
    seq_tiles = cos.shape[0] // tm
    slab = 2 * LANES
    return pl.pallas_call(
        functools.partial(_qkv_rope_body, q_cols=q_cols, k_cols=k_cols, q_scale=q_scale),
        in_specs=[x_spec, g_spec, w_spec,
                  pl.BlockSpec((1, slab), lambda i: (0, 0)),
                  pl.BlockSpec((1, slab), lambda i: (0, 0)),
                  pl.BlockSpec((tm, LANES), lambda i: (i % seq_tiles, 0)),
                  pl.BlockSpec((tm, LANES), lambda i: (i % seq_tiles, 0)),
                  pl.BlockSpec((slab, slab), lambda i: (0, 0))],
        name="qkv_proj_rope", **common,
    )(h2, gain, w, qg, kg, cos, sin, seg)


def _qkv_b_body(x_ref, g_ref, w_ref, *rest, q_cols, q_scale):
    out_refs, y_scr = rest[:-1], rest[-1]
    tm = x_ref.shape[0]
    hn = _rms_rows(x_ref[...], g_ref[...]).astype(BF16)
    y = jnp.dot(hn, w_ref[...], preferred_element_type=F32)
    for s in range(y_scr.shape[0]):
        ys = y[:, s * LANES:(s + 1) * LANES]
        y_scr[s] = ys * q_scale if s * LANES < q_cols else ys
    n_g = len(B_GROUPS)
    q_slabs = B_HEADS_PER_GROUP * HEAD_DIM // LANES
    k0 = q_cols // LANES
    for g, (_, dil) in enumerate(B_GROUPS):
        slabs = list(range(g * q_slabs, (g + 1) * q_slabs)) + [k0 + g, k0 + n_g + g]
        for r in range(dil):
            rows = pl.ds(r, tm // dil, stride=dil)
            piece = jnp.concatenate([y_scr[s, rows, :] for s in slabs], axis=1)
            out_refs[g][:, r * B_GROUP_COLS:(r + 1) * B_GROUP_COLS] = piece.astype(BF16)


def _qkv_proj_b(h2, gain, w, *, q_cols, q_scale, tm=1024):
    t, d = h2.shape
    n = w.shape[1]
    return pl.pallas_call(
        functools.partial(_qkv_b_body, q_cols=q_cols, q_scale=q_scale),
        grid=(t // tm,),
        in_specs=[pl.BlockSpec((tm, d), lambda i: (i, 0)),
                  pl.BlockSpec((1, d), lambda i: (0, 0)),
                  pl.BlockSpec((d, n), lambda i: (0, 0))],
        out_specs=[pl.BlockSpec((tm // dil, dil * B_GROUP_COLS), lambda i: (i, 0)) for _, dil in B_GROUPS],
        out_shape=[jax.ShapeDtypeStruct((t // dil, dil * B_GROUP_COLS), BF16) for _, dil in B_GROUPS],
        scratch_shapes=[pltpu.VMEM((n // LANES, tm, LANES), F32)],
        compiler_params=_params("parallel"),
        name="qkv_proj_b",
    )(h2, gain, w)


def _rope_tables(seq):
    inv_freq = ROPE_THETA ** (-jnp.arange(0, HALF_HEAD, 2, dtype=F32) / HALF_HEAD)
    rows = seq // GRID_W
    ang_row = jnp.arange(rows, dtype=F32)[:, None] * inv_freq
    ang_col = jnp.arange(GRID_W, dtype=F32)[:, None] * inv_freq
    per_row = lambda x: jnp.repeat(x, GRID_W, axis=0)
    per_col = lambda x: jnp.tile(x, (rows, 1))
    cos = jnp.concatenate([per_row(jnp.cos(ang_row))] * 2 + [per_col(jnp.cos(ang_col))] * 2, axis=1)
    sin_r, sin_c = per_row(jnp.sin(ang_row)), per_col(jnp.sin(ang_col))
    sin = jnp.concatenate([-sin_r, sin_r, -sin_c, sin_c], axis=1)
    return jnp.concatenate([cos, cos], axis=1), jnp.concatenate([sin, sin], axis=1)


def _attn_a_body(q_ref, k_ref, v_ref, o_ref, klo, khi, vt1, s_buf, *, tq, tk, seq, tiles_per_step):
    def fill(head_in_high_lanes):
        low = lax.broadcasted_iota(jnp.int32, (seq, LANES), 1) < HEAD_DIM
        zero = jnp.zeros((seq, LANES), BF16)
        x = k_ref[0]
        swapped = jnp.concatenate([x[:, HEAD_DIM:], x[:, :HEAD_DIM]], axis=1)
        in_low, in_high = (swapped, x) if head_in_high_lanes else (x, swapped)
        klo[...] = jnp.where(low, in_low, zero)
        khi[...] = jnp.where(low, zero, in_high)
        ones_row = (lax.broadcasted_iota(jnp.int32, (VT_ROWS - HEAD_DIM, tk), 0) == 0).astype(F32)
        for c in range(seq // tk):
            xt = v_ref[0, c * tk:(c + 1) * tk, :].astype(F32).T
            vt = xt[HEAD_DIM:] if head_in_high_lanes else xt[:HEAD_DIM]
            vt1[:, c * tk:(c + 1) * tk] = jnp.concatenate([vt, ones_row], axis=0).astype(BF16)

    odd = (pl.program_id(1) % 2) == 1
    pl.when(jnp.logical_not(odd))(lambda: fill(False))
    pl.when(odd)(lambda: fill(True))

    cols = 2 * tq
    n_chunks = seq // tk
    n_tiles = seq // tq

    def scores(tile, c, slot):
        q = q_ref[0, pl.ds(pl.multiple_of(tile * tq, tq), tq), :]
        q2 = jnp.concatenate([q[:, :LANES], q[:, LANES:]], axis=0)
        cms = []
        for par, k_s in enumerate((klo, khi)):
            s = _nt_dot(k_s[c * tk:(c + 1) * tk, :], q2)
            s_buf[slot, par] = s
            cms.append(jnp.max(s, axis=0, keepdims=True))
        return tuple(cms)

    def accumulate(c, slot, cms, state):
        vt = vt1[:, c * tk:(c + 1) * tk]
        new = []
        for par in range(2):
            m, acc = state[2 * par], state[2 * par + 1]
            mn = jnp.maximum(m, cms[par])
            p = jnp.exp2(s_buf[slot, par] - mn)
            new += [mn, acc * jnp.exp2(m - mn) + jnp.dot(vt, p.astype(BF16), preferred_element_type=F32)]
        return tuple(new)

    neg = jnp.full((1, cols), -jnp.inf, F32)
    acc0 = jnp.zeros((VT_ROWS, cols), F32)

    def step(j, cms):
        for t in range(tiles_per_step):
            tile = j * tiles_per_step + t
            state = (neg, acc0, neg, acc0)
            for c in range(n_chunks):
                n = t * n_chunks + c
                nxt_tile = tile if c + 1 < n_chunks else jnp.minimum(tile + 1, n_tiles - 1)
                cms_next = scores(nxt_tile, (c + 1) % n_chunks, (n + 1) % 2)
                state = accumulate(c, n % 2, cms, state)
                cms = cms_next
            out = jnp.concatenate(
                [acc[:HEAD_DIM] / acc[HEAD_DIM:HEAD_DIM + 1] for acc in (state[1], state[3])], axis=0)
            o_ref[0, pl.ds(pl.multiple_of(tile * tq, tq), tq), :] = jnp.concatenate(
                [out[:, :tq].T, out[:, tq:].T], axis=1).astype(o_ref.dtype)
        return cms

    lax.fori_loop(0, n_tiles // tiles_per_step, step, scores(0, 0, 0))


def _attn_a(qkv, *, n_heads, tq=256, tk=512, tiles_per_step=2):
    b, seq, _ = qkv.shape
    kv_heads = n_heads // A_REP
    q_cols = n_heads * HEAD_DIM
    q_w = A_REP * HEAD_DIM
    k_blk0 = q_cols // LANES
    v_blk0 = (q_cols + kv_heads * HEAD_DIM) // LANES
    assert (seq // tk) % 2 == 0 and seq % (tq * tiles_per_step) == 0
    return pl.pallas_call(
        functools.partial(_attn_a_body, tq=tq, tk=tk, seq=seq, tiles_per_step=tiles_per_step),
        grid=(b, kv_heads),
        in_specs=[pl.BlockSpec((1, seq, q_w), lambda bi, h: (bi, 0, h)),
                  pl.BlockSpec((1, seq, LANES), lambda bi, h: (bi, 0, k_blk0 + h // 2)),
                  pl.BlockSpec((1, seq, LANES), lambda bi, h: (bi, 0, v_blk0 + h // 2))],
        out_specs=pl.BlockSpec((1, seq, q_w), lambda bi, h: (bi, 0, h)),
        out_shape=jax.ShapeDtypeStruct((b, seq, q_cols), BF16),
        scratch_shapes=[pltpu.VMEM((seq, LANES), BF16), pltpu.VMEM((seq, LANES), BF16),
                        pltpu.VMEM((VT_ROWS, seq), BF16), pltpu.VMEM((2, 2, tk, 2 * tq), F32)],
        compiler_params=_params("parallel", "arbitrary"),
        name="attn_a",
    )(qkv, qkv, qkv)


def _fill_band_bias(bias_ref, slopes, half_window, dist_scale):
    span = bias_ref.shape[2]
    c = lax.broadcasted_iota(jnp.int32, (span, BLOCK_Q), 0)
    r = lax.broadcasted_iota(jnp.int32, (span, BLOCK_Q), 1)
    dist = jnp.abs(c - half_window - r)
    inside = dist <= half_window
    penalty = dist.astype(F32) * (dist_scale * LOG2E)
    for variant, ok in enumerate((inside & (c >= half_window), inside, inside & (c < half_window + BLOCK_Q))):
        for h, slope in enumerate(slopes):
            bias_ref[variant, h] = jnp.where(ok, -slope * penalty, NEG_BIG)


def _band_pipeline(jobs, bias_ref, s_buf):
    def scores(job, slot):
        kmat, q_rows, head_ids, _, _, variant, sink_row = job
        width = len(head_ids) * BLOCK_Q
        s = _nt_dot(kmat, q_rows) + jnp.concatenate([bias_ref[variant, h] for h in head_ids], axis=1)
        s_buf[slot, :, :width] = s
        m = jnp.max(s, axis=0, keepdims=True)
        return m if sink_row is None else jnp.maximum(m, sink_row)

    def values(job, slot, m):
        _, _, head_ids, kv_ids, vt1s, _, sink_row = job
        n = len(head_ids)
        p = jnp.exp2(s_buf[slot, :, :n * BLOCK_Q] - m).astype(BF16)
        accs, start = [], 0
        while start < n:
            stop = start
            while stop < n and kv_ids[stop] == kv_ids[start]:
                stop += 1
            accs.append(jnp.dot(vt1s[kv_ids[start]], p[:, start * BLOCK_Q:stop * BLOCK_Q],
                                preferred_element_type=F32))
            start = stop
        acc = accs[0] if len(accs) == 1 else jnp.concatenate(accs, axis=1)
        den = acc[HEAD_DIM:HEAD_DIM + 1]
        if sink_row is not None:
            den = den + jnp.exp2(sink_row - m)
        return acc[:HEAD_DIM] / den, m, den

    results = []
    n_slots = s_buf.shape[0]
    ahead = n_slots - 1
    maxima = [scores(jobs[c], c % n_slots) for c in range(min(ahead, len(jobs)))]
    for c, job in enumerate(jobs):
        if c + ahead < len(jobs):
            maxima.append(scores(jobs[c + ahead], (c + ahead) % n_slots))
        results.append(values(job, c % n_slots, maxima[c]))
    return results


def _band_operands(k_refs, v_refs):
    k = jnp.concatenate([r[0] for r in k_refs], axis=0)
    vt = jnp.concatenate([r[0] for r in v_refs], axis=0).astype(F32).T
    return k, vt


def _vt_with_ones(vt, kv, start, span):
    ones_row = (lax.broadcasted_iota(jnp.int32, (VT_ROWS - HEAD_DIM, span), 0) == 0).astype(F32)
    rows = vt[kv * HEAD_DIM:(kv + 1) * HEAD_DIM, start:start + span]
    return jnp.concatenate([rows, ones_row], axis=0).astype(BF16)


def _swap_halves(x):
    return jnp.concatenate([x[:, HEAD_DIM:], x[:, :HEAD_DIM]], axis=1)


def _attn_c_body(sink_ref, q_ref, kp_ref, kc_ref, kn_ref, vp_ref, vc_ref, vn_ref, o_ref, bias_ref, s_buf,
                 *, n_heads, slopes):
    i = pl.program_id(1)
    pl.when(i == 0)(lambda: _fill_band_bias(bias_ref, slopes, C_WINDOW, 1.0))
    n_blocks = C_TILE // BLOCK_Q
    block_variants = ([jnp.where(i == 0, 0, 1)] + [1] * (n_blocks - 2)
                      + [jnp.where(i == pl.num_programs(1) - 1, 2, 1)])
    q = q_ref[0]
    k, vt = _band_operands((kp_ref, kc_ref, kn_ref), (vp_ref, vc_ref, vn_ref))
    span = BLOCK_Q + 2 * C_WINDOW
    low = lax.broadcasted_iota(jnp.int32, (BLOCK_Q, LANES), 1) < HEAD_DIM
    zero = jnp.zeros((BLOCK_Q, LANES), BF16)
    first_block = lax.broadcasted_iota(jnp.int32, (1, 2 * BLOCK_Q), 1) < BLOCK_Q
    jobs, job_block = [], []
    for blk in range(n_blocks):
        q_lo = blk * BLOCK_Q
        for pair in range(n_heads // A_REP // 2):
            ks = k[q_lo:q_lo + span, pair * LANES:(pair + 1) * LANES]
            ks_swapped = _swap_halves(ks)
            for which in range(2):
                kv = 2 * pair + which
                vt1s = {kv: _vt_with_ones(vt, kv, q_lo, span)}
                slabs = [q[q_lo:q_lo + BLOCK_Q, (2 * kv + j) * LANES:(2 * kv + j + 1) * LANES] for j in range(2)]
                for parity in range(2):
                    q_rows = jnp.concatenate(
                        [jnp.where(low, s_, zero) if parity == 0 else jnp.where(low, zero, s_) for s_ in slabs],
                        axis=0)
                    heads = (A_REP * kv + parity, A_REP * kv + 2 + parity)
                    sink_row = jnp.where(first_block, sink_ref[heads[0]], sink_ref[heads[1]]) * LOG2E
                    jobs.append((ks if parity == which else ks_swapped, q_rows, heads, (kv, kv), vt1s,
                                 block_variants[blk], sink_row))
                    job_block.append(blk)
    o_t = {}
    for job, blk, (ot, _, _) in zip(jobs, job_block, _band_pipeline(jobs, bias_ref, s_buf)):
        o_t[blk, job[2][0]], o_t[blk, job[2][1]] = ot[:, :BLOCK_Q], ot[:, BLOCK_Q:]
    for blk in range(n_blocks):
        for j in range(n_heads // 2):
            pair_t = jnp.concatenate([o_t[blk, 2 * j], o_t[blk, 2 * j + 1]], axis=0)
            o_ref[0, blk * BLOCK_Q:(blk + 1) * BLOCK_Q, j * LANES:(j + 1) * LANES] = pair_t.T.astype(o_ref.dtype)


def _alibi_slopes(n):
    return [2.0 ** (-8.0 * (i + 1) / n) for i in range(n)]


def _attn_c(qkv, sinks, *, n_heads):
    b, seq, _ = qkv.shape
    kv_heads = n_heads // A_REP
    q_cols = n_heads * HEAD_DIM
    kv_cols = kv_heads * HEAD_DIM
    n_tiles = seq // C_TILE
    n_edges = seq // C_WINDOW
    edges_per_tile = C_TILE // C_WINDOW
    k_blk = q_cols // kv_cols
    v_blk = k_blk + 1
    span = BLOCK_Q + 2 * C_WINDOW

    def kv_specs(col_blk):
        edge = lambda offset: pl.BlockSpec(
            (1, C_WINDOW, kv_cols),
            lambda bi, i: (bi, jnp.clip(i * edges_per_tile + offset, 0, n_edges - 1), col_blk))
        return [edge(-1), pl.BlockSpec((1, C_TILE, kv_cols), lambda bi, i: (bi, i, col_blk)),
                edge(edges_per_tile)]

    return pl.pallas_call(
        functools.partial(_attn_c_body, n_heads=n_heads, slopes=_alibi_slopes(n_heads)),
        grid=(b, n_tiles),
        in_specs=[pl.BlockSpec(memory_space=pltpu.SMEM),
                  pl.BlockSpec((1, C_TILE, q_cols), lambda bi, i: (bi, i, 0))] + kv_specs(k_blk) + kv_specs(v_blk),
        out_specs=pl.BlockSpec((1, C_TILE, q_cols), lambda bi, i: (bi, i, 0)),
        out_shape=jax.ShapeDtypeStruct((b, seq, q_cols), BF16),
        scratch_shapes=[pltpu.VMEM((3, n_heads, span, BLOCK_Q), F32),
                        pltpu.VMEM((BAND_SLOTS, span, 2 * BLOCK_Q), F32)],
        compiler_params=_params("parallel", "arbitrary"),
        name="attn_c",
    )(sinks, qkv, qkv, qkv, qkv, qkv, qkv, qkv)


def _attn_b_body(q0_ref, q1_ref, q2_ref, kp_ref, kc_ref, kn_ref, vp_ref, vc_ref, vn_ref, o_ref, lse_ref,
                 bias_ref, s_buf, *, slopes, dil):
    i = pl.program_id(2)
    pl.when(jnp.logical_and(pl.program_id(1) == 0, i == 0))(
        lambda: _fill_band_bias(bias_ref, slopes, B_HALF, float(dil)))
    n_blocks = q0_ref.shape[1] // BLOCK_Q
    block_variants = ([jnp.where(i == 0, 0, 1)] + [1] * (n_blocks - 2)
                      + [jnp.where(i == pl.num_programs(2) - 1, 2, 1)])
    k, vt = _band_operands((kp_ref, kc_ref, kn_ref), (vp_ref, vc_ref, vn_ref))
    low = lax.broadcasted_iota(jnp.int32, (BLOCK_Q, LANES), 1) < HEAD_DIM
    zero = jnp.zeros((BLOCK_Q, LANES), BF16)
    slabs = [q0_ref[0], q1_ref[0], q2_ref[0]]
    heads_all = range(B_HEADS_PER_GROUP)
    plain = [h for h in heads_all if h % 2 == h // B_REP]
    crossed = [h for h in heads_all if h % 2 != h // B_REP]
    jobs, job_block = [], []
    for blk in range(n_blocks):
        q_lo = blk * BLOCK_Q
        ks = k[q_lo:q_lo + B_SPAN]
        ks_swapped = _swap_halves(ks)
        vt1s = {kv: _vt_with_ones(vt, kv, q_lo, B_SPAN) for kv in range(B_KV_PER_GROUP)}
        for kmat, heads in ((ks, plain[:2]), (ks_swapped, crossed), (ks, plain[2:])):
            q_rows = jnp.concatenate(
                [jnp.where(low, slabs[h // 2][q_lo:q_lo + BLOCK_Q], zero) if h % 2 == 0
                 else jnp.where(low, zero, slabs[h // 2][q_lo:q_lo + BLOCK_Q]) for h in heads], axis=0)
            jobs.append((kmat, q_rows, heads, [h // B_REP for h in heads], vt1s, block_variants[blk], None))
            job_block.append(blk)
    o_t, lse_t = {}, {}
    for job, blk, (ot, m, den) in zip(jobs, job_block, _band_pipeline(jobs, bias_ref, s_buf)):
        lse = (m + jnp.log2(den)) * LN2
        for n, h in enumerate(job[2]):
            o_t[blk, h] = ot[:, n * BLOCK_Q:(n + 1) * BLOCK_Q]
            lse_t[blk, h] = jnp.broadcast_to(lse[:, n * BLOCK_Q:(n + 1) * BLOCK_Q], (HEAD_DIM, BLOCK_Q))
    for blk in range(n_blocks):
        rows = slice(blk * BLOCK_Q, (blk + 1) * BLOCK_Q)
        for j in range(B_HEADS_PER_GROUP // 2):
            cols = slice(j * LANES, (j + 1) * LANES)
            pair = lambda t: jnp.concatenate([t[blk, 2 * j], t[blk, 2 * j + 1]], axis=0).T
            o_ref[0, rows, cols] = pair(o_t).astype(o_ref.dtype)
            lse_ref[0, rows, cols] = pair(lse_t)


def _attn_b(group_qkv, b):
    n_g = len(B_GROUPS)
    gq = B_HEADS_PER_GROUP * HEAD_DIM
    slabs_per_token = B_GROUP_COLS // LANES
    q_slabs = gq // LANES
    all_slopes = _alibi_slopes(n_g * B_HEADS_PER_GROUP)
    outs, lses = [], []
    for g, (window, dil) in enumerate(B_GROUPS):
        assert (window // 2) // dil == B_HALF
        sub = group_qkv[g].shape[0] // b
        tile = min(B_TILE, sub)
        assert tile >= 2 * BLOCK_Q and sub % tile == 0
        n_tiles = sub // tile
        n_halves = sub // B_HALF
        halves_per_tile = tile // B_HALF
        view = group_qkv[g].reshape(b, sub, dil * B_GROUP_COLS)

        def tile_spec(slab, tile=tile):
            return pl.BlockSpec((1, tile, LANES), lambda bi, r, i: (bi, i, r * slabs_per_token + slab))

        def edge_spec(slab, offset, n_halves=n_halves, halves_per_tile=halves_per_tile):
            return pl.BlockSpec(
                (1, B_HALF, LANES),
                lambda bi, r, i: (bi, jnp.clip(i * halves_per_tile + offset, 0, n_halves - 1),
                                  r * slabs_per_token + slab))

        def kv_specs(slab):
            return [edge_spec(slab, -1), tile_spec(slab), edge_spec(slab, halves_per_tile)]

        out_map = lambda bi, r, i: (bi, i, r)
        o, lse = pl.pallas_call(
            functools.partial(_attn_b_body, dil=dil,
                              slopes=all_slopes[g * B_HEADS_PER_GROUP:(g + 1) * B_HEADS_PER_GROUP]),
            grid=(b, dil, n_tiles),
            in_specs=[tile_spec(j) for j in range(q_slabs)] + kv_specs(q_slabs) + kv_specs(q_slabs + 1),
            out_specs=[pl.BlockSpec((1, tile, gq), out_map), pl.BlockSpec((1, tile, gq), out_map)],
            out_shape=[jax.ShapeDtypeStruct((b, sub, dil * gq), BF16),
                       jax.ShapeDtypeStruct((b, sub, dil * gq), F32)],
            scratch_shapes=[pltpu.VMEM((3, B_HEADS_PER_GROUP, B_SPAN, BLOCK_Q), F32),
                            pltpu.VMEM((BAND_SLOTS, B_SPAN, 2 * BLOCK_Q), F32)],
            compiler_params=_params("parallel", "arbitrary", "arbitrary"),
            name=f"attn_b_g{g}",
        )(*([view] * (q_slabs + 6)))
        outs.append(o.reshape(b * sub, dil * gq))
        lses.append(lse.reshape(b * sub, dil * gq))
    return outs, lses


PROLOGUE_ROW_CHUNKS = 4


def _mix_plain(refs, scratch):
    return lambda rows: refs[0][rows, :]


def _mix_groups(refs, scratch):
    n_g = len(B_GROUPS)
    o_scr, l_scr = scratch
    slabs = o_scr.shape[0] // n_g
    tm = o_scr.shape[1]
    gq = slabs * LANES
    for g, (_, dil) in enumerate(B_GROUPS):
        for r in range(dil):
            rows = pl.ds(r, tm // dil, stride=dil)
            for s in range(slabs):
                cols = slice(r * gq + s * LANES, r * gq + (s + 1) * LANES)
                o_scr[g * slabs + s, rows, :] = refs[g][:, cols].astype(F32)
                l_scr[g * slabs + s, rows, :] = refs[n_g + g][:, cols]
    def lhs(rows):
        group = lambda scr, g: jnp.concatenate([scr[g * slabs + s, rows, :] for s in range(slabs)], axis=1)
        ls = [group(l_scr, g) for g in range(n_g)]
        mx = functools.reduce(jnp.maximum, ls)
        es = [jnp.exp(l - mx) for l in ls]
        inv = 1.0 / functools.reduce(lambda a, b: a + b, es)
        return jnp.concatenate([(group(o_scr, g) * (es[g] * inv)).astype(BF16) for g in range(n_g)], axis=1)

    return lhs


MLP_HIDDEN_CHUNK = 1024


def _block_body(*refs, n_mix, mix_fn, final_norm):
    h_ref, mix_refs = refs[0], refs[1:1 + n_mix]
    wo_ref, g_ref, w1_ref, w2_ref, fg_ref, out_ref, hn_ref = refs[1 + n_mix:8 + n_mix]
    mix_scratch = refs[8 + n_mix:]

    lhs = mix_fn(mix_refs, mix_scratch)
    chunk = h_ref.shape[0] // PROLOGUE_ROW_CHUNKS
    rows = [slice(c * chunk, (c + 1) * chunk) for c in range(PROLOGUE_ROW_CHUNKS)]
    project = lambda c: jnp.dot(lhs(rows[c]), wo_ref[...], preferred_element_type=F32)
    nxt = project(0)
    for c in range(PROLOGUE_ROW_CHUNKS):
        cur = nxt
        if c + 1 < PROLOGUE_ROW_CHUNKS:
            nxt = project(c + 1)
        h1 = h_ref[rows[c], :] + cur
        out_ref[rows[c], :] = h1
        hn_ref[rows[c], :] = _rms_rows(h1, g_ref[...]).astype(BF16)

    hn = hn_ref[...]
    for c in range(w1_ref.shape[1] // MLP_HIDDEN_CHUNK):
        cols = slice(c * MLP_HIDDEN_CHUNK, (c + 1) * MLP_HIDDEN_CHUNK)
        u = jnp.maximum(jnp.dot(hn, w1_ref[:, cols], preferred_element_type=F32), 0.0)
        out_ref[...] += jnp.dot((u * u).astype(BF16), w2_ref[cols, :], preferred_element_type=F32)

    if final_norm:
        out_ref[...] = _rms_rows(out_ref[...], fg_ref[...])


def _block(h2, mix_inputs, w_o, *, gain, w1, w2, layer, final_gain, final_norm, tm=512):
    t, d = h2.shape
    row_tile = lambda cols: pl.BlockSpec((tm, cols), lambda i: (i, 0))
    whole = lambda a: pl.BlockSpec(a.shape, lambda i: (0, 0), pipeline_mode=pl.Buffered(1))
    of_layer = lambda a: pl.BlockSpec((None,) + a.shape[1:], lambda i: (layer, 0, 0),
                                      pipeline_mode=pl.Buffered(1))
    scratch = [pltpu.VMEM((tm, d), BF16)]
    if len(mix_inputs) == 1:
        mix_fn, mix_specs = _mix_plain, [row_tile(mix_inputs[0].shape[1])]
    else:
        n_g = len(B_GROUPS)
        gq = B_HEADS_PER_GROUP * HEAD_DIM
        mix_fn = _mix_groups
        mix_specs = [pl.BlockSpec((tm // dil, dil * gq), lambda i: (i, 0)) for _, dil in B_GROUPS] * 2
        scratch += [pltpu.VMEM((n_g * gq // LANES, tm, LANES), F32)] * 2
    return pl.pallas_call(
        functools.partial(_block_body, n_mix=len(mix_inputs), mix_fn=mix_fn, final_norm=final_norm),
        grid=(t // tm,),
        in_specs=[row_tile(d)] + mix_specs
                 + [whole(w_o), whole(gain), of_layer(w1), of_layer(w2), whole(final_gain)],
        out_specs=row_tile(d),
        out_shape=jax.ShapeDtypeStruct((t, d), F32),
        scratch_shapes=scratch,
        compiler_params=_params("parallel"),
        name="block_mlp",
    )(h2, *mix_inputs, w_o, gain, w1, w2, final_gain)


def kernel(x, attn_norm, mlp_norm, a_w_qkv, a_q_gain, a_k_gain, a_w_o, b_w_qkv, b_w_o,
           c_w_qkv, c_sinks, c_w_o, mlp_w1, mlp_w2, final_norm):
    b, seq, d = x.shape
    depth = attn_norm.shape[0]
    h = x.reshape(b * seq, d)
    scale = HEAD_DIM ** -0.5 * LOG2E
    a_heads = a_w_o.shape[1] // HEAD_DIM
    c_heads = c_w_o.shape[1] // HEAD_DIM
    cos, sin = _rope_tables(seq)
    head = jnp.arange(2 * LANES) // HEAD_DIM
    seg = (head[:, None] == head[None, :]).astype(BF16)
    tile4 = lambda g: jnp.tile(g, 2 * LANES // HEAD_DIM)[None, :]
    w1_all, w2_all = mlp_w1.astype(BF16), mlp_w2.astype(BF16)
    used = [0, 0, 0]
    for layer in range(depth):
        kind = layer % N_MIXERS
        j = used[kind]
        used[kind] += 1
        gain = attn_norm[layer][None, :]
        block = functools.partial(_block, gain=mlp_norm[layer][None, :], w1=w1_all, w2=w2_all, layer=layer,
                                  final_gain=final_norm[None, :], final_norm=(layer == depth - 1))
        if kind == 0:
            q_cols = a_heads * HEAD_DIM
            k_cols = (a_w_qkv.shape[2] - q_cols) // 2
            qkv = _qkv_proj(h, gain, a_w_qkv[j].astype(BF16), q_cols=q_cols, q_scale=scale,
                            rope=(tile4(a_q_gain[j]), tile4(a_k_gain[j]), cos, sin, seg, k_cols))
            o = _attn_a(qkv.reshape(b, seq, -1), n_heads=a_heads).reshape(b * seq, q_cols)
            h = block(h, [o], a_w_o[j].astype(BF16))
        elif kind == 1:
            q_cols = len(B_GROUPS) * B_HEADS_PER_GROUP * HEAD_DIM
            groups = _qkv_proj_b(h, gain, b_w_qkv[j].astype(BF16), q_cols=q_cols, q_scale=scale)
            outs, lses = _attn_b(groups, b)
            h = block(h, outs + lses, b_w_o[j].astype(BF16))
        else:
            q_cols = c_heads * HEAD_DIM
            qkv = _qkv_proj(h, gain, c_w_qkv[j].astype(BF16), q_cols=q_cols, q_scale=scale)
            o = _attn_c(qkv.reshape(b, seq, -1), c_sinks[j], n_heads=c_heads).reshape(b * seq, q_cols)
            h = block(h, [o], c_w_o[j].astype(BF16))
    return h.reshape(b, seq, d)
```

```python
import functools
import math

import jax
import jax.numpy as jnp
from jax import lax
from jax.experimental import pallas as pl
from jax.experimental.pallas import tpu as pltpu

HEAD_DIM = 64
HALF_HEAD = HEAD_DIM // 2
RMS_EPS = 1e-6
GRID_W = 64
ROPE_THETA = 10000.0
N_MIXERS = 3
A_REP = 4
B_GROUPS = ((128, 1), (512, 4), (2048, 16))
B_HEADS_PER_GROUP = 6
B_KV_PER_GROUP = 2
B_REP = B_HEADS_PER_GROUP // B_KV_PER_GROUP
C_WINDOW = 128
LANES = 128
BF16_SUBLANES = 16
BLOCK_Q = 128
BAND_SLOTS = 4
C_TILE = 4 * BLOCK_Q
B_HALF = BLOCK_Q // 2
B_TILE = 4 * BLOCK_Q
B_SPAN = BLOCK_Q + 2 * B_HALF
B_GROUP_COLS = (B_HEADS_PER_GROUP + 2 * B_KV_PER_GROUP) * HEAD_DIM
VT_ROWS = HEAD_DIM + BF16_SUBLANES
V7X_VMEM_LIMIT_BYTES = 48 * 1024 * 1024
NEG_BIG = -1e30
LOG2E = math.log2(math.e)
LN2 = math.log(2.0)

BF16 = jnp.bfloat16
F32 = jnp.float32


def _params(*sem):
    return pltpu.CompilerParams(dimension_semantics=sem, vmem_limit_bytes=V7X_VMEM_LIMIT_BYTES)


def _nt_dot(a, b):
    return lax.dot_general(a, b, (((1,), (1,)), ((), ())), preferred_element_type=F32)


def _rms_rows(x, gain):
    ms = jnp.mean(x * x, axis=-1, keepdims=True)
    return x * lax.rsqrt(ms + RMS_EPS) * gain


def _qkv_body(x_ref, g_ref, w_ref, o_ref, *, q_cols, q_scale):
    hn = _rms_rows(x_ref[...], g_ref[...]).astype(BF16)
    y = jnp.dot(hn, w_ref[...], preferred_element_type=F32)
    o_ref[:, :q_cols] = (y[:, :q_cols] * q_scale).astype(o_ref.dtype)
    o_ref[:, q_cols:] = y[:, q_cols:].astype(o_ref.dtype)


def _qkv_rope_body(x_ref, g_ref, w_ref, qg_ref, kg_ref, cos_ref, sin_ref, seg_ref, o_ref,
                   *, q_cols, k_cols, q_scale):
    hn = _rms_rows(x_ref[...], g_ref[...]).astype(BF16)
    slab = 2 * LANES
    cos = jnp.concatenate([cos_ref[...], cos_ref[...]], axis=1)
    sin = jnp.concatenate([sin_ref[...], sin_ref[...]], axis=1)
    lane = lax.broadcasted_iota(jnp.int32, (x_ref.shape[0], slab), 1)
    first_half = (lane % HALF_HEAD) < (HALF_HEAD // 2)
    seg = seg_ref[...]
    n_qk = (q_cols + k_cols) // slab
    project = lambda s: jnp.dot(hn, w_ref[:, s * slab:(s + 1) * slab], preferred_element_type=F32)
    y_next = project(0)
    for s in range(n_qk):
        is_q = s * slab < q_cols
        ys = y_next
        y_next = project(s + 1) if s + 1 < n_qk else jnp.dot(
            hn, w_ref[:, q_cols + k_cols:], preferred_element_type=F32)
        sq = ys * ys
        hi = sq.astype(BF16)
        lo = (sq - hi.astype(F32)).astype(BF16)
        ss = jnp.dot(hi, seg, preferred_element_type=F32) + jnp.dot(lo, seg, preferred_element_type=F32)
        gain = qg_ref[...] if is_q else kg_ref[...]
        yn = ys * lax.rsqrt(ss * (1.0 / HEAD_DIM) + RMS_EPS) * gain
        partner = jnp.where(first_half,
                            pltpu.roll(yn, slab - HALF_HEAD // 2, axis=1),
                            pltpu.roll(yn, HALF_HEAD // 2, axis=1))
        r = yn * cos + partner * sin
        if is_q:
            r = r * q_scale
        o_ref[:, s * slab:(s + 1) * slab] = r.astype(o_ref.dtype)
    o_ref[:, q_cols + k_cols:] = y_next.astype(o_ref.dtype)


def _qkv_proj(h2, gain, w, *, q_cols, q_scale, tm=1024, rope=None):
    t, d = h2.shape
    n = w.shape[1]
    assert t % tm == 0 and q_cols % (2 * LANES) == 0 and n % LANES == 0
    common = dict(
        grid=(t // tm,),
        out_specs=pl.BlockSpec((tm, n), lambda i: (i, 0)),
        out_shape=jax.ShapeDtypeStruct((t, n), BF16),
        compiler_params=_params("parallel"),
    )
    x_spec = pl.BlockSpec((tm, d), lambda i: (i, 0))
    g_spec = pl.BlockSpec((1, d), lambda i: (0, 0))
    w_spec = pl.BlockSpec((d, n), lambda i: (0, 0))
    if rope is None:
        return pl.pallas_call(
            functools.partial(_qkv_body, q_cols=q_cols, q_scale=q_scale),
            in_specs=[x_spec, g_spec, w_spec], name="qkv_proj", **common,
        )(h2, gain, w)
    qg, kg, cos, sin, seg, k_cols = rope
    seq_tiles = cos.shape[0] // tm
    slab = 2 * LANES
    return pl.pallas_call(
        functools.partial(_qkv_rope_body, q_cols=q_cols, k_cols=k_cols, q_scale=q_scale),
        in_specs=[x_spec, g_spec, w_spec,
                  pl.BlockSpec((1, slab), lambda i: (0, 0)),
                  pl.BlockSpec((1, slab), lambda i: (0, 0)),
                  pl.BlockSpec((tm, LANES), lambda i: (i % seq_tiles, 0)),
                  pl.BlockSpec((tm, LANES), lambda i: (i % seq_tiles, 0)),
                  pl.BlockSpec((slab, slab), lambda i: (0, 0))],
        name="qkv_proj_rope", **common,
    )(h2, gain, w, qg, kg, cos, sin, seg)


def _qkv_b_body(x_ref, g_ref, w_ref, *rest, q_cols, q_scale):
    out_refs, y_scr = rest[:-1], rest[-1]
    tm = x_ref.shape[0]
    hn = _rms_rows(x_ref[...], g_ref[...]).astype(BF16)
    y = jnp.dot(hn, w_ref[...], preferred_element_type=F32)
    for s in range(y_scr.shape[0]):
        ys = y[:, s * LANES:(s + 1) * LANES]
        y_scr[s] = ys * q_scale if s * LANES < q_cols else ys
    n_g = len(B_GROUPS)
    q_slabs = B_HEADS_PER_GROUP * HEAD_DIM // LANES
    k0 = q_cols // LANES
    for g, (_, dil) in enumerate(B_GROUPS):
        slabs = list(range(g * q_slabs, (g + 1) * q_slabs)) + [k0 + g, k0 + n_g + g]
        for r in range(dil):
            rows = pl.ds(r, tm // dil, stride=dil)
            piece = jnp.concatenate([y_scr[s, rows, :] for s in slabs], axis=1)
            out_refs[g][:, r * B_GROUP_COLS:(r + 1) * B_GROUP_COLS] = piece.astype(BF16)


def _qkv_proj_b(h2, gain, w, *, q_cols, q_scale, tm=1024):
    t, d = h2.shape
    n = w.shape[1]
    assert t % tm == 0 and n == len(B_GROUPS) * B_GROUP_COLS
    assert all((tm // dil) % BF16_SUBLANES == 0 for _, dil in B_GROUPS)
    return pl.pallas_call(
        functools.partial(_qkv_b_body, q_cols=q_cols, q_scale=q_scale),
        grid=(t // tm,),
        in_specs=[pl.BlockSpec((tm, d), lambda i: (i, 0)),
                  pl.BlockSpec((1, d), lambda i: (0, 0)),
                  pl.BlockSpec((d, n), lambda i: (0, 0))],
        out_specs=[pl.BlockSpec((tm // dil, dil * B_GROUP_COLS), lambda i: (i, 0)) for _, dil in B_GROUPS],
        out_shape=[jax.ShapeDtypeStruct((t // dil, dil * B_GROUP_COLS), BF16) for _, dil in B_GROUPS],
        scratch_shapes=[pltpu.VMEM((n // LANES, tm, LANES), F32)],
        compiler_params=_params("parallel"),
        name="qkv_proj_b",
    )(h2, gain, w)


def _rope_tables(seq):
    inv_freq = ROPE_THETA ** (-jnp.arange(0, HALF_HEAD, 2, dtype=F32) / HALF_HEAD)
    rows = seq // GRID_W
    ang_row = jnp.arange(rows, dtype=F32)[:, None] * inv_freq
    ang_col = jnp.arange(GRID_W, dtype=F32)[:, None] * inv_freq
    per_row = lambda x: jnp.repeat(x, GRID_W, axis=0)
    per_col = lambda x: jnp.tile(x, (rows, 1))
    cos = jnp.concatenate([per_row(jnp.cos(ang_row))] * 2 + [per_col(jnp.cos(ang_col))] * 2, axis=1)
    sin_r, sin_c = per_row(jnp.sin(ang_row)), per_col(jnp.sin(ang_col))
    sin = jnp.concatenate([-sin_r, sin_r, -sin_c, sin_c], axis=1)
    return jnp.concatenate([cos, cos], axis=1), jnp.concatenate([sin, sin], axis=1)


def _attn_a_body(q_ref, k_ref, v_ref, o_ref, klo, khi, vt1, s_buf, *, tq, tk, seq, tiles_per_step):
    def fill(head_in_high_lanes):
        low = lax.broadcasted_iota(jnp.int32, (seq, LANES), 1) < HEAD_DIM
        zero = jnp.zeros((seq, LANES), BF16)
        x = k_ref[0]
        swapped = jnp.concatenate([x[:, HEAD_DIM:], x[:, :HEAD_DIM]], axis=1)
        in_low, in_high = (swapped, x) if head_in_high_lanes else (x, swapped)
        klo[...] = jnp.where(low, in_low, zero)
        khi[...] = jnp.where(low, zero, in_high)
        ones_row = (lax.broadcasted_iota(jnp.int32, (VT_ROWS - HEAD_DIM, tk), 0) == 0).astype(F32)
        for c in range(seq // tk):
            xt = v_ref[0, c * tk:(c + 1) * tk, :].astype(F32).T
            vt = xt[HEAD_DIM:] if head_in_high_lanes else xt[:HEAD_DIM]
            vt1[:, c * tk:(c + 1) * tk] = jnp.concatenate([vt, ones_row], axis=0).astype(BF16)

    odd = (pl.program_id(1) % 2) == 1
    pl.when(jnp.logical_not(odd))(lambda: fill(False))
    pl.when(odd)(lambda: fill(True))

    cols = 2 * tq
    n_chunks = seq // tk
    n_tiles = seq // tq

    def scores(tile, c, slot):
        q = q_ref[0, pl.ds(pl.multiple_of(tile * tq, tq), tq), :]
        q2 = jnp.concatenate([q[:, :LANES], q[:, LANES:]], axis=0)
        cms = []
        for par, k_s in enumerate((klo, khi)):
            s = _nt_dot(k_s[c * tk:(c + 1) * tk, :], q2)
            s_buf[slot, par] = s
            cms.append(jnp.max(s, axis=0, keepdims=True))
        return tuple(cms)

    def accumulate(c, slot, cms, state):
        vt = vt1[:, c * tk:(c + 1) * tk]
        new = []
        for par in range(2):
            m, acc = state[2 * par], state[2 * par + 1]
            mn = jnp.maximum(m, cms[par])
            p = jnp.exp2(s_buf[slot, par] - mn)
            new += [mn, acc * jnp.exp2(m - mn) + jnp.dot(vt, p.astype(BF16), preferred_element_type=F32)]
        return tuple(new)

    neg = jnp.full((1, cols), -jnp.inf, F32)
    acc0 = jnp.zeros((VT_ROWS, cols), F32)

    def step(j, cms):
        for t in range(tiles_per_step):
            tile = j * tiles_per_step + t
            state = (neg, acc0, neg, acc0)
            for c in range(n_chunks):
                n = t * n_chunks + c
                nxt_tile = tile if c + 1 < n_chunks else jnp.minimum(tile + 1, n_tiles - 1)
                cms_next = scores(nxt_tile, (c + 1) % n_chunks, (n + 1) % 2)
                state = accumulate(c, n % 2, cms, state)
                cms = cms_next
            out = jnp.concatenate(
                [acc[:HEAD_DIM] / acc[HEAD_DIM:HEAD_DIM + 1] for acc in (state[1], state[3])], axis=0)
            o_ref[0, pl.ds(pl.multiple_of(tile * tq, tq), tq), :] = jnp.concatenate(
                [out[:, :tq].T, out[:, tq:].T], axis=1).astype(o_ref.dtype)
        return cms

    lax.fori_loop(0, n_tiles // tiles_per_step, step, scores(0, 0, 0))


def _attn_a(qkv, *, n_heads, tq=256, tk=512, tiles_per_step=4):
    b, seq, _ = qkv.shape
    kv_heads = n_heads // A_REP
    q_cols = n_heads * HEAD_DIM
    q_w = A_REP * HEAD_DIM
    k_blk0 = q_cols // LANES
    v_blk0 = (q_cols + kv_heads * HEAD_DIM) // LANES
    assert (seq // tk) % 2 == 0 and seq % (tq * tiles_per_step) == 0
    return pl.pallas_call(
        functools.partial(_attn_a_body, tq=tq, tk=tk, seq=seq, tiles_per_step=tiles_per_step),
        grid=(b, kv_heads),
        in_specs=[pl.BlockSpec((1, seq, q_w), lambda bi, h: (bi, 0, h)),
                  pl.BlockSpec((1, seq, LANES), lambda bi, h: (bi, 0, k_blk0 + h // 2)),
                  pl.BlockSpec((1, seq, LANES), lambda bi, h: (bi, 0, v_blk0 + h // 2))],
        out_specs=pl.BlockSpec((1, seq, q_w), lambda bi, h: (bi, 0, h)),
        out_shape=jax.ShapeDtypeStruct((b, seq, q_cols), BF16),
        scratch_shapes=[pltpu.VMEM((seq, LANES), BF16), pltpu.VMEM((seq, LANES), BF16),
                        pltpu.VMEM((VT_ROWS, seq), BF16), pltpu.VMEM((2, 2, tk, 2 * tq), F32)],
        compiler_params=_params("parallel", "arbitrary"),
        name="attn_a",
    )(qkv, qkv, qkv)


def _fill_band_bias(bias_ref, slopes, half_window, dist_scale):
    span = bias_ref.shape[2]
    c = lax.broadcasted_iota(jnp.int32, (span, BLOCK_Q), 0)
    r = lax.broadcasted_iota(jnp.int32, (span, BLOCK_Q), 1)
    dist = jnp.abs(c - half_window - r)
    inside = dist <= half_window
    penalty = dist.astype(F32) * (dist_scale * LOG2E)
    for variant, ok in enumerate((inside & (c >= half_window), inside, inside & (c < half_window + BLOCK_Q))):
        for h, slope in enumerate(slopes):
            bias_ref[variant, h] = jnp.where(ok, -slope * penalty, NEG_BIG)


def _band_pipeline(jobs, bias_ref, s_buf):
    def scores(job, slot):
        kmat, q_rows, head_ids, _, _, variant, sink_row = job
        width = len(head_ids) * BLOCK_Q
        s = _nt_dot(kmat, q_rows) + jnp.concatenate([bias_ref[variant, h] for h in head_ids], axis=1)
        s_buf[slot, :, :width] = s
        m = jnp.max(s, axis=0, keepdims=True)
        return m if sink_row is None else jnp.maximum(m, sink_row)

    def values(job, slot, m):
        _, _, head_ids, kv_ids, vt1s, _, sink_row = job
        n = len(head_ids)
        p = jnp.exp2(s_buf[slot, :, :n * BLOCK_Q] - m).astype(BF16)
        accs, start = [], 0
        while start < n:
            stop = start
            while stop < n and kv_ids[stop] == kv_ids[start]:
                stop += 1
            accs.append(jnp.dot(vt1s[kv_ids[start]], p[:, start * BLOCK_Q:stop * BLOCK_Q],
                                preferred_element_type=F32))
            start = stop
        acc = accs[0] if len(accs) == 1 else jnp.concatenate(accs, axis=1)
        den = acc[HEAD_DIM:HEAD_DIM + 1]
        if sink_row is not None:
            den = den + jnp.exp2(sink_row - m)
        return acc[:HEAD_DIM] / den, m, den

    results = []
    n_slots = s_buf.shape[0]
    ahead = n_slots - 1
    maxima = [scores(jobs[c], c % n_slots) for c in range(min(ahead, len(jobs)))]
    for c, job in enumerate(jobs):
        if c + ahead < len(jobs):
            maxima.append(scores(jobs[c + ahead], (c + ahead) % n_slots))
        results.append(values(job, c % n_slots, maxima[c]))
    return results


def _band_operands(k_refs, v_refs):
    k = jnp.concatenate([r[0] for r in k_refs], axis=0)
    vt = jnp.concatenate([r[0] for r in v_refs], axis=0).astype(F32).T
    return k, vt


def _vt_with_ones(vt, kv, start, span):
    ones_row = (lax.broadcasted_iota(jnp.int32, (VT_ROWS - HEAD_DIM, span), 0) == 0).astype(F32)
    rows = vt[kv * HEAD_DIM:(kv + 1) * HEAD_DIM, start:start + span]
    return jnp.concatenate([rows, ones_row], axis=0).astype(BF16)


def _swap_halves(x):
    return jnp.concatenate([x[:, HEAD_DIM:], x[:, :HEAD_DIM]], axis=1)


def _attn_c_body(sink_ref, q_ref, kp_ref, kc_ref, kn_ref, vp_ref, vc_ref, vn_ref, o_ref, bias_ref, s_buf,
                 *, n_heads, slopes):
    i = pl.program_id(1)
    pl.when(i == 0)(lambda: _fill_band_bias(bias_ref, slopes, C_WINDOW, 1.0))
    n_blocks = C_TILE // BLOCK_Q
    block_variants = ([jnp.where(i == 0, 0, 1)] + [1] * (n_blocks - 2)
                      + [jnp.where(i == pl.num_programs(1) - 1, 2, 1)])
    q = q_ref[0]
    k, vt = _band_operands((kp_ref, kc_ref, kn_ref), (vp_ref, vc_ref, vn_ref))
    span = BLOCK_Q + 2 * C_WINDOW
    low = lax.broadcasted_iota(jnp.int32, (BLOCK_Q, LANES), 1) < HEAD_DIM
    zero = jnp.zeros((BLOCK_Q, LANES), BF16)
    first_block = lax.broadcasted_iota(jnp.int32, (1, 2 * BLOCK_Q), 1) < BLOCK_Q
    jobs, job_block = [], []
    for blk in range(n_blocks):
        q_lo = blk * BLOCK_Q
        for pair in range(n_heads // A_REP // 2):
            ks = k[q_lo:q_lo + span, pair * LANES:(pair + 1) * LANES]
            ks_swapped = _swap_halves(ks)
            for which in range(2):
                kv = 2 * pair + which
                vt1s = {kv: _vt_with_ones(vt, kv, q_lo, span)}
                slabs = [q[q_lo:q_lo + BLOCK_Q, (2 * kv + j) * LANES:(2 * kv + j + 1) * LANES] for j in range(2)]
                for parity in range(2):
                    q_rows = jnp.concatenate(
                        [jnp.where(low, s_, zero) if parity == 0 else jnp.where(low, zero, s_) for s_ in slabs],
                        axis=0)
                    heads = (A_REP * kv + parity, A_REP * kv + 2 + parity)
                    sink_row = jnp.where(first_block, sink_ref[heads[0]], sink_ref[heads[1]]) * LOG2E
                    jobs.append((ks if parity == which else ks_swapped, q_rows, heads, (kv, kv), vt1s,
                                 block_variants[blk], sink_row))
                    job_block.append(blk)
    o_t = {}
    for job, blk, (ot, _, _) in zip(jobs, job_block, _band_pipeline(jobs, bias_ref, s_buf)):
        o_t[blk, job[2][0]], o_t[blk, job[2][1]] = ot[:, :BLOCK_Q], ot[:, BLOCK_Q:]
    for blk in range(n_blocks):
        for j in range(n_heads // 2):
            pair_t = jnp.concatenate([o_t[blk, 2 * j], o_t[blk, 2 * j + 1]], axis=0)
            o_ref[0, blk * BLOCK_Q:(blk + 1) * BLOCK_Q, j * LANES:(j + 1) * LANES] = pair_t.T.astype(o_ref.dtype)


def _alibi_slopes(n):
    return [2.0 ** (-8.0 * (i + 1) / n) for i in range(n)]


def _attn_c(qkv, sinks, *, n_heads):
    b, seq, _ = qkv.shape
    kv_heads = n_heads // A_REP
    q_cols = n_heads * HEAD_DIM
    kv_cols = kv_heads * HEAD_DIM
    assert seq % C_TILE == 0 and C_TILE >= 2 * BLOCK_Q
    n_tiles = seq // C_TILE
    n_edges = seq // C_WINDOW
    edges_per_tile = C_TILE // C_WINDOW
    k_blk = q_cols // kv_cols
    v_blk = k_blk + 1
    span = BLOCK_Q + 2 * C_WINDOW

    def kv_specs(col_blk):
        edge = lambda offset: pl.BlockSpec(
            (1, C_WINDOW, kv_cols),
            lambda bi, i: (bi, jnp.clip(i * edges_per_tile + offset, 0, n_edges - 1), col_blk))
        return [edge(-1), pl.BlockSpec((1, C_TILE, kv_cols), lambda bi, i: (bi, i, col_blk)),
                edge(edges_per_tile)]

    return pl.pallas_call(
        functools.partial(_attn_c_body, n_heads=n_heads, slopes=_alibi_slopes(n_heads)),
        grid=(b, n_tiles),
        in_specs=[pl.BlockSpec(memory_space=pltpu.SMEM),
                  pl.BlockSpec((1, C_TILE, q_cols), lambda bi, i: (bi, i, 0))] + kv_specs(k_blk) + kv_specs(v_blk),
        out_specs=pl.BlockSpec((1, C_TILE, q_cols), lambda bi, i: (bi, i, 0)),
        out_shape=jax.ShapeDtypeStruct((b, seq, q_cols), BF16),
        scratch_shapes=[pltpu.VMEM((3, n_heads, span, BLOCK_Q), F32),
                        pltpu.VMEM((BAND_SLOTS, span, 2 * BLOCK_Q), F32)],
        compiler_params=_params("parallel", "arbitrary"),
        name="attn_c",
    )(sinks, qkv, qkv, qkv, qkv, qkv, qkv, qkv)


def _attn_b_body(q0_ref, q1_ref, q2_ref, kp_ref, kc_ref, kn_ref, vp_ref, vc_ref, vn_ref, o_ref, lse_ref,
                 bias_ref, s_buf, *, slopes, dil):
    i = pl.program_id(2)
    pl.when(jnp.logical_and(pl.program_id(1) == 0, i == 0))(
        lambda: _fill_band_bias(bias_ref, slopes, B_HALF, float(dil)))
    n_blocks = q0_ref.shape[1] // BLOCK_Q
    block_variants = ([jnp.where(i == 0, 0, 1)] + [1] * (n_blocks - 2)
                      + [jnp.where(i == pl.num_programs(2) - 1, 2, 1)])
    k, vt = _band_operands((kp_ref, kc_ref, kn_ref), (vp_ref, vc_ref, vn_ref))
    low = lax.broadcasted_iota(jnp.int32, (BLOCK_Q, LANES), 1) < HEAD_DIM
    zero = jnp.zeros((BLOCK_Q, LANES), BF16)
    slabs = [q0_ref[0], q1_ref[0], q2_ref[0]]
    heads_all = range(B_HEADS_PER_GROUP)
    plain = [h for h in heads_all if h % 2 == h // B_REP]
    crossed = [h for h in heads_all if h % 2 != h // B_REP]
    jobs, job_block = [], []
    for blk in range(n_blocks):
        q_lo = blk * BLOCK_Q
        ks = k[q_lo:q_lo + B_SPAN]
        ks_swapped = _swap_halves(ks)
        vt1s = {kv: _vt_with_ones(vt, kv, q_lo, B_SPAN) for kv in range(B_KV_PER_GROUP)}
        for kmat, heads in ((ks, plain[:2]), (ks_swapped, crossed), (ks, plain[2:])):
            q_rows = jnp.concatenate(
                [jnp.where(low, slabs[h // 2][q_lo:q_lo + BLOCK_Q], zero) if h % 2 == 0
                 else jnp.where(low, zero, slabs[h // 2][q_lo:q_lo + BLOCK_Q]) for h in heads], axis=0)
            jobs.append((kmat, q_rows, heads, [h // B_REP for h in heads], vt1s, block_variants[blk], None))
            job_block.append(blk)
    o_t, lse_t = {}, {}
    for job, blk, (ot, m, den) in zip(jobs, job_block, _band_pipeline(jobs, bias_ref, s_buf)):
        lse = (m + jnp.log2(den)) * LN2
        for n, h in enumerate(job[2]):
            o_t[blk, h] = ot[:, n * BLOCK_Q:(n + 1) * BLOCK_Q]
            lse_t[blk, h] = jnp.broadcast_to(lse[:, n * BLOCK_Q:(n + 1) * BLOCK_Q], (HEAD_DIM, BLOCK_Q))
    for blk in range(n_blocks):
        rows = slice(blk * BLOCK_Q, (blk + 1) * BLOCK_Q)
        for j in range(B_HEADS_PER_GROUP // 2):
            cols = slice(j * LANES, (j + 1) * LANES)
            pair = lambda t: jnp.concatenate([t[blk, 2 * j], t[blk, 2 * j + 1]], axis=0).T
            o_ref[0, rows, cols] = pair(o_t).astype(o_ref.dtype)
            lse_ref[0, rows, cols] = pair(lse_t)


def _attn_b(group_qkv, b):
    n_g = len(B_GROUPS)
    gq = B_HEADS_PER_GROUP * HEAD_DIM
    slabs_per_token = B_GROUP_COLS // LANES
    q_slabs = gq // LANES
    all_slopes = _alibi_slopes(n_g * B_HEADS_PER_GROUP)
    outs, lses = [], []
    for g, (window, dil) in enumerate(B_GROUPS):
        assert (window // 2) // dil == B_HALF
        sub = group_qkv[g].shape[0] // b
        tile = min(B_TILE, sub)
        assert tile >= 2 * BLOCK_Q and sub % tile == 0
        n_tiles = sub // tile
        n_halves = sub // B_HALF
        halves_per_tile = tile // B_HALF
        view = group_qkv[g].reshape(b, sub, dil * B_GROUP_COLS)

        def tile_spec(slab, tile=tile):
            return pl.BlockSpec((1, tile, LANES), lambda bi, r, i: (bi, i, r * slabs_per_token + slab))

        def edge_spec(slab, offset, n_halves=n_halves, halves_per_tile=halves_per_tile):
            return pl.BlockSpec(
                (1, B_HALF, LANES),
                lambda bi, r, i: (bi, jnp.clip(i * halves_per_tile + offset, 0, n_halves - 1),
                                  r * slabs_per_token + slab))

        def kv_specs(slab):
            return [edge_spec(slab, -1), tile_spec(slab), edge_spec(slab, halves_per_tile)]

        out_map = lambda bi, r, i: (bi, i, r)
        o, lse = pl.pallas_call(
            functools.partial(_attn_b_body, dil=dil,
                              slopes=all_slopes[g * B_HEADS_PER_GROUP:(g + 1) * B_HEADS_PER_GROUP]),
            grid=(b, dil, n_tiles),
            in_specs=[tile_spec(j) for j in range(q_slabs)] + kv_specs(q_slabs) + kv_specs(q_slabs + 1),
            out_specs=[pl.BlockSpec((1, tile, gq), out_map), pl.BlockSpec((1, tile, gq), out_map)],
            out_shape=[jax.ShapeDtypeStruct((b, sub, dil * gq), BF16),
                       jax.ShapeDtypeStruct((b, sub, dil * gq), F32)],
            scratch_shapes=[pltpu.VMEM((3, B_HEADS_PER_GROUP, B_SPAN, BLOCK_Q), F32),
                            pltpu.VMEM((BAND_SLOTS, B_SPAN, 2 * BLOCK_Q), F32)],
            compiler_params=_params("parallel", "arbitrary", "arbitrary"),
            name=f"attn_b_g{g}",
        )(*([view] * (q_slabs + 6)))
        outs.append(o.reshape(b * sub, dil * gq))
        lses.append(lse.reshape(b * sub, dil * gq))
    return outs, lses


PROLOGUE_ROW_CHUNKS = 4


def _mix_plain(refs, scratch):
    return lambda rows: refs[0][rows, :]


def _mix_groups(refs, scratch):
    n_g = len(B_GROUPS)
    o_scr, l_scr = scratch
    slabs = o_scr.shape[0] // n_g
    tm = o_scr.shape[1]
    gq = slabs * LANES
    for g, (_, dil) in enumerate(B_GROUPS):
        for r in range(dil):
            rows = pl.ds(r, tm // dil, stride=dil)
            for s in range(slabs):
                cols = slice(r * gq + s * LANES, r * gq + (s + 1) * LANES)
                o_scr[g * slabs + s, rows, :] = refs[g][:, cols].astype(F32)
                l_scr[g * slabs + s, rows, :] = refs[n_g + g][:, cols]
    def lhs(rows):
        group = lambda scr, g: jnp.concatenate([scr[g * slabs + s, rows, :] for s in range(slabs)], axis=1)
        ls = [group(l_scr, g) for g in range(n_g)]
        mx = functools.reduce(jnp.maximum, ls)
        es = [jnp.exp(l - mx) for l in ls]
        inv = 1.0 / functools.reduce(lambda a, b: a + b, es)
        return jnp.concatenate([(group(o_scr, g) * (es[g] * inv)).astype(BF16) for g in range(n_g)], axis=1)

    return lhs


MLP_HIDDEN_CHUNK = 1024


def _block_body(*refs, n_mix, mix_fn, final_norm):
    h_ref, mix_refs = refs[0], refs[1:1 + n_mix]
    wo_ref, g_ref, w1_ref, w2_ref, fg_ref, out_ref, hn_ref = refs[1 + n_mix:8 + n_mix]
    mix_scratch = refs[8 + n_mix:]

    lhs = mix_fn(mix_refs, mix_scratch)
    chunk = h_ref.shape[0] // PROLOGUE_ROW_CHUNKS
    rows = [slice(c * chunk, (c + 1) * chunk) for c in range(PROLOGUE_ROW_CHUNKS)]
    project = lambda c: jnp.dot(lhs(rows[c]), wo_ref[...], preferred_element_type=F32)
    nxt = project(0)
    for c in range(PROLOGUE_ROW_CHUNKS):
        cur = nxt
        if c + 1 < PROLOGUE_ROW_CHUNKS:
            nxt = project(c + 1)
        h1 = h_ref[rows[c], :] + cur
        out_ref[rows[c], :] = h1
        hn_ref[rows[c], :] = _rms_rows(h1, g_ref[...]).astype(BF16)

    hn = hn_ref[...]
    for c in range(w1_ref.shape[1] // MLP_HIDDEN_CHUNK):
        cols = slice(c * MLP_HIDDEN_CHUNK, (c + 1) * MLP_HIDDEN_CHUNK)
        u = jnp.maximum(jnp.dot(hn, w1_ref[:, cols], preferred_element_type=F32), 0.0)
        out_ref[...] += jnp.dot((u * u).astype(BF16), w2_ref[cols, :], preferred_element_type=F32)

    if final_norm:
        out_ref[...] = _rms_rows(out_ref[...], fg_ref[...])


def _block(h2, mix_inputs, w_o, *, gain, w1, w2, layer, final_gain, final_norm, tm=512):
    t, d = h2.shape
    assert t % tm == 0 and tm % PROLOGUE_ROW_CHUNKS == 0 and w1.shape[2] % MLP_HIDDEN_CHUNK == 0
    row_tile = lambda cols: pl.BlockSpec((tm, cols), lambda i: (i, 0))
    whole = lambda a: pl.BlockSpec(a.shape, lambda i: (0, 0), pipeline_mode=pl.Buffered(1))
    of_layer = lambda a: pl.BlockSpec((None,) + a.shape[1:], lambda i: (layer, 0, 0),
                                      pipeline_mode=pl.Buffered(1))
    scratch = [pltpu.VMEM((tm, d), BF16)]
    if len(mix_inputs) == 1:
        mix_fn, mix_specs = _mix_plain, [row_tile(mix_inputs[0].shape[1])]
    else:
        n_g = len(B_GROUPS)
        gq = B_HEADS_PER_GROUP * HEAD_DIM
        mix_fn = _mix_groups
        mix_specs = [pl.BlockSpec((tm // dil, dil * gq), lambda i: (i, 0)) for _, dil in B_GROUPS] * 2
        scratch += [pltpu.VMEM((n_g * gq // LANES, tm, LANES), F32)] * 2
    return pl.pallas_call(
        functools.partial(_block_body, n_mix=len(mix_inputs), mix_fn=mix_fn, final_norm=final_norm),
        grid=(t // tm,),
        in_specs=[row_tile(d)] + mix_specs
                 + [whole(w_o), whole(gain), of_layer(w1), of_layer(w2), whole(final_gain)],
        out_specs=row_tile(d),
        out_shape=jax.ShapeDtypeStruct((t, d), F32),
        scratch_shapes=scratch,
        compiler_params=_params("parallel"),
        name="block_mlp",
    )(h2, *mix_inputs, w_o, gain, w1, w2, final_gain)


def kernel(x, attn_norm, mlp_norm, a_w_qkv, a_q_gain, a_k_gain, a_w_o, b_w_qkv, b_w_o,
           c_w_qkv, c_sinks, c_w_o, mlp_w1, mlp_w2, final_norm):
    b, seq, d = x.shape
    depth = attn_norm.shape[0]
    h = x.reshape(b * seq, d)
    scale = HEAD_DIM ** -0.5 * LOG2E
    a_heads = a_w_o.shape[1] // HEAD_DIM
    c_heads = c_w_o.shape[1] // HEAD_DIM
    cos, sin = _rope_tables(seq)
    head = jnp.arange(2 * LANES) // HEAD_DIM
    seg = (head[:, None] == head[None, :]).astype(BF16)
    tile4 = lambda g: jnp.tile(g, 2 * LANES // HEAD_DIM)[None, :]
    w1_all, w2_all = mlp_w1.astype(BF16), mlp_w2.astype(BF16)
    used = [0, 0, 0]
    for layer in range(depth):
        kind = layer % N_MIXERS
        j = used[kind]
        used[kind] += 1
        gain = attn_norm[layer][None, :]
        block = functools.partial(_block, gain=mlp_norm[layer][None, :], w1=w1_all, w2=w2_all, layer=layer,
                                  final_gain=final_norm[None, :], final_norm=(layer == depth - 1))
        if kind == 0:
            q_cols = a_heads * HEAD_DIM
            k_cols = (a_w_qkv.shape[2] - q_cols) // 2
            qkv = _qkv_proj(h, gain, a_w_qkv[j].astype(BF16), q_cols=q_cols, q_scale=scale,
                            rope=(tile4(a_q_gain[j]), tile4(a_k_gain[j]), cos, sin, seg, k_cols))
            o = _attn_a(qkv.reshape(b, seq, -1), n_heads=a_heads).reshape(b * seq, q_cols)
            h = block(h, [o], a_w_o[j].astype(BF16))
        elif kind == 1:
            q_cols = len(B_GROUPS) * B_HEADS_PER_GROUP * HEAD_DIM
            groups = _qkv_proj_b(h, gain, b_w_qkv[j].astype(BF16), q_cols=q_cols, q_scale=scale)
            outs, lses = _attn_b(groups, b)
            h = block(h, outs + lses, b_w_o[j].astype(BF16))
        else:
            q_cols = c_heads * HEAD_DIM
            qkv = _qkv_proj(h, gain, c_w_qkv[j].astype(BF16), q_cols=q_cols, q_scale=scale)
            o = _attn_c(qkv.reshape(b, seq, -1), c_sinks[j], n_heads=c_heads).reshape(b * seq, q_cols)
            h = block(h, [o], c_w_o[j].astype(BF16))
    return h.reshape(b, seq, d)
```

```python
import functools
import math

import jax
import jax.numpy as jnp
from jax import lax
from jax.experimental import pallas as pl
from jax.experimental.pallas import tpu as pltpu

HEAD_DIM = 64
HALF_HEAD = HEAD_DIM // 2
RMS_EPS = 1e-6
GRID_W = 64
ROPE_THETA = 10000.0
N_MIXERS = 3
A_REP = 4
B_GROUPS = ((128, 1), (512, 4), (2048, 16))
B_HEADS_PER_GROUP = 6
B_KV_PER_GROUP = 2
B_REP = B_HEADS_PER_GROUP // B_KV_PER_GROUP
C_WINDOW = 128
LANES = 128
BF16_SUBLANES = 16
BLOCK_Q = 128
BAND_SLOTS = 4
C_TILE = 4 * BLOCK_Q
B_HALF = BLOCK_Q // 2
B_TILE = 4 * BLOCK_Q
B_SPAN = BLOCK_Q + 2 * B_HALF
B_GROUP_COLS = (B_HEADS_PER_GROUP + 2 * B_KV_PER_GROUP) * HEAD_DIM
VT_ROWS = HEAD_DIM + BF16_SUBLANES
V7X_VMEM_LIMIT_BYTES = 48 * 1024 * 1024
NEG_BIG = -1e30
LOG2E = math.log2(math.e)
LN2 = math.log(2.0)

BF16 = jnp.bfloat16
F32 = jnp.float32


def _params(*sem):
    return pltpu.CompilerParams(dimension_semantics=sem, vmem_limit_bytes=V7X_VMEM_LIMIT_BYTES)


def _nt_dot(a, b):
    return lax.dot_general(a, b, (((1,), (1,)), ((), ())), preferred_element_type=F32)


def _rms_rows(x, gain):
    ms = jnp.mean(x * x, axis=-1, keepdims=True)
    return x * lax.rsqrt(ms + RMS_EPS) * gain


def _qkv_body(x_ref, g_ref, w_ref, o_ref, *, q_cols, q_scale):
    hn = _rms_rows(x_ref[...], g_ref[...]).astype(BF16)
    y = jnp.dot(hn, w_ref[...], preferred_element_type=F32)
    o_ref[:, :q_cols] = (y[:, :q_cols] * q_scale).astype(o_ref.dtype)
    o_ref[:, q_cols:] = y[:, q_cols:].astype(o_ref.dtype)


def _qkv_rope_body(x_ref, g_ref, w_ref, qg_ref, kg_ref, cos_ref, sin_ref, seg_ref, o_ref,
                   *, q_cols, k_cols, q_scale):
    hn = _rms_rows(x_ref[...], g_ref[...]).astype(BF16)
    slab = 2 * LANES
    cos = jnp.concatenate([cos_ref[...], cos_ref[...]], axis=1)
    sin = jnp.concatenate([sin_ref[...], sin_ref[...]], axis=1)
    lane = lax.broadcasted_iota(jnp.int32, (x_ref.shape[0], slab), 1)
    first_half = (lane % HALF_HEAD) < (HALF_HEAD // 2)
    seg = seg_ref[...]
    n_qk = (q_cols + k_cols) // slab
    project = lambda s: jnp.dot(hn, w_ref[:, s * slab:(s + 1) * slab], preferred_element_type=F32)
    y_next = project(0)
    for s in range(n_qk):
        is_q = s * slab < q_cols
        ys = y_next
        y_next = project(s + 1) if s + 1 < n_qk else jnp.dot(
            hn, w_ref[:, q_cols + k_cols:], preferred_element_type=F32)
        sq = ys * ys
        hi = sq.astype(BF16)
        lo = (sq - hi.astype(F32)).astype(BF16)
        ss = jnp.dot(hi, seg, preferred_element_type=F32) + jnp.dot(lo, seg, preferred_element_type=F32)
        gain = qg_ref[...] if is_q else kg_ref[...]
        yn = ys * lax.rsqrt(ss * (1.0 / HEAD_DIM) + RMS_EPS) * gain
        partner = jnp.where(first_half,
                            pltpu.roll(yn, slab - HALF_HEAD // 2, axis=1),
                            pltpu.roll(yn, HALF_HEAD // 2, axis=1))
        r = yn * cos + partner * sin
        if is_q:
            r = r * q_scale
        o_ref[:, s * slab:(s + 1) * slab] = r.astype(o_ref.dtype)
    o_ref[:, q_cols + k_cols:] = y_next.astype(o_ref.dtype)


def _qkv_proj(h2, gain, w, *, q_cols, q_scale, tm=1024, rope=None):
    t, d = h2.shape
    n = w.shape[1]
    assert t % tm == 0 and q_cols % (2 * LANES) == 0 and n % LANES == 0
    common = dict(
        grid=(t // tm,),
        out_specs=pl.BlockSpec((tm, n), lambda i: (i, 0)),
        out_shape=jax.ShapeDtypeStruct((t, n), BF16),
        compiler_params=_params("parallel"),
    )
    x_spec = pl.BlockSpec((tm, d), lambda i: (i, 0))
    g_spec = pl.BlockSpec((1, d), lambda i: (0, 0))
    w_spec = pl.BlockSpec((d, n), lambda i: (0, 0))
    if rope is None:
        return pl.pallas_call(
            functools.partial(_qkv_body, q_cols=q_cols, q_scale=q_scale),
            in_specs=[x_spec, g_spec, w_spec], name="qkv_proj", **common,
        )(h2, gain, w)
    qg, kg, cos, sin, seg, k_cols = rope
    seq_tiles = cos.shape[0] // tm
    slab = 2 * LANES
    return pl.pallas_call(
        functools.partial(_qkv_rope_body, q_cols=q_cols, k_cols=k_cols, q_scale=q_scale),
        in_specs=[x_spec, g_spec, w_spec,
                  pl.BlockSpec((1, slab), lambda i: (0, 0)),
                  pl.BlockSpec((1, slab), lambda i: (0, 0)),
                  pl.BlockSpec((tm, LANES), lambda i: (i % seq_tiles, 0)),
                  pl.BlockSpec((tm, LANES), lambda i: (i % seq_tiles, 0)),
                  pl.BlockSpec((slab, slab), lambda i: (0, 0))],
        name="qkv_proj_rope", **common,
    )(h2, gain, w, qg, kg, cos, sin, seg)


def _qkv_b_body(x_ref, g_ref, w_ref, *rest, q_cols, q_scale):
    out_refs, y_scr = rest[:-1], rest[-1]
    tm = x_ref.shape[0]
    hn = _rms_rows(x_ref[...], g_ref[...]).astype(BF16)
    y = jnp.dot(hn, w_ref[...], preferred_element_type=F32)
    for s in range(y_scr.shape[0]):
        ys = y[:, s * LANES:(s + 1) * LANES]
        y_scr[s] = ys * q_scale if s * LANES < q_cols else ys
    n_g = len(B_GROUPS)
    q_slabs = B_HEADS_PER_GROUP * HEAD_DIM // LANES
    k0 = q_cols // LANES
    for g, (_, dil) in enumerate(B_GROUPS):
        slabs = list(range(g * q_slabs, (g + 1) * q_slabs)) + [k0 + g, k0 + n_g + g]
        for r in range(dil):
            rows = pl.ds(r, tm // dil, stride=dil)
            piece = jnp.concatenate([y_scr[s, rows, :] for s in slabs], axis=1)
            out_refs[g][:, r * B_GROUP_COLS:(r + 1) * B_GROUP_COLS] = piece.astype(BF16)


def _qkv_proj_b(h2, gain, w, *, q_cols, q_scale, tm=1024):
    t, d = h2.shape
    n = w.shape[1]
    assert t % tm == 0 and n == len(B_GROUPS) * B_GROUP_COLS
    assert all((tm // dil) % BF16_SUBLANES == 0 for _, dil in B_GROUPS)
    return pl.pallas_call(
        functools.partial(_qkv_b_body, q_cols=q_cols, q_scale=q_scale),
        grid=(t // tm,),
        in_specs=[pl.BlockSpec((tm, d), lambda i: (i, 0)),
                  pl.BlockSpec((1, d), lambda i: (0, 0)),
                  pl.BlockSpec((d, n), lambda i: (0, 0))],
        out_specs=[pl.BlockSpec((tm // dil, dil * B_GROUP_COLS), lambda i: (i, 0)) for _, dil in B_GROUPS],
        out_shape=[jax.ShapeDtypeStruct((t // dil, dil * B_GROUP_COLS), BF16) for _, dil in B_GROUPS],
        scratch_shapes=[pltpu.VMEM((n // LANES, tm, LANES), F32)],
        compiler_params=_params("parallel"),
        name="qkv_proj_b",
    )(h2, gain, w)


def _rope_tables(seq):
    inv_freq = ROPE_THETA ** (-jnp.arange(0, HALF_HEAD, 2, dtype=F32) / HALF_HEAD)
    rows = seq // GRID_W
    ang_row = jnp.arange(rows, dtype=F32)[:, None] * inv_freq
    ang_col = jnp.arange(GRID_W, dtype=F32)[:, None] * inv_freq
    per_row = lambda x: jnp.repeat(x, GRID_W, axis=0)
    per_col = lambda x: jnp.tile(x, (rows, 1))
    cos = jnp.concatenate([per_row(jnp.cos(ang_row))] * 2 + [per_col(jnp.cos(ang_col))] * 2, axis=1)
    sin_r, sin_c = per_row(jnp.sin(ang_row)), per_col(jnp.sin(ang_col))
    sin = jnp.concatenate([-sin_r, sin_r, -sin_c, sin_c], axis=1)
    return jnp.concatenate([cos, cos], axis=1), jnp.concatenate([sin, sin], axis=1)


def _attn_a_body(q_ref, k_ref, v_ref, o_ref, klo, khi, vt1, s_buf, *, tq, tk, seq, tiles_per_step):
    def fill(head_in_high_lanes):
        low = lax.broadcasted_iota(jnp.int32, (seq, LANES), 1) < HEAD_DIM
        zero = jnp.zeros((seq, LANES), BF16)
        x = k_ref[0]
        swapped = jnp.concatenate([x[:, HEAD_DIM:], x[:, :HEAD_DIM]], axis=1)
        in_low, in_high = (swapped, x) if head_in_high_lanes else (x, swapped)
        klo[...] = jnp.where(low, in_low, zero)
        khi[...] = jnp.where(low, zero, in_high)
        ones_row = (lax.broadcasted_iota(jnp.int32, (VT_ROWS - HEAD_DIM, tk), 0) == 0).astype(F32)
        for c in range(seq // tk):
            xt = v_ref[0, c * tk:(c + 1) * tk, :].astype(F32).T
            vt = xt[HEAD_DIM:] if head_in_high_lanes else xt[:HEAD_DIM]
            vt1[:, c * tk:(c + 1) * tk] = jnp.concatenate([vt, ones_row], axis=0).astype(BF16)

    odd = (pl.program_id(1) % 2) == 1
    pl.when(jnp.logical_not(odd))(lambda: fill(False))
    pl.when(odd)(lambda: fill(True))

    cols = 2 * tq
    n_chunks = seq // tk
    n_tiles = seq // tq

    def scores(tile, c, slot):
        q = q_ref[0, pl.ds(pl.multiple_of(tile * tq, tq), tq), :]
        q2 = jnp.concatenate([q[:, :LANES], q[:, LANES:]], axis=0)
        cms = []
        for par, k_s in enumerate((klo, khi)):
            s = _nt_dot(k_s[c * tk:(c + 1) * tk, :], q2)
            s_buf[slot, par] = s
            cms.append(jnp.max(s, axis=0, keepdims=True))
        return tuple(cms)

    def accumulate(c, slot, cms, state):
        vt = vt1[:, c * tk:(c + 1) * tk]
        new = []
        for par in range(2):
            m, acc = state[2 * par], state[2 * par + 1]
            mn = jnp.maximum(m, cms[par])
            p = jnp.exp2(s_buf[slot, par] - mn)
            new += [mn, acc * jnp.exp2(m - mn) + jnp.dot(vt, p.astype(BF16), preferred_element_type=F32)]
        return tuple(new)

    neg = jnp.full((1, cols), -jnp.inf, F32)
    acc0 = jnp.zeros((VT_ROWS, cols), F32)

    def step(j, cms):
        for t in range(tiles_per_step):
            tile = j * tiles_per_step + t
            state = (neg, acc0, neg, acc0)
            for c in range(n_chunks):
                n = t * n_chunks + c
                nxt_tile = tile if c + 1 < n_chunks else jnp.minimum(tile + 1, n_tiles - 1)
                cms_next = scores(nxt_tile, (c + 1) % n_chunks, (n + 1) % 2)
                state = accumulate(c, n % 2, cms, state)
                cms = cms_next
            out = jnp.concatenate(
                [acc[:HEAD_DIM] / acc[HEAD_DIM:HEAD_DIM + 1] for acc in (state[1], state[3])], axis=0)
            o_ref[0, pl.ds(pl.multiple_of(tile * tq, tq), tq), :] = jnp.concatenate(
                [out[:, :tq].T, out[:, tq:].T], axis=1).astype(o_ref.dtype)
        return cms

    lax.fori_loop(0, n_tiles // tiles_per_step, step, scores(0, 0, 0))


def _attn_a(qkv, *, n_heads, tq=256, tk=512, tiles_per_step=4):
    b, seq, _ = qkv.shape
    kv_heads = n_heads // A_REP
    q_cols = n_heads * HEAD_DIM
    q_w = A_REP * HEAD_DIM
    k_blk0 = q_cols // LANES
    v_blk0 = (q_cols + kv_heads * HEAD_DIM) // LANES
    assert (seq // tk) % 2 == 0 and seq % (tq * tiles_per_step) == 0
    return pl.pallas_call(
        functools.partial(_attn_a_body, tq=tq, tk=tk, seq=seq, tiles_per_step=tiles_per_step),
        grid=(b, kv_heads),
        in_specs=[pl.BlockSpec((1, seq, q_w), lambda bi, h: (bi, 0, h)),
                  pl.BlockSpec((1, seq, LANES), lambda bi, h: (bi, 0, k_blk0 + h // 2)),
                  pl.BlockSpec((1, seq, LANES), lambda bi, h: (bi, 0, v_blk0 + h // 2))],
        out_specs=pl.BlockSpec((1, seq, q_w), lambda bi, h: (bi, 0, h)),
        out_shape=jax.ShapeDtypeStruct((b, seq, q_cols), BF16),
        scratch_shapes=[pltpu.VMEM((seq, LANES), BF16), pltpu.VMEM((seq, LANES), BF16),
                        pltpu.VMEM((VT_ROWS, seq), BF16), pltpu.VMEM((2, 2, tk, 2 * tq), F32)],
        compiler_params=_params("parallel", "arbitrary"),
        name="attn_a",
    )(qkv, qkv, qkv)


def _fill_band_bias(bias_ref, slopes, half_window, dist_scale):
    span = bias_ref.shape[2]
    c = lax.broadcasted_iota(jnp.int32, (span, BLOCK_Q), 0)
    r = lax.broadcasted_iota(jnp.int32, (span, BLOCK_Q), 1)
    dist = jnp.abs(c - half_window - r)
    inside = dist <= half_window
    penalty = dist.astype(F32) * (dist_scale * LOG2E)
    for variant, ok in enumerate((inside & (c >= half_window), inside, inside & (c < half_window + BLOCK_Q))):
        for h, slope in enumerate(slopes):
            bias_ref[variant, h] = jnp.where(ok, -slope * penalty, NEG_BIG)


def _band_pipeline(jobs, bias_ref, s_buf):
    def scores(job, slot):
        kmat, q_rows, head_ids, _, _, variant, sink_row = job
        width = len(head_ids) * BLOCK_Q
        s = _nt_dot(kmat, q_rows) + jnp.concatenate([bias_ref[variant, h] for h in head_ids], axis=1)
        s_buf[slot, :, :width] = s
        m = jnp.max(s, axis=0, keepdims=True)
        return m if sink_row is None else jnp.maximum(m, sink_row)

    def values(job, slot, m):
        _, _, head_ids, kv_ids, vt1s, _, sink_row = job
        n = len(head_ids)
        p = jnp.exp2(s_buf[slot, :, :n * BLOCK_Q] - m).astype(BF16)
        accs, start = [], 0
        while start < n:
            stop = start
            while stop < n and kv_ids[stop] == kv_ids[start]:
                stop += 1
            accs.append(jnp.dot(vt1s[kv_ids[start]], p[:, start * BLOCK_Q:stop * BLOCK_Q],
                                preferred_element_type=F32))
            start = stop
        acc = accs[0] if len(accs) == 1 else jnp.concatenate(accs, axis=1)
        den = acc[HEAD_DIM:HEAD_DIM + 1]
        if sink_row is not None:
            den = den + jnp.exp2(sink_row - m)
        return acc[:HEAD_DIM] / den, m, den

    results = []
    n_slots = s_buf.shape[0]
    ahead = n_slots - 1
    maxima = [scores(jobs[c], c % n_slots) for c in range(min(ahead, len(jobs)))]
    for c, job in enumerate(jobs):
        if c + ahead < len(jobs):
            maxima.append(scores(jobs[c + ahead], (c + ahead) % n_slots))
        results.append(values(job, c % n_slots, maxima[c]))
    return results


def _band_operands(k_refs, v_refs):
    k = jnp.concatenate([r[0] for r in k_refs], axis=0)
    vt = jnp.concatenate([r[0] for r in v_refs], axis=0).astype(F32).T
    return k, vt


def _vt_with_ones(vt, kv, start, span):
    ones_row = (lax.broadcasted_iota(jnp.int32, (VT_ROWS - HEAD_DIM, span), 0) == 0).astype(F32)
    rows = vt[kv * HEAD_DIM:(kv + 1) * HEAD_DIM, start:start + span]
    return jnp.concatenate([rows, ones_row], axis=0).astype(BF16)


def _swap_halves(x):
    return jnp.concatenate([x[:, HEAD_DIM:], x[:, :HEAD_DIM]], axis=1)


def _attn_c_body(sink_ref, q_ref, kp_ref, kc_ref, kn_ref, vp_ref, vc_ref, vn_ref, o_ref, bias_ref, s_buf,
                 *, n_heads, slopes):
    i = pl.program_id(1)
    pl.when(i == 0)(lambda: _fill_band_bias(bias_ref, slopes, C_WINDOW, 1.0))
    n_blocks = C_TILE // BLOCK_Q
    block_variants = ([jnp.where(i == 0, 0, 1)] + [1] * (n_blocks - 2)
                      + [jnp.where(i == pl.num_programs(1) - 1, 2, 1)])
    q = q_ref[0]
    k, vt = _band_operands((kp_ref, kc_ref, kn_ref), (vp_ref, vc_ref, vn_ref))
    span = BLOCK_Q + 2 * C_WINDOW
    low = lax.broadcasted_iota(jnp.int32, (BLOCK_Q, LANES), 1) < HEAD_DIM
    zero = jnp.zeros((BLOCK_Q, LANES), BF16)
    first_block = lax.broadcasted_iota(jnp.int32, (1, 2 * BLOCK_Q), 1) < BLOCK_Q
    jobs, job_block = [], []
    for blk in range(n_blocks):
        q_lo = blk * BLOCK_Q
        for pair in range(n_heads // A_REP // 2):
            ks = k[q_lo:q_lo + span, pair * LANES:(pair + 1) * LANES]
            ks_swapped = _swap_halves(ks)
            for which in range(2):
                kv = 2 * pair + which
                vt1s = {kv: _vt_with_ones(vt, kv, q_lo, span)}
                slabs = [q[q_lo:q_lo + BLOCK_Q, (2 * kv + j) * LANES:(2 * kv + j + 1) * LANES] for j in range(2)]
                for parity in range(2):
                    q_rows = jnp.concatenate(
                        [jnp.where(low, s_, zero) if parity == 0 else jnp.where(low, zero, s_) for s_ in slabs],
                        axis=0)
                    heads = (A_REP * kv + parity, A_REP * kv + 2 + parity)
                    sink_row = jnp.where(first_block, sink_ref[heads[0]], sink_ref[heads[1]]) * LOG2E
                    jobs.append((ks if parity == which else ks_swapped, q_rows, heads, (kv, kv), vt1s,
                                 block_variants[blk], sink_row))
                    job_block.append(blk)
    o_t = {}
    for job, blk, (ot, _, _) in zip(jobs, job_block, _band_pipeline(jobs, bias_ref, s_buf)):
        o_t[blk, job[2][0]], o_t[blk, job[2][1]] = ot[:, :BLOCK_Q], ot[:, BLOCK_Q:]
    for blk in range(n_blocks):
        for j in range(n_heads // 2):
            pair_t = jnp.concatenate([o_t[blk, 2 * j], o_t[blk, 2 * j + 1]], axis=0)
            o_ref[0, blk * BLOCK_Q:(blk + 1) * BLOCK_Q, j * LANES:(j + 1) * LANES] = pair_t.T.astype(o_ref.dtype)


def _alibi_slopes(n):
    return [2.0 ** (-8.0 * (i + 1) / n) for i in range(n)]


def _attn_c(qkv, sinks, *, n_heads):
    b, seq, _ = qkv.shape
    kv_heads = n_heads // A_REP
    q_cols = n_heads * HEAD_DIM
    kv_cols = kv_heads * HEAD_DIM
    assert seq % C_TILE == 0 and C_TILE >= 2 * BLOCK_Q
    n_tiles = seq // C_TILE
    n_edges = seq // C_WINDOW
    edges_per_tile = C_TILE // C_WINDOW
    k_blk = q_cols // kv_cols
    v_blk = k_blk + 1
    span = BLOCK_Q + 2 * C_WINDOW

    def kv_specs(col_blk):
        edge = lambda offset: pl.BlockSpec(
            (1, C_WINDOW, kv_cols),
            lambda bi, i: (bi, jnp.clip(i * edges_per_tile + offset, 0, n_edges - 1), col_blk))
        return [edge(-1), pl.BlockSpec((1, C_TILE, kv_cols), lambda bi, i: (bi, i, col_blk)),
                edge(edges_per_tile)]

    return pl.pallas_call(
        functools.partial(_attn_c_body, n_heads=n_heads, slopes=_alibi_slopes(n_heads)),
        grid=(b, n_tiles),
        in_specs=[pl.BlockSpec(memory_space=pltpu.SMEM),
                  pl.BlockSpec((1, C_TILE, q_cols), lambda bi, i: (bi, i, 0))] + kv_specs(k_blk) + kv_specs(v_blk),
        out_specs=pl.BlockSpec((1, C_TILE, q_cols), lambda bi, i: (bi, i, 0)),
        out_shape=jax.ShapeDtypeStruct((b, seq, q_cols), BF16),
        scratch_shapes=[pltpu.VMEM((3, n_heads, span, BLOCK_Q), F32),
                        pltpu.VMEM((BAND_SLOTS, span, 2 * BLOCK_Q), F32)],
        compiler_params=_params("parallel", "arbitrary"),
        name="attn_c",
    )(sinks, qkv, qkv, qkv, qkv, qkv, qkv, qkv)


REFS_PER_SEQ = 9


def _attn_b_body(*refs, slopes, dil, n_seq):
    in_refs, (o_ref, lse_ref, bias_ref, s_buf) = refs[:n_seq * REFS_PER_SEQ], refs[n_seq * REFS_PER_SEQ:]
    i = pl.program_id(2)
    pl.when(jnp.logical_and(pl.program_id(1) == 0, i == 0))(
        lambda: _fill_band_bias(bias_ref, slopes, B_HALF, float(dil)))
    n_blocks = in_refs[0].shape[1] // BLOCK_Q
    block_variants = ([jnp.where(i == 0, 0, 1)] + [1] * (n_blocks - 2)
                      + [jnp.where(i == pl.num_programs(2) - 1, 2, 1)])
    low = lax.broadcasted_iota(jnp.int32, (BLOCK_Q, LANES), 1) < HEAD_DIM
    zero = jnp.zeros((BLOCK_Q, LANES), BF16)
    heads_all = range(B_HEADS_PER_GROUP)
    plain = [h for h in heads_all if h % 2 == h // B_REP]
    crossed = [h for h in heads_all if h % 2 != h // B_REP]
    jobs, job_place = [], []
    for seq in range(n_seq):
        q0, q1, q2, kp, kc, kn, vp, vc, vn = in_refs[seq * REFS_PER_SEQ:(seq + 1) * REFS_PER_SEQ]
        k, vt = _band_operands((kp, kc, kn), (vp, vc, vn))
        slabs = [q0[0], q1[0], q2[0]]
        for blk in range(n_blocks):
            q_lo = blk * BLOCK_Q
            ks = k[q_lo:q_lo + B_SPAN]
            ks_swapped = _swap_halves(ks)
            vt1s = {kv: _vt_with_ones(vt, kv, q_lo, B_SPAN) for kv in range(B_KV_PER_GROUP)}
            for kmat, heads in ((ks, plain[:2]), (ks_swapped, crossed), (ks, plain[2:])):
                q_rows = jnp.concatenate(
                    [jnp.where(low, slabs[h // 2][q_lo:q_lo + BLOCK_Q], zero) if h % 2 == 0
                     else jnp.where(low, zero, slabs[h // 2][q_lo:q_lo + BLOCK_Q]) for h in heads], axis=0)
                jobs.append((kmat, q_rows, heads, [h // B_REP for h in heads], vt1s, block_variants[blk], None))
                job_place.append((seq, blk))
    o_t, lse_t = {}, {}
    for job, place, (ot, m, den) in zip(jobs, job_place, _band_pipeline(jobs, bias_ref, s_buf)):
        lse = (m + jnp.log2(den)) * LN2
        for n, h in enumerate(job[2]):
            o_t[place, h] = ot[:, n * BLOCK_Q:(n + 1) * BLOCK_Q]
            lse_t[place, h] = jnp.broadcast_to(lse[:, n * BLOCK_Q:(n + 1) * BLOCK_Q], (HEAD_DIM, BLOCK_Q))
    gq = B_HEADS_PER_GROUP * HEAD_DIM
    for seq in range(n_seq):
        for blk in range(n_blocks):
            rows = slice(blk * BLOCK_Q, (blk + 1) * BLOCK_Q)
            for j in range(B_HEADS_PER_GROUP // 2):
                cols = slice(seq * gq + j * LANES, seq * gq + (j + 1) * LANES)
                pair = lambda t: jnp.concatenate([t[(seq, blk), 2 * j], t[(seq, blk), 2 * j + 1]], axis=0).T
                o_ref[0, rows, cols] = pair(o_t).astype(o_ref.dtype)
                lse_ref[0, rows, cols] = pair(lse_t)


def _attn_b(group_qkv, b):
    n_g = len(B_GROUPS)
    gq = B_HEADS_PER_GROUP * HEAD_DIM
    slabs_per_token = B_GROUP_COLS // LANES
    q_slabs = gq // LANES
    all_slopes = _alibi_slopes(n_g * B_HEADS_PER_GROUP)
    outs, lses = [], []
    for g, (window, dil) in enumerate(B_GROUPS):
        assert (window // 2) // dil == B_HALF
        sub = group_qkv[g].shape[0] // b
        tile = min(B_TILE, sub)
        n_seq = min(B_TILE // tile, dil)
        assert tile >= 2 * BLOCK_Q and sub % tile == 0 and dil % n_seq == 0
        n_tiles = sub // tile
        n_halves = sub // B_HALF
        halves_per_tile = tile // B_HALF
        view = group_qkv[g].reshape(b, sub, dil * B_GROUP_COLS)

        def seq_specs(seq, tile=tile, n_seq=n_seq, n_halves=n_halves, halves_per_tile=halves_per_tile):
            col = lambda slab: (lambda bi, rr, i: (rr * n_seq + seq) * slabs_per_token + slab)

            def tile_spec(slab):
                return pl.BlockSpec((1, tile, LANES), lambda bi, rr, i: (bi, i, col(slab)(bi, rr, i)))

            def edge_spec(slab, offset):
                return pl.BlockSpec(
                    (1, B_HALF, LANES),
                    lambda bi, rr, i: (bi, jnp.clip(i * halves_per_tile + offset, 0, n_halves - 1),
                                       col(slab)(bi, rr, i)))

            kv_specs = lambda slab: [edge_spec(slab, -1), tile_spec(slab), edge_spec(slab, halves_per_tile)]
            return [tile_spec(j) for j in range(q_slabs)] + kv_specs(q_slabs) + kv_specs(q_slabs + 1)

        out_map = lambda bi, rr, i: (bi, i, rr)
        out_spec = pl.BlockSpec((1, tile, n_seq * gq), out_map)
        o, lse = pl.pallas_call(
            functools.partial(_attn_b_body, dil=dil, n_seq=n_seq,
                              slopes=all_slopes[g * B_HEADS_PER_GROUP:(g + 1) * B_HEADS_PER_GROUP]),
            grid=(b, dil // n_seq, n_tiles),
            in_specs=sum((seq_specs(seq) for seq in range(n_seq)), []),
            out_specs=[out_spec, out_spec],
            out_shape=[jax.ShapeDtypeStruct((b, sub, dil * gq), BF16),
                       jax.ShapeDtypeStruct((b, sub, dil * gq), F32)],
            scratch_shapes=[pltpu.VMEM((3, B_HEADS_PER_GROUP, B_SPAN, BLOCK_Q), F32),
                            pltpu.VMEM((BAND_SLOTS, B_SPAN, 2 * BLOCK_Q), F32)],
            compiler_params=_params("parallel", "arbitrary", "arbitrary"),
            name=f"attn_b_g{g}",
        )(*([view] * (n_seq * REFS_PER_SEQ)))
        outs.append(o.reshape(b * sub, dil * gq))
        lses.append(lse.reshape(b * sub, dil * gq))
    return outs, lses


PROLOGUE_ROW_CHUNKS = 4


def _mix_plain(refs, scratch):
    return lambda rows: refs[0][rows, :]


def _mix_groups(refs, scratch):
    n_g = len(B_GROUPS)
    o_scr, l_scr = scratch
    slabs = o_scr.shape[0] // n_g
    tm = o_scr.shape[1]
    gq = slabs * LANES
    for g, (_, dil) in enumerate(B_GROUPS):
        for r in range(dil):
            rows = pl.ds(r, tm // dil, stride=dil)
            for s in range(slabs):
                cols = slice(r * gq + s * LANES, r * gq + (s + 1) * LANES)
                o_scr[g * slabs + s, rows, :] = refs[g][:, cols].astype(F32)
                l_scr[g * slabs + s, rows, :] = refs[n_g + g][:, cols]
    def lhs(rows):
        group = lambda scr, g: jnp.concatenate([scr[g * slabs + s, rows, :] for s in range(slabs)], axis=1)
        ls = [group(l_scr, g) for g in range(n_g)]
        mx = functools.reduce(jnp.maximum, ls)
        es = [jnp.exp(l - mx) for l in ls]
        inv = 1.0 / functools.reduce(lambda a, b: a + b, es)
        return jnp.concatenate([(group(o_scr, g) * (es[g] * inv)).astype(BF16) for g in range(n_g)], axis=1)

    return lhs


MLP_HIDDEN_CHUNK = 1024


def _block_body(*refs, n_mix, mix_fn, final_norm):
    h_ref, mix_refs = refs[0], refs[1:1 + n_mix]
    wo_ref, g_ref, w1_ref, w2_ref, fg_ref, out_ref, hn_ref = refs[1 + n_mix:8 + n_mix]
    mix_scratch = refs[8 + n_mix:]

    lhs = mix_fn(mix_refs, mix_scratch)
    chunk = h_ref.shape[0] // PROLOGUE_ROW_CHUNKS
    rows = [slice(c * chunk, (c + 1) * chunk) for c in range(PROLOGUE_ROW_CHUNKS)]
    project = lambda c: jnp.dot(lhs(rows[c]), wo_ref[...], preferred_element_type=F32)
    nxt = project(0)
    for c in range(PROLOGUE_ROW_CHUNKS):
        cur = nxt
        if c + 1 < PROLOGUE_ROW_CHUNKS:
            nxt = project(c + 1)
        h1 = h_ref[rows[c], :] + cur
        out_ref[rows[c], :] = h1
        hn_ref[rows[c], :] = _rms_rows(h1, g_ref[...]).astype(BF16)

    hn = hn_ref[...]
    for c in range(w1_ref.shape[1] // MLP_HIDDEN_CHUNK):
        cols = slice(c * MLP_HIDDEN_CHUNK, (c + 1) * MLP_HIDDEN_CHUNK)
        u = jnp.maximum(jnp.dot(hn, w1_ref[:, cols], preferred_element_type=F32), 0.0)
        out_ref[...] += jnp.dot((u * u).astype(BF16), w2_ref[cols, :], preferred_element_type=F32)

    if final_norm:
        out_ref[...] = _rms_rows(out_ref[...], fg_ref[...])


def _block(h2, mix_inputs, w_o, *, gain, w1, w2, layer, final_gain, final_norm, tm=512):
    t, d = h2.shape
    assert t % tm == 0 and tm % PROLOGUE_ROW_CHUNKS == 0 and w1.shape[2] % MLP_HIDDEN_CHUNK == 0
    row_tile = lambda cols: pl.BlockSpec((tm, cols), lambda i: (i, 0))
    whole = lambda a: pl.BlockSpec(a.shape, lambda i: (0, 0), pipeline_mode=pl.Buffered(1))
    of_layer = lambda a: pl.BlockSpec((None,) + a.shape[1:], lambda i: (layer, 0, 0),
                                      pipeline_mode=pl.Buffered(1))
    scratch = [pltpu.VMEM((tm, d), BF16)]
    if len(mix_inputs) == 1:
        mix_fn, mix_specs = _mix_plain, [row_tile(mix_inputs[0].shape[1])]
    else:
        n_g = len(B_GROUPS)
        gq = B_HEADS_PER_GROUP * HEAD_DIM
        mix_fn = _mix_groups
        mix_specs = [pl.BlockSpec((tm // dil, dil * gq), lambda i: (i, 0)) for _, dil in B_GROUPS] * 2
        scratch += [pltpu.VMEM((n_g * gq // LANES, tm, LANES), F32)] * 2
    return pl.pallas_call(
        functools.partial(_block_body, n_mix=len(mix_inputs), mix_fn=mix_fn, final_norm=final_norm),
        grid=(t // tm,),
        in_specs=[row_tile(d)] + mix_specs
                 + [whole(w_o), whole(gain), of_layer(w1), of_layer(w2), whole(final_gain)],
        out_specs=row_tile(d),
        out_shape=jax.ShapeDtypeStruct((t, d), F32),
        scratch_shapes=scratch,
        compiler_params=_params("parallel"),
        name="block_mlp",
    )(h2, *mix_inputs, w_o, gain, w1, w2, final_gain)


def kernel(x, attn_norm, mlp_norm, a_w_qkv, a_q_gain, a_k_gain, a_w_o, b_w_qkv, b_w_o,
           c_w_qkv, c_sinks, c_w_o, mlp_w1, mlp_w2, final_norm):
    b, seq, d = x.shape
    depth = attn_norm.shape[0]
    h = x.reshape(b * seq, d)
    scale = HEAD_DIM ** -0.5 * LOG2E
    a_heads = a_w_o.shape[1] // HEAD_DIM
    c_heads = c_w_o.shape[1] // HEAD_DIM
    cos, sin = _rope_tables(seq)
    head = jnp.arange(2 * LANES) // HEAD_DIM
    seg = (head[:, None] == head[None, :]).astype(BF16)
    tile4 = lambda g: jnp.tile(g, 2 * LANES // HEAD_DIM)[None, :]
    w1_all, w2_all = mlp_w1.astype(BF16), mlp_w2.astype(BF16)
    used = [0, 0, 0]
    for layer in range(depth):
        kind = layer % N_MIXERS
        j = used[kind]
        used[kind] += 1
        gain = attn_norm[layer][None, :]
        block = functools.partial(_block, gain=mlp_norm[layer][None, :], w1=w1_all, w2=w2_all, layer=layer,
                                  final_gain=final_norm[None, :], final_norm=(layer == depth - 1))
        if kind == 0:
            q_cols = a_heads * HEAD_DIM
            k_cols = (a_w_qkv.shape[2] - q_cols) // 2
            qkv = _qkv_proj(h, gain, a_w_qkv[j].astype(BF16), q_cols=q_cols, q_scale=scale,
                            rope=(tile4(a_q_gain[j]), tile4(a_k_gain[j]), cos, sin, seg, k_cols))
            o = _attn_a(qkv.reshape(b, seq, -1), n_heads=a_heads).reshape(b * seq, q_cols)
            h = block(h, [o], a_w_o[j].astype(BF16))
        elif kind == 1:
            q_cols = len(B_GROUPS) * B_HEADS_PER_GROUP * HEAD_DIM
            groups = _qkv_proj_b(h, gain, b_w_qkv[j].astype(BF16), q_cols=q_cols, q_scale=scale)
            outs, lses = _attn_b(groups, b)
            h = block(h, outs + lses, b_w_o[j].astype(BF16))
        else:
            q_cols = c_heads * HEAD_DIM
            qkv = _qkv_proj(h, gain, c_w_qkv[j].astype(BF16), q_cols=q_cols, q_scale=scale)
            o = _attn_c(qkv.reshape(b, seq, -1), c_sinks[j], n_heads=c_heads).reshape(b * seq, q_cols)
            h = block(h, [o], c_w_o[j].astype(BF16))
    return h.reshape(b, seq, d)
```

```python
import functools
import math

import jax
import jax.numpy as jnp
from jax import lax
from jax.experimental import pallas as pl
from jax.experimental.pallas import tpu as pltpu

HEAD_DIM = 64
HALF_HEAD = HEAD_DIM // 2
RMS_EPS = 1e-6
GRID_W = 64
ROPE_THETA = 10000.0
N_MIXERS = 3
A_REP = 4
B_GROUPS = ((128, 1), (512, 4), (2048, 16))
B_HEADS_PER_GROUP = 6
B_KV_PER_GROUP = 2
B_REP = B_HEADS_PER_GROUP // B_KV_PER_GROUP
C_WINDOW = 128
LANES = 128
BF16_SUBLANES = 16
BLOCK_Q = 128
BAND_SLOTS = 4
C_TILE = 8 * BLOCK_Q
B_HALF = BLOCK_Q // 2
B_TILE = 8 * BLOCK_Q
B_SPAN = BLOCK_Q + 2 * B_HALF
B_GROUP_COLS = (B_HEADS_PER_GROUP + 2 * B_KV_PER_GROUP) * HEAD_DIM
VT_ROWS = HEAD_DIM + BF16_SUBLANES
V7X_VMEM_LIMIT_BYTES = 48 * 1024 * 1024
NEG_BIG = -1e30
LOG2E = math.log2(math.e)
LN2 = math.log(2.0)

BF16 = jnp.bfloat16
F32 = jnp.float32


def _params(*sem):
    return pltpu.CompilerParams(dimension_semantics=sem, vmem_limit_bytes=V7X_VMEM_LIMIT_BYTES)


def _nt_dot(a, b):
    return lax.dot_general(a, b, (((1,), (1,)), ((), ())), preferred_element_type=F32)


def _rms_rows(x, gain):
    ms = jnp.mean(x * x, axis=-1, keepdims=True)
    return x * lax.rsqrt(ms + RMS_EPS) * gain


def _qkv_body(x_ref, g_ref, w_ref, o_ref, *, q_cols, q_scale):
    hn = _rms_rows(x_ref[...], g_ref[...]).astype(BF16)
    y = jnp.dot(hn, w_ref[...], preferred_element_type=F32)
    o_ref[:, :q_cols] = (y[:, :q_cols] * q_scale).astype(o_ref.dtype)
    o_ref[:, q_cols:] = y[:, q_cols:].astype(o_ref.dtype)


def _qkv_rope_body(x_ref, g_ref, w_ref, qg_ref, kg_ref, cos_ref, sin_ref, seg_ref, o_ref,
                   *, q_cols, k_cols, q_scale):
    hn = _rms_rows(x_ref[...], g_ref[...]).astype(BF16)
    slab = 2 * LANES
    cos = jnp.concatenate([cos_ref[...], cos_ref[...]], axis=1)
    sin = jnp.concatenate([sin_ref[...], sin_ref[...]], axis=1)
    lane = lax.broadcasted_iota(jnp.int32, (x_ref.shape[0], slab), 1)
    first_half = (lane % HALF_HEAD) < (HALF_HEAD // 2)
    seg = seg_ref[...]
    n_qk = (q_cols + k_cols) // slab
    project = lambda s: jnp.dot(hn, w_ref[:, s * slab:(s + 1) * slab], preferred_element_type=F32)
    y_next = project(0)
    for s in range(n_qk):
        is_q = s * slab < q_cols
        ys = y_next
        y_next = project(s + 1) if s + 1 < n_qk else jnp.dot(
            hn, w_ref[:, q_cols + k_cols:], preferred_element_type=F32)
        sq = ys * ys
        hi = sq.astype(BF16)
        lo = (sq - hi.astype(F32)).astype(BF16)
        ss = jnp.dot(hi, seg, preferred_element_type=F32) + jnp.dot(lo, seg, preferred_element_type=F32)
        gain = qg_ref[...] if is_q else kg_ref[...]
        yn = ys * lax.rsqrt(ss * (1.0 / HEAD_DIM) + RMS_EPS) * gain
        partner = jnp.where(first_half,
                            pltpu.roll(yn, slab - HALF_HEAD // 2, axis=1),
                            pltpu.roll(yn, HALF_HEAD // 2, axis=1))
        r = yn * cos + partner * sin
        if is_q:
            r = r * q_scale
        o_ref[:, s * slab:(s + 1) * slab] = r.astype(o_ref.dtype)
    o_ref[:, q_cols + k_cols:] = y_next.astype(o_ref.dtype)


def _qkv_proj(h2, gain, w, *, q_cols, q_scale, tm=1024, rope=None):
    t, d = h2.shape
    n = w.shape[1]
    assert t % tm == 0 and q_cols % (2 * LANES) == 0 and n % LANES == 0
    common = dict(
        grid=(t // tm,),
        out_specs=pl.BlockSpec((tm, n), lambda i: (i, 0)),
        out_shape=jax.ShapeDtypeStruct((t, n), BF16),
        compiler_params=_params("parallel"),
    )
    x_spec = pl.BlockSpec((tm, d), lambda i: (i, 0))
    g_spec = pl.BlockSpec((1, d), lambda i: (0, 0))
    w_spec = pl.BlockSpec((d, n), lambda i: (0, 0))
    if rope is None:
        return pl.pallas_call(
            functools.partial(_qkv_body, q_cols=q_cols, q_scale=q_scale),
            in_specs=[x_spec, g_spec, w_spec], name="qkv_proj", **common,
        )(h2, gain, w)
    qg, kg, cos, sin, seg, k_cols = rope
    seq_tiles = cos.shape[0] // tm
    slab = 2 * LANES
    return pl.pallas_call(
        functools.partial(_qkv_rope_body, q_cols=q_cols, k_cols=k_cols, q_scale=q_scale),
        in_specs=[x_spec, g_spec, w_spec,
                  pl.BlockSpec((1, slab), lambda i: (0, 0)),
                  pl.BlockSpec((1, slab), lambda i: (0, 0)),
                  pl.BlockSpec((tm, LANES), lambda i: (i % seq_tiles, 0)),
                  pl.BlockSpec((tm, LANES), lambda i: (i % seq_tiles, 0)),
                  pl.BlockSpec((slab, slab), lambda i: (0, 0))],
        name="qkv_proj_rope", **common,
    )(h2, gain, w, qg, kg, cos, sin, seg)


def _qkv_b_body(x_ref, g_ref, w_ref, *rest, q_cols, q_scale):
    out_refs, y_scr = rest[:-1], rest[-1]
    tm = x_ref.shape[0]
    hn = _rms_rows(x_ref[...], g_ref[...]).astype(BF16)
    y = jnp.dot(hn, w_ref[...], preferred_element_type=F32)
    for s in range(y_scr.shape[0]):
        ys = y[:, s * LANES:(s + 1) * LANES]
        y_scr[s] = ys * q_scale if s * LANES < q_cols else ys
    n_g = len(B_GROUPS)
    q_slabs = B_HEADS_PER_GROUP * HEAD_DIM // LANES
    k0 = q_cols // LANES
    for g, (_, dil) in enumerate(B_GROUPS):
        slabs = list(range(g * q_slabs, (g + 1) * q_slabs)) + [k0 + g, k0 + n_g + g]
        for r in range(dil):
            rows = pl.ds(r, tm // dil, stride=dil)
            piece = jnp.concatenate([y_scr[s, rows, :] for s in slabs], axis=1)
            out_refs[g][:, r * B_GROUP_COLS:(r + 1) * B_GROUP_COLS] = piece.astype(BF16)


def _qkv_proj_b(h2, gain, w, *, q_cols, q_scale, tm=1024):
    t, d = h2.shape
    n = w.shape[1]
    assert t % tm == 0 and n == len(B_GROUPS) * B_GROUP_COLS
    assert all((tm // dil) % BF16_SUBLANES == 0 for _, dil in B_GROUPS)
    return pl.pallas_call(
        functools.partial(_qkv_b_body, q_cols=q_cols, q_scale=q_scale),
        grid=(t // tm,),
        in_specs=[pl.BlockSpec((tm, d), lambda i: (i, 0)),
                  pl.BlockSpec((1, d), lambda i: (0, 0)),
                  pl.BlockSpec((d, n), lambda i: (0, 0))],
        out_specs=[pl.BlockSpec((tm // dil, dil * B_GROUP_COLS), lambda i: (i, 0)) for _, dil in B_GROUPS],
        out_shape=[jax.ShapeDtypeStruct((t // dil, dil * B_GROUP_COLS), BF16) for _, dil in B_GROUPS],
        scratch_shapes=[pltpu.VMEM((n // LANES, tm, LANES), F32)],
        compiler_params=_params("parallel"),
        name="qkv_proj_b",
    )(h2, gain, w)


def _rope_tables(seq):
    inv_freq = ROPE_THETA ** (-jnp.arange(0, HALF_HEAD, 2, dtype=F32) / HALF_HEAD)
    rows = seq // GRID_W
    ang_row = jnp.arange(rows, dtype=F32)[:, None] * inv_freq
    ang_col = jnp.arange(GRID_W, dtype=F32)[:, None] * inv_freq
    per_row = lambda x: jnp.repeat(x, GRID_W, axis=0)
    per_col = lambda x: jnp.tile(x, (rows, 1))
    cos = jnp.concatenate([per_row(jnp.cos(ang_row))] * 2 + [per_col(jnp.cos(ang_col))] * 2, axis=1)
    sin_r, sin_c = per_row(jnp.sin(ang_row)), per_col(jnp.sin(ang_col))
    sin = jnp.concatenate([-sin_r, sin_r, -sin_c, sin_c], axis=1)
    return jnp.concatenate([cos, cos], axis=1), jnp.concatenate([sin, sin], axis=1)


def _attn_a_body(q_ref, k_ref, v_ref, o_ref, klo, khi, vt1, s_buf, *, tq, tk, seq, tiles_per_step):
    def fill(head_in_high_lanes):
        low = lax.broadcasted_iota(jnp.int32, (seq, LANES), 1) < HEAD_DIM
        zero = jnp.zeros((seq, LANES), BF16)
        x = k_ref[0]
        swapped = jnp.concatenate([x[:, HEAD_DIM:], x[:, :HEAD_DIM]], axis=1)
        in_low, in_high = (swapped, x) if head_in_high_lanes else (x, swapped)
        klo[...] = jnp.where(low, in_low, zero)
        khi[...] = jnp.where(low, zero, in_high)
        ones_row = (lax.broadcasted_iota(jnp.int32, (VT_ROWS - HEAD_DIM, tk), 0) == 0).astype(F32)
        for c in range(seq // tk):
            xt = v_ref[0, c * tk:(c + 1) * tk, :].astype(F32).T
            vt = xt[HEAD_DIM:] if head_in_high_lanes else xt[:HEAD_DIM]
            vt1[:, c * tk:(c + 1) * tk] = jnp.concatenate([vt, ones_row], axis=0).astype(BF16)

    odd = (pl.program_id(1) % 2) == 1
    pl.when(jnp.logical_not(odd))(lambda: fill(False))
    pl.when(odd)(lambda: fill(True))

    cols = 2 * tq
    n_chunks = seq // tk
    n_tiles = seq // tq

    def scores(tile, c, slot):
        q = q_ref[0, pl.ds(pl.multiple_of(tile * tq, tq), tq), :]
        q2 = jnp.concatenate([q[:, :LANES], q[:, LANES:]], axis=0)
        cms = []
        for par, k_s in enumerate((klo, khi)):
            s = _nt_dot(k_s[c * tk:(c + 1) * tk, :], q2)
            s_buf[slot, par] = s
            cms.append(jnp.max(s, axis=0, keepdims=True))
        return tuple(cms)

    def accumulate(c, slot, cms, state):
        vt = vt1[:, c * tk:(c + 1) * tk]
        new = []
        for par in range(2):
            m, acc = state[2 * par], state[2 * par + 1]
            mn = jnp.maximum(m, cms[par])
            p = jnp.exp2(s_buf[slot, par] - mn)
            new += [mn, acc * jnp.exp2(m - mn) + jnp.dot(vt, p.astype(BF16), preferred_element_type=F32)]
        return tuple(new)

    neg = jnp.full((1, cols), -jnp.inf, F32)
    acc0 = jnp.zeros((VT_ROWS, cols), F32)

    def step(j, cms):
        for t in range(tiles_per_step):
            tile = j * tiles_per_step + t
            state = (neg, acc0, neg, acc0)
            for c in range(n_chunks):
                n = t * n_chunks + c
                nxt_tile = tile if c + 1 < n_chunks else jnp.minimum(tile + 1, n_tiles - 1)
                cms_next = scores(nxt_tile, (c + 1) % n_chunks, (n + 1) % 2)
                state = accumulate(c, n % 2, cms, state)
                cms = cms_next
            out = jnp.concatenate(
                [acc[:HEAD_DIM] / acc[HEAD_DIM:HEAD_DIM + 1] for acc in (state[1], state[3])], axis=0)
            o_ref[0, pl.ds(pl.multiple_of(tile * tq, tq), tq), :] = jnp.concatenate(
                [out[:, :tq].T, out[:, tq:].T], axis=1).astype(o_ref.dtype)
        return cms

    lax.fori_loop(0, n_tiles // tiles_per_step, step, scores(0, 0, 0))


def _attn_a(qkv, *, n_heads, tq=256, tk=512, tiles_per_step=4):
    b, seq, _ = qkv.shape
    kv_heads = n_heads // A_REP
    q_cols = n_heads * HEAD_DIM
    q_w = A_REP * HEAD_DIM
    k_blk0 = q_cols // LANES
    v_blk0 = (q_cols + kv_heads * HEAD_DIM) // LANES
    assert (seq // tk) % 2 == 0 and seq % (tq * tiles_per_step) == 0
    return pl.pallas_call(
        functools.partial(_attn_a_body, tq=tq, tk=tk, seq=seq, tiles_per_step=tiles_per_step),
        grid=(b, kv_heads),
        in_specs=[pl.BlockSpec((1, seq, q_w), lambda bi, h: (bi, 0, h)),
                  pl.BlockSpec((1, seq, LANES), lambda bi, h: (bi, 0, k_blk0 + h // 2)),
                  pl.BlockSpec((1, seq, LANES), lambda bi, h: (bi, 0, v_blk0 + h // 2))],
        out_specs=pl.BlockSpec((1, seq, q_w), lambda bi, h: (bi, 0, h)),
        out_shape=jax.ShapeDtypeStruct((b, seq, q_cols), BF16),
        scratch_shapes=[pltpu.VMEM((seq, LANES), BF16), pltpu.VMEM((seq, LANES), BF16),
                        pltpu.VMEM((VT_ROWS, seq), BF16), pltpu.VMEM((2, 2, tk, 2 * tq), F32)],
        compiler_params=_params("parallel", "arbitrary"),
        name="attn_a",
    )(qkv, qkv, qkv)


def _fill_band_bias(bias_ref, slopes, half_window, dist_scale):
    span = bias_ref.shape[2]
    c = lax.broadcasted_iota(jnp.int32, (span, BLOCK_Q), 0)
    r = lax.broadcasted_iota(jnp.int32, (span, BLOCK_Q), 1)
    dist = jnp.abs(c - half_window - r)
    inside = dist <= half_window
    penalty = dist.astype(F32) * (dist_scale * LOG2E)
    for variant, ok in enumerate((inside & (c >= half_window), inside, inside & (c < half_window + BLOCK_Q))):
        for h, slope in enumerate(slopes):
            bias_ref[variant, h] = jnp.where(ok, -slope * penalty, NEG_BIG)


def _band_pipeline(jobs, bias_ref, s_buf):
    def scores(job, slot):
        kmat, q_rows, head_ids, _, _, variant, sink_row = job
        width = len(head_ids) * BLOCK_Q
        s = _nt_dot(kmat, q_rows) + jnp.concatenate([bias_ref[variant, h] for h in head_ids], axis=1)
        s_buf[slot, :, :width] = s
        m = jnp.max(s, axis=0, keepdims=True)
        return m if sink_row is None else jnp.maximum(m, sink_row)

    def values(job, slot, m):
        _, _, head_ids, kv_ids, vt1s, _, sink_row = job
        n = len(head_ids)
        p = jnp.exp2(s_buf[slot, :, :n * BLOCK_Q] - m).astype(BF16)
        accs, start = [], 0
        while start < n:
            stop = start
            while stop < n and kv_ids[stop] == kv_ids[start]:
                stop += 1
            accs.append(jnp.dot(vt1s[kv_ids[start]], p[:, start * BLOCK_Q:stop * BLOCK_Q],
                                preferred_element_type=F32))
            start = stop
        acc = accs[0] if len(accs) == 1 else jnp.concatenate(accs, axis=1)
        den = acc[HEAD_DIM:HEAD_DIM + 1]
        if sink_row is not None:
            den = den + jnp.exp2(sink_row - m)
        return acc[:HEAD_DIM] / den, m, den

    results = []
    n_slots = s_buf.shape[0]
    ahead = n_slots - 1
    maxima = [scores(jobs[c], c % n_slots) for c in range(min(ahead, len(jobs)))]
    for c, job in enumerate(jobs):
        if c + ahead < len(jobs):
            maxima.append(scores(jobs[c + ahead], (c + ahead) % n_slots))
        results.append(values(job, c % n_slots, maxima[c]))
    return results


def _band_operands(k_refs, v_refs):
    k = jnp.concatenate([r[0] for r in k_refs], axis=0)
    vt = jnp.concatenate([r[0] for r in v_refs], axis=0).astype(F32).T
    return k, vt


def _vt_with_ones(vt, kv, start, span):
    ones_row = (lax.broadcasted_iota(jnp.int32, (VT_ROWS - HEAD_DIM, span), 0) == 0).astype(F32)
    rows = vt[kv * HEAD_DIM:(kv + 1) * HEAD_DIM, start:start + span]
    return jnp.concatenate([rows, ones_row], axis=0).astype(BF16)


def _swap_halves(x):
    return jnp.concatenate([x[:, HEAD_DIM:], x[:, :HEAD_DIM]], axis=1)


def _attn_c_body(sink_ref, q_ref, kp_ref, kc_ref, kn_ref, vp_ref, vc_ref, vn_ref, o_ref, bias_ref, s_buf,
                 *, n_heads, slopes):
    i = pl.program_id(1)
    pl.when(i == 0)(lambda: _fill_band_bias(bias_ref, slopes, C_WINDOW, 1.0))
    n_blocks = C_TILE // BLOCK_Q
    block_variants = ([jnp.where(i == 0, 0, 1)] + [1] * (n_blocks - 2)
                      + [jnp.where(i == pl.num_programs(1) - 1, 2, 1)])
    q = q_ref[0]
    k, vt = _band_operands((kp_ref, kc_ref, kn_ref), (vp_ref, vc_ref, vn_ref))
    span = BLOCK_Q + 2 * C_WINDOW
    low = lax.broadcasted_iota(jnp.int32, (BLOCK_Q, LANES), 1) < HEAD_DIM
    zero = jnp.zeros((BLOCK_Q, LANES), BF16)
    first_block = lax.broadcasted_iota(jnp.int32, (1, 2 * BLOCK_Q), 1) < BLOCK_Q
    jobs, job_block = [], []
    for blk in range(n_blocks):
        q_lo = blk * BLOCK_Q
        for pair in range(n_heads // A_REP // 2):
            ks = k[q_lo:q_lo + span, pair * LANES:(pair + 1) * LANES]
            ks_swapped = _swap_halves(ks)
            for which in range(2):
                kv = 2 * pair + which
                vt1s = {kv: _vt_with_ones(vt, kv, q_lo, span)}
                slabs = [q[q_lo:q_lo + BLOCK_Q, (2 * kv + j) * LANES:(2 * kv + j + 1) * LANES] for j in range(2)]
                for parity in range(2):
                    q_rows = jnp.concatenate(
                        [jnp.where(low, s_, zero) if parity == 0 else jnp.where(low, zero, s_) for s_ in slabs],
                        axis=0)
                    heads = (A_REP * kv + parity, A_REP * kv + 2 + parity)
                    sink_row = jnp.where(first_block, sink_ref[heads[0]], sink_ref[heads[1]]) * LOG2E
                    jobs.append((ks if parity == which else ks_swapped, q_rows, heads, (kv, kv), vt1s,
                                 block_variants[blk], sink_row))
                    job_block.append(blk)
    o_t = {}
    for job, blk, (ot, _, _) in zip(jobs, job_block, _band_pipeline(jobs, bias_ref, s_buf)):
        o_t[blk, job[2][0]], o_t[blk, job[2][1]] = ot[:, :BLOCK_Q], ot[:, BLOCK_Q:]
    for blk in range(n_blocks):
        for j in range(n_heads // 2):
            pair_t = jnp.concatenate([o_t[blk, 2 * j], o_t[blk, 2 * j + 1]], axis=0)
            o_ref[0, blk * BLOCK_Q:(blk + 1) * BLOCK_Q, j * LANES:(j + 1) * LANES] = pair_t.T.astype(o_ref.dtype)


def _alibi_slopes(n):
    return [2.0 ** (-8.0 * (i + 1) / n) for i in range(n)]


def _attn_c(qkv, sinks, *, n_heads):
    b, seq, _ = qkv.shape
    kv_heads = n_heads // A_REP
    q_cols = n_heads * HEAD_DIM
    kv_cols = kv_heads * HEAD_DIM
    assert seq % C_TILE == 0 and C_TILE >= 2 * BLOCK_Q
    n_tiles = seq // C_TILE
    n_edges = seq // C_WINDOW
    edges_per_tile = C_TILE // C_WINDOW
    k_blk = q_cols // kv_cols
    v_blk = k_blk + 1
    span = BLOCK_Q + 2 * C_WINDOW

    def kv_specs(col_blk):
        edge = lambda offset: pl.BlockSpec(
            (1, C_WINDOW, kv_cols),
            lambda bi, i: (bi, jnp.clip(i * edges_per_tile + offset, 0, n_edges - 1), col_blk))
        return [edge(-1), pl.BlockSpec((1, C_TILE, kv_cols), lambda bi, i: (bi, i, col_blk)),
                edge(edges_per_tile)]

    return pl.pallas_call(
        functools.partial(_attn_c_body, n_heads=n_heads, slopes=_alibi_slopes(n_heads)),
        grid=(b, n_tiles),
        in_specs=[pl.BlockSpec(memory_space=pltpu.SMEM),
                  pl.BlockSpec((1, C_TILE, q_cols), lambda bi, i: (bi, i, 0))] + kv_specs(k_blk) + kv_specs(v_blk),
        out_specs=pl.BlockSpec((1, C_TILE, q_cols), lambda bi, i: (bi, i, 0)),
        out_shape=jax.ShapeDtypeStruct((b, seq, q_cols), BF16),
        scratch_shapes=[pltpu.VMEM((3, n_heads, span, BLOCK_Q), F32),
                        pltpu.VMEM((BAND_SLOTS, span, 2 * BLOCK_Q), F32)],
        compiler_params=_params("parallel", "arbitrary"),
        name="attn_c",
    )(sinks, qkv, qkv, qkv, qkv, qkv, qkv, qkv)


REFS_PER_SEQ = 9


def _attn_b_body(*refs, slopes, dil, n_seq):
    in_refs, (o_ref, lse_ref, bias_ref, s_buf) = refs[:n_seq * REFS_PER_SEQ], refs[n_seq * REFS_PER_SEQ:]
    i = pl.program_id(2)
    pl.when(jnp.logical_and(pl.program_id(1) == 0, i == 0))(
        lambda: _fill_band_bias(bias_ref, slopes, B_HALF, float(dil)))
    n_blocks = in_refs[0].shape[1] // BLOCK_Q
    block_variants = ([jnp.where(i == 0, 0, 1)] + [1] * (n_blocks - 2)
                      + [jnp.where(i == pl.num_programs(2) - 1, 2, 1)])
    low = lax.broadcasted_iota(jnp.int32, (BLOCK_Q, LANES), 1) < HEAD_DIM
    zero = jnp.zeros((BLOCK_Q, LANES), BF16)
    heads_all = range(B_HEADS_PER_GROUP)
    plain = [h for h in heads_all if h % 2 == h // B_REP]
    crossed = [h for h in heads_all if h % 2 != h // B_REP]
    jobs, job_place = [], []
    for seq in range(n_seq):
        q0, q1, q2, kp, kc, kn, vp, vc, vn = in_refs[seq * REFS_PER_SEQ:(seq + 1) * REFS_PER_SEQ]
        k, vt = _band_operands((kp, kc, kn), (vp, vc, vn))
        slabs = [q0[0], q1[0], q2[0]]
        for blk in range(n_blocks):
            q_lo = blk * BLOCK_Q
            ks = k[q_lo:q_lo + B_SPAN]
            ks_swapped = _swap_halves(ks)
            vt1s = {kv: _vt_with_ones(vt, kv, q_lo, B_SPAN) for kv in range(B_KV_PER_GROUP)}
            for kmat, heads in ((ks, plain[:2]), (ks_swapped, crossed), (ks, plain[2:])):
                q_rows = jnp.concatenate(
                    [jnp.where(low, slabs[h // 2][q_lo:q_lo + BLOCK_Q], zero) if h % 2 == 0
                     else jnp.where(low, zero, slabs[h // 2][q_lo:q_lo + BLOCK_Q]) for h in heads], axis=0)
                jobs.append((kmat, q_rows, heads, [h // B_REP for h in heads], vt1s, block_variants[blk], None))
                job_place.append((seq, blk))
    o_t, lse_t = {}, {}
    for job, place, (ot, m, den) in zip(jobs, job_place, _band_pipeline(jobs, bias_ref, s_buf)):
        lse = (m + jnp.log2(den)) * LN2
        for n, h in enumerate(job[2]):
            o_t[place, h] = ot[:, n * BLOCK_Q:(n + 1) * BLOCK_Q]
            lse_t[place, h] = jnp.broadcast_to(lse[:, n * BLOCK_Q:(n + 1) * BLOCK_Q], (HEAD_DIM, BLOCK_Q))
    gq = B_HEADS_PER_GROUP * HEAD_DIM
    for seq in range(n_seq):
        for blk in range(n_blocks):
            rows = slice(blk * BLOCK_Q, (blk + 1) * BLOCK_Q)
            for j in range(B_HEADS_PER_GROUP // 2):
                cols = slice(seq * gq + j * LANES, seq * gq + (j + 1) * LANES)
                pair = lambda t: jnp.concatenate([t[(seq, blk), 2 * j], t[(seq, blk), 2 * j + 1]], axis=0).T
                o_ref[0, rows, cols] = pair(o_t).astype(o_ref.dtype)
                lse_ref[0, rows, cols] = pair(lse_t)


def _attn_b(group_qkv, b):
    n_g = len(B_GROUPS)
    gq = B_HEADS_PER_GROUP * HEAD_DIM
    slabs_per_token = B_GROUP_COLS // LANES
    q_slabs = gq // LANES
    all_slopes = _alibi_slopes(n_g * B_HEADS_PER_GROUP)
    outs, lses = [], []
    for g, (window, dil) in enumerate(B_GROUPS):
        assert (window // 2) // dil == B_HALF
        sub = group_qkv[g].shape[0] // b
        tile = min(B_TILE, sub)
        n_seq = min(B_TILE // tile, dil)
        assert tile >= 2 * BLOCK_Q and sub % tile == 0 and dil % n_seq == 0
        n_tiles = sub // tile
        n_halves = sub // B_HALF
        halves_per_tile = tile // B_HALF
        view = group_qkv[g].reshape(b, sub, dil * B_GROUP_COLS)

        def seq_specs(seq, tile=tile, n_seq=n_seq, n_halves=n_halves, halves_per_tile=halves_per_tile):
            col = lambda slab: (lambda bi, rr, i: (rr * n_seq + seq) * slabs_per_token + slab)

            def tile_spec(slab):
                return pl.BlockSpec((1, tile, LANES), lambda bi, rr, i: (bi, i, col(slab)(bi, rr, i)))

            def edge_spec(slab, offset):
                return pl.BlockSpec(
                    (1, B_HALF, LANES),
                    lambda bi, rr, i: (bi, jnp.clip(i * halves_per_tile + offset, 0, n_halves - 1),
                                       col(slab)(bi, rr, i)))

            kv_specs = lambda slab: [edge_spec(slab, -1), tile_spec(slab), edge_spec(slab, halves_per_tile)]
            return [tile_spec(j) for j in range(q_slabs)] + kv_specs(q_slabs) + kv_specs(q_slabs + 1)

        out_map = lambda bi, rr, i: (bi, i, rr)
        out_spec = pl.BlockSpec((1, tile, n_seq * gq), out_map)
        o, lse = pl.pallas_call(
            functools.partial(_attn_b_body, dil=dil, n_seq=n_seq,
                              slopes=all_slopes[g * B_HEADS_PER_GROUP:(g + 1) * B_HEADS_PER_GROUP]),
            grid=(b, dil // n_seq, n_tiles),
            in_specs=sum((seq_specs(seq) for seq in range(n_seq)), []),
            out_specs=[out_spec, out_spec],
            out_shape=[jax.ShapeDtypeStruct((b, sub, dil * gq), BF16),
                       jax.ShapeDtypeStruct((b, sub, dil * gq), F32)],
            scratch_shapes=[pltpu.VMEM((3, B_HEADS_PER_GROUP, B_SPAN, BLOCK_Q), F32),
                            pltpu.VMEM((BAND_SLOTS, B_SPAN, 2 * BLOCK_Q), F32)],
            compiler_params=_params("parallel", "arbitrary", "arbitrary"),
            name=f"attn_b_g{g}",
        )(*([view] * (n_seq * REFS_PER_SEQ)))
        outs.append(o.reshape(b * sub, dil * gq))
        lses.append(lse.reshape(b * sub, dil * gq))
    return outs, lses


PROLOGUE_ROW_CHUNKS = 4


def _mix_plain(refs, scratch):
    return lambda rows: refs[0][rows, :]


def _mix_groups(refs, scratch):
    n_g = len(B_GROUPS)
    o_scr, l_scr = scratch
    slabs = o_scr.shape[0] // n_g
    tm = o_scr.shape[1]
    gq = slabs * LANES
    for g, (_, dil) in enumerate(B_GROUPS):
        for r in range(dil):
            rows = pl.ds(r, tm // dil, stride=dil)
            for s in range(slabs):
                cols = slice(r * gq + s * LANES, r * gq + (s + 1) * LANES)
                o_scr[g * slabs + s, rows, :] = refs[g][:, cols].astype(F32)
                l_scr[g * slabs + s, rows, :] = refs[n_g + g][:, cols]
    def lhs(rows):
        group = lambda scr, g: jnp.concatenate([scr[g * slabs + s, rows, :] for s in range(slabs)], axis=1)
        ls = [group(l_scr, g) for g in range(n_g)]
        mx = functools.reduce(jnp.maximum, ls)
        es = [jnp.exp(l - mx) for l in ls]
        inv = 1.0 / functools.reduce(lambda a, b: a + b, es)
        return jnp.concatenate([(group(o_scr, g) * (es[g] * inv)).astype(BF16) for g in range(n_g)], axis=1)

    return lhs


MLP_HIDDEN_CHUNK = 1024


def _block_body(*refs, n_mix, mix_fn, final_norm):
    h_ref, mix_refs = refs[0], refs[1:1 + n_mix]
    wo_ref, g_ref, w1_ref, w2_ref, fg_ref, out_ref, hn_ref = refs[1 + n_mix:8 + n_mix]
    mix_scratch = refs[8 + n_mix:]

    lhs = mix_fn(mix_refs, mix_scratch)
    chunk = h_ref.shape[0] // PROLOGUE_ROW_CHUNKS
    rows = [slice(c * chunk, (c + 1) * chunk) for c in range(PROLOGUE_ROW_CHUNKS)]
    project = lambda c: jnp.dot(lhs(rows[c]), wo_ref[...], preferred_element_type=F32)
    nxt = project(0)
    for c in range(PROLOGUE_ROW_CHUNKS):
        cur = nxt
        if c + 1 < PROLOGUE_ROW_CHUNKS:
            nxt = project(c + 1)
        h1 = h_ref[rows[c], :] + cur
        out_ref[rows[c], :] = h1
        hn_ref[rows[c], :] = _rms_rows(h1, g_ref[...]).astype(BF16)

    hn = hn_ref[...]
    for c in range(w1_ref.shape[1] // MLP_HIDDEN_CHUNK):
        cols = slice(c * MLP_HIDDEN_CHUNK, (c + 1) * MLP_HIDDEN_CHUNK)
        u = jnp.maximum(jnp.dot(hn, w1_ref[:, cols], preferred_element_type=F32), 0.0)
        out_ref[...] += jnp.dot((u * u).astype(BF16), w2_ref[cols, :], preferred_element_type=F32)

    if final_norm:
        out_ref[...] = _rms_rows(out_ref[...], fg_ref[...])


def _block(h2, mix_inputs, w_o, *, gain, w1, w2, layer, final_gain, final_norm, tm=512):
    t, d = h2.shape
    assert t % tm == 0 and tm % PROLOGUE_ROW_CHUNKS == 0 and w1.shape[2] % MLP_HIDDEN_CHUNK == 0
    row_tile = lambda cols: pl.BlockSpec((tm, cols), lambda i: (i, 0))
    whole = lambda a: pl.BlockSpec(a.shape, lambda i: (0, 0), pipeline_mode=pl.Buffered(1))
    of_layer = lambda a: pl.BlockSpec((None,) + a.shape[1:], lambda i: (layer, 0, 0),
                                      pipeline_mode=pl.Buffered(1))
    scratch = [pltpu.VMEM((tm, d), BF16)]
    if len(mix_inputs) == 1:
        mix_fn, mix_specs = _mix_plain, [row_tile(mix_inputs[0].shape[1])]
    else:
        n_g = len(B_GROUPS)
        gq = B_HEADS_PER_GROUP * HEAD_DIM
        mix_fn = _mix_groups
        mix_specs = [pl.BlockSpec((tm // dil, dil * gq), lambda i: (i, 0)) for _, dil in B_GROUPS] * 2
        scratch += [pltpu.VMEM((n_g * gq // LANES, tm, LANES), F32)] * 2
    return pl.pallas_call(
        functools.partial(_block_body, n_mix=len(mix_inputs), mix_fn=mix_fn, final_norm=final_norm),
        grid=(t // tm,),
        in_specs=[row_tile(d)] + mix_specs
                 + [whole(w_o), whole(gain), of_layer(w1), of_layer(w2), whole(final_gain)],
        out_specs=row_tile(d),
        out_shape=jax.ShapeDtypeStruct((t, d), F32),
        scratch_shapes=scratch,
        compiler_params=_params("parallel"),
        name="block_mlp",
    )(h2, *mix_inputs, w_o, gain, w1, w2, final_gain)


def kernel(x, attn_norm, mlp_norm, a_w_qkv, a_q_gain, a_k_gain, a_w_o, b_w_qkv, b_w_o,
           c_w_qkv, c_sinks, c_w_o, mlp_w1, mlp_w2, final_norm):
    b, seq, d = x.shape
    depth = attn_norm.shape[0]
    h = x.reshape(b * seq, d)
    scale = HEAD_DIM ** -0.5 * LOG2E
    a_heads = a_w_o.shape[1] // HEAD_DIM
    c_heads = c_w_o.shape[1] // HEAD_DIM
    cos, sin = _rope_tables(seq)
    head = jnp.arange(2 * LANES) // HEAD_DIM
    seg = (head[:, None] == head[None, :]).astype(BF16)
    tile4 = lambda g: jnp.tile(g, 2 * LANES // HEAD_DIM)[None, :]
    w1_all, w2_all = mlp_w1.astype(BF16), mlp_w2.astype(BF16)
    used = [0, 0, 0]
    for layer in range(depth):
        kind = layer % N_MIXERS
        j = used[kind]
        used[kind] += 1
        gain = attn_norm[layer][None, :]
        block = functools.partial(_block, gain=mlp_norm[layer][None, :], w1=w1_all, w2=w2_all, layer=layer,
                                  final_gain=final_norm[None, :], final_norm=(layer == depth - 1))
        if kind == 0:
            q_cols = a_heads * HEAD_DIM
            k_cols = (a_w_qkv.shape[2] - q_cols) // 2
            qkv = _qkv_proj(h, gain, a_w_qkv[j].astype(BF16), q_cols=q_cols, q_scale=scale,
                            rope=(tile4(a_q_gain[j]), tile4(a_k_gain[j]), cos, sin, seg, k_cols))
            o = _attn_a(qkv.reshape(b, seq, -1), n_heads=a_heads).reshape(b * seq, q_cols)
            h = block(h, [o], a_w_o[j].astype(BF16))
        elif kind == 1:
            q_cols = len(B_GROUPS) * B_HEADS_PER_GROUP * HEAD_DIM
            groups = _qkv_proj_b(h, gain, b_w_qkv[j].astype(BF16), q_cols=q_cols, q_scale=scale)
            outs, lses = _attn_b(groups, b)
            h = block(h, outs + lses, b_w_o[j].astype(BF16))
        else:
            q_cols = c_heads * HEAD_DIM
            qkv = _qkv_proj(h, gain, c_w_qkv[j].astype(BF16), q_cols=q_cols, q_scale=scale)
            o = _attn_c(qkv.reshape(b, seq, -1), c_sinks[j], n_heads=c_heads).reshape(b * seq, q_cols)
            h = block(h, [o], c_w_o[j].astype(BF16))
    return h.reshape(b, seq, d)
```

```python
import functools
import math

import jax
import jax.numpy as jnp
from jax import lax
from jax.experimental import pallas as pl
from jax.experimental.pallas import tpu as pltpu

HEAD_DIM = 64
HALF_HEAD = HEAD_DIM // 2
RMS_EPS = 1e-6
GRID_W = 64
ROPE_THETA = 10000.0
N_MIXERS = 3
A_REP = 4
B_GROUPS = ((128, 1), (512, 4), (2048, 16))
B_HEADS_PER_GROUP = 6
B_KV_PER_GROUP = 2
B_REP = B_HEADS_PER_GROUP // B_KV_PER_GROUP
C_WINDOW = 128
LANES = 128
BF16_SUBLANES = 16
BLOCK_Q = 128
BAND_SLOTS = 4
C_TILE = 8 * BLOCK_Q
B_HALF = BLOCK_Q // 2
B_TILE = 8 * BLOCK_Q
B_SPAN = BLOCK_Q + 2 * B_HALF
B_GROUP_COLS = (B_HEADS_PER_GROUP + 2 * B_KV_PER_GROUP) * HEAD_DIM
VT_ROWS = HEAD_DIM + BF16_SUBLANES
V7X_VMEM_LIMIT_BYTES = 48 * 1024 * 1024
NEG_BIG = -1e30
LOG2E = math.log2(math.e)
LN2 = math.log(2.0)

BF16 = jnp.bfloat16
F32 = jnp.float32


def _params(*sem):
    return pltpu.CompilerParams(dimension_semantics=sem, vmem_limit_bytes=V7X_VMEM_LIMIT_BYTES)


def _nt_dot(a, b):
    return lax.dot_general(a, b, (((1,), (1,)), ((), ())), preferred_element_type=F32)


def _rms_rows(x, gain):
    ms = jnp.mean(x * x, axis=-1, keepdims=True)
    return x * lax.rsqrt(ms + RMS_EPS) * gain


def _qkv_body(x_ref, g_ref, w_ref, o_ref, *, q_cols, q_scale):
    hn = _rms_rows(x_ref[...], g_ref[...]).astype(BF16)
    y = jnp.dot(hn, w_ref[...], preferred_element_type=F32)
    o_ref[:, :q_cols] = (y[:, :q_cols] * q_scale).astype(o_ref.dtype)
    o_ref[:, q_cols:] = y[:, q_cols:].astype(o_ref.dtype)


def _qkv_rope_body(x_ref, g_ref, w_ref, qg_ref, kg_ref, cos_ref, sin_ref, seg_ref, o_ref,
                   *, q_cols, k_cols, q_scale):
    hn = _rms_rows(x_ref[...], g_ref[...]).astype(BF16)
    slab = 2 * LANES
    cos = jnp.concatenate([cos_ref[...], cos_ref[...]], axis=1)
    sin = jnp.concatenate([sin_ref[...], sin_ref[...]], axis=1)
    lane = lax.broadcasted_iota(jnp.int32, (x_ref.shape[0], slab), 1)
    first_half = (lane % HALF_HEAD) < (HALF_HEAD // 2)
    seg = seg_ref[...]
    n_qk = (q_cols + k_cols) // slab
    project = lambda s: jnp.dot(hn, w_ref[:, s * slab:(s + 1) * slab], preferred_element_type=F32)
    y_next = project(0)
    for s in range(n_qk):
        is_q = s * slab < q_cols
        ys = y_next
        y_next = project(s + 1) if s + 1 < n_qk else jnp.dot(
            hn, w_ref[:, q_cols + k_cols:], preferred_element_type=F32)
        sq = ys * ys
        hi = sq.astype(BF16)
        lo = (sq - hi.astype(F32)).astype(BF16)
        ss = jnp.dot(hi, seg, preferred_element_type=F32) + jnp.dot(lo, seg, preferred_element_type=F32)
        gain = qg_ref[...] if is_q else kg_ref[...]
        yn = ys * lax.rsqrt(ss * (1.0 / HEAD_DIM) + RMS_EPS) * gain
        partner = jnp.where(first_half,
                            pltpu.roll(yn, slab - HALF_HEAD // 2, axis=1),
                            pltpu.roll(yn, HALF_HEAD // 2, axis=1))
        r = yn * cos + partner * sin
        if is_q:
            r = r * q_scale
        o_ref[:, s * slab:(s + 1) * slab] = r.astype(o_ref.dtype)
    o_ref[:, q_cols + k_cols:] = y_next.astype(o_ref.dtype)


def _qkv_proj(h2, gain, w, *, q_cols, q_scale, tm=1024, rope=None):
    t, d = h2.shape
    n = w.shape[1]
    assert t % tm == 0 and q_cols % (2 * LANES) == 0 and n % LANES == 0
    common = dict(
        grid=(t // tm,),
        out_specs=pl.BlockSpec((tm, n), lambda i: (i, 0)),
        out_shape=jax.ShapeDtypeStruct((t, n), BF16),
        compiler_params=_params("parallel"),
    )
    x_spec = pl.BlockSpec((tm, d), lambda i: (i, 0))
    g_spec = pl.BlockSpec((1, d), lambda i: (0, 0))
    w_spec = pl.BlockSpec((d, n), lambda i: (0, 0))
    if rope is None:
        return pl.pallas_call(
            functools.partial(_qkv_body, q_cols=q_cols, q_scale=q_scale),
            in_specs=[x_spec, g_spec, w_spec], name="qkv_proj", **common,
        )(h2, gain, w)
    qg, kg, cos, sin, seg, k_cols = rope
    seq_tiles = cos.shape[0] // tm
    slab = 2 * LANES
    return pl.pallas_call(
        functools.partial(_qkv_rope_body, q_cols=q_cols, k_cols=k_cols, q_scale=q_scale),
        in_specs=[x_spec, g_spec, w_spec,
                  pl.BlockSpec((1, slab), lambda i: (0, 0)),
                  pl.BlockSpec((1, slab), lambda i: (0, 0)),
                  pl.BlockSpec((tm, LANES), lambda i: (i % seq_tiles, 0)),
                  pl.BlockSpec((tm, LANES), lambda i: (i % seq_tiles, 0)),
                  pl.BlockSpec((slab, slab), lambda i: (0, 0))],
        name="qkv_proj_rope", **common,
    )(h2, gain, w, qg, kg, cos, sin, seg)


def _qkv_b_body(x_ref, g_ref, w_ref, *rest, q_cols, q_scale):
    out_refs, y_scr = rest[:-1], rest[-1]
    tm = x_ref.shape[0]
    hn = _rms_rows(x_ref[...], g_ref[...]).astype(BF16)
    y = jnp.dot(hn, w_ref[...], preferred_element_type=F32)
    for s in range(y_scr.shape[0]):
        ys = y[:, s * LANES:(s + 1) * LANES]
        y_scr[s] = ys * q_scale if s * LANES < q_cols else ys
    n_g = len(B_GROUPS)
    q_slabs = B_HEADS_PER_GROUP * HEAD_DIM // LANES
    k0 = q_cols // LANES
    for g, (_, dil) in enumerate(B_GROUPS):
        slabs = list(range(g * q_slabs, (g + 1) * q_slabs)) + [k0 + g, k0 + n_g + g]
        for r in range(dil):
            rows = pl.ds(r, tm // dil, stride=dil)
            piece = jnp.concatenate([y_scr[s, rows, :] for s in slabs], axis=1)
            out_refs[g][:, r * B_GROUP_COLS:(r + 1) * B_GROUP_COLS] = piece.astype(BF16)


def _qkv_proj_b(h2, gain, w, *, q_cols, q_scale, tm=1024):
    t, d = h2.shape
    n = w.shape[1]
    assert t % tm == 0 and n == len(B_GROUPS) * B_GROUP_COLS
    assert all((tm // dil) % BF16_SUBLANES == 0 for _, dil in B_GROUPS)
    return pl.pallas_call(
        functools.partial(_qkv_b_body, q_cols=q_cols, q_scale=q_scale),
        grid=(t // tm,),
        in_specs=[pl.BlockSpec((tm, d), lambda i: (i, 0)),
                  pl.BlockSpec((1, d), lambda i: (0, 0)),
                  pl.BlockSpec((d, n), lambda i: (0, 0))],
        out_specs=[pl.BlockSpec((tm // dil, dil * B_GROUP_COLS), lambda i: (i, 0)) for _, dil in B_GROUPS],
        out_shape=[jax.ShapeDtypeStruct((t // dil, dil * B_GROUP_COLS), BF16) for _, dil in B_GROUPS],
        scratch_shapes=[pltpu.VMEM((n // LANES, tm, LANES), F32)],
        compiler_params=_params("parallel"),
        name="qkv_proj_b",
    )(h2, gain, w)


def _rope_tables(seq):
    inv_freq = ROPE_THETA ** (-jnp.arange(0, HALF_HEAD, 2, dtype=F32) / HALF_HEAD)
    rows = seq // GRID_W
    ang_row = jnp.arange(rows, dtype=F32)[:, None] * inv_freq
    ang_col = jnp.arange(GRID_W, dtype=F32)[:, None] * inv_freq
    per_row = lambda x: jnp.repeat(x, GRID_W, axis=0)
    per_col = lambda x: jnp.tile(x, (rows, 1))
    cos = jnp.concatenate([per_row(jnp.cos(ang_row))] * 2 + [per_col(jnp.cos(ang_col))] * 2, axis=1)
    sin_r, sin_c = per_row(jnp.sin(ang_row)), per_col(jnp.sin(ang_col))
    sin = jnp.concatenate([-sin_r, sin_r, -sin_c, sin_c], axis=1)
    return jnp.concatenate([cos, cos], axis=1), jnp.concatenate([sin, sin], axis=1)


def _attn_a_body(q_ref, k_ref, v_ref, o_ref, klo, khi, vt1, s_buf, *, tq, tk, seq, tiles_per_step):
    def fill(head_in_high_lanes):
        low = lax.broadcasted_iota(jnp.int32, (seq, LANES), 1) < HEAD_DIM
        zero = jnp.zeros((seq, LANES), BF16)
        x = k_ref[0]
        swapped = jnp.concatenate([x[:, HEAD_DIM:], x[:, :HEAD_DIM]], axis=1)
        in_low, in_high = (swapped, x) if head_in_high_lanes else (x, swapped)
        klo[...] = jnp.where(low, in_low, zero)
        khi[...] = jnp.where(low, zero, in_high)
        ones_row = (lax.broadcasted_iota(jnp.int32, (VT_ROWS - HEAD_DIM, tk), 0) == 0).astype(F32)
        for c in range(seq // tk):
            xt = v_ref[0, c * tk:(c + 1) * tk, :].astype(F32).T
            vt = xt[HEAD_DIM:] if head_in_high_lanes else xt[:HEAD_DIM]
            vt1[:, c * tk:(c + 1) * tk] = jnp.concatenate([vt, ones_row], axis=0).astype(BF16)

    odd = (pl.program_id(1) % 2) == 1
    pl.when(jnp.logical_not(odd))(lambda: fill(False))
    pl.when(odd)(lambda: fill(True))

    cols = 2 * tq
    n_chunks = seq // tk
    n_tiles = seq // tq

    def q_transposed(tile):
        q = q_ref[0, pl.ds(pl.multiple_of(tile * tq, tq), tq), :]
        q2 = jnp.concatenate([q[:, :LANES], q[:, LANES:]], axis=0)
        return q2.astype(F32).T.astype(BF16)

    def scores(q2t, c, slot):
        cms = []
        for par, k_s in enumerate((klo, khi)):
            s = jnp.dot(k_s[c * tk:(c + 1) * tk, :], q2t, preferred_element_type=F32)
            s_buf[slot, par] = s
            cms.append(jnp.max(s, axis=0, keepdims=True))
        return tuple(cms)

    def accumulate(c, slot, cms, state):
        vt = vt1[:, c * tk:(c + 1) * tk]
        new = []
        for par in range(2):
            m, acc = state[2 * par], state[2 * par + 1]
            mn = jnp.maximum(m, cms[par])
            p = jnp.exp2(s_buf[slot, par] - mn)
            new += [mn, acc * jnp.exp2(m - mn) + jnp.dot(vt, p.astype(BF16), preferred_element_type=F32)]
        return tuple(new)

    neg = jnp.full((1, cols), -jnp.inf, F32)
    acc0 = jnp.zeros((VT_ROWS, cols), F32)

    def step(j, cms):
        q2t = q_transposed(j * tiles_per_step)
        for t in range(tiles_per_step):
            tile = j * tiles_per_step + t
            state = (neg, acc0, neg, acc0)
            for c in range(n_chunks):
                n = t * n_chunks + c
                if c + 1 == n_chunks:
                    q2t = q_transposed(jnp.minimum(tile + 1, n_tiles - 1))
                cms_next = scores(q2t, (c + 1) % n_chunks, (n + 1) % 2)
                state = accumulate(c, n % 2, cms, state)
                cms = cms_next
            out = jnp.concatenate(
                [acc[:HEAD_DIM] / acc[HEAD_DIM:HEAD_DIM + 1] for acc in (state[1], state[3])], axis=0)
            o_ref[0, pl.ds(pl.multiple_of(tile * tq, tq), tq), :] = jnp.concatenate(
                [out[:, :tq].T, out[:, tq:].T], axis=1).astype(o_ref.dtype)
        return cms

    lax.fori_loop(0, n_tiles // tiles_per_step, step, scores(q_transposed(0), 0, 0))


def _attn_a(qkv, *, n_heads, tq=256, tk=512, tiles_per_step=4):
    b, seq, _ = qkv.shape
    kv_heads = n_heads // A_REP
    q_cols = n_heads * HEAD_DIM
    q_w = A_REP * HEAD_DIM
    k_blk0 = q_cols // LANES
    v_blk0 = (q_cols + kv_heads * HEAD_DIM) // LANES
    assert (seq // tk) % 2 == 0 and seq % (tq * tiles_per_step) == 0
    return pl.pallas_call(
        functools.partial(_attn_a_body, tq=tq, tk=tk, seq=seq, tiles_per_step=tiles_per_step),
        grid=(b, kv_heads),
        in_specs=[pl.BlockSpec((1, seq, q_w), lambda bi, h: (bi, 0, h)),
                  pl.BlockSpec((1, seq, LANES), lambda bi, h: (bi, 0, k_blk0 + h // 2)),
                  pl.BlockSpec((1, seq, LANES), lambda bi, h: (bi, 0, v_blk0 + h // 2))],
        out_specs=pl.BlockSpec((1, seq, q_w), lambda bi, h: (bi, 0, h)),
        out_shape=jax.ShapeDtypeStruct((b, seq, q_cols), BF16),
        scratch_shapes=[pltpu.VMEM((seq, LANES), BF16), pltpu.VMEM((seq, LANES), BF16),
                        pltpu.VMEM((VT_ROWS, seq), BF16), pltpu.VMEM((2, 2, tk, 2 * tq), F32)],
        compiler_params=_params("parallel", "arbitrary"),
        name="attn_a",
    )(qkv, qkv, qkv)


def _fill_band_bias(bias_ref, slopes, half_window, dist_scale):
    span = bias_ref.shape[2]
    c = lax.broadcasted_iota(jnp.int32, (span, BLOCK_Q), 0)
    r = lax.broadcasted_iota(jnp.int32, (span, BLOCK_Q), 1)
    dist = jnp.abs(c - half_window - r)
    inside = dist <= half_window
    penalty = dist.astype(F32) * (dist_scale * LOG2E)
    for variant, ok in enumerate((inside & (c >= half_window), inside, inside & (c < half_window + BLOCK_Q))):
        for h, slope in enumerate(slopes):
            bias_ref[variant, h] = jnp.where(ok, -slope * penalty, NEG_BIG)


def _band_pipeline(jobs, bias_ref, s_buf):
    def scores(job, slot):
        kmat, q_rows, head_ids, _, _, variant, sink_row = job
        width = len(head_ids) * BLOCK_Q
        s = _nt_dot(kmat, q_rows) + jnp.concatenate([bias_ref[variant, h] for h in head_ids], axis=1)
        s_buf[slot, :, :width] = s
        m = jnp.max(s, axis=0, keepdims=True)
        return m if sink_row is None else jnp.maximum(m, sink_row)

    def values(job, slot, m):
        _, _, head_ids, kv_ids, vt1s, _, sink_row = job
        n = len(head_ids)
        p = jnp.exp2(s_buf[slot, :, :n * BLOCK_Q] - m).astype(BF16)
        accs, start = [], 0
        while start < n:
            stop = start
            while stop < n and kv_ids[stop] == kv_ids[start]:
                stop += 1
            accs.append(jnp.dot(vt1s[kv_ids[start]], p[:, start * BLOCK_Q:stop * BLOCK_Q],
                                preferred_element_type=F32))
            start = stop
        acc = accs[0] if len(accs) == 1 else jnp.concatenate(accs, axis=1)
        den = acc[HEAD_DIM:HEAD_DIM + 1]
        if sink_row is not None:
            den = den + jnp.exp2(sink_row - m)
        return acc[:HEAD_DIM] / den, m, den

    results = []
    n_slots = s_buf.shape[0]
    ahead = n_slots - 1
    maxima = [scores(jobs[c], c % n_slots) for c in range(min(ahead, len(jobs)))]
    for c, job in enumerate(jobs):
        if c + ahead < len(jobs):
            maxima.append(scores(jobs[c + ahead], (c + ahead) % n_slots))
        results.append(values(job, c % n_slots, maxima[c]))
    return results


def _band_operands(k_refs, v_refs):
    k = jnp.concatenate([r[0] for r in k_refs], axis=0)
    vt = jnp.concatenate([r[0] for r in v_refs], axis=0).astype(F32).T
    return k, vt


def _vt_with_ones(vt, kv, start, span):
    ones_row = (lax.broadcasted_iota(jnp.int32, (VT_ROWS - HEAD_DIM, span), 0) == 0).astype(F32)
    rows = vt[kv * HEAD_DIM:(kv + 1) * HEAD_DIM, start:start + span]
    return jnp.concatenate([rows, ones_row], axis=0).astype(BF16)


def _swap_halves(x):
    return jnp.concatenate([x[:, HEAD_DIM:], x[:, :HEAD_DIM]], axis=1)


def _attn_c_body(sink_ref, q_ref, kp_ref, kc_ref, kn_ref, vp_ref, vc_ref, vn_ref, o_ref, bias_ref, s_buf,
                 *, n_heads, slopes):
    i = pl.program_id(1)
    pl.when(i == 0)(lambda: _fill_band_bias(bias_ref, slopes, C_WINDOW, 1.0))
    n_blocks = C_TILE // BLOCK_Q
    block_variants = ([jnp.where(i == 0, 0, 1)] + [1] * (n_blocks - 2)
                      + [jnp.where(i == pl.num_programs(1) - 1, 2, 1)])
    q = q_ref[0]
    k, vt = _band_operands((kp_ref, kc_ref, kn_ref), (vp_ref, vc_ref, vn_ref))
    span = BLOCK_Q + 2 * C_WINDOW
    low = lax.broadcasted_iota(jnp.int32, (BLOCK_Q, LANES), 1) < HEAD_DIM
    zero = jnp.zeros((BLOCK_Q, LANES), BF16)
    first_block = lax.broadcasted_iota(jnp.int32, (1, 2 * BLOCK_Q), 1) < BLOCK_Q
    jobs, job_block = [], []
    for blk in range(n_blocks):
        q_lo = blk * BLOCK_Q
        for pair in range(n_heads // A_REP // 2):
            ks = k[q_lo:q_lo + span, pair * LANES:(pair + 1) * LANES]
            ks_swapped = _swap_halves(ks)
            for which in range(2):
                kv = 2 * pair + which
                vt1s = {kv: _vt_with_ones(vt, kv, q_lo, span)}
                slabs = [q[q_lo:q_lo + BLOCK_Q, (2 * kv + j) * LANES:(2 * kv + j + 1) * LANES] for j in range(2)]
                for parity in range(2):
                    q_rows = jnp.concatenate(
                        [jnp.where(low, s_, zero) if parity == 0 else jnp.where(low, zero, s_) for s_ in slabs],
                        axis=0)
                    heads = (A_REP * kv + parity, A_REP * kv + 2 + parity)
                    sink_row = jnp.where(first_block, sink_ref[heads[0]], sink_ref[heads[1]]) * LOG2E
                    jobs.append((ks if parity == which else ks_swapped, q_rows, heads, (kv, kv), vt1s,
                                 block_variants[blk], sink_row))
                    job_block.append(blk)
    o_t = {}
    for job, blk, (ot, _, _) in zip(jobs, job_block, _band_pipeline(jobs, bias_ref, s_buf)):
        o_t[blk, job[2][0]], o_t[blk, job[2][1]] = ot[:, :BLOCK_Q], ot[:, BLOCK_Q:]
    for blk in range(n_blocks):
        for j in range(n_heads // 2):
            pair_t = jnp.concatenate([o_t[blk, 2 * j], o_t[blk, 2 * j + 1]], axis=0)
            o_ref[0, blk * BLOCK_Q:(blk + 1) * BLOCK_Q, j * LANES:(j + 1) * LANES] = pair_t.T.astype(o_ref.dtype)


def _alibi_slopes(n):
    return [2.0 ** (-8.0 * (i + 1) / n) for i in range(n)]


def _attn_c(qkv, sinks, *, n_heads):
    b, seq, _ = qkv.shape
    kv_heads = n_heads // A_REP
    q_cols = n_heads * HEAD_DIM
    kv_cols = kv_heads * HEAD_DIM
    assert seq % C_TILE == 0 and C_TILE >= 2 * BLOCK_Q
    n_tiles = seq // C_TILE
    n_edges = seq // C_WINDOW
    edges_per_tile = C_TILE // C_WINDOW
    k_blk = q_cols // kv_cols
    v_blk = k_blk + 1
    span = BLOCK_Q + 2 * C_WINDOW

    def kv_specs(col_blk):
        edge = lambda offset: pl.BlockSpec(
            (1, C_WINDOW, kv_cols),
            lambda bi, i: (bi, jnp.clip(i * edges_per_tile + offset, 0, n_edges - 1), col_blk))
        return [edge(-1), pl.BlockSpec((1, C_TILE, kv_cols), lambda bi, i: (bi, i, col_blk)),
                edge(edges_per_tile)]

    return pl.pallas_call(
        functools.partial(_attn_c_body, n_heads=n_heads, slopes=_alibi_slopes(n_heads)),
        grid=(b, n_tiles),
        in_specs=[pl.BlockSpec(memory_space=pltpu.SMEM),
                  pl.BlockSpec((1, C_TILE, q_cols), lambda bi, i: (bi, i, 0))] + kv_specs(k_blk) + kv_specs(v_blk),
        out_specs=pl.BlockSpec((1, C_TILE, q_cols), lambda bi, i: (bi, i, 0)),
        out_shape=jax.ShapeDtypeStruct((b, seq, q_cols), BF16),
        scratch_shapes=[pltpu.VMEM((3, n_heads, span, BLOCK_Q), F32),
                        pltpu.VMEM((BAND_SLOTS, span, 2 * BLOCK_Q), F32)],
        compiler_params=_params("parallel", "arbitrary"),
        name="attn_c",
    )(sinks, qkv, qkv, qkv, qkv, qkv, qkv, qkv)


REFS_PER_SEQ = 9


def _attn_b_body(*refs, slopes, dil, n_seq):
    in_refs, (o_ref, lse_ref, bias_ref, s_buf) = refs[:n_seq * REFS_PER_SEQ], refs[n_seq * REFS_PER_SEQ:]
    i = pl.program_id(2)
    pl.when(jnp.logical_and(pl.program_id(1) == 0, i == 0))(
        lambda: _fill_band_bias(bias_ref, slopes, B_HALF, float(dil)))
    n_blocks = in_refs[0].shape[1] // BLOCK_Q
    block_variants = ([jnp.where(i == 0, 0, 1)] + [1] * (n_blocks - 2)
                      + [jnp.where(i == pl.num_programs(2) - 1, 2, 1)])
    low = lax.broadcasted_iota(jnp.int32, (BLOCK_Q, LANES), 1) < HEAD_DIM
    zero = jnp.zeros((BLOCK_Q, LANES), BF16)
    heads_all = range(B_HEADS_PER_GROUP)
    plain = [h for h in heads_all if h % 2 == h // B_REP]
    crossed = [h for h in heads_all if h % 2 != h // B_REP]
    jobs, job_place = [], []
    for seq in range(n_seq):
        q0, q1, q2, kp, kc, kn, vp, vc, vn = in_refs[seq * REFS_PER_SEQ:(seq + 1) * REFS_PER_SEQ]
        k, vt = _band_operands((kp, kc, kn), (vp, vc, vn))
        slabs = [q0[0], q1[0], q2[0]]
        for blk in range(n_blocks):
            q_lo = blk * BLOCK_Q
            ks = k[q_lo:q_lo + B_SPAN]
            ks_swapped = _swap_halves(ks)
            vt1s = {kv: _vt_with_ones(vt, kv, q_lo, B_SPAN) for kv in range(B_KV_PER_GROUP)}
            for kmat, heads in ((ks, plain[:2]), (ks_swapped, crossed), (ks, plain[2:])):
                q_rows = jnp.concatenate(
                    [jnp.where(low, slabs[h // 2][q_lo:q_lo + BLOCK_Q], zero) if h % 2 == 0
                     else jnp.where(low, zero, slabs[h // 2][q_lo:q_lo + BLOCK_Q]) for h in heads], axis=0)
                jobs.append((kmat, q_rows, heads, [h // B_REP for h in heads], vt1s, block_variants[blk], None))
                job_place.append((seq, blk))
    o_t, lse_t = {}, {}
    for job, place, (ot, m, den) in zip(jobs, job_place, _band_pipeline(jobs, bias_ref, s_buf)):
        lse = (m + jnp.log2(den)) * LN2
        for n, h in enumerate(job[2]):
            o_t[place, h] = ot[:, n * BLOCK_Q:(n + 1) * BLOCK_Q]
            lse_t[place, h] = jnp.broadcast_to(lse[:, n * BLOCK_Q:(n + 1) * BLOCK_Q], (HEAD_DIM, BLOCK_Q))
    gq = B_HEADS_PER_GROUP * HEAD_DIM
    for seq in range(n_seq):
        for blk in range(n_blocks):
            rows = slice(blk * BLOCK_Q, (blk + 1) * BLOCK_Q)
            for j in range(B_HEADS_PER_GROUP // 2):
                cols = slice(seq * gq + j * LANES, seq * gq + (j + 1) * LANES)
                pair = lambda t: jnp.concatenate([t[(seq, blk), 2 * j], t[(seq, blk), 2 * j + 1]], axis=0).T
                o_ref[0, rows, cols] = pair(o_t).astype(o_ref.dtype)
                lse_ref[0, rows, cols] = pair(lse_t)


def _attn_b(group_qkv, b):
    n_g = len(B_GROUPS)
    gq = B_HEADS_PER_GROUP * HEAD_DIM
    slabs_per_token = B_GROUP_COLS // LANES
    q_slabs = gq // LANES
    all_slopes = _alibi_slopes(n_g * B_HEADS_PER_GROUP)
    outs, lses = [], []
    for g, (window, dil) in enumerate(B_GROUPS):
        assert (window // 2) // dil == B_HALF
        sub = group_qkv[g].shape[0] // b
        tile = min(B_TILE, sub)
        n_seq = min(B_TILE // tile, dil)
        assert tile >= 2 * BLOCK_Q and sub % tile == 0 and dil % n_seq == 0
        n_tiles = sub // tile
        n_halves = sub // B_HALF
        halves_per_tile = tile // B_HALF
        view = group_qkv[g].reshape(b, sub, dil * B_GROUP_COLS)

        def seq_specs(seq, tile=tile, n_seq=n_seq, n_halves=n_halves, halves_per_tile=halves_per_tile):
            col = lambda slab: (lambda bi, rr, i: (rr * n_seq + seq) * slabs_per_token + slab)

            def tile_spec(slab):
                return pl.BlockSpec((1, tile, LANES), lambda bi, rr, i: (bi, i, col(slab)(bi, rr, i)))

            def edge_spec(slab, offset):
                return pl.BlockSpec(
                    (1, B_HALF, LANES),
                    lambda bi, rr, i: (bi, jnp.clip(i * halves_per_tile + offset, 0, n_halves - 1),
                                       col(slab)(bi, rr, i)))

            kv_specs = lambda slab: [edge_spec(slab, -1), tile_spec(slab), edge_spec(slab, halves_per_tile)]
            return [tile_spec(j) for j in range(q_slabs)] + kv_specs(q_slabs) + kv_specs(q_slabs + 1)

        out_map = lambda bi, rr, i: (bi, i, rr)
        out_spec = pl.BlockSpec((1, tile, n_seq * gq), out_map)
        o, lse = pl.pallas_call(
            functools.partial(_attn_b_body, dil=dil, n_seq=n_seq,
                              slopes=all_slopes[g * B_HEADS_PER_GROUP:(g + 1) * B_HEADS_PER_GROUP]),
            grid=(b, dil // n_seq, n_tiles),
            in_specs=sum((seq_specs(seq) for seq in range(n_seq)), []),
            out_specs=[out_spec, out_spec],
            out_shape=[jax.ShapeDtypeStruct((b, sub, dil * gq), BF16),
                       jax.ShapeDtypeStruct((b, sub, dil * gq), F32)],
            scratch_shapes=[pltpu.VMEM((3, B_HEADS_PER_GROUP, B_SPAN, BLOCK_Q), F32),
                            pltpu.VMEM((BAND_SLOTS, B_SPAN, 2 * BLOCK_Q), F32)],
            compiler_params=_params("parallel", "arbitrary", "arbitrary"),
            name=f"attn_b_g{g}",
        )(*([view] * (n_seq * REFS_PER_SEQ)))
        outs.append(o.reshape(b * sub, dil * gq))
        lses.append(lse.reshape(b * sub, dil * gq))
    return outs, lses


PROLOGUE_ROW_CHUNKS = 4


def _mix_plain(refs, scratch):
    return lambda rows: refs[0][rows, :]


def _mix_groups(refs, scratch):
    n_g = len(B_GROUPS)
    o_scr, l_scr = scratch
    slabs = o_scr.shape[0] // n_g
    tm = o_scr.shape[1]
    gq = slabs * LANES
    for g, (_, dil) in enumerate(B_GROUPS):
        for r in range(dil):
            rows = pl.ds(r, tm // dil, stride=dil)
            for s in range(slabs):
                cols = slice(r * gq + s * LANES, r * gq + (s + 1) * LANES)
                o_scr[g * slabs + s, rows, :] = refs[g][:, cols].astype(F32)
                l_scr[g * slabs + s, rows, :] = refs[n_g + g][:, cols]
    def lhs(rows):
        group = lambda scr, g: jnp.concatenate([scr[g * slabs + s, rows, :] for s in range(slabs)], axis=1)
        ls = [group(l_scr, g) for g in range(n_g)]
        mx = functools.reduce(jnp.maximum, ls)
        es = [jnp.exp(l - mx) for l in ls]
        inv = 1.0 / functools.reduce(lambda a, b: a + b, es)
        return jnp.concatenate([(group(o_scr, g) * (es[g] * inv)).astype(BF16) for g in range(n_g)], axis=1)

    return lhs


MLP_HIDDEN_CHUNK = 1024


def _block_body(*refs, n_mix, mix_fn, final_norm):
    h_ref, mix_refs = refs[0], refs[1:1 + n_mix]
    wo_ref, g_ref, w1_ref, w2_ref, fg_ref, out_ref, hn_ref = refs[1 + n_mix:8 + n_mix]
    mix_scratch = refs[8 + n_mix:]

    lhs = mix_fn(mix_refs, mix_scratch)
    chunk = h_ref.shape[0] // PROLOGUE_ROW_CHUNKS
    rows = [slice(c * chunk, (c + 1) * chunk) for c in range(PROLOGUE_ROW_CHUNKS)]
    project = lambda c: jnp.dot(lhs(rows[c]), wo_ref[...], preferred_element_type=F32)
    nxt = project(0)
    for c in range(PROLOGUE_ROW_CHUNKS):
        cur = nxt
        if c + 1 < PROLOGUE_ROW_CHUNKS:
            nxt = project(c + 1)
        h1 = h_ref[rows[c], :] + cur
        out_ref[rows[c], :] = h1
        hn_ref[rows[c], :] = _rms_rows(h1, g_ref[...]).astype(BF16)

    hn = hn_ref[...]
    for c in range(w1_ref.shape[1] // MLP_HIDDEN_CHUNK):
        cols = slice(c * MLP_HIDDEN_CHUNK, (c + 1) * MLP_HIDDEN_CHUNK)
        u = jnp.maximum(jnp.dot(hn, w1_ref[:, cols], preferred_element_type=F32), 0.0)
        out_ref[...] += jnp.dot((u * u).astype(BF16), w2_ref[cols, :], preferred_element_type=F32)

    if final_norm:
        out_ref[...] = _rms_rows(out_ref[...], fg_ref[...])


def _block(h2, mix_inputs, w_o, *, gain, w1, w2, layer, final_gain, final_norm, tm=512):
    t, d = h2.shape
    assert t % tm == 0 and tm % PROLOGUE_ROW_CHUNKS == 0 and w1.shape[2] % MLP_HIDDEN_CHUNK == 0
    row_tile = lambda cols: pl.BlockSpec((tm, cols), lambda i: (i, 0))
    whole = lambda a: pl.BlockSpec(a.shape, lambda i: (0, 0), pipeline_mode=pl.Buffered(1))
    of_layer = lambda a: pl.BlockSpec((None,) + a.shape[1:], lambda i: (layer, 0, 0),
                                      pipeline_mode=pl.Buffered(1))
    scratch = [pltpu.VMEM((tm, d), BF16)]
    if len(mix_inputs) == 1:
        mix_fn, mix_specs = _mix_plain, [row_tile(mix_inputs[0].shape[1])]
    else:
        n_g = len(B_GROUPS)
        gq = B_HEADS_PER_GROUP * HEAD_DIM
        mix_fn = _mix_groups
        mix_specs = [pl.BlockSpec((tm // dil, dil * gq), lambda i: (i, 0)) for _, dil in B_GROUPS] * 2
        scratch += [pltpu.VMEM((n_g * gq // LANES, tm, LANES), F32)] * 2
    return pl.pallas_call(
        functools.partial(_block_body, n_mix=len(mix_inputs), mix_fn=mix_fn, final_norm=final_norm),
        grid=(t // tm,),
        in_specs=[row_tile(d)] + mix_specs
                 + [whole(w_o), whole(gain), of_layer(w1), of_layer(w2), whole(final_gain)],
        out_specs=row_tile(d),
        out_shape=jax.ShapeDtypeStruct((t, d), F32),
        scratch_shapes=scratch,
        compiler_params=_params("parallel"),
        name="block_mlp",
    )(h2, *mix_inputs, w_o, gain, w1, w2, final_gain)


def kernel(x, attn_norm, mlp_norm, a_w_qkv, a_q_gain, a_k_gain, a_w_o, b_w_qkv, b_w_o,
           c_w_qkv, c_sinks, c_w_o, mlp_w1, mlp_w2, final_norm):
    b, seq, d = x.shape
    depth = attn_norm.shape[0]
    h = x.reshape(b * seq, d)
    scale = HEAD_DIM ** -0.5 * LOG2E
    a_heads = a_w_o.shape[1] // HEAD_DIM
    c_heads = c_w_o.shape[1] // HEAD_DIM
    cos, sin = _rope_tables(seq)
    head = jnp.arange(2 * LANES) // HEAD_DIM
    seg = (head[:, None] == head[None, :]).astype(BF16)
    tile4 = lambda g: jnp.tile(g, 2 * LANES // HEAD_DIM)[None, :]
    w1_all, w2_all = mlp_w1.astype(BF16), mlp_w2.astype(BF16)
    used = [0, 0, 0]
    for layer in range(depth):
        kind = layer % N_MIXERS
        j = used[kind]
        used[kind] += 1
        gain = attn_norm[layer][None, :]
        block = functools.partial(_block, gain=mlp_norm[layer][None, :], w1=w1_all, w2=w2_all, layer=layer,
                                  final_gain=final_norm[None, :], final_norm=(layer == depth - 1))
        if kind == 0:
            q_cols = a_heads * HEAD_DIM
            k_cols = (a_w_qkv.shape[2] - q_cols) // 2
            qkv = _qkv_proj(h, gain, a_w_qkv[j].astype(BF16), q_cols=q_cols, q_scale=scale,
                            rope=(tile4(a_q_gain[j]), tile4(a_k_gain[j]), cos, sin, seg, k_cols))
            o = _attn_a(qkv.reshape(b, seq, -1), n_heads=a_heads).reshape(b * seq, q_cols)
            h = block(h, [o], a_w_o[j].astype(BF16))
        elif kind == 1:
            q_cols = len(B_GROUPS) * B_HEADS_PER_GROUP * HEAD_DIM
            groups = _qkv_proj_b(h, gain, b_w_qkv[j].astype(BF16), q_cols=q_cols, q_scale=scale)
            outs, lses = _attn_b(groups, b)
            h = block(h, outs + lses, b_w_o[j].astype(BF16))
        else:
            q_cols = c_heads * HEAD_DIM
            qkv = _qkv_proj(h, gain, c_w_qkv[j].astype(BF16), q_cols=q_cols, q_scale=scale)
            o = _attn_c(qkv.reshape(b, seq, -1), c_sinks[j], n_heads=c_heads).reshape(b * seq, q_cols)
            h = block(h, [o], c_w_o[j].astype(BF16))
    return h.reshape(b, seq, d)
```

```python
import functools
import math

import jax
import jax.numpy as jnp
from jax import lax
from jax.experimental import pallas as pl
from jax.experimental.pallas import tpu as pltpu

HEAD_DIM = 64
HALF_HEAD = HEAD_DIM // 2
RMS_EPS = 1e-6
GRID_W = 64
ROPE_THETA = 10000.0
N_MIXERS = 3
A_REP = 4
B_GROUPS = ((128, 1), (512, 4), (2048, 16))
B_HEADS_PER_GROUP = 6
B_KV_PER_GROUP = 2
B_REP = B_HEADS_PER_GROUP // B_KV_PER_GROUP
C_WINDOW = 128
LANES = 128
BF16_SUBLANES = 16
BLOCK_Q = 128
BAND_SLOTS = 4
C_TILE = 8 * BLOCK_Q
B_HALF = BLOCK_Q // 2
B_TILE = 8 * BLOCK_Q
B_SPAN = BLOCK_Q + 2 * B_HALF
B_GROUP_COLS = (B_HEADS_PER_GROUP + 2 * B_KV_PER_GROUP) * HEAD_DIM
VT_ROWS = HEAD_DIM + BF16_SUBLANES
V7X_VMEM_LIMIT_BYTES = 48 * 1024 * 1024
NEG_BIG = -1e30
LOG2E = math.log2(math.e)
LN2 = math.log(2.0)

BF16 = jnp.bfloat16
F32 = jnp.float32


def _params(*sem):
    return pltpu.CompilerParams(dimension_semantics=sem, vmem_limit_bytes=V7X_VMEM_LIMIT_BYTES)


def _nt_dot(a, b):
    return lax.dot_general(a, b, (((1,), (1,)), ((), ())), preferred_element_type=F32)


def _rms_rows(x, gain):
    ms = jnp.mean(x * x, axis=-1, keepdims=True)
    return x * lax.rsqrt(ms + RMS_EPS) * gain


def _qkv_body(x_ref, g_ref, w_ref, o_ref, *, q_cols, q_scale):
    hn = _rms_rows(x_ref[...], g_ref[...]).astype(BF16)
    y = jnp.dot(hn, w_ref[...], preferred_element_type=F32)
    o_ref[:, :q_cols] = (y[:, :q_cols] * q_scale).astype(o_ref.dtype)
    o_ref[:, q_cols:] = y[:, q_cols:].astype(o_ref.dtype)


def _qkv_rope_body(x_ref, g_ref, w_ref, qg_ref, kg_ref, cos_ref, sin_ref, seg_ref, o_ref,
                   *, q_cols, k_cols, q_scale):
    hn = _rms_rows(x_ref[...], g_ref[...]).astype(BF16)
    slab = 2 * LANES
    cos = jnp.concatenate([cos_ref[...], cos_ref[...]], axis=1)
    sin = jnp.concatenate([sin_ref[...], sin_ref[...]], axis=1)
    lane = lax.broadcasted_iota(jnp.int32, (x_ref.shape[0], slab), 1)
    first_half = (lane % HALF_HEAD) < (HALF_HEAD // 2)
    seg = seg_ref[...]
    n_qk = (q_cols + k_cols) // slab
    project = lambda s: jnp.dot(hn, w_ref[:, s * slab:(s + 1) * slab], preferred_element_type=F32)
    y_next = project(0)
    for s in range(n_qk):
        is_q = s * slab < q_cols
        ys = y_next
        y_next = project(s + 1) if s + 1 < n_qk else jnp.dot(
            hn, w_ref[:, q_cols + k_cols:], preferred_element_type=F32)
        sq = ys * ys
        hi = sq.astype(BF16)
        lo = (sq - hi.astype(F32)).astype(BF16)
        ss = jnp.dot(hi, seg, preferred_element_type=F32) + jnp.dot(lo, seg, preferred_element_type=F32)
        gain = qg_ref[...] if is_q else kg_ref[...]
        yn = ys * lax.rsqrt(ss * (1.0 / HEAD_DIM) + RMS_EPS) * gain
        partner = jnp.where(first_half,
                            pltpu.roll(yn, slab - HALF_HEAD // 2, axis=1),
                            pltpu.roll(yn, HALF_HEAD // 2, axis=1))
        r = yn * cos + partner * sin
        if is_q:
            r = r * q_scale
        o_ref[:, s * slab:(s + 1) * slab] = r.astype(o_ref.dtype)
    o_ref[:, q_cols + k_cols:] = y_next.astype(o_ref.dtype)


def _qkv_proj(h2, gain, w, *, q_cols, q_scale, tm=1024, rope=None):
    t, d = h2.shape
    n = w.shape[1]
    assert t % tm == 0 and q_cols % (2 * LANES) == 0 and n % LANES == 0
    common = dict(
        grid=(t // tm,),
        out_specs=pl.BlockSpec((tm, n), lambda i: (i, 0)),
        out_shape=jax.ShapeDtypeStruct((t, n), BF16),
        compiler_params=_params("parallel"),
    )
    x_spec = pl.BlockSpec((tm, d), lambda i: (i, 0))
    g_spec = pl.BlockSpec((1, d), lambda i: (0, 0))
    w_spec = pl.BlockSpec((d, n), lambda i: (0, 0))
    if rope is None:
        return pl.pallas_call(
            functools.partial(_qkv_body, q_cols=q_cols, q_scale=q_scale),
            in_specs=[x_spec, g_spec, w_spec], name="qkv_proj", **common,
        )(h2, gain, w)
    qg, kg, cos, sin, seg, k_cols = rope
    seq_tiles = cos.shape[0] // tm
    slab = 2 * LANES
    return pl.pallas_call(
        functools.partial(_qkv_rope_body, q_cols=q_cols, k_cols=k_cols, q_scale=q_scale),
        in_specs=[x_spec, g_spec, w_spec,
                  pl.BlockSpec((1, slab), lambda i: (0, 0)),
                  pl.BlockSpec((1, slab), lambda i: (0, 0)),
                  pl.BlockSpec((tm, LANES), lambda i: (i % seq_tiles, 0)),
                  pl.BlockSpec((tm, LANES), lambda i: (i % seq_tiles, 0)),
                  pl.BlockSpec((slab, slab), lambda i: (0, 0))],
        name="qkv_proj_rope", **common,
    )(h2, gain, w, qg, kg, cos, sin, seg)


def _qkv_b_body(x_ref, g_ref, w_ref, *rest, q_cols, q_scale):
    out_refs, y_scr = rest[:-1], rest[-1]
    tm = x_ref.shape[0]
    hn = _rms_rows(x_ref[...], g_ref[...]).astype(BF16)
    y = jnp.dot(hn, w_ref[...], preferred_element_type=F32)
    for s in range(y_scr.shape[0]):
        ys = y[:, s * LANES:(s + 1) * LANES]
        y_scr[s] = ys * q_scale if s * LANES < q_cols else ys
    n_g = len(B_GROUPS)
    q_slabs = B_HEADS_PER_GROUP * HEAD_DIM // LANES
    k0 = q_cols // LANES
    for g, (_, dil) in enumerate(B_GROUPS):
        slabs = list(range(g * q_slabs, (g + 1) * q_slabs)) + [k0 + g, k0 + n_g + g]
        for r in range(dil):
            rows = pl.ds(r, tm // dil, stride=dil)
            piece = jnp.concatenate([y_scr[s, rows, :] for s in slabs], axis=1)
            out_refs[g][:, r * B_GROUP_COLS:(r + 1) * B_GROUP_COLS] = piece.astype(BF16)


def _qkv_proj_b(h2, gain, w, *, q_cols, q_scale, tm=1024):
    t, d = h2.shape
    n = w.shape[1]
    assert t % tm == 0 and n == len(B_GROUPS) * B_GROUP_COLS
    assert all((tm // dil) % BF16_SUBLANES == 0 for _, dil in B_GROUPS)
    return pl.pallas_call(
        functools.partial(_qkv_b_body, q_cols=q_cols, q_scale=q_scale),
        grid=(t // tm,),
        in_specs=[pl.BlockSpec((tm, d), lambda i: (i, 0)),
                  pl.BlockSpec((1, d), lambda i: (0, 0)),
                  pl.BlockSpec((d, n), lambda i: (0, 0))],
        out_specs=[pl.BlockSpec((tm // dil, dil * B_GROUP_COLS), lambda i: (i, 0)) for _, dil in B_GROUPS],
        out_shape=[jax.ShapeDtypeStruct((t // dil, dil * B_GROUP_COLS), BF16) for _, dil in B_GROUPS],
        scratch_shapes=[pltpu.VMEM((n // LANES, tm, LANES), F32)],
        compiler_params=_params("parallel"),
        name="qkv_proj_b",
    )(h2, gain, w)


def _rope_tables(seq):
    inv_freq = ROPE_THETA ** (-jnp.arange(0, HALF_HEAD, 2, dtype=F32) / HALF_HEAD)
    rows = seq // GRID_W
    ang_row = jnp.arange(rows, dtype=F32)[:, None] * inv_freq
    ang_col = jnp.arange(GRID_W, dtype=F32)[:, None] * inv_freq
    per_row = lambda x: jnp.repeat(x, GRID_W, axis=0)
    per_col = lambda x: jnp.tile(x, (rows, 1))
    cos = jnp.concatenate([per_row(jnp.cos(ang_row))] * 2 + [per_col(jnp.cos(ang_col))] * 2, axis=1)
    sin_r, sin_c = per_row(jnp.sin(ang_row)), per_col(jnp.sin(ang_col))
    sin = jnp.concatenate([-sin_r, sin_r, -sin_c, sin_c], axis=1)
    return jnp.concatenate([cos, cos], axis=1), jnp.concatenate([sin, sin], axis=1)


def _attn_a_body(q_ref, k_ref, v_ref, o_ref, klo, khi, vt1, s_buf, *, tq, tk, seq, tiles_per_step):
    def fill(head_in_high_lanes):
        low = lax.broadcasted_iota(jnp.int32, (seq, LANES), 1) < HEAD_DIM
        zero = jnp.zeros((seq, LANES), BF16)
        x = k_ref[0]
        swapped = jnp.concatenate([x[:, HEAD_DIM:], x[:, :HEAD_DIM]], axis=1)
        in_low, in_high = (swapped, x) if head_in_high_lanes else (x, swapped)
        klo[...] = jnp.where(low, in_low, zero)
        khi[...] = jnp.where(low, zero, in_high)
        ones_row = (lax.broadcasted_iota(jnp.int32, (VT_ROWS - HEAD_DIM, tk), 0) == 0).astype(F32)
        for c in range(seq // tk):
            xt = v_ref[0, c * tk:(c + 1) * tk, :].astype(F32).T
            vt = xt[HEAD_DIM:] if head_in_high_lanes else xt[:HEAD_DIM]
            vt1[:, c * tk:(c + 1) * tk] = jnp.concatenate([vt, ones_row], axis=0).astype(BF16)

    odd = (pl.program_id(1) % 2) == 1
    pl.when(jnp.logical_not(odd))(lambda: fill(False))
    pl.when(odd)(lambda: fill(True))

    cols = 2 * tq
    n_chunks = seq // tk
    n_tiles = seq // tq

    def scores(tile, c, slot):
        q = q_ref[0, pl.ds(pl.multiple_of(tile * tq, tq), tq), :]
        q2 = jnp.concatenate([q[:, :LANES], q[:, LANES:]], axis=0)
        cms = []
        for par, k_s in enumerate((klo, khi)):
            s = _nt_dot(k_s[c * tk:(c + 1) * tk, :], q2)
            s_buf[slot, par] = s
            cms.append(jnp.max(s, axis=0, keepdims=True))
        return tuple(cms)

    def accumulate(c, slot, cms, state):
        vt = vt1[:, c * tk:(c + 1) * tk]
        new = []
        for par in range(2):
            m, acc = state[2 * par], state[2 * par + 1]
            mn = jnp.maximum(m, cms[par])
            p = jnp.exp2(s_buf[slot, par] - mn)
            new += [mn, acc * jnp.exp2(m - mn) + jnp.dot(vt, p.astype(BF16), preferred_element_type=F32)]
        return tuple(new)

    neg = jnp.full((1, cols), -jnp.inf, F32)
    acc0 = jnp.zeros((VT_ROWS, cols), F32)

    def step(j, cms):
        for t in range(tiles_per_step):
            tile = j * tiles_per_step + t
            state = (neg, acc0, neg, acc0)
            for c in range(n_chunks):
                n = t * n_chunks + c
                nxt_tile = tile if c + 1 < n_chunks else jnp.minimum(tile + 1, n_tiles - 1)
                cms_next = scores(nxt_tile, (c + 1) % n_chunks, (n + 1) % 2)
                state = accumulate(c, n % 2, cms, state)
                cms = cms_next
            out = jnp.concatenate(
                [acc[:HEAD_DIM] / acc[HEAD_DIM:HEAD_DIM + 1] for acc in (state[1], state[3])], axis=0)
            o_ref[0, pl.ds(pl.multiple_of(tile * tq, tq), tq), :] = jnp.concatenate(
                [out[:, :tq].T, out[:, tq:].T], axis=1).astype(o_ref.dtype)
        return cms

    lax.fori_loop(0, n_tiles // tiles_per_step, step, scores(0, 0, 0))


def _attn_a(qkv, *, n_heads, tq=256, tk=512, tiles_per_step=4):
    b, seq, _ = qkv.shape
    kv_heads = n_heads // A_REP
    q_cols = n_heads * HEAD_DIM
    q_w = A_REP * HEAD_DIM
    k_blk0 = q_cols // LANES
    v_blk0 = (q_cols + kv_heads * HEAD_DIM) // LANES
    assert (seq // tk) % 2 == 0 and seq % (tq * tiles_per_step) == 0
    return pl.pallas_call(
        functools.partial(_attn_a_body, tq=tq, tk=tk, seq=seq, tiles_per_step=tiles_per_step),
        grid=(b, kv_heads),
        in_specs=[pl.BlockSpec((1, seq, q_w), lambda bi, h: (bi, 0, h)),
                  pl.BlockSpec((1, seq, LANES), lambda bi, h: (bi, 0, k_blk0 + h // 2)),
                  pl.BlockSpec((1, seq, LANES), lambda bi, h: (bi, 0, v_blk0 + h // 2))],
        out_specs=pl.BlockSpec((1, seq, q_w), lambda bi, h: (bi, 0, h)),
        out_shape=jax.ShapeDtypeStruct((b, seq, q_cols), BF16),
        scratch_shapes=[pltpu.VMEM((seq, LANES), BF16), pltpu.VMEM((seq, LANES), BF16),
                        pltpu.VMEM((VT_ROWS, seq), BF16), pltpu.VMEM((2, 2, tk, 2 * tq), F32)],
        compiler_params=_params("parallel", "arbitrary"),
        name="attn_a",
    )(qkv, qkv, qkv)


def _fill_band_bias(bias_ref, slopes, half_window, dist_scale):
    span = bias_ref.shape[2]
    c = lax.broadcasted_iota(jnp.int32, (span, BLOCK_Q), 0)
    r = lax.broadcasted_iota(jnp.int32, (span, BLOCK_Q), 1)
    dist = jnp.abs(c - half_window - r)
    inside = dist <= half_window
    penalty = dist.astype(F32) * (dist_scale * LOG2E)
    for variant, ok in enumerate((inside & (c >= half_window), inside, inside & (c < half_window + BLOCK_Q))):
        for h, slope in enumerate(slopes):
            bias_ref[variant, h] = jnp.where(ok, -slope * penalty, NEG_BIG)


def _band_pipeline(jobs, bias_ref, s_buf):
    def scores(job, slot):
        kmat, q_rows, head_ids, _, _, variant, sink_row = job
        width = len(head_ids) * BLOCK_Q
        s = _nt_dot(kmat, q_rows) + jnp.concatenate([bias_ref[variant, h] for h in head_ids], axis=1)
        s_buf[slot, :, :width] = s
        m = jnp.max(s, axis=0, keepdims=True)
        return m if sink_row is None else jnp.maximum(m, sink_row)

    def values(job, slot, m):
        _, _, head_ids, kv_ids, vt1s, _, sink_row = job
        n = len(head_ids)
        p = jnp.exp2(s_buf[slot, :, :n * BLOCK_Q] - m).astype(BF16)
        accs, start = [], 0
        while start < n:
            stop = start
            while stop < n and kv_ids[stop] == kv_ids[start]:
                stop += 1
            accs.append(jnp.dot(vt1s[kv_ids[start]], p[:, start * BLOCK_Q:stop * BLOCK_Q],
                                preferred_element_type=F32))
            start = stop
        acc = accs[0] if len(accs) == 1 else jnp.concatenate(accs, axis=1)
        den = acc[HEAD_DIM:HEAD_DIM + 1]
        if sink_row is not None:
            den = den + jnp.exp2(sink_row - m)
        return acc[:HEAD_DIM] / den, m, den

    results = []
    n_slots = s_buf.shape[0]
    ahead = n_slots - 1
    maxima = [scores(jobs[c], c % n_slots) for c in range(min(ahead, len(jobs)))]
    for c, job in enumerate(jobs):
        if c + ahead < len(jobs):
            maxima.append(scores(jobs[c + ahead], (c + ahead) % n_slots))
        results.append(values(job, c % n_slots, maxima[c]))
    return results


def _band_operands(k_refs, v_refs):
    k = jnp.concatenate([r[0] for r in k_refs], axis=0)
    vt = jnp.concatenate([r[0] for r in v_refs], axis=0).astype(F32).T
    return k, vt


def _vt_with_ones(vt, kv, start, span):
    ones_row = (lax.broadcasted_iota(jnp.int32, (VT_ROWS - HEAD_DIM, span), 0) == 0).astype(F32)
    rows = vt[kv * HEAD_DIM:(kv + 1) * HEAD_DIM, start:start + span]
    return jnp.concatenate([rows, ones_row], axis=0).astype(BF16)


def _swap_halves(x):
    return jnp.concatenate([x[:, HEAD_DIM:], x[:, :HEAD_DIM]], axis=1)


def _attn_c_body(sink_ref, q_ref, kp_ref, kc_ref, kn_ref, vp_ref, vc_ref, vn_ref, o_ref, bias_ref, s_buf,
                 *, n_heads, slopes):
    i = pl.program_id(1)
    pl.when(i == 0)(lambda: _fill_band_bias(bias_ref, slopes, C_WINDOW, 1.0))
    n_blocks = C_TILE // BLOCK_Q
    block_variants = ([jnp.where(i == 0, 0, 1)] + [1] * (n_blocks - 2)
                      + [jnp.where(i == pl.num_programs(1) - 1, 2, 1)])
    q = q_ref[0]
    k, vt = _band_operands((kp_ref, kc_ref, kn_ref), (vp_ref, vc_ref, vn_ref))
    span = BLOCK_Q + 2 * C_WINDOW
    low = lax.broadcasted_iota(jnp.int32, (BLOCK_Q, LANES), 1) < HEAD_DIM
    zero = jnp.zeros((BLOCK_Q, LANES), BF16)
    first_block = lax.broadcasted_iota(jnp.int32, (1, 2 * BLOCK_Q), 1) < BLOCK_Q
    jobs, job_block = [], []
    for blk in range(n_blocks):
        q_lo = blk * BLOCK_Q
        for pair in range(n_heads // A_REP // 2):
            ks = k[q_lo:q_lo + span, pair * LANES:(pair + 1) * LANES]
            ks_swapped = _swap_halves(ks)
            for which in range(2):
                kv = 2 * pair + which
                vt1s = {kv: _vt_with_ones(vt, kv, q_lo, span)}
                slabs = [q[q_lo:q_lo + BLOCK_Q, (2 * kv + j) * LANES:(2 * kv + j + 1) * LANES] for j in range(2)]
                for parity in range(2):
                    q_rows = jnp.concatenate(
                        [jnp.where(low, s_, zero) if parity == 0 else jnp.where(low, zero, s_) for s_ in slabs],
                        axis=0)
                    heads = (A_REP * kv + parity, A_REP * kv + 2 + parity)
                    sink_row = jnp.where(first_block, sink_ref[heads[0]], sink_ref[heads[1]]) * LOG2E
                    jobs.append((ks if parity == which else ks_swapped, q_rows, heads, (kv, kv), vt1s,
                                 block_variants[blk], sink_row))
                    job_block.append(blk)
    o_t = {}
    for job, blk, (ot, _, _) in zip(jobs, job_block, _band_pipeline(jobs, bias_ref, s_buf)):
        o_t[blk, job[2][0]], o_t[blk, job[2][1]] = ot[:, :BLOCK_Q], ot[:, BLOCK_Q:]
    for blk in range(n_blocks):
        for j in range(n_heads // 2):
            pair_t = jnp.concatenate([o_t[blk, 2 * j], o_t[blk, 2 * j + 1]], axis=0)
            o_ref[0, blk * BLOCK_Q:(blk + 1) * BLOCK_Q, j * LANES:(j + 1) * LANES] = pair_t.T.astype(o_ref.dtype)


def _alibi_slopes(n):
    return [2.0 ** (-8.0 * (i + 1) / n) for i in range(n)]


def _attn_c(qkv, sinks, *, n_heads):
    b, seq, _ = qkv.shape
    kv_heads = n_heads // A_REP
    q_cols = n_heads * HEAD_DIM
    kv_cols = kv_heads * HEAD_DIM
    assert seq % C_TILE == 0 and C_TILE >= 2 * BLOCK_Q
    n_tiles = seq // C_TILE
    n_edges = seq // C_WINDOW
    edges_per_tile = C_TILE // C_WINDOW
    k_blk = q_cols // kv_cols
    v_blk = k_blk + 1
    span = BLOCK_Q + 2 * C_WINDOW

    def kv_specs(col_blk):
        edge = lambda offset: pl.BlockSpec(
            (1, C_WINDOW, kv_cols),
            lambda bi, i: (bi, jnp.clip(i * edges_per_tile + offset, 0, n_edges - 1), col_blk))
        return [edge(-1), pl.BlockSpec((1, C_TILE, kv_cols), lambda bi, i: (bi, i, col_blk)),
                edge(edges_per_tile)]

    return pl.pallas_call(
        functools.partial(_attn_c_body, n_heads=n_heads, slopes=_alibi_slopes(n_heads)),
        grid=(b, n_tiles),
        in_specs=[pl.BlockSpec(memory_space=pltpu.SMEM),
                  pl.BlockSpec((1, C_TILE, q_cols), lambda bi, i: (bi, i, 0))] + kv_specs(k_blk) + kv_specs(v_blk),
        out_specs=pl.BlockSpec((1, C_TILE, q_cols), lambda bi, i: (bi, i, 0)),
        out_shape=jax.ShapeDtypeStruct((b, seq, q_cols), BF16),
        scratch_shapes=[pltpu.VMEM((3, n_heads, span, BLOCK_Q), F32),
                        pltpu.VMEM((BAND_SLOTS, span, 2 * BLOCK_Q), F32)],
        compiler_params=_params("parallel", "arbitrary"),
        name="attn_c",
    )(sinks, qkv, qkv, qkv, qkv, qkv, qkv, qkv)


REFS_PER_SEQ = 9


def _attn_b_body(*refs, slopes, dil, n_seq):
    in_refs, (o_ref, lse_ref, bias_ref, s_buf) = refs[:n_seq * REFS_PER_SEQ], refs[n_seq * REFS_PER_SEQ:]
    i = pl.program_id(2)
    pl.when(jnp.logical_and(pl.program_id(1) == 0, i == 0))(
        lambda: _fill_band_bias(bias_ref, slopes, B_HALF, float(dil)))
    n_blocks = in_refs[0].shape[1] // BLOCK_Q
    block_variants = ([jnp.where(i == 0, 0, 1)] + [1] * (n_blocks - 2)
                      + [jnp.where(i == pl.num_programs(2) - 1, 2, 1)])
    low = lax.broadcasted_iota(jnp.int32, (BLOCK_Q, LANES), 1) < HEAD_DIM
    zero = jnp.zeros((BLOCK_Q, LANES), BF16)
    heads_all = range(B_HEADS_PER_GROUP)
    plain = [h for h in heads_all if h % 2 == h // B_REP]
    crossed = [h for h in heads_all if h % 2 != h // B_REP]
    jobs, job_place = [], []
    for seq in range(n_seq):
        q0, q1, q2, kp, kc, kn, vp, vc, vn = in_refs[seq * REFS_PER_SEQ:(seq + 1) * REFS_PER_SEQ]
        k, vt = _band_operands((kp, kc, kn), (vp, vc, vn))
        slabs = [q0[0], q1[0], q2[0]]
        for blk in range(n_blocks):
            q_lo = blk * BLOCK_Q
            ks = k[q_lo:q_lo + B_SPAN]
            ks_swapped = _swap_halves(ks)
            vt1s = {kv: _vt_with_ones(vt, kv, q_lo, B_SPAN) for kv in range(B_KV_PER_GROUP)}
            for kmat, heads in ((ks, plain[:2]), (ks_swapped, crossed), (ks, plain[2:])):
                q_rows = jnp.concatenate(
                    [jnp.where(low, slabs[h // 2][q_lo:q_lo + BLOCK_Q], zero) if h % 2 == 0
                     else jnp.where(low, zero, slabs[h // 2][q_lo:q_lo + BLOCK_Q]) for h in heads], axis=0)
                jobs.append((kmat, q_rows, heads, [h // B_REP for h in heads], vt1s, block_variants[blk], None))
                job_place.append((seq, blk))
    o_t, lse_t = {}, {}
    for job, place, (ot, m, den) in zip(jobs, job_place, _band_pipeline(jobs, bias_ref, s_buf)):
        lse = (m + jnp.log2(den)) * LN2
        for n, h in enumerate(job[2]):
            o_t[place, h] = ot[:, n * BLOCK_Q:(n + 1) * BLOCK_Q]
            lse_t[place, h] = jnp.broadcast_to(lse[:, n * BLOCK_Q:(n + 1) * BLOCK_Q], (HEAD_DIM, BLOCK_Q))
    gq = B_HEADS_PER_GROUP * HEAD_DIM
    for seq in range(n_seq):
        for blk in range(n_blocks):
            rows = slice(blk * BLOCK_Q, (blk + 1) * BLOCK_Q)
            for j in range(B_HEADS_PER_GROUP // 2):
                cols = slice(seq * gq + j * LANES, seq * gq + (j + 1) * LANES)
                pair = lambda t: jnp.concatenate([t[(seq, blk), 2 * j], t[(seq, blk), 2 * j + 1]], axis=0).T
                o_ref[0, rows, cols] = pair(o_t).astype(o_ref.dtype)
                lse_ref[0, rows, cols] = pair(lse_t)


def _attn_b(group_qkv, b):
    n_g = len(B_GROUPS)
    gq = B_HEADS_PER_GROUP * HEAD_DIM
    slabs_per_token = B_GROUP_COLS // LANES
    q_slabs = gq // LANES
    all_slopes = _alibi_slopes(n_g * B_HEADS_PER_GROUP)
    outs, lses = [], []
    for g, (window, dil) in enumerate(B_GROUPS):
        assert (window // 2) // dil == B_HALF
        sub = group_qkv[g].shape[0] // b
        tile = min(B_TILE, sub)
        n_seq = min(B_TILE // tile, dil)
        assert tile >= 2 * BLOCK_Q and sub % tile == 0 and dil % n_seq == 0
        n_tiles = sub // tile
        n_halves = sub // B_HALF
        halves_per_tile = tile // B_HALF
        view = group_qkv[g].reshape(b, sub, dil * B_GROUP_COLS)

        def seq_specs(seq, tile=tile, n_seq=n_seq, n_halves=n_halves, halves_per_tile=halves_per_tile):
            col = lambda slab: (lambda bi, rr, i: (rr * n_seq + seq) * slabs_per_token + slab)

            def tile_spec(slab):
                return pl.BlockSpec((1, tile, LANES), lambda bi, rr, i: (bi, i, col(slab)(bi, rr, i)))

            def edge_spec(slab, offset):
                return pl.BlockSpec(
                    (1, B_HALF, LANES),
                    lambda bi, rr, i: (bi, jnp.clip(i * halves_per_tile + offset, 0, n_halves - 1),
                                       col(slab)(bi, rr, i)))

            kv_specs = lambda slab: [edge_spec(slab, -1), tile_spec(slab), edge_spec(slab, halves_per_tile)]
            return [tile_spec(j) for j in range(q_slabs)] + kv_specs(q_slabs) + kv_specs(q_slabs + 1)

        out_map = lambda bi, rr, i: (bi, i, rr)
        out_spec = pl.BlockSpec((1, tile, n_seq * gq), out_map)
        o, lse = pl.pallas_call(
            functools.partial(_attn_b_body, dil=dil, n_seq=n_seq,
                              slopes=all_slopes[g * B_HEADS_PER_GROUP:(g + 1) * B_HEADS_PER_GROUP]),
            grid=(b, dil // n_seq, n_tiles),
            in_specs=sum((seq_specs(seq) for seq in range(n_seq)), []),
            out_specs=[out_spec, out_spec],
            out_shape=[jax.ShapeDtypeStruct((b, sub, dil * gq), BF16),
                       jax.ShapeDtypeStruct((b, sub, dil * gq), F32)],
            scratch_shapes=[pltpu.VMEM((3, B_HEADS_PER_GROUP, B_SPAN, BLOCK_Q), F32),
                            pltpu.VMEM((BAND_SLOTS, B_SPAN, 2 * BLOCK_Q), F32)],
            compiler_params=_params("parallel", "arbitrary", "arbitrary"),
            name=f"attn_b_g{g}",
        )(*([view] * (n_seq * REFS_PER_SEQ)))
        outs.append(o.reshape(b * sub, dil * gq))
        lses.append(lse.reshape(b * sub, dil * gq))
    return outs, lses


PROLOGUE_ROW_CHUNKS = 2


def _mix_plain(refs, scratch):
    return lambda rows: refs[0][rows, :]


def _mix_groups(refs, scratch):
    n_g = len(B_GROUPS)
    o_scr, l_scr = scratch
    slabs = o_scr.shape[0] // n_g
    tm = o_scr.shape[1]
    gq = slabs * LANES
    for g, (_, dil) in enumerate(B_GROUPS):
        for r in range(dil):
            rows = pl.ds(r, tm // dil, stride=dil)
            for s in range(slabs):
                cols = slice(r * gq + s * LANES, r * gq + (s + 1) * LANES)
                o_scr[g * slabs + s, rows, :] = refs[g][:, cols].astype(F32)
                l_scr[g * slabs + s, rows, :] = refs[n_g + g][:, cols]
    def lhs(rows):
        group = lambda scr, g: jnp.concatenate([scr[g * slabs + s, rows, :] for s in range(slabs)], axis=1)
        ls = [group(l_scr, g) for g in range(n_g)]
        mx = functools.reduce(jnp.maximum, ls)
        es = [jnp.exp(l - mx) for l in ls]
        inv = 1.0 / functools.reduce(lambda a, b: a + b, es)
        return jnp.concatenate([(group(o_scr, g) * (es[g] * inv)).astype(BF16) for g in range(n_g)], axis=1)

    return lhs


MLP_HIDDEN_CHUNK = 1024


def _block_body(*refs, n_mix, mix_fn, final_norm):
    h_ref, mix_refs = refs[0], refs[1:1 + n_mix]
    wo_ref, g_ref, w1_ref, w2_ref, fg_ref, out_ref, hn_ref = refs[1 + n_mix:8 + n_mix]
    mix_scratch = refs[8 + n_mix:]

    lhs = mix_fn(mix_refs, mix_scratch)
    chunk = h_ref.shape[0] // PROLOGUE_ROW_CHUNKS
    rows = [slice(c * chunk, (c + 1) * chunk) for c in range(PROLOGUE_ROW_CHUNKS)]
    project = lambda c: jnp.dot(lhs(rows[c]), wo_ref[...], preferred_element_type=F32)
    nxt = project(0)
    for c in range(PROLOGUE_ROW_CHUNKS):
        cur = nxt
        if c + 1 < PROLOGUE_ROW_CHUNKS:
            nxt = project(c + 1)
        h1 = h_ref[rows[c], :] + cur
        out_ref[rows[c], :] = h1
        hn_ref[rows[c], :] = _rms_rows(h1, g_ref[...]).astype(BF16)

    hn = hn_ref[...]
    for c in range(w1_ref.shape[1] // MLP_HIDDEN_CHUNK):
        cols = slice(c * MLP_HIDDEN_CHUNK, (c + 1) * MLP_HIDDEN_CHUNK)
        u = jnp.maximum(jnp.dot(hn, w1_ref[:, cols], preferred_element_type=F32), 0.0)
        out_ref[...] += jnp.dot((u * u).astype(BF16), w2_ref[cols, :], preferred_element_type=F32)

    if final_norm:
        out_ref[...] = _rms_rows(out_ref[...], fg_ref[...])


def _block(h2, mix_inputs, w_o, *, gain, w1, w2, layer, final_gain, final_norm, tm=512):
    t, d = h2.shape
    assert t % tm == 0 and tm % PROLOGUE_ROW_CHUNKS == 0 and w1.shape[2] % MLP_HIDDEN_CHUNK == 0
    row_tile = lambda cols: pl.BlockSpec((tm, cols), lambda i: (i, 0))
    whole = lambda a: pl.BlockSpec(a.shape, lambda i: (0, 0), pipeline_mode=pl.Buffered(1))
    of_layer = lambda a: pl.BlockSpec((None,) + a.shape[1:], lambda i: (layer, 0, 0),
                                      pipeline_mode=pl.Buffered(1))
    scratch = [pltpu.VMEM((tm, d), BF16)]
    if len(mix_inputs) == 1:
        mix_fn, mix_specs = _mix_plain, [row_tile(mix_inputs[0].shape[1])]
    else:
        n_g = len(B_GROUPS)
        gq = B_HEADS_PER_GROUP * HEAD_DIM
        mix_fn = _mix_groups
        mix_specs = [pl.BlockSpec((tm // dil, dil * gq), lambda i: (i, 0)) for _, dil in B_GROUPS] * 2
        scratch += [pltpu.VMEM((n_g * gq // LANES, tm, LANES), F32)] * 2
    return pl.pallas_call(
        functools.partial(_block_body, n_mix=len(mix_inputs), mix_fn=mix_fn, final_norm=final_norm),
        grid=(t // tm,),
        in_specs=[row_tile(d)] + mix_specs
                 + [whole(w_o), whole(gain), of_layer(w1), of_layer(w2), whole(final_gain)],
        out_specs=row_tile(d),
        out_shape=jax.ShapeDtypeStruct((t, d), F32),
        scratch_shapes=scratch,
        compiler_params=_params("parallel"),
        name="block_mlp",
    )(h2, *mix_inputs, w_o, gain, w1, w2, final_gain)


def kernel(x, attn_norm, mlp_norm, a_w_qkv, a_q_gain, a_k_gain, a_w_o, b_w_qkv, b_w_o,
           c_w_qkv, c_sinks, c_w_o, mlp_w1, mlp_w2, final_norm):
    b, seq, d = x.shape
    depth = attn_norm.shape[0]
    h = x.reshape(b * seq, d)
    scale = HEAD_DIM ** -0.5 * LOG2E
    a_heads = a_w_o.shape[1] // HEAD_DIM
    c_heads = c_w_o.shape[1] // HEAD_DIM
    cos, sin = _rope_tables(seq)
    head = jnp.arange(2 * LANES) // HEAD_DIM
    seg = (head[:, None] == head[None, :]).astype(BF16)
    tile4 = lambda g: jnp.tile(g, 2 * LANES // HEAD_DIM)[None, :]
    w1_all, w2_all = mlp_w1.astype(BF16), mlp_w2.astype(BF16)
    used = [0, 0, 0]
    for layer in range(depth):
        kind = layer % N_MIXERS
        j = used[kind]
        used[kind] += 1
        gain = attn_norm[layer][None, :]
        block = functools.partial(_block, gain=mlp_norm[layer][None, :], w1=w1_all, w2=w2_all, layer=layer,
                                  final_gain=final_norm[None, :], final_norm=(layer == depth - 1))
        if kind == 0:
            q_cols = a_heads * HEAD_DIM
            k_cols = (a_w_qkv.shape[2] - q_cols) // 2
            qkv = _qkv_proj(h, gain, a_w_qkv[j].astype(BF16), q_cols=q_cols, q_scale=scale,
                            rope=(tile4(a_q_gain[j]), tile4(a_k_gain[j]), cos, sin, seg, k_cols))
            o = _attn_a(qkv.reshape(b, seq, -1), n_heads=a_heads).reshape(b * seq, q_cols)
            h = block(h, [o], a_w_o[j].astype(BF16))
        elif kind == 1:
            q_cols = len(B_GROUPS) * B_HEADS_PER_GROUP * HEAD_DIM
            groups = _qkv_proj_b(h, gain, b_w_qkv[j].astype(BF16), q_cols=q_cols, q_scale=scale)
            outs, lses = _attn_b(groups, b)
            h = block(h, outs + lses, b_w_o[j].astype(BF16))
        else:
            q_cols = c_heads * HEAD_DIM
            qkv = _qkv_proj(h, gain, c_w_qkv[j].astype(BF16), q_cols=q_cols, q_scale=scale)
            o = _attn_c(qkv.reshape(b, seq, -1), c_sinks[j], n_heads=c_heads).reshape(b * seq, q_cols)
            h = block(h, [o], c_w_o[j].astype(BF16))
    return h.reshape(b, seq, d)
```

```python
import functools
import math

import jax
import jax.numpy as jnp
from jax import lax
from jax.experimental import pallas as pl
from jax.experimental.pallas import tpu as pltpu

HEAD_DIM = 64
HALF_HEAD = HEAD_DIM // 2
RMS_EPS = 1e-6
GRID_W = 64
ROPE_THETA = 10000.0
N_MIXERS = 3
A_REP = 4
B_GROUPS = ((128, 1), (512, 4), (2048, 16))
B_HEADS_PER_GROUP = 6
B_KV_PER_GROUP = 2
B_REP = B_HEADS_PER_GROUP // B_KV_PER_GROUP
C_WINDOW = 128
LANES = 128
BF16_SUBLANES = 16
BLOCK_Q = 128
BAND_SLOTS = 6
C_TILE = 8 * BLOCK_Q
B_HALF = BLOCK_Q // 2
B_TILE = 8 * BLOCK_Q
B_SPAN = BLOCK_Q + 2 * B_HALF
B_GROUP_COLS = (B_HEADS_PER_GROUP + 2 * B_KV_PER_GROUP) * HEAD_DIM
VT_ROWS = HEAD_DIM + BF16_SUBLANES
V7X_VMEM_LIMIT_BYTES = 48 * 1024 * 1024
NEG_BIG = -1e30
LOG2E = math.log2(math.e)
LN2 = math.log(2.0)

BF16 = jnp.bfloat16
F32 = jnp.float32


def _params(*sem):
    return pltpu.CompilerParams(dimension_semantics=sem, vmem_limit_bytes=V7X_VMEM_LIMIT_BYTES)


def _nt_dot(a, b):
    return lax.dot_general(a, b, (((1,), (1,)), ((), ())), preferred_element_type=F32)


def _rms_rows(x, gain):
    ms = jnp.mean(x * x, axis=-1, keepdims=True)
    return x * lax.rsqrt(ms + RMS_EPS) * gain


def _qkv_body(x_ref, g_ref, w_ref, o_ref, *, q_cols, q_scale):
    hn = _rms_rows(x_ref[...], g_ref[...]).astype(BF16)
    y = jnp.dot(hn, w_ref[...], preferred_element_type=F32)
    o_ref[:, :q_cols] = (y[:, :q_cols] * q_scale).astype(o_ref.dtype)
    o_ref[:, q_cols:] = y[:, q_cols:].astype(o_ref.dtype)


def _qkv_rope_body(x_ref, g_ref, w_ref, qg_ref, kg_ref, cos_ref, sin_ref, seg_ref, o_ref,
                   *, q_cols, k_cols, q_scale):
    hn = _rms_rows(x_ref[...], g_ref[...]).astype(BF16)
    slab = 2 * LANES
    cos = jnp.concatenate([cos_ref[...], cos_ref[...]], axis=1)
    sin = jnp.concatenate([sin_ref[...], sin_ref[...]], axis=1)
    lane = lax.broadcasted_iota(jnp.int32, (x_ref.shape[0], slab), 1)
    first_half = (lane % HALF_HEAD) < (HALF_HEAD // 2)
    seg = seg_ref[...]
    n_qk = (q_cols + k_cols) // slab
    project = lambda s: jnp.dot(hn, w_ref[:, s * slab:(s + 1) * slab], preferred_element_type=F32)
    y_next = project(0)
    for s in range(n_qk):
        is_q = s * slab < q_cols
        ys = y_next
        y_next = project(s + 1) if s + 1 < n_qk else jnp.dot(
            hn, w_ref[:, q_cols + k_cols:], preferred_element_type=F32)
        sq = ys * ys
        hi = sq.astype(BF16)
        lo = (sq - hi.astype(F32)).astype(BF16)
        ss = jnp.dot(hi, seg, preferred_element_type=F32) + jnp.dot(lo, seg, preferred_element_type=F32)
        gain = qg_ref[...] if is_q else kg_ref[...]
        yn = ys * lax.rsqrt(ss * (1.0 / HEAD_DIM) + RMS_EPS) * gain
        partner = jnp.where(first_half,
                            pltpu.roll(yn, slab - HALF_HEAD // 2, axis=1),
                            pltpu.roll(yn, HALF_HEAD // 2, axis=1))
        r = yn * cos + partner * sin
        if is_q:
            r = r * q_scale
        o_ref[:, s * slab:(s + 1) * slab] = r.astype(o_ref.dtype)
    o_ref[:, q_cols + k_cols:] = y_next.astype(o_ref.dtype)


def _qkv_proj(h2, gain, w, *, q_cols, q_scale, tm=1024, rope=None):
    t, d = h2.shape
    n = w.shape[1]
    assert t % tm == 0 and q_cols % (2 * LANES) == 0 and n % LANES == 0
    common = dict(
        grid=(t // tm,),
        out_specs=pl.BlockSpec((tm, n), lambda i: (i, 0)),
        out_shape=jax.ShapeDtypeStruct((t, n), BF16),
        compiler_params=_params("parallel"),
    )
    x_spec = pl.BlockSpec((tm, d), lambda i: (i, 0))
    g_spec = pl.BlockSpec((1, d), lambda i: (0, 0))
    w_spec = pl.BlockSpec((d, n), lambda i: (0, 0))
    if rope is None:
        return pl.pallas_call(
            functools.partial(_qkv_body, q_cols=q_cols, q_scale=q_scale),
            in_specs=[x_spec, g_spec, w_spec], name="qkv_proj", **common,
        )(h2, gain, w)
    qg, kg, cos, sin, seg, k_cols = rope
    seq_tiles = cos.shape[0] // tm
    slab = 2 * LANES
    return pl.pallas_call(
        functools.partial(_qkv_rope_body, q_cols=q_cols, k_cols=k_cols, q_scale=q_scale),
        in_specs=[x_spec, g_spec, w_spec,
                  pl.BlockSpec((1, slab), lambda i: (0, 0)),
                  pl.BlockSpec((1, slab), lambda i: (0, 0)),
                  pl.BlockSpec((tm, LANES), lambda i: (i % seq_tiles, 0)),
                  pl.BlockSpec((tm, LANES), lambda i: (i % seq_tiles, 0)),
                  pl.BlockSpec((slab, slab), lambda i: (0, 0))],
        name="qkv_proj_rope", **common,
    )(h2, gain, w, qg, kg, cos, sin, seg)


def _qkv_b_body(x_ref, g_ref, w_ref, *rest, q_cols, q_scale):
    out_refs, y_scr = rest[:-1], rest[-1]
    tm = x_ref.shape[0]
    hn = _rms_rows(x_ref[...], g_ref[...]).astype(BF16)
    y = jnp.dot(hn, w_ref[...], preferred_element_type=F32)
    for s in range(y_scr.shape[0]):
        ys = y[:, s * LANES:(s + 1) * LANES]
        y_scr[s] = ys * q_scale if s * LANES < q_cols else ys
    n_g = len(B_GROUPS)
    q_slabs = B_HEADS_PER_GROUP * HEAD_DIM // LANES
    k0 = q_cols // LANES
    for g, (_, dil) in enumerate(B_GROUPS):
        slabs = list(range(g * q_slabs, (g + 1) * q_slabs)) + [k0 + g, k0 + n_g + g]
        for r in range(dil):
            rows = pl.ds(r, tm // dil, stride=dil)
            piece = jnp.concatenate([y_scr[s, rows, :] for s in slabs], axis=1)
            out_refs[g][:, r * B_GROUP_COLS:(r + 1) * B_GROUP_COLS] = piece.astype(BF16)


def _qkv_proj_b(h2, gain, w, *, q_cols, q_scale, tm=1024):
    t, d = h2.shape
    n = w.shape[1]
    assert t % tm == 0 and n == len(B_GROUPS) * B_GROUP_COLS
    assert all((tm // dil) % BF16_SUBLANES == 0 for _, dil in B_GROUPS)
    return pl.pallas_call(
        functools.partial(_qkv_b_body, q_cols=q_cols, q_scale=q_scale),
        grid=(t // tm,),
        in_specs=[pl.BlockSpec((tm, d), lambda i: (i, 0)),
                  pl.BlockSpec((1, d), lambda i: (0, 0)),
                  pl.BlockSpec((d, n), lambda i: (0, 0))],
        out_specs=[pl.BlockSpec((tm // dil, dil * B_GROUP_COLS), lambda i: (i, 0)) for _, dil in B_GROUPS],
        out_shape=[jax.ShapeDtypeStruct((t // dil, dil * B_GROUP_COLS), BF16) for _, dil in B_GROUPS],
        scratch_shapes=[pltpu.VMEM((n // LANES, tm, LANES), F32)],
        compiler_params=_params("parallel"),
        name="qkv_proj_b",
    )(h2, gain, w)


def _rope_tables(seq):
    inv_freq = ROPE_THETA ** (-jnp.arange(0, HALF_HEAD, 2, dtype=F32) / HALF_HEAD)
    rows = seq // GRID_W
    ang_row = jnp.arange(rows, dtype=F32)[:, None] * inv_freq
    ang_col = jnp.arange(GRID_W, dtype=F32)[:, None] * inv_freq
    per_row = lambda x: jnp.repeat(x, GRID_W, axis=0)
    per_col = lambda x: jnp.tile(x, (rows, 1))
    cos = jnp.concatenate([per_row(jnp.cos(ang_row))] * 2 + [per_col(jnp.cos(ang_col))] * 2, axis=1)
    sin_r, sin_c = per_row(jnp.sin(ang_row)), per_col(jnp.sin(ang_col))
    sin = jnp.concatenate([-sin_r, sin_r, -sin_c, sin_c], axis=1)
    return jnp.concatenate([cos, cos], axis=1), jnp.concatenate([sin, sin], axis=1)


def _attn_a_body(q_ref, k_ref, v_ref, o_ref, klo, khi, vt1, s_buf, *, tq, tk, seq, tiles_per_step):
    def fill(head_in_high_lanes):
        low = lax.broadcasted_iota(jnp.int32, (seq, LANES), 1) < HEAD_DIM
        zero = jnp.zeros((seq, LANES), BF16)
        x = k_ref[0]
        swapped = jnp.concatenate([x[:, HEAD_DIM:], x[:, :HEAD_DIM]], axis=1)
        in_low, in_high = (swapped, x) if head_in_high_lanes else (x, swapped)
        klo[...] = jnp.where(low, in_low, zero)
        khi[...] = jnp.where(low, zero, in_high)
        ones_row = (lax.broadcasted_iota(jnp.int32, (VT_ROWS - HEAD_DIM, tk), 0) == 0).astype(F32)
        for c in range(seq // tk):
            xt = v_ref[0, c * tk:(c + 1) * tk, :].astype(F32).T
            vt = xt[HEAD_DIM:] if head_in_high_lanes else xt[:HEAD_DIM]
            vt1[:, c * tk:(c + 1) * tk] = jnp.concatenate([vt, ones_row], axis=0).astype(BF16)

    odd = (pl.program_id(1) % 2) == 1
    pl.when(jnp.logical_not(odd))(lambda: fill(False))
    pl.when(odd)(lambda: fill(True))

    cols = 2 * tq
    n_chunks = seq // tk
    n_tiles = seq // tq

    def scores(tile, c, slot):
        q = q_ref[0, pl.ds(pl.multiple_of(tile * tq, tq), tq), :]
        q2 = jnp.concatenate([q[:, :LANES], q[:, LANES:]], axis=0)
        cms = []
        for par, k_s in enumerate((klo, khi)):
            s = _nt_dot(k_s[c * tk:(c + 1) * tk, :], q2)
            s_buf[slot, par] = s
            cms.append(jnp.max(s, axis=0, keepdims=True))
        return tuple(cms)

    def accumulate(c, slot, cms, state):
        vt = vt1[:, c * tk:(c + 1) * tk]
        new = []
        for par in range(2):
            m, acc = state[2 * par], state[2 * par + 1]
            mn = jnp.maximum(m, cms[par])
            p = jnp.exp2(s_buf[slot, par] - mn)
            new += [mn, acc * jnp.exp2(m - mn) + jnp.dot(vt, p.astype(BF16), preferred_element_type=F32)]
        return tuple(new)

    neg = jnp.full((1, cols), -jnp.inf, F32)
    acc0 = jnp.zeros((VT_ROWS, cols), F32)

    def step(j, cms):
        for t in range(tiles_per_step):
            tile = j * tiles_per_step + t
            state = (neg, acc0, neg, acc0)
            for c in range(n_chunks):
                n = t * n_chunks + c
                nxt_tile = tile if c + 1 < n_chunks else jnp.minimum(tile + 1, n_tiles - 1)
                cms_next = scores(nxt_tile, (c + 1) % n_chunks, (n + 1) % 2)
                state = accumulate(c, n % 2, cms, state)
                cms = cms_next
            out = jnp.concatenate(
                [acc[:HEAD_DIM] / acc[HEAD_DIM:HEAD_DIM + 1] for acc in (state[1], state[3])], axis=0)
            o_ref[0, pl.ds(pl.multiple_of(tile * tq, tq), tq), :] = jnp.concatenate(
                [out[:, :tq].T, out[:, tq:].T], axis=1).astype(o_ref.dtype)
        return cms

    lax.fori_loop(0, n_tiles // tiles_per_step, step, scores(0, 0, 0))


def _attn_a(qkv, *, n_heads, tq=256, tk=512, tiles_per_step=4):
    b, seq, _ = qkv.shape
    kv_heads = n_heads // A_REP
    q_cols = n_heads * HEAD_DIM
    q_w = A_REP * HEAD_DIM
    k_blk0 = q_cols // LANES
    v_blk0 = (q_cols + kv_heads * HEAD_DIM) // LANES
    assert (seq // tk) % 2 == 0 and seq % (tq * tiles_per_step) == 0
    return pl.pallas_call(
        functools.partial(_attn_a_body, tq=tq, tk=tk, seq=seq, tiles_per_step=tiles_per_step),
        grid=(b, kv_heads),
        in_specs=[pl.BlockSpec((1, seq, q_w), lambda bi, h: (bi, 0, h)),
                  pl.BlockSpec((1, seq, LANES), lambda bi, h: (bi, 0, k_blk0 + h // 2)),
                  pl.BlockSpec((1, seq, LANES), lambda bi, h: (bi, 0, v_blk0 + h // 2))],
        out_specs=pl.BlockSpec((1, seq, q_w), lambda bi, h: (bi, 0, h)),
        out_shape=jax.ShapeDtypeStruct((b, seq, q_cols), BF16),
        scratch_shapes=[pltpu.VMEM((seq, LANES), BF16), pltpu.VMEM((seq, LANES), BF16),
                        pltpu.VMEM((VT_ROWS, seq), BF16), pltpu.VMEM((2, 2, tk, 2 * tq), F32)],
        compiler_params=_params("parallel", "arbitrary"),
        name="attn_a",
    )(qkv, qkv, qkv)


def _fill_band_bias(bias_ref, slopes, half_window, dist_scale):
    span = bias_ref.shape[2]
    c = lax.broadcasted_iota(jnp.int32, (span, BLOCK_Q), 0)
    r = lax.broadcasted_iota(jnp.int32, (span, BLOCK_Q), 1)
    dist = jnp.abs(c - half_window - r)
    inside = dist <= half_window
    penalty = dist.astype(F32) * (dist_scale * LOG2E)
    for variant, ok in enumerate((inside & (c >= half_window), inside, inside & (c < half_window + BLOCK_Q))):
        for h, slope in enumerate(slopes):
            bias_ref[variant, h] = jnp.where(ok, -slope * penalty, NEG_BIG)


def _band_pipeline(jobs, bias_ref, s_buf):
    def scores(job, slot):
        kmat, q_rows, head_ids, _, _, variant, sink_row = job
        width = len(head_ids) * BLOCK_Q
        s = _nt_dot(kmat, q_rows) + jnp.concatenate([bias_ref[variant, h] for h in head_ids], axis=1)
        s_buf[slot, :, :width] = s
        m = jnp.max(s, axis=0, keepdims=True)
        return m if sink_row is None else jnp.maximum(m, sink_row)

    def values(job, slot, m):
        _, _, head_ids, kv_ids, vt1s, _, sink_row = job
        n = len(head_ids)
        p = jnp.exp2(s_buf[slot, :, :n * BLOCK_Q] - m).astype(BF16)
        accs, start = [], 0
        while start < n:
            stop = start
            while stop < n and kv_ids[stop] == kv_ids[start]:
                stop += 1
            accs.append(jnp.dot(vt1s[kv_ids[start]], p[:, start * BLOCK_Q:stop * BLOCK_Q],
                                preferred_element_type=F32))
            start = stop
        acc = accs[0] if len(accs) == 1 else jnp.concatenate(accs, axis=1)
        den = acc[HEAD_DIM:HEAD_DIM + 1]
        if sink_row is not None:
            den = den + jnp.exp2(sink_row - m)
        return acc[:HEAD_DIM] / den, m, den

    results = []
    n_slots = s_buf.shape[0]
    ahead = n_slots - 1
    maxima = [scores(jobs[c], c % n_slots) for c in range(min(ahead, len(jobs)))]
    for c, job in enumerate(jobs):
        if c + ahead < len(jobs):
            maxima.append(scores(jobs[c + ahead], (c + ahead) % n_slots))
        results.append(values(job, c % n_slots, maxima[c]))
    return results


def _band_operands(k_refs, v_refs):
    k = jnp.concatenate([r[0] for r in k_refs], axis=0)
    vt = jnp.concatenate([r[0] for r in v_refs], axis=0).astype(F32).T
    return k, vt


def _vt_with_ones(vt, kv, start, span):
    ones_row = (lax.broadcasted_iota(jnp.int32, (VT_ROWS - HEAD_DIM, span), 0) == 0).astype(F32)
    rows = vt[kv * HEAD_DIM:(kv + 1) * HEAD_DIM, start:start + span]
    return jnp.concatenate([rows, ones_row], axis=0).astype(BF16)


def _swap_halves(x):
    return jnp.concatenate([x[:, HEAD_DIM:], x[:, :HEAD_DIM]], axis=1)


def _attn_c_body(sink_ref, q_ref, kp_ref, kc_ref, kn_ref, vp_ref, vc_ref, vn_ref, o_ref, bias_ref, s_buf,
                 *, n_heads, slopes):
    i = pl.program_id(1)
    pl.when(i == 0)(lambda: _fill_band_bias(bias_ref, slopes, C_WINDOW, 1.0))
    n_blocks = C_TILE // BLOCK_Q
    block_variants = ([jnp.where(i == 0, 0, 1)] + [1] * (n_blocks - 2)
                      + [jnp.where(i == pl.num_programs(1) - 1, 2, 1)])
    q = q_ref[0]
    k, vt = _band_operands((kp_ref, kc_ref, kn_ref), (vp_ref, vc_ref, vn_ref))
    span = BLOCK_Q + 2 * C_WINDOW
    low = lax.broadcasted_iota(jnp.int32, (BLOCK_Q, LANES), 1) < HEAD_DIM
    zero = jnp.zeros((BLOCK_Q, LANES), BF16)
    first_block = lax.broadcasted_iota(jnp.int32, (1, 2 * BLOCK_Q), 1) < BLOCK_Q
    jobs, job_block = [], []
    for blk in range(n_blocks):
        q_lo = blk * BLOCK_Q
        for pair in range(n_heads // A_REP // 2):
            ks = k[q_lo:q_lo + span, pair * LANES:(pair + 1) * LANES]
            ks_swapped = _swap_halves(ks)
            for which in range(2):
                kv = 2 * pair + which
                vt1s = {kv: _vt_with_ones(vt, kv, q_lo, span)}
                slabs = [q[q_lo:q_lo + BLOCK_Q, (2 * kv + j) * LANES:(2 * kv + j + 1) * LANES] for j in range(2)]
                for parity in range(2):
                    q_rows = jnp.concatenate(
                        [jnp.where(low, s_, zero) if parity == 0 else jnp.where(low, zero, s_) for s_ in slabs],
                        axis=0)
                    heads = (A_REP * kv + parity, A_REP * kv + 2 + parity)
                    sink_row = jnp.where(first_block, sink_ref[heads[0]], sink_ref[heads[1]]) * LOG2E
                    jobs.append((ks if parity == which else ks_swapped, q_rows, heads, (kv, kv), vt1s,
                                 block_variants[blk], sink_row))
                    job_block.append(blk)
    o_t = {}
    for job, blk, (ot, _, _) in zip(jobs, job_block, _band_pipeline(jobs, bias_ref, s_buf)):
        o_t[blk, job[2][0]], o_t[blk, job[2][1]] = ot[:, :BLOCK_Q], ot[:, BLOCK_Q:]
    for blk in range(n_blocks):
        for j in range(n_heads // 2):
            pair_t = jnp.concatenate([o_t[blk, 2 * j], o_t[blk, 2 * j + 1]], axis=0)
            o_ref[0, blk * BLOCK_Q:(blk + 1) * BLOCK_Q, j * LANES:(j + 1) * LANES] = pair_t.T.astype(o_ref.dtype)


def _alibi_slopes(n):
    return [2.0 ** (-8.0 * (i + 1) / n) for i in range(n)]


def _attn_c(qkv, sinks, *, n_heads):
    b, seq, _ = qkv.shape
    kv_heads = n_heads // A_REP
    q_cols = n_heads * HEAD_DIM
    kv_cols = kv_heads * HEAD_DIM
    assert seq % C_TILE == 0 and C_TILE >= 2 * BLOCK_Q
    n_tiles = seq // C_TILE
    n_edges = seq // C_WINDOW
    edges_per_tile = C_TILE // C_WINDOW
    k_blk = q_cols // kv_cols
    v_blk = k_blk + 1
    span = BLOCK_Q + 2 * C_WINDOW

    def kv_specs(col_blk):
        edge = lambda offset: pl.BlockSpec(
            (1, C_WINDOW, kv_cols),
            lambda bi, i: (bi, jnp.clip(i * edges_per_tile + offset, 0, n_edges - 1), col_blk))
        return [edge(-1), pl.BlockSpec((1, C_TILE, kv_cols), lambda bi, i: (bi, i, col_blk)),
                edge(edges_per_tile)]

    return pl.pallas_call(
        functools.partial(_attn_c_body, n_heads=n_heads, slopes=_alibi_slopes(n_heads)),
        grid=(b, n_tiles),
        in_specs=[pl.BlockSpec(memory_space=pltpu.SMEM),
                  pl.BlockSpec((1, C_TILE, q_cols), lambda bi, i: (bi, i, 0))] + kv_specs(k_blk) + kv_specs(v_blk),
        out_specs=pl.BlockSpec((1, C_TILE, q_cols), lambda bi, i: (bi, i, 0)),
        out_shape=jax.ShapeDtypeStruct((b, seq, q_cols), BF16),
        scratch_shapes=[pltpu.VMEM((3, n_heads, span, BLOCK_Q), F32),
                        pltpu.VMEM((BAND_SLOTS, span, 2 * BLOCK_Q), F32)],
        compiler_params=_params("parallel", "arbitrary"),
        name="attn_c",
    )(sinks, qkv, qkv, qkv, qkv, qkv, qkv, qkv)


REFS_PER_SEQ = 9


def _attn_b_body(*refs, slopes, dil, n_seq):
    in_refs, (o_ref, lse_ref, bias_ref, s_buf) = refs[:n_seq * REFS_PER_SEQ], refs[n_seq * REFS_PER_SEQ:]
    i = pl.program_id(2)
    pl.when(jnp.logical_and(pl.program_id(1) == 0, i == 0))(
        lambda: _fill_band_bias(bias_ref, slopes, B_HALF, float(dil)))
    n_blocks = in_refs[0].shape[1] // BLOCK_Q
    block_variants = ([jnp.where(i == 0, 0, 1)] + [1] * (n_blocks - 2)
                      + [jnp.where(i == pl.num_programs(2) - 1, 2, 1)])
    low = lax.broadcasted_iota(jnp.int32, (BLOCK_Q, LANES), 1) < HEAD_DIM
    zero = jnp.zeros((BLOCK_Q, LANES), BF16)
    heads_all = range(B_HEADS_PER_GROUP)
    plain = [h for h in heads_all if h % 2 == h // B_REP]
    crossed = [h for h in heads_all if h % 2 != h // B_REP]
    jobs, job_place = [], []
    for seq in range(n_seq):
        q0, q1, q2, kp, kc, kn, vp, vc, vn = in_refs[seq * REFS_PER_SEQ:(seq + 1) * REFS_PER_SEQ]
        k, vt = _band_operands((kp, kc, kn), (vp, vc, vn))
        slabs = [q0[0], q1[0], q2[0]]
        for blk in range(n_blocks):
            q_lo = blk * BLOCK_Q
            ks = k[q_lo:q_lo + B_SPAN]
            ks_swapped = _swap_halves(ks)
            vt1s = {kv: _vt_with_ones(vt, kv, q_lo, B_SPAN) for kv in range(B_KV_PER_GROUP)}
            for kmat, heads in ((ks, plain[:2]), (ks_swapped, crossed), (ks, plain[2:])):
                q_rows = jnp.concatenate(
                    [jnp.where(low, slabs[h // 2][q_lo:q_lo + BLOCK_Q], zero) if h % 2 == 0
                     else jnp.where(low, zero, slabs[h // 2][q_lo:q_lo + BLOCK_Q]) for h in heads], axis=0)
                jobs.append((kmat, q_rows, heads, [h // B_REP for h in heads], vt1s, block_variants[blk], None))
                job_place.append((seq, blk))
    o_t, lse_t = {}, {}
    for job, place, (ot, m, den) in zip(jobs, job_place, _band_pipeline(jobs, bias_ref, s_buf)):
        lse = (m + jnp.log2(den)) * LN2
        for n, h in enumerate(job[2]):
            o_t[place, h] = ot[:, n * BLOCK_Q:(n + 1) * BLOCK_Q]
            lse_t[place, h] = jnp.broadcast_to(lse[:, n * BLOCK_Q:(n + 1) * BLOCK_Q], (HEAD_DIM, BLOCK_Q))
    gq = B_HEADS_PER_GROUP * HEAD_DIM
    for seq in range(n_seq):
        for blk in range(n_blocks):
            rows = slice(blk * BLOCK_Q, (blk + 1) * BLOCK_Q)
            for j in range(B_HEADS_PER_GROUP // 2):
                cols = slice(seq * gq + j * LANES, seq * gq + (j + 1) * LANES)
                pair = lambda t: jnp.concatenate([t[(seq, blk), 2 * j], t[(seq, blk), 2 * j + 1]], axis=0).T
                o_ref[0, rows, cols] = pair(o_t).astype(o_ref.dtype)
                lse_ref[0, rows, cols] = pair(lse_t)


def _attn_b(group_qkv, b):
    n_g = len(B_GROUPS)
    gq = B_HEADS_PER_GROUP * HEAD_DIM
    slabs_per_token = B_GROUP_COLS // LANES
    q_slabs = gq // LANES
    all_slopes = _alibi_slopes(n_g * B_HEADS_PER_GROUP)
    outs, lses = [], []
    for g, (window, dil) in enumerate(B_GROUPS):
        assert (window // 2) // dil == B_HALF
        sub = group_qkv[g].shape[0] // b
        tile = min(B_TILE, sub)
        n_seq = min(B_TILE // tile, dil)
        assert tile >= 2 * BLOCK_Q and sub % tile == 0 and dil % n_seq == 0
        n_tiles = sub // tile
        n_halves = sub // B_HALF
        halves_per_tile = tile // B_HALF
        view = group_qkv[g].reshape(b, sub, dil * B_GROUP_COLS)

        def seq_specs(seq, tile=tile, n_seq=n_seq, n_halves=n_halves, halves_per_tile=halves_per_tile):
            col = lambda slab: (lambda bi, rr, i: (rr * n_seq + seq) * slabs_per_token + slab)

            def tile_spec(slab):
                return pl.BlockSpec((1, tile, LANES), lambda bi, rr, i: (bi, i, col(slab)(bi, rr, i)))

            def edge_spec(slab, offset):
                return pl.BlockSpec(
                    (1, B_HALF, LANES),
                    lambda bi, rr, i: (bi, jnp.clip(i * halves_per_tile + offset, 0, n_halves - 1),
                                       col(slab)(bi, rr, i)))

            kv_specs = lambda slab: [edge_spec(slab, -1), tile_spec(slab), edge_spec(slab, halves_per_tile)]
            return [tile_spec(j) for j in range(q_slabs)] + kv_specs(q_slabs) + kv_specs(q_slabs + 1)

        out_map = lambda bi, rr, i: (bi, i, rr)
        out_spec = pl.BlockSpec((1, tile, n_seq * gq), out_map)
        o, lse = pl.pallas_call(
            functools.partial(_attn_b_body, dil=dil, n_seq=n_seq,
                              slopes=all_slopes[g * B_HEADS_PER_GROUP:(g + 1) * B_HEADS_PER_GROUP]),
            grid=(b, dil // n_seq, n_tiles),
            in_specs=sum((seq_specs(seq) for seq in range(n_seq)), []),
            out_specs=[out_spec, out_spec],
            out_shape=[jax.ShapeDtypeStruct((b, sub, dil * gq), BF16),
                       jax.ShapeDtypeStruct((b, sub, dil * gq), F32)],
            scratch_shapes=[pltpu.VMEM((3, B_HEADS_PER_GROUP, B_SPAN, BLOCK_Q), F32),
                            pltpu.VMEM((BAND_SLOTS, B_SPAN, 2 * BLOCK_Q), F32)],
            compiler_params=_params("parallel", "arbitrary", "arbitrary"),
            name=f"attn_b_g{g}",
        )(*([view] * (n_seq * REFS_PER_SEQ)))
        outs.append(o.reshape(b * sub, dil * gq))
        lses.append(lse.reshape(b * sub, dil * gq))
    return outs, lses


PROLOGUE_ROW_CHUNKS = 2


def _mix_plain(refs, scratch):
    return lambda rows: refs[0][rows, :]


def _mix_groups(refs, scratch):
    n_g = len(B_GROUPS)
    o_scr, l_scr = scratch
    slabs = o_scr.shape[0] // n_g
    tm = o_scr.shape[1]
    gq = slabs * LANES
    for g, (_, dil) in enumerate(B_GROUPS):
        for r in range(dil):
            rows = pl.ds(r, tm // dil, stride=dil)
            for s in range(slabs):
                cols = slice(r * gq + s * LANES, r * gq + (s + 1) * LANES)
                o_scr[g * slabs + s, rows, :] = refs[g][:, cols].astype(F32)
                l_scr[g * slabs + s, rows, :] = refs[n_g + g][:, cols]
    def lhs(rows):
        group = lambda scr, g: jnp.concatenate([scr[g * slabs + s, rows, :] for s in range(slabs)], axis=1)
        ls = [group(l_scr, g) for g in range(n_g)]
        mx = functools.reduce(jnp.maximum, ls)
        es = [jnp.exp(l - mx) for l in ls]
        inv = 1.0 / functools.reduce(lambda a, b: a + b, es)
        return jnp.concatenate([(group(o_scr, g) * (es[g] * inv)).astype(BF16) for g in range(n_g)], axis=1)

    return lhs


MLP_HIDDEN_CHUNK = 1024


def _block_body(*refs, n_mix, mix_fn, final_norm):
    h_ref, mix_refs = refs[0], refs[1:1 + n_mix]
    wo_ref, g_ref, w1_ref, w2_ref, fg_ref, out_ref, hn_ref = refs[1 + n_mix:8 + n_mix]
    mix_scratch = refs[8 + n_mix:]

    lhs = mix_fn(mix_refs, mix_scratch)
    chunk = h_ref.shape[0] // PROLOGUE_ROW_CHUNKS
    rows = [slice(c * chunk, (c + 1) * chunk) for c in range(PROLOGUE_ROW_CHUNKS)]
    project = lambda c: jnp.dot(lhs(rows[c]), wo_ref[...], preferred_element_type=F32)
    nxt = project(0)
    for c in range(PROLOGUE_ROW_CHUNKS):
        cur = nxt
        if c + 1 < PROLOGUE_ROW_CHUNKS:
            nxt = project(c + 1)
        h1 = h_ref[rows[c], :] + cur
        out_ref[rows[c], :] = h1
        hn_ref[rows[c], :] = _rms_rows(h1, g_ref[...]).astype(BF16)

    hn = hn_ref[...]
    for c in range(w1_ref.shape[1] // MLP_HIDDEN_CHUNK):
        cols = slice(c * MLP_HIDDEN_CHUNK, (c + 1) * MLP_HIDDEN_CHUNK)
        u = jnp.maximum(jnp.dot(hn, w1_ref[:, cols], preferred_element_type=F32), 0.0)
        out_ref[...] += jnp.dot((u * u).astype(BF16), w2_ref[cols, :], preferred_element_type=F32)

    if final_norm:
        out_ref[...] = _rms_rows(out_ref[...], fg_ref[...])


def _block(h2, mix_inputs, w_o, *, gain, w1, w2, layer, final_gain, final_norm, tm=512):
    t, d = h2.shape
    assert t % tm == 0 and tm % PROLOGUE_ROW_CHUNKS == 0 and w1.shape[2] % MLP_HIDDEN_CHUNK == 0
    row_tile = lambda cols: pl.BlockSpec((tm, cols), lambda i: (i, 0))
    whole = lambda a: pl.BlockSpec(a.shape, lambda i: (0, 0), pipeline_mode=pl.Buffered(1))
    of_layer = lambda a: pl.BlockSpec((None,) + a.shape[1:], lambda i: (layer, 0, 0),
                                      pipeline_mode=pl.Buffered(1))
    scratch = [pltpu.VMEM((tm, d), BF16)]
    if len(mix_inputs) == 1:
        mix_fn, mix_specs = _mix_plain, [row_tile(mix_inputs[0].shape[1])]
    else:
        n_g = len(B_GROUPS)
        gq = B_HEADS_PER_GROUP * HEAD_DIM
        mix_fn = _mix_groups
        mix_specs = [pl.BlockSpec((tm // dil, dil * gq), lambda i: (i, 0)) for _, dil in B_GROUPS] * 2
        scratch += [pltpu.VMEM((n_g * gq // LANES, tm, LANES), F32)] * 2
    return pl.pallas_call(
        functools.partial(_block_body, n_mix=len(mix_inputs), mix_fn=mix_fn, final_norm=final_norm),
        grid=(t // tm,),
        in_specs=[row_tile(d)] + mix_specs
                 + [whole(w_o), whole(gain), of_layer(w1), of_layer(w2), whole(final_gain)],
        out_specs=row_tile(d),
        out_shape=jax.ShapeDtypeStruct((t, d), F32),
        scratch_shapes=scratch,
        compiler_params=_params("parallel"),
        name="block_mlp",
    )(h2, *mix_inputs, w_o, gain, w1, w2, final_gain)


def kernel(x, attn_norm, mlp_norm, a_w_qkv, a_q_gain, a_k_gain, a_w_o, b_w_qkv, b_w_o,
           c_w_qkv, c_sinks, c_w_o, mlp_w1, mlp_w2, final_norm):
    b, seq, d = x.shape
    depth = attn_norm.shape[0]
    h = x.reshape(b * seq, d)
    scale = HEAD_DIM ** -0.5 * LOG2E
    a_heads = a_w_o.shape[1] // HEAD_DIM
    c_heads = c_w_o.shape[1] // HEAD_DIM
    cos, sin = _rope_tables(seq)
    head = jnp.arange(2 * LANES) // HEAD_DIM
    seg = (head[:, None] == head[None, :]).astype(BF16)
    tile4 = lambda g: jnp.tile(g, 2 * LANES // HEAD_DIM)[None, :]
    w1_all, w2_all = mlp_w1.astype(BF16), mlp_w2.astype(BF16)
    used = [0, 0, 0]
    for layer in range(depth):
        kind = layer % N_MIXERS
        j = used[kind]
        used[kind] += 1
        gain = attn_norm[layer][None, :]
        block = functools.partial(_block, gain=mlp_norm[layer][None, :], w1=w1_all, w2=w2_all, layer=layer,
                                  final_gain=final_norm[None, :], final_norm=(layer == depth - 1))
        if kind == 0:
            q_cols = a_heads * HEAD_DIM
            k_cols = (a_w_qkv.shape[2] - q_cols) // 2
            qkv = _qkv_proj(h, gain, a_w_qkv[j].astype(BF16), q_cols=q_cols, q_scale=scale,
                            rope=(tile4(a_q_gain[j]), tile4(a_k_gain[j]), cos, sin, seg, k_cols))
            o = _attn_a(qkv.reshape(b, seq, -1), n_heads=a_heads).reshape(b * seq, q_cols)
            h = block(h, [o], a_w_o[j].astype(BF16))
        elif kind == 1:
            q_cols = len(B_GROUPS) * B_HEADS_PER_GROUP * HEAD_DIM
            groups = _qkv_proj_b(h, gain, b_w_qkv[j].astype(BF16), q_cols=q_cols, q_scale=scale)
            outs, lses = _attn_b(groups, b)
            h = block(h, outs + lses, b_w_o[j].astype(BF16))
        else:
            q_cols = c_heads * HEAD_DIM
            qkv = _qkv_proj(h, gain, c_w_qkv[j].astype(BF16), q_cols=q_cols, q_scale=scale)
            o = _attn_c(qkv.reshape(b, seq, -1), c_sinks[j], n_heads=c_heads).reshape(b * seq, q_cols)
            h = block(h, [o], c_w_o[j].astype(BF16))
    return h.reshape(b, seq, d)
```
